```python
import math
import jax, jax.numpy as jnp
from jax import lax
import numpy as np

D_MODEL = 1024
BATCH = 8
SEQ = 2048
DEPTH = 1
DEC_BATCH = 128
DEC_SEQ = 1
PAST_LEN = 16384
PAGE_SIZE = 128

D_MIX = 2 * D_MODEL
D_SSD = D_MIX // 2
D_SC = D_MIX - D_SSD
SSD_HEAD_DIM = 64
SSD_HEADS = D_SSD // SSD_HEAD_DIM
SSD_GROUPS = 2
SSD_STATE = 128
SSD_CONV = 4
SSD_CHUNK = 128
SSD_CONV_DIM = D_SSD + 2 * SSD_GROUPS * SSD_STATE
SC_GROUPS = 16
SC_CONV = 3
D_FF = ((8 * D_MODEL // 3 + 127) // 128) * 128
FFN_CONV = 3
EPS = 1e-5
SPLITS = [D_SSD, D_SSD + SSD_CONV_DIM, D_SSD + SSD_CONV_DIM + SSD_HEADS,
          D_SSD + SSD_CONV_DIM + SSD_HEADS + D_SC,
          D_SSD + SSD_CONV_DIM + SSD_HEADS + 2 * D_SC]
D_IN_PROJ = D_SSD + SSD_CONV_DIM + SSD_HEADS + 3 * D_SC

kernel_name = "hybrid_ssd_shortconv_convffn_step"


def rms_norm(x, w):
    xf = x.astype(jnp.float32)
    y = xf * lax.rsqrt(jnp.mean(xf * xf, axis=-1, keepdims=True) + EPS)
    return (y * w.astype(jnp.float32)).astype(x.dtype)


def causal_dwconv(x, buf, w, b=None):
    K = w.shape[0]
    L = x.shape[1]
    xp = jnp.concatenate([buf.astype(x.dtype), x], axis=1)
    y = xp[:, 0:L] * w[0]
    for k in range(1, K):
        y = y + xp[:, k:k + L] * w[k]
    if b is not None:
        y = y + b
    return y, xp[:, L:]


def ssd_scan(x, dt, a, b_in, c_in, h0):
    bsz, l, nh, p = x.shape
    g = b_in.shape[2]
    r = nh // g
    n = b_in.shape[-1]
    q = SSD_CHUNK if l % SSD_CHUNK == 0 else l
    nc = l // q
    xd = (x * dt[..., None]).reshape(bsz, nc, q, g, r, p)
    da = (dt * a).reshape(bsz, nc, q, g, r)
    bc = b_in.reshape(bsz, nc, q, g, n)
    cc = c_in.reshape(bsz, nc, q, g, n)
    a_cum = jnp.cumsum(da, axis=2)
    mask = jnp.tril(jnp.ones((q, q), dtype=bool))
    seg = a_cum[:, :, :, None] - a_cum[:, :, None, :]
    decay = jnp.exp(jnp.where(mask[:, :, None, None], seg, -jnp.inf))
    cb = jnp.einsum('bclgn,bcsgn->bclsg', cc, bc)
    y_diag = jnp.einsum('bclsg,bclsgr,bcsgrp->bclgrp', cb, decay, xd)
    decay_s = jnp.exp(a_cum[:, :, -1:] - a_cum)
    states = jnp.einsum('bcsgn,bcsgr,bcsgrp->bcgrpn', bc, decay_s, xd)
    chunk_decay = jnp.exp(a_cum[:, :, -1])

    def step(h, inp):
        st, dec = inp
        return dec[..., None, None] * h + st, h

    h_last, h_prev = lax.scan(step, h0.reshape(bsz, g, r, p, n),
                              (jnp.swapaxes(states, 0, 1), jnp.swapaxes(chunk_decay, 0, 1)))
    h_prev = jnp.swapaxes(h_prev, 0, 1)
    y_off = jnp.einsum('bclgn,bcgrpn,bclgr->bclgrp', cc, h_prev, jnp.exp(a_cum))
    y = (y_diag + y_off).reshape(bsz, l, nh, p)
    return y, h_last.reshape(bsz, nh, p, n)


def hybrid_mixer(u, st_ssm, st_ssd_conv, st_sc_conv, w_in, ssd_conv_w, ssd_conv_b,
                 ssd_dt_bias, ssd_a_log, ssd_d, ssd_norm_w, sc_conv_w, w_out):
    bsz, l, _ = u.shape
    proj = u @ w_in
    z, xbc, dt_raw, g_b, g_c, h = jnp.split(proj, SPLITS, axis=-1)
    xbc, new_ssd_conv = causal_dwconv(xbc, st_ssd_conv, ssd_conv_w, ssd_conv_b)
    xbc = jax.nn.silu(xbc).astype(jnp.float32)
    xs, bs, cs = jnp.split(xbc, [D_SSD, D_SSD + SSD_GROUPS * SSD_STATE], axis=-1)
    xs = xs.reshape(bsz, l, SSD_HEADS, SSD_HEAD_DIM)
    dt = jax.nn.softplus(dt_raw.astype(jnp.float32) + ssd_dt_bias.astype(jnp.float32))
    a = -jnp.exp(ssd_a_log.astype(jnp.float32))
    y, new_ssm = ssd_scan(xs, dt, a,
                          bs.reshape(bsz, l, SSD_GROUPS, SSD_STATE),
                          cs.reshape(bsz, l, SSD_GROUPS, SSD_STATE),
                          st_ssm.astype(jnp.float32))
    y = y + ssd_d.astype(jnp.float32)[:, None] * xs
    y = y.reshape(bsz, l, D_SSD) * jax.nn.silu(z.astype(jnp.float32))
    yg = y.reshape(bsz, l, SSD_GROUPS, D_SSD // SSD_GROUPS)
    yg = yg * lax.rsqrt(jnp.mean(yg * yg, axis=-1, keepdims=True) + EPS)
    y_ssd = (yg.reshape(bsz, l, D_SSD) * ssd_norm_w.astype(jnp.float32)).astype(u.dtype)
    sc, new_sc_conv = causal_dwconv(g_c * h, st_sc_conv, sc_conv_w)
    y_sc = g_b * sc
    out = jnp.concatenate([y_ssd, y_sc], axis=-1) @ w_out
    return out, new_ssm.astype(u.dtype), new_ssd_conv, new_sc_conv


def conv_ffn(u, st_ffn, w_ffn_in, ffn_conv_w, ffn_conv_b, w_down):
    gate, up = jnp.split(u @ w_ffn_in, [D_FF], axis=-1)
    gate, new_st = causal_dwconv(gate, st_ffn, ffn_conv_w, ffn_conv_b)
    return (jax.nn.silu(gate) * up) @ w_down, new_st


def run_trunk(x, st_ssm, st_ssd_conv, st_sc_conv, st_ffn, norm_mix_w, w_in, ssd_conv_w,
              ssd_conv_b, ssd_dt_bias, ssd_a_log, ssd_d, ssd_norm_w, sc_conv_w, w_out,
              norm_ffn_w, w_ffn_in, ffn_conv_w, ffn_conv_b, w_down, norm_final_w):
    n_ssm, n_ssd_conv, n_sc_conv, n_ffn = [], [], [], []
    for i in range(DEPTH):
        m, s1, s2, s3 = hybrid_mixer(rms_norm(x, norm_mix_w[i]), st_ssm[i], st_ssd_conv[i],
                                     st_sc_conv[i], w_in[i], ssd_conv_w[i], ssd_conv_b[i],
                                     ssd_dt_bias[i], ssd_a_log[i], ssd_d[i], ssd_norm_w[i],
                                     sc_conv_w[i], w_out[i])
        x = x + m
        f, s4 = conv_ffn(rms_norm(x, norm_ffn_w[i]), st_ffn[i], w_ffn_in[i],
                         ffn_conv_w[i], ffn_conv_b[i], w_down[i])
        x = x + f
        n_ssm.append(s1); n_ssd_conv.append(s2); n_sc_conv.append(s3); n_ffn.append(s4)
    y = rms_norm(x, norm_final_w)
    return (y, jnp.stack(n_ssm), jnp.stack(n_ssd_conv), jnp.stack(n_sc_conv), jnp.stack(n_ffn))


def setup_inputs(seed: int = 0) -> dict:
    key = jax.random.key(seed)
    ks = jax.random.split(key, 24)
    f32 = jnp.float32
    nrm = lambda k, shape, s: jax.random.normal(k, shape, f32) * s
    dt0 = jnp.exp(jax.random.uniform(ks[10], (DEPTH, SSD_HEADS), f32,
                                     math.log(1e-3), math.log(1e-1)))
    return {
        "x_prompt": nrm(ks[0], (BATCH, SEQ, D_MODEL), 1.0),
        "x_sample": nrm(ks[1], (DEC_BATCH, DEC_SEQ, D_MODEL), 1.0),
        "state_ssm": nrm(ks[2], (DEPTH, DEC_BATCH, SSD_HEADS, SSD_HEAD_DIM, SSD_STATE), 0.1),
        "state_ssd_conv": nrm(ks[3], (DEPTH, DEC_BATCH, SSD_CONV - 1, SSD_CONV_DIM), 1.0),
        "state_short_conv": nrm(ks[4], (DEPTH, DEC_BATCH, SC_CONV - 1, D_SC), 1.0),
        "state_ffn_conv": nrm(ks[5], (DEPTH, DEC_BATCH, FFN_CONV - 1, D_FF), 1.0),
        "norm_mix_w": 1.0 + nrm(ks[6], (DEPTH, D_MODEL), 0.02),
        "w_in": nrm(ks[7], (DEPTH, D_MODEL, D_IN_PROJ), D_MODEL ** -0.5),
        "ssd_conv_w": nrm(ks[8], (DEPTH, SSD_CONV, SSD_CONV_DIM), SSD_CONV ** -0.5),
        "ssd_conv_b": nrm(ks[9], (DEPTH, SSD_CONV_DIM), 0.02),
        "ssd_dt_bias": dt0 + jnp.log(-jnp.expm1(-dt0)),
        "ssd_a_log": jnp.log(jax.random.uniform(ks[11], (DEPTH, SSD_HEADS), f32, 1.0, 16.0)),
        "ssd_d": 1.0 + nrm(ks[12], (DEPTH, SSD_HEADS), 0.1),
        "ssd_norm_w": 1.0 + nrm(ks[13], (DEPTH, D_SSD), 0.02),
        "sc_conv_w": nrm(ks[14], (DEPTH, SC_CONV, D_SC), SC_CONV ** -0.5),
        "w_out": nrm(ks[15], (DEPTH, D_MIX, D_MODEL), D_MIX ** -0.5),
        "norm_ffn_w": 1.0 + nrm(ks[16], (DEPTH, D_MODEL), 0.02),
        "w_ffn_in": nrm(ks[17], (DEPTH, D_MODEL, 2 * D_FF), D_MODEL ** -0.5),
        "ffn_conv_w": nrm(ks[18], (DEPTH, FFN_CONV, D_FF), FFN_CONV ** -0.5),
        "ffn_conv_b": nrm(ks[19], (DEPTH, D_FF), 0.02),
        "w_down": nrm(ks[20], (DEPTH, D_FF, D_MODEL), D_FF ** -0.5),
        "norm_final_w": 1.0 + nrm(ks[21], (D_MODEL,), 0.02),
    }


def reference(x_prompt, x_sample, state_ssm, state_ssd_conv, state_short_conv, state_ffn_conv,
              norm_mix_w, w_in, ssd_conv_w, ssd_conv_b, ssd_dt_bias, ssd_a_log, ssd_d,
              ssd_norm_w, sc_conv_w, w_out, norm_ffn_w, w_ffn_in, ffn_conv_w, ffn_conv_b,
              w_down, norm_final_w):
    bp = x_prompt.shape[0]
    dtp = x_prompt.dtype
    z_ssm = jnp.zeros((DEPTH, bp, SSD_HEADS, SSD_HEAD_DIM, SSD_STATE), dtp)
    z_ssd_conv = jnp.zeros((DEPTH, bp, SSD_CONV - 1, SSD_CONV_DIM), dtp)
    z_sc_conv = jnp.zeros((DEPTH, bp, SC_CONV - 1, D_SC), dtp)
    z_ffn = jnp.zeros((DEPTH, bp, FFN_CONV - 1, D_FF), dtp)
    y_prompt, p_ssm, p_ssd_conv, p_sc_conv, p_ffn = run_trunk(
        x_prompt, z_ssm, z_ssd_conv, z_sc_conv, z_ffn, norm_mix_w, w_in, ssd_conv_w,
        ssd_conv_b, ssd_dt_bias, ssd_a_log, ssd_d, ssd_norm_w, sc_conv_w, w_out,
        norm_ffn_w, w_ffn_in, ffn_conv_w, ffn_conv_b, w_down, norm_final_w)
    y_sample, s_ssm, s_ssd_conv, s_sc_conv, s_ffn = run_trunk(
        x_sample, state_ssm, state_ssd_conv, state_short_conv, state_ffn_conv, norm_mix_w,
        w_in, ssd_conv_w, ssd_conv_b, ssd_dt_bias, ssd_a_log, ssd_d, ssd_norm_w, sc_conv_w,
        w_out, norm_ffn_w, w_ffn_in, ffn_conv_w, ffn_conv_b, w_down, norm_final_w)
    return (y_prompt, y_sample, p_ssm, p_ssd_conv, p_sc_conv, p_ffn,
            s_ssm, s_ssd_conv, s_sc_conv, s_ffn)
```

```python
import functools

import jax
import jax.numpy as jnp
from jax import lax
from jax.experimental import pallas as pl
from jax.experimental.pallas import tpu as pltpu

D_MODEL = 1024
D_SSD = 1024
D_SC = 1024
NHEADS = 16
HEAD_DIM = 64
NGROUPS = 2
NSTATE = 128
SSD_CONV = 4
CONV_DIM = D_SSD + 2 * NGROUPS * NSTATE
SC_CONV = 3
D_FF = 2816
FFN_CONV = 3
EPS = 1e-5
D_MIX = D_SSD + D_SC

LANES = 128
DT_PAD = LANES
OFF_Z = 0
OFF_XBC = OFF_Z + D_SSD
OFF_GB = OFF_XBC + CONV_DIM
OFF_GC = OFF_GB + D_SC
OFF_H = OFF_GC + D_SC
OFF_DT = OFF_H + D_SC
D_IN_PACKED = OFF_DT + DT_PAD

CHUNK = 128
CARRY = 8
SEQ_TILE = 256
VMEM_LIMIT = 56 * 1024 * 1024

F32 = jnp.float32
BF16 = jnp.bfloat16
HIGHEST = lax.Precision.HIGHEST


def _dot(a, b):
    return jnp.dot(a, b, preferred_element_type=F32)


def _silu(v):
    return v * (1.0 / (1.0 + jnp.exp(-v)))


def _softplus(v):
    return jnp.maximum(v, 0.0) + jnp.log1p(jnp.exp(-jnp.abs(v)))


def _rms_rows(x, w):
    ms = jnp.mean(x * x, axis=-1, keepdims=True)
    return x * lax.rsqrt(ms + EPS) * w


def _mixer_prompt_kernel(x_ref, nw_ref, win_ref, cw_ref, cb_ref, dtb_ref, a_ref, dx_ref,
                         gnw_ref, scw_ref, wout_ref, e_ref,
                         x1_ref, ssm_ref, cst_ref, scst_ref,
                         xbc_ext, sc_ext, act_ref, dt_ref, s_ref, y_ref, ycat_ref):
    tile = x_ref.shape[1]
    t = pl.program_id(1)

    @pl.when(t == 0)
    def _():
        xbc_ext[0:CARRY, :] = jnp.zeros((CARRY, CONV_DIM), F32)
        sc_ext[0:CARRY, :] = jnp.zeros((CARRY, D_SC), F32)
        s_ref[...] = jnp.zeros_like(s_ref)

    x = x_ref[0]
    u = _rms_rows(x, nw_ref[...]).astype(BF16)

    xbc_ext[CARRY:CARRY + tile, :] = _dot(u, win_ref[:, OFF_XBC:OFF_XBC + CONV_DIM])
    dt_ref[...] = _dot(u, win_ref[:, OFF_DT:OFF_DT + DT_PAD])
    gc = _dot(u, win_ref[:, OFF_GC:OFF_GC + D_SC])
    hh = _dot(u, win_ref[:, OFF_H:OFF_H + D_SC])
    sc_ext[CARRY:CARRY + tile, :] = gc * hh

    conv = cb_ref[...] + xbc_ext[CARRY - 3:CARRY - 3 + tile, :] * cw_ref[0:1, :]
    for k in range(1, SSD_CONV):
        conv = conv + xbc_ext[CARRY - 3 + k:CARRY - 3 + k + tile, :] * cw_ref[k:k + 1, :]
    act_ref[...] = _silu(conv)
    tail = xbc_ext[tile + CARRY - 3:tile + CARRY, :]
    cst_ref[0] = tail
    xbc_ext[CARRY - 3:CARRY, :] = tail

    row_i = lax.broadcasted_iota(jnp.int32, (CHUNK, CHUNK), 0)
    col_i = lax.broadcasted_iota(jnp.int32, (CHUNK, CHUNK), 1)
    causal = row_i >= col_i
    tri = causal.astype(F32)
    lane_i = lax.broadcasted_iota(jnp.int32, (1, LANES), 1)
    lo_half = lane_i < HEAD_DIM
    neg_inf = jnp.float32(-jnp.inf)

    for c in range(tile // CHUNK):
        r0 = c * CHUNK
        dt = _softplus(dt_ref[r0:r0 + CHUNK, :] + dtb_ref[...])
        da = dt * a_ref[...]
        acum = jnp.dot(tri, da, precision=HIGHEST, preferred_element_type=F32)
        acum_t = acum.T
        dt_t = dt.T
        w_t = jnp.exp(acum_t[:, CHUNK - 1:CHUNK] - acum_t) * dt_t
        cd = jnp.exp(acum[CHUNK - 1:CHUNK, :])
        cdx = jnp.dot(jnp.broadcast_to(cd, (8, LANES)), e_ref[...], precision=HIGHEST,
                      preferred_element_type=F32)[0:1, :]
        for g in range(NGROUPS):
            bg = act_ref[r0:r0 + CHUNK, D_SSD + g * NSTATE:D_SSD + (g + 1) * NSTATE]
            cg = act_ref[r0:r0 + CHUNK, D_SSD + (NGROUPS + g) * NSTATE:D_SSD + (NGROUPS + g + 1) * NSTATE]
            cb = lax.dot_general(cg.astype(BF16), bg.astype(BF16), (((1,), (1,)), ((), ())),
                                 preferred_element_type=F32)
            bg_t = bg.T
            heads_per_group = NHEADS // NGROUPS
            for jp in range(heads_per_group // 2):
                j = g * (heads_per_group // 2) + jp
                xp = act_ref[r0:r0 + CHUNK, j * LANES:(j + 1) * LANES]
                sp = s_ref[:, j * LANES:(j + 1) * LANES]
                ypair = jnp.zeros((CHUNK, LANES), F32)
                snew = jnp.zeros((NSTATE, LANES), F32)
                for half in range(2):
                    h = 2 * j + half
                    keep = lo_half if half == 0 else jnp.logical_not(lo_half)
                    xh = jnp.where(keep, xp, 0.0).astype(BF16)
                    sh = jnp.where(keep, sp, 0.0).astype(BF16)
                    col = acum[:, h:h + 1]
                    seg = col - acum_t[h:h + 1, :]
                    decay = jnp.exp(jnp.where(causal, seg, neg_inf))
                    m = cb * decay * dt_t[h:h + 1, :]
                    eac = jnp.exp(col) * cg
                    lhs = jnp.concatenate([m, eac], axis=1).astype(BF16)
                    rhs = jnp.concatenate([xh, sh], axis=0)
                    ypair = ypair + _dot(lhs, rhs)
                    bw_t = (bg_t * w_t[h:h + 1, :]).astype(BF16)
                    snew = snew + _dot(bw_t, xh)
                y_ref[r0:r0 + CHUNK, j * LANES:(j + 1) * LANES] = ypair
                s_ref[:, j * LANES:(j + 1) * LANES] = cdx[:, j * LANES:(j + 1) * LANES] * sp + snew

    z = _dot(u, win_ref[:, OFF_Z:OFF_Z + D_SSD])
    y = (y_ref[...] + dx_ref[...] * act_ref[:, 0:D_SSD]) * _silu(z)
    gw = D_SSD // NGROUPS
    for g in range(NGROUPS):
        yg = y[:, g * gw:(g + 1) * gw]
        ycat_ref[:, g * gw:(g + 1) * gw] = _rms_rows(yg, gnw_ref[:, g * gw:(g + 1) * gw]).astype(BF16)

    sc = sc_ext[CARRY - 2:CARRY - 2 + tile, :] * scw_ref[0:1, :]
    for k in range(1, SC_CONV):
        sc = sc + sc_ext[CARRY - 2 + k:CARRY - 2 + k + tile, :] * scw_ref[k:k + 1, :]
    gb = _dot(u, win_ref[:, OFF_GB:OFF_GB + D_SC])
    ycat_ref[:, D_SSD:D_MIX] = (gb * sc).astype(BF16)
    sc_tail = sc_ext[tile + CARRY - 2:tile + CARRY, :]
    scst_ref[0] = sc_tail
    sc_ext[CARRY - 2:CARRY, :] = sc_tail

    x1_ref[0] = x + _dot(ycat_ref[...], wout_ref[...])

    @pl.when(t == pl.num_programs(1) - 1)
    def _():
        for j in range(NHEADS // 2):
            ssm_ref[0, 2 * j:2 * j + 2] = s_ref[:, j * LANES:(j + 1) * LANES].T.reshape(2, HEAD_DIM, NSTATE)


def _const_spec(shape):
    nd = len(shape)
    return pl.BlockSpec(shape, lambda *_: (0,) * nd, pipeline_mode=pl.Buffered(1))


def _mixer_prompt(x, nw, win, cw, cb, dtb, a_pad, dx, gnw, scw, wout, expand):
    nb, seq, _ = x.shape
    tile = SEQ_TILE
    grid = (nb, seq // tile)
    tok_spec = pl.BlockSpec((1, tile, D_MODEL), lambda b, t: (b, t, 0))
    return pl.pallas_call(
        _mixer_prompt_kernel,
        grid=grid,
        in_specs=[tok_spec] + [_const_spec(p.shape) for p in
                               (nw, win, cw, cb, dtb, a_pad, dx, gnw, scw, wout, expand)],
        out_specs=[
            tok_spec,
            pl.BlockSpec((1, NHEADS, HEAD_DIM, NSTATE), lambda b, t: (b, 0, 0, 0)),
            pl.BlockSpec((1, SSD_CONV - 1, CONV_DIM), lambda b, t: (b, 0, 0)),
            pl.BlockSpec((1, SC_CONV - 1, D_SC), lambda b, t: (b, 0, 0)),
        ],
        out_shape=[
            jax.ShapeDtypeStruct((nb, seq, D_MODEL), F32),
            jax.ShapeDtypeStruct((nb, NHEADS, HEAD_DIM, NSTATE), F32),
            jax.ShapeDtypeStruct((nb, SSD_CONV - 1, CONV_DIM), F32),
            jax.ShapeDtypeStruct((nb, SC_CONV - 1, D_SC), F32),
        ],
        scratch_shapes=[
            pltpu.VMEM((tile + CARRY, CONV_DIM), F32),
            pltpu.VMEM((tile + CARRY, D_SC), F32),
            pltpu.VMEM((tile, CONV_DIM), F32),
            pltpu.VMEM((tile, DT_PAD), F32),
            pltpu.VMEM((NSTATE, D_SSD), F32),
            pltpu.VMEM((tile, D_SSD), F32),
            pltpu.VMEM((tile, D_MIX), BF16),
        ],
        compiler_params=pltpu.CompilerParams(
            dimension_semantics=("arbitrary", "arbitrary"), vmem_limit_bytes=VMEM_LIMIT),
        name="mixer_prompt",
    )(x, nw, win, cw, cb, dtb, a_pad, dx, gnw, scw, wout, expand)


def _ffn_prompt_kernel(x1_ref, nw_ref, wffn_ref, fcw_ref, fcb_ref, wdown_ref, nfw_ref,
                       y_ref, fst_ref, gate_ext):
    tile = x1_ref.shape[1]
    t = pl.program_id(1)

    @pl.when(t == 0)
    def _():
        gate_ext[0:CARRY, :] = jnp.zeros((CARRY, D_FF), F32)

    x1 = x1_ref[0]
    u = _rms_rows(x1, nw_ref[...]).astype(BF16)
    gate_ext[CARRY:CARRY + tile, :] = _dot(u, wffn_ref[:, 0:D_FF])
    up = _dot(u, wffn_ref[:, D_FF:2 * D_FF])
    g = fcb_ref[...] + gate_ext[CARRY - 2:CARRY - 2 + tile, :] * fcw_ref[0:1, :]
    for k in range(1, FFN_CONV):
        g = g + gate_ext[CARRY - 2 + k:CARRY - 2 + k + tile, :] * fcw_ref[k:k + 1, :]
    tail = gate_ext[tile + CARRY - 2:tile + CARRY, :]
    fst_ref[0] = tail
    gate_ext[CARRY - 2:CARRY, :] = tail
    act = (_silu(g) * up).astype(BF16)
    x2 = x1 + _dot(act, wdown_ref[...])
    y_ref[0] = _rms_rows(x2, nfw_ref[...])


def _ffn_prompt(x1, nw, wffn, fcw, fcb, wdown, nfw):
    nb, seq, _ = x1.shape
    tile = SEQ_TILE
    tok_spec = pl.BlockSpec((1, tile, D_MODEL), lambda b, t: (b, t, 0))
    return pl.pallas_call(
        _ffn_prompt_kernel,
        grid=(nb, seq // tile),
        in_specs=[tok_spec] + [_const_spec(p.shape) for p in (nw, wffn, fcw, fcb, wdown, nfw)],
        out_specs=[tok_spec, pl.BlockSpec((1, FFN_CONV - 1, D_FF), lambda b, t: (b, 0, 0))],
        out_shape=[jax.ShapeDtypeStruct((nb, seq, D_MODEL), F32),
                   jax.ShapeDtypeStruct((nb, FFN_CONV - 1, D_FF), F32)],
        scratch_shapes=[pltpu.VMEM((tile + CARRY, D_FF), F32)],
        compiler_params=pltpu.CompilerParams(
            dimension_semantics=("arbitrary", "arbitrary"), vmem_limit_bytes=VMEM_LIMIT),
        name="ffn_prompt",
    )(x1, nw, wffn, fcw, fcb, wdown, nfw)


def _sample_pre_kernel(x_ref, nw_ref, win_ref, cw_ref, cb_ref, dtb_ref, a_ref, scw_ref, e_ref,
                       cst_ref, scst_ref,
                       z_ref, xs_ref, xdt_ref, dax_ref, bc_ref, ysc_ref, ncst_ref, nscst_ref):
    x = x_ref[...]
    u = _rms_rows(x, nw_ref[...]).astype(BF16)
    z_ref[...] = _dot(u, win_ref[:, OFF_Z:OFF_Z + D_SSD])
    xbc = _dot(u, win_ref[:, OFF_XBC:OFF_XBC + CONV_DIM])
    conv = cb_ref[...] + xbc * cw_ref[SSD_CONV - 1:SSD_CONV, :]
    for k in range(SSD_CONV - 1):
        conv = conv + cst_ref[k] * cw_ref[k:k + 1, :]
    for k in range(SSD_CONV - 2):
        ncst_ref[k] = cst_ref[k + 1]
    ncst_ref[SSD_CONV - 2] = xbc
    act = _silu(conv)
    xs = act[:, 0:D_SSD]
    xs_ref[...] = xs
    bc_ref[...] = act[:, D_SSD:CONV_DIM]
    dt = _softplus(_dot(u, win_ref[:, OFF_DT:OFF_DT + DT_PAD]) + dtb_ref[...])
    dtx = jnp.dot(dt, e_ref[...], precision=HIGHEST, preferred_element_type=F32)
    dax = jnp.dot(dt * a_ref[...], e_ref[...], precision=HIGHEST, preferred_element_type=F32)
    xdt_ref[...] = xs * dtx
    dax_ref[...] = jnp.exp(dax)
    gc = _dot(u, win_ref[:, OFF_GC:OFF_GC + D_SC])
    hh = _dot(u, win_ref[:, OFF_H:OFF_H + D_SC])
    gch = gc * hh
    sc = gch * scw_ref[SC_CONV - 1:SC_CONV, :]
    for k in range(SC_CONV - 1):
        sc = sc + scst_ref[k] * scw_ref[k:k + 1, :]
    for k in range(SC_CONV - 2):
        nscst_ref[k] = scst_ref[k + 1]
    nscst_ref[SC_CONV - 2] = gch
    gb = _dot(u, win_ref[:, OFF_GB:OFF_GB + D_SC])
    ysc_ref[...] = gb * sc


def _sample_pre(x, nw, win, cw, cb, dtb, a_pad, scw, expand, cst, scst):
    nb = x.shape[0]
    outs = [
        jax.ShapeDtypeStruct((nb, D_SSD), F32),
        jax.ShapeDtypeStruct((nb, D_SSD), F32),
        jax.ShapeDtypeStruct((nb, D_SSD), F32),
        jax.ShapeDtypeStruct((nb, D_SSD), F32),
        jax.ShapeDtypeStruct((nb, 2 * NGROUPS * NSTATE), F32),
        jax.ShapeDtypeStruct((nb, D_SC), F32),
        jax.ShapeDtypeStruct((SSD_CONV - 1, nb, CONV_DIM), F32),
        jax.ShapeDtypeStruct((SC_CONV - 1, nb, D_SC), F32),
    ]
    return pl.pallas_call(
        _sample_pre_kernel,
        out_shape=outs,
        compiler_params=pltpu.CompilerParams(vmem_limit_bytes=VMEM_LIMIT),
        name="sample_pre",
    )(x, nw, win, cw, cb, dtb, a_pad, scw, expand, cst, scst)


SAMPLE_BT = 8


def _sample_state_kernel(st_ref, xdt_t_ref, dax_t_ref, bc_ref, nst_ref, y_ref):
    rows = NHEADS * HEAD_DIM
    half = rows // NGROUPS
    for b in range(SAMPLE_BT):
        hst = st_ref[b].reshape(rows, NSTATE)
        dcol = dax_t_ref[0, :, b:b + 1]
        xcol = xdt_t_ref[0, :, b:b + 1]
        brow = jnp.concatenate(
            [jnp.broadcast_to(bc_ref[b:b + 1, g * NSTATE:(g + 1) * NSTATE], (half, NSTATE))
             for g in range(NGROUPS)], axis=0)
        hnew = hst * dcol + xcol * brow
        nst_ref[b] = hnew.reshape(NHEADS, HEAD_DIM, NSTATE)
        cmat = jnp.concatenate(
            [bc_ref[b:b + 1, (NGROUPS + g) * NSTATE:(NGROUPS + g + 1) * NSTATE] for g in range(NGROUPS)]
            + [jnp.zeros((8 - NGROUPS, NSTATE), F32)], axis=0)
        yr = lax.dot_general(cmat.astype(BF16), hnew.astype(BF16), (((1,), (1,)), ((), ())),
                             preferred_element_type=F32)
        lane = lax.broadcasted_iota(jnp.int32, (1, rows), 1)
        y_ref[b:b + 1, :] = jnp.where(lane < half, yr[0:1, :], yr[1:2, :])


def _sample_state(state, xdt_t, dax_t, bc):
    nb = state.shape[0]
    rows = NHEADS * HEAD_DIM
    st_spec = pl.BlockSpec((SAMPLE_BT, NHEADS, HEAD_DIM, NSTATE), lambda i: (i, 0, 0, 0))
    col_spec = pl.BlockSpec((1, rows, SAMPLE_BT), lambda i: (i, 0, 0))
    return pl.pallas_call(
        _sample_state_kernel,
        grid=(nb // SAMPLE_BT,),
        in_specs=[st_spec, col_spec, col_spec,
                  pl.BlockSpec((SAMPLE_BT, 2 * NGROUPS * NSTATE), lambda i: (i, 0))],
        out_specs=[st_spec, pl.BlockSpec((SAMPLE_BT, rows), lambda i: (i, 0))],
        out_shape=[jax.ShapeDtypeStruct(state.shape, F32), jax.ShapeDtypeStruct((nb, rows), F32)],
        compiler_params=pltpu.CompilerParams(
            dimension_semantics=("arbitrary",), vmem_limit_bytes=VMEM_LIMIT),
        name="sample_state",
    )(state, xdt_t, dax_t, bc)


def _sample_post_kernel(x_ref, y_ref, xs_ref, z_ref, ysc_ref, dx_ref, gnw_ref, wout_ref,
                        nw_ref, wffn_ref, fcw_ref, fcb_ref, wdown_ref, nfw_ref, fst_ref,
                        out_ref, nfst_ref):
    y = (y_ref[...] + dx_ref[...] * xs_ref[...]) * _silu(z_ref[...])
    gw = D_SSD // NGROUPS
    parts = [_rms_rows(y[:, g * gw:(g + 1) * gw], gnw_ref[:, g * gw:(g + 1) * gw]) for g in range(NGROUPS)]
    ycat = jnp.concatenate(parts + [ysc_ref[...]], axis=1).astype(BF16)
    x1 = x_ref[...] + _dot(ycat, wout_ref[...])
    u = _rms_rows(x1, nw_ref[...]).astype(BF16)
    gate = _dot(u, wffn_ref[:, 0:D_FF])
    up = _dot(u, wffn_ref[:, D_FF:2 * D_FF])
    g = fcb_ref[...] + gate * fcw_ref[FFN_CONV - 1:FFN_CONV, :]
    for k in range(FFN_CONV - 1):
        g = g + fst_ref[k] * fcw_ref[k:k + 1, :]
    for k in range(FFN_CONV - 2):
        nfst_ref[k] = fst_ref[k + 1]
    nfst_ref[FFN_CONV - 2] = gate
    act = (_silu(g) * up).astype(BF16)
    x2 = x1 + _dot(act, wdown_ref[...])
    out_ref[...] = _rms_rows(x2, nfw_ref[...])


def _sample_post(x, y, xs, z, ysc, dx, gnw, wout, nw, wffn, fcw, fcb, wdown, nfw, fst):
    nb = x.shape[0]
    return pl.pallas_call(
        _sample_post_kernel,
        out_shape=[jax.ShapeDtypeStruct((nb, D_MODEL), F32),
                   jax.ShapeDtypeStruct((FFN_CONV - 1, nb, D_FF), F32)],
        compiler_params=pltpu.CompilerParams(vmem_limit_bytes=VMEM_LIMIT),
        name="sample_post",
    )(x, y, xs, z, ysc, dx, gnw, wout, nw, wffn, fcw, fcb, wdown, nfw, fst)


def _pack_w_in(w_in):
    o = 0
    z = w_in[:, o:o + D_SSD]; o += D_SSD
    xbc = w_in[:, o:o + CONV_DIM]; o += CONV_DIM
    dt = w_in[:, o:o + NHEADS]; o += NHEADS
    gb = w_in[:, o:o + D_SC]; o += D_SC
    gc = w_in[:, o:o + D_SC]; o += D_SC
    hh = w_in[:, o:o + D_SC]
    dt = jnp.pad(dt, ((0, 0), (0, DT_PAD - NHEADS)))
    return jnp.concatenate([z, xbc, gb, gc, hh, dt], axis=1).astype(BF16)


def kernel(x_prompt, x_sample, state_ssm, state_ssd_conv, state_short_conv, state_ffn_conv,
           norm_mix_w, w_in, ssd_conv_w, ssd_conv_b, ssd_dt_bias, ssd_a_log, ssd_d, ssd_norm_w,
           sc_conv_w, w_out, norm_ffn_w, w_ffn_in, ffn_conv_w, ffn_conv_b, w_down, norm_final_w):
    depth = w_in.shape[0]
    assert depth == 1
    win = _pack_w_in(w_in[0])
    wout = w_out[0].astype(BF16)
    wffn = w_ffn_in[0].astype(BF16)
    wdown = w_down[0].astype(BF16)
    nw = norm_mix_w[0].reshape(1, D_MODEL)
    nw2 = norm_ffn_w[0].reshape(1, D_MODEL)
    nfw = norm_final_w.reshape(1, D_MODEL)
    cw = ssd_conv_w[0]
    cb = ssd_conv_b[0].reshape(1, CONV_DIM)
    dtb = jnp.pad(ssd_dt_bias[0], (0, DT_PAD - NHEADS)).reshape(1, DT_PAD)
    a_pad = jnp.pad(-jnp.exp(ssd_a_log[0]), (0, DT_PAD - NHEADS)).reshape(1, DT_PAD)
    dx = jnp.repeat(ssd_d[0], HEAD_DIM).reshape(1, D_SSD)
    gnw = ssd_norm_w[0].reshape(1, D_SSD)
    scw = sc_conv_w[0]
    fcw = ffn_conv_w[0]
    fcb = ffn_conv_b[0].reshape(1, D_FF)
    head_of_lane = jnp.arange(D_SSD, dtype=jnp.int32) // HEAD_DIM
    expand = (jnp.arange(DT_PAD, dtype=jnp.int32)[:, None] == head_of_lane[None, :]).astype(F32)

    x1, p_ssm, p_cst, p_scst = _mixer_prompt(x_prompt, nw, win, cw, cb, dtb, a_pad, dx, gnw, scw, wout, expand)
    y_prompt, p_fst = _ffn_prompt(x1, nw2, wffn, fcw, fcb, wdown, nfw)

    nbs = x_sample.shape[0]
    xs_in = x_sample.reshape(nbs, D_MODEL)
    cst = jnp.swapaxes(state_ssd_conv[0], 0, 1)
    scst = jnp.swapaxes(state_short_conv[0], 0, 1)
    fst = jnp.swapaxes(state_ffn_conv[0], 0, 1)
    z, xs, xdt, dax, bc, ysc, ncst, nscst = _sample_pre(xs_in, nw, win, cw, cb, dtb, a_pad, scw, expand, cst, scst)
    ntile = nbs // SAMPLE_BT
    to_cols = lambda v: jnp.transpose(v.reshape(ntile, SAMPLE_BT, D_SSD), (0, 2, 1))
    s_ssm, y_s = _sample_state(state_ssm[0], to_cols(xdt), to_cols(dax), bc)
    y_sample, nfst = _sample_post(xs_in, y_s, xs, z, ysc, dx, gnw, wout, nw2, wffn, fcw, fcb, wdown, nfw, fst)

    return (y_prompt, y_sample.reshape(nbs, 1, D_MODEL),
            p_ssm[None], p_cst[None], p_scst[None], p_fst[None],
            s_ssm[None], jnp.swapaxes(ncst, 0, 1)[None], jnp.swapaxes(nscst, 0, 1)[None],
            jnp.swapaxes(nfst, 0, 1)[None])
```

```python
import jax
import jax.numpy as jnp
from jax import lax
from jax.experimental import pallas as pl
from jax.experimental.pallas import tpu as pltpu

D_MODEL = 1024
D_SSD = 1024
D_SC = 1024
NHEADS = 16
HEAD_DIM = 64
NGROUPS = 2
NSTATE = 128
SSD_CONV = 4
CONV_DIM = D_SSD + 2 * NGROUPS * NSTATE
SC_CONV = 3
D_FF = 2816
FFN_CONV = 3
EPS = 1e-5
D_MIX = D_SSD + D_SC
D_IN_PROJ = D_SSD + CONV_DIM + NHEADS + 3 * D_SC

LANES = 128
MXU_COLS = 256
DT_PAD = LANES
OFF_Z = 0
OFF_XBC = OFF_Z + D_SSD
OFF_GB = OFF_XBC + CONV_DIM
OFF_GC = OFF_GB + D_SC
OFF_H = OFF_GC + D_SC
OFF_DT = OFF_H + D_SC
D_IN_PACKED = OFF_DT + DT_PAD

CHUNK = 128
CARRY = 8
SEQ_TILE = 256
VMEM_LIMIT = 56 * 1024 * 1024

F32 = jnp.float32
BF16 = jnp.bfloat16


def _dot(a, b):
    return jnp.dot(a, b, preferred_element_type=F32)


def _split3(v):
    hi = v.astype(BF16)
    r = v - hi.astype(F32)
    mid = r.astype(BF16)
    lo = (r - mid.astype(F32)).astype(BF16)
    return hi, mid, lo


def _dot_sel_rhs(v, sel):
    hi, mid, lo = _split3(v)
    return (_dot(hi, sel) + _dot(mid, sel)) + _dot(lo, sel)


def _dot_sel_lhs(sel, v):
    hi, mid, lo = _split3(v)
    return (_dot(sel, hi) + _dot(sel, mid)) + _dot(sel, lo)


def _silu(v):
    return v * (1.0 / (1.0 + jnp.exp(-v)))


def _softplus(v):
    return jnp.maximum(v, 0.0) + jnp.log1p(jnp.exp(-jnp.abs(v)))


def _rms_rows(x, w):
    ms = jnp.mean(x * x, axis=-1, keepdims=True)
    return x * lax.rsqrt(ms + EPS) * w


def _pack_w_in_kernel(w_ref, o_ref):
    rows = w_ref.shape[0]
    dt0 = D_SSD + CONV_DIM
    o_ref[:, OFF_Z:OFF_GB] = w_ref[:, 0:dt0].astype(BF16)
    o_ref[:, OFF_GB:OFF_DT] = w_ref[:, dt0 + NHEADS:D_IN_PROJ].astype(BF16)
    o_ref[:, OFF_DT:OFF_DT + DT_PAD] = jnp.concatenate(
        [w_ref[:, dt0:dt0 + NHEADS], jnp.zeros((rows, DT_PAD - NHEADS), F32)], axis=1).astype(BF16)


def _pack_w_in(w_in):
    rows = 256
    return pl.pallas_call(
        _pack_w_in_kernel,
        grid=(D_MODEL // rows,),
        in_specs=[pl.BlockSpec((rows, D_IN_PROJ), lambda i: (i, 0))],
        out_specs=pl.BlockSpec((rows, D_IN_PACKED), lambda i: (i, 0)),
        out_shape=jax.ShapeDtypeStruct((D_MODEL, D_IN_PACKED), BF16),
        compiler_params=pltpu.CompilerParams(
            dimension_semantics=("arbitrary",), vmem_limit_bytes=VMEM_LIMIT),
        name="pack_w_in",
    )(w_in)


def _mixer_prompt_kernel(x_ref, nw_ref, win_ref, cw_ref, cb_ref, dtb_ref, a_ref, dx_ref,
                         gnw_ref, scw_ref, wout_ref, e_ref,
                         x1_ref, ssm_ref, cst_ref, scst_ref,
                         xbc_ext, sc_ext, u_ref, act_ref, dt_ref, z_ref, gb_ref, s_ref, y_ref, ycat_ref):
    tile = x_ref.shape[1]
    t = pl.program_id(1)

    @pl.when(t == 0)
    def _():
        xbc_ext[0:CARRY, :] = jnp.zeros((CARRY, CONV_DIM), F32)
        sc_ext[0:CARRY, :] = jnp.zeros((CARRY, D_SC), F32)
        s_ref[...] = jnp.zeros_like(s_ref)

    u_ref[...] = _rms_rows(x_ref[0], nw_ref[...]).astype(BF16)

    def project(dst, row0, base, off, width=MXU_COLS):
        dst[row0:row0 + tile, off:off + width] = _dot(u_ref[...], win_ref[:, base + off:base + off + width])

    def conv_block(off):
        conv = cb_ref[:, off:off + LANES] + (
            xbc_ext[CARRY - 3:CARRY - 3 + tile, off:off + LANES] * cw_ref[0:1, off:off + LANES])
        for k in range(1, SSD_CONV):
            conv = conv + (xbc_ext[CARRY - 3 + k:CARRY - 3 + k + tile, off:off + LANES]
                           * cw_ref[k:k + 1, off:off + LANES])
        act_ref[:, off:off + LANES] = _silu(conv)

    def sc_block(off):
        sc = sc_ext[CARRY - 2:CARRY - 2 + tile, off:off + LANES] * scw_ref[0:1, off:off + LANES]
        for k in range(1, SC_CONV):
            sc = sc + (sc_ext[CARRY - 2 + k:CARRY - 2 + k + tile, off:off + LANES]
                       * scw_ref[k:k + 1, off:off + LANES])
        ycat_ref[:, D_SSD + off:D_SSD + off + LANES] = (gb_ref[:, off:off + LANES] * sc).astype(BF16)

    for off in range(0, CONV_DIM, MXU_COLS):
        project(xbc_ext, CARRY, OFF_XBC, off)
        for sub in range(off, off + MXU_COLS, LANES):
            conv_block(sub)
    tail = xbc_ext[tile + CARRY - 3:tile + CARRY, :]
    cst_ref[0] = tail
    xbc_ext[CARRY - 3:CARRY, :] = tail
    project(dt_ref, 0, OFF_DT, 0, DT_PAD)
    for off in range(0, D_SC, MXU_COLS):
        u = u_ref[...]
        sc_ext[CARRY:CARRY + tile, off:off + MXU_COLS] = (
            _dot(u, win_ref[:, OFF_GC + off:OFF_GC + off + MXU_COLS])
            * _dot(u, win_ref[:, OFF_H + off:OFF_H + off + MXU_COLS]))

    fillers = []
    for k in range(D_SC // MXU_COLS):
        fillers.append(lambda k=k: project(gb_ref, 0, OFF_GB, k * MXU_COLS))
        fillers.append(lambda k=k: sc_block(k * MXU_COLS))
        fillers.append(lambda k=k: sc_block(k * MXU_COLS + LANES))
    for k in range(D_SSD // MXU_COLS):
        fillers.append(lambda k=k: project(z_ref, 0, OFF_Z, k * MXU_COLS))

    row_i = lax.broadcasted_iota(jnp.int32, (CHUNK, CHUNK), 0)
    col_i = lax.broadcasted_iota(jnp.int32, (CHUNK, CHUNK), 1)
    causal = row_i >= col_i
    tri = causal.astype(BF16)
    lane_i = lax.broadcasted_iota(jnp.int32, (1, LANES), 1)
    keeps = ((lane_i < HEAD_DIM).astype(BF16), (lane_i >= HEAD_DIM).astype(BF16))
    neg_inf = jnp.float32(-jnp.inf)
    pairs_per_group = NHEADS // NGROUPS // 2

    for c in range(tile // CHUNK):
        r0 = c * CHUNK
        dt = _softplus(dt_ref[r0:r0 + CHUNK, :] + dtb_ref[...])
        acum = _dot_sel_lhs(tri, dt * a_ref[...])
        acum_t = acum.T
        dt_t = dt.T
        ea = jnp.exp(acum)
        w_t = jnp.exp(acum_t[:, CHUNK - 1:CHUNK] - acum_t) * dt_t
        cd = jnp.broadcast_to(ea[CHUNK - 1:CHUNK, :], (8, LANES))
        cdx = _dot_sel_rhs(cd, e_ref[...])[0:1, :]
        for grp in range(NGROUPS):
            bg = act_ref[r0:r0 + CHUNK, D_SSD + grp * NSTATE:D_SSD + (grp + 1) * NSTATE]
            cg = act_ref[r0:r0 + CHUNK, D_SSD + (NGROUPS + grp) * NSTATE:D_SSD + (NGROUPS + grp + 1) * NSTATE]
            cb = lax.dot_general(cg.astype(BF16), bg.astype(BF16), (((1,), (1,)), ((), ())),
                                 preferred_element_type=F32)
            bg_t = bg.T
            for jp in range(pairs_per_group):
                j = grp * pairs_per_group + jp
                xp = act_ref[r0:r0 + CHUNK, j * LANES:(j + 1) * LANES].astype(BF16)
                sp = s_ref[:, j * LANES:(j + 1) * LANES]
                spb = sp.astype(BF16)
                lhs_parts, rhs_parts, bw_parts, x_parts = [], [], [], []
                for half, keep in enumerate(keeps):
                    h = 2 * j + half
                    xh = xp * keep
                    seg = acum[:, h:h + 1] - acum_t[h:h + 1, :]
                    decay = jnp.exp(jnp.where(causal, seg, neg_inf))
                    m = cb * decay * dt_t[h:h + 1, :]
                    eac = ea[:, h:h + 1] * cg
                    lhs_parts += [m.astype(BF16), eac.astype(BF16)]
                    rhs_parts += [xh, spb * keep]
                    bw_parts.append((bg_t * w_t[h:h + 1, :]).astype(BF16))
                    x_parts.append(xh)
                y_ref[r0:r0 + CHUNK, j * LANES:(j + 1) * LANES] = _dot(
                    jnp.concatenate(lhs_parts, axis=1), jnp.concatenate(rhs_parts, axis=0))
                snew = _dot(jnp.concatenate(bw_parts, axis=1), jnp.concatenate(x_parts, axis=0))
                s_ref[:, j * LANES:(j + 1) * LANES] = cdx[:, j * LANES:(j + 1) * LANES] * sp + snew
                if fillers:
                    fillers.pop(0)()
    while fillers:
        fillers.pop(0)()
    sc_tail = sc_ext[tile + CARRY - 2:tile + CARRY, :]
    scst_ref[0] = sc_tail
    sc_ext[CARRY - 2:CARRY, :] = sc_tail

    gw = D_SSD // NGROUPS
    for grp in range(NGROUPS):
        ssq = jnp.zeros((tile, LANES), F32)
        for off in range(grp * gw, (grp + 1) * gw, LANES):
            yb = ((y_ref[:, off:off + LANES] + dx_ref[:, off:off + LANES] * act_ref[:, off:off + LANES])
                  * _silu(z_ref[:, off:off + LANES]))
            y_ref[:, off:off + LANES] = yb
            ssq = ssq + yb * yb
        scale = lax.rsqrt(jnp.sum(ssq, axis=-1, keepdims=True) * (1.0 / gw) + EPS)
        for off in range(grp * gw, (grp + 1) * gw, LANES):
            ycat_ref[:, off:off + LANES] = (y_ref[:, off:off + LANES] * scale
                                            * gnw_ref[:, off:off + LANES]).astype(BF16)

    for off in range(0, D_MODEL, MXU_COLS):
        x1_ref[0, :, off:off + MXU_COLS] = x_ref[0, :, off:off + MXU_COLS] + _dot(
            ycat_ref[...], wout_ref[:, off:off + MXU_COLS])

    @pl.when(t == pl.num_programs(1) - 1)
    def _():
        for j in range(NHEADS // 2):
            ssm_ref[0, 2 * j:2 * j + 2] = s_ref[:, j * LANES:(j + 1) * LANES].T.reshape(2, HEAD_DIM, NSTATE)


def _const_spec(shape):
    nd = len(shape)
    return pl.BlockSpec(shape, lambda *_: (0,) * nd, pipeline_mode=pl.Buffered(1))


def _mixer_prompt(x, nw, win, cw, cb, dtb, a_pad, dx, gnw, scw, wout, expand):
    nb, seq, _ = x.shape
    tile = SEQ_TILE
    tok_spec = pl.BlockSpec((1, tile, D_MODEL), lambda b, t: (b, t, 0))
    consts = (nw, win, cw, cb, dtb, a_pad, dx, gnw, scw, wout, expand)
    return pl.pallas_call(
        _mixer_prompt_kernel,
        grid=(nb, seq // tile),
        in_specs=[tok_spec] + [_const_spec(p.shape) for p in consts],
        out_specs=[
            tok_spec,
            pl.BlockSpec((1, NHEADS, HEAD_DIM, NSTATE), lambda b, t: (b, 0, 0, 0)),
            pl.BlockSpec((1, SSD_CONV - 1, CONV_DIM), lambda b, t: (b, 0, 0)),
            pl.BlockSpec((1, SC_CONV - 1, D_SC), lambda b, t: (b, 0, 0)),
        ],
        out_shape=[
            jax.ShapeDtypeStruct((nb, seq, D_MODEL), F32),
            jax.ShapeDtypeStruct((nb, NHEADS, HEAD_DIM, NSTATE), F32),
            jax.ShapeDtypeStruct((nb, SSD_CONV - 1, CONV_DIM), F32),
            jax.ShapeDtypeStruct((nb, SC_CONV - 1, D_SC), F32),
        ],
        scratch_shapes=[
            pltpu.VMEM((tile + CARRY, CONV_DIM), F32),
            pltpu.VMEM((tile + CARRY, D_SC), F32),
            pltpu.VMEM((tile, D_MODEL), BF16),
            pltpu.VMEM((tile, CONV_DIM), F32),
            pltpu.VMEM((tile, DT_PAD), F32),
            pltpu.VMEM((tile, D_SSD), F32),
            pltpu.VMEM((tile, D_SC), F32),
            pltpu.VMEM((NSTATE, D_SSD), F32),
            pltpu.VMEM((tile, D_SSD), F32),
            pltpu.VMEM((tile, D_MIX), BF16),
        ],
        compiler_params=pltpu.CompilerParams(
            dimension_semantics=("arbitrary", "arbitrary"), vmem_limit_bytes=VMEM_LIMIT),
        name="mixer_prompt",
    )(x, *consts)


def _ffn_prompt_kernel(x1_ref, nw_ref, wffn_ref, fcw_ref, fcb_ref, wdown_ref, nfw_ref,
                       y_ref, fst_ref, gate_ext):
    tile = x1_ref.shape[1]
    t = pl.program_id(1)

    @pl.when(t == 0)
    def _():
        gate_ext[0:CARRY, :] = jnp.zeros((CARRY, D_FF), F32)

    x1 = x1_ref[0]
    u = _rms_rows(x1, nw_ref[...]).astype(BF16)
    gate_ext[CARRY:CARRY + tile, :] = _dot(u, wffn_ref[:, 0:D_FF])
    up = _dot(u, wffn_ref[:, D_FF:2 * D_FF])
    g = fcb_ref[...] + gate_ext[CARRY - 2:CARRY - 2 + tile, :] * fcw_ref[0:1, :]
    for k in range(1, FFN_CONV):
        g = g + gate_ext[CARRY - 2 + k:CARRY - 2 + k + tile, :] * fcw_ref[k:k + 1, :]
    tail = gate_ext[tile + CARRY - 2:tile + CARRY, :]
    fst_ref[0] = tail
    gate_ext[CARRY - 2:CARRY, :] = tail
    act = (_silu(g) * up).astype(BF16)
    x2 = x1 + _dot(act, wdown_ref[...])
    y_ref[0] = _rms_rows(x2, nfw_ref[...])


def _ffn_prompt(x1, nw, wffn, fcw, fcb, wdown, nfw):
    nb, seq, _ = x1.shape
    tile = SEQ_TILE
    tok_spec = pl.BlockSpec((1, tile, D_MODEL), lambda b, t: (b, t, 0))
    return pl.pallas_call(
        _ffn_prompt_kernel,
        grid=(nb, seq // tile),
        in_specs=[tok_spec] + [_const_spec(p.shape) for p in (nw, wffn, fcw, fcb, wdown, nfw)],
        out_specs=[tok_spec, pl.BlockSpec((1, FFN_CONV - 1, D_FF), lambda b, t: (b, 0, 0))],
        out_shape=[jax.ShapeDtypeStruct((nb, seq, D_MODEL), F32),
                   jax.ShapeDtypeStruct((nb, FFN_CONV - 1, D_FF), F32)],
        scratch_shapes=[pltpu.VMEM((tile + CARRY, D_FF), F32)],
        compiler_params=pltpu.CompilerParams(
            dimension_semantics=("arbitrary", "arbitrary"), vmem_limit_bytes=VMEM_LIMIT),
        name="ffn_prompt",
    )(x1, nw, wffn, fcw, fcb, wdown, nfw)


def _sample_pre_kernel(x_ref, nw_ref, win_ref, cw_ref, cb_ref, dtb_ref, a_ref, scw_ref, e_ref,
                       cst_ref, scst_ref,
                       z_ref, xs_ref, xdt_ref, dax_ref, bc_ref, ysc_ref, ncst_ref, nscst_ref):
    x = x_ref[...]
    u = _rms_rows(x, nw_ref[...]).astype(BF16)
    z_ref[...] = _dot(u, win_ref[:, OFF_Z:OFF_Z + D_SSD])
    xbc = _dot(u, win_ref[:, OFF_XBC:OFF_XBC + CONV_DIM])
    conv = cb_ref[...] + xbc * cw_ref[SSD_CONV - 1:SSD_CONV, :]
    for k in range(SSD_CONV - 1):
        conv = conv + cst_ref[k] * cw_ref[k:k + 1, :]
    for k in range(SSD_CONV - 2):
        ncst_ref[k] = cst_ref[k + 1]
    ncst_ref[SSD_CONV - 2] = xbc
    act = _silu(conv)
    xs = act[:, 0:D_SSD]
    xs_ref[...] = xs
    bc_ref[...] = act[:, D_SSD:CONV_DIM]
    dt = _softplus(_dot(u, win_ref[:, OFF_DT:OFF_DT + DT_PAD]) + dtb_ref[...])
    xdt_ref[...] = xs * _dot_sel_rhs(dt, e_ref[...])
    dax_ref[...] = jnp.exp(_dot_sel_rhs(dt * a_ref[...], e_ref[...]))
    gc = _dot(u, win_ref[:, OFF_GC:OFF_GC + D_SC])
    hh = _dot(u, win_ref[:, OFF_H:OFF_H + D_SC])
    gch = gc * hh
    sc = gch * scw_ref[SC_CONV - 1:SC_CONV, :]
    for k in range(SC_CONV - 1):
        sc = sc + scst_ref[k] * scw_ref[k:k + 1, :]
    for k in range(SC_CONV - 2):
        nscst_ref[k] = scst_ref[k + 1]
    nscst_ref[SC_CONV - 2] = gch
    gb = _dot(u, win_ref[:, OFF_GB:OFF_GB + D_SC])
    ysc_ref[...] = gb * sc


def _sample_pre(x, nw, win, cw, cb, dtb, a_pad, scw, expand, cst, scst):
    nb = x.shape[0]
    outs = [
        jax.ShapeDtypeStruct((nb, D_SSD), F32),
        jax.ShapeDtypeStruct((nb, D_SSD), F32),
        jax.ShapeDtypeStruct((nb, D_SSD), F32),
        jax.ShapeDtypeStruct((nb, D_SSD), F32),
        jax.ShapeDtypeStruct((nb, 2 * NGROUPS * NSTATE), F32),
        jax.ShapeDtypeStruct((nb, D_SC), F32),
        jax.ShapeDtypeStruct((SSD_CONV - 1, nb, CONV_DIM), F32),
        jax.ShapeDtypeStruct((SC_CONV - 1, nb, D_SC), F32),
    ]
    return pl.pallas_call(
        _sample_pre_kernel,
        out_shape=outs,
        compiler_params=pltpu.CompilerParams(vmem_limit_bytes=VMEM_LIMIT),
        name="sample_pre",
    )(x, nw, win, cw, cb, dtb, a_pad, scw, expand, cst, scst)


SAMPLE_BT = 8


def _sample_state_kernel(st_ref, xdt_t_ref, dax_t_ref, bc_ref, nst_ref, y_ref):
    rows = NHEADS * HEAD_DIM
    half = rows // NGROUPS
    for b in range(SAMPLE_BT):
        hst = st_ref[b].reshape(rows, NSTATE)
        dcol = dax_t_ref[0, :, b:b + 1]
        xcol = xdt_t_ref[0, :, b:b + 1]
        brow = jnp.concatenate(
            [jnp.broadcast_to(bc_ref[b:b + 1, g * NSTATE:(g + 1) * NSTATE], (half, NSTATE))
             for g in range(NGROUPS)], axis=0)
        hnew = hst * dcol + xcol * brow
        nst_ref[b] = hnew.reshape(NHEADS, HEAD_DIM, NSTATE)
        cmat = jnp.concatenate(
            [bc_ref[b:b + 1, (NGROUPS + g) * NSTATE:(NGROUPS + g + 1) * NSTATE] for g in range(NGROUPS)]
            + [jnp.zeros((8 - NGROUPS, NSTATE), F32)], axis=0)
        yr = lax.dot_general(cmat.astype(BF16), hnew.astype(BF16), (((1,), (1,)), ((), ())),
                             preferred_element_type=F32)
        lane = lax.broadcasted_iota(jnp.int32, (1, rows), 1)
        y_ref[b:b + 1, :] = jnp.where(lane < half, yr[0:1, :], yr[1:2, :])


def _sample_state(state, xdt_t, dax_t, bc):
    nb = state.shape[0]
    rows = NHEADS * HEAD_DIM
    st_spec = pl.BlockSpec((SAMPLE_BT, NHEADS, HEAD_DIM, NSTATE), lambda i: (i, 0, 0, 0))
    col_spec = pl.BlockSpec((1, rows, SAMPLE_BT), lambda i: (i, 0, 0))
    return pl.pallas_call(
        _sample_state_kernel,
        grid=(nb // SAMPLE_BT,),
        in_specs=[st_spec, col_spec, col_spec,
                  pl.BlockSpec((SAMPLE_BT, 2 * NGROUPS * NSTATE), lambda i: (i, 0))],
        out_specs=[st_spec, pl.BlockSpec((SAMPLE_BT, rows), lambda i: (i, 0))],
        out_shape=[jax.ShapeDtypeStruct(state.shape, F32), jax.ShapeDtypeStruct((nb, rows), F32)],
        compiler_params=pltpu.CompilerParams(
            dimension_semantics=("arbitrary",), vmem_limit_bytes=VMEM_LIMIT),
        name="sample_state",
    )(state, xdt_t, dax_t, bc)


def _sample_post_kernel(x_ref, y_ref, xs_ref, z_ref, ysc_ref, dx_ref, gnw_ref, wout_ref,
                        nw_ref, wffn_ref, fcw_ref, fcb_ref, wdown_ref, nfw_ref, fst_ref,
                        out_ref, nfst_ref):
    y = (y_ref[...] + dx_ref[...] * xs_ref[...]) * _silu(z_ref[...])
    gw = D_SSD // NGROUPS
    parts = [_rms_rows(y[:, g * gw:(g + 1) * gw], gnw_ref[:, g * gw:(g + 1) * gw]) for g in range(NGROUPS)]
    ycat = jnp.concatenate(parts + [ysc_ref[...]], axis=1).astype(BF16)
    x1 = x_ref[...] + _dot(ycat, wout_ref[...])
    u = _rms_rows(x1, nw_ref[...]).astype(BF16)
    gate = _dot(u, wffn_ref[:, 0:D_FF])
    up = _dot(u, wffn_ref[:, D_FF:2 * D_FF])
    g = fcb_ref[...] + gate * fcw_ref[FFN_CONV - 1:FFN_CONV, :]
    for k in range(FFN_CONV - 1):
        g = g + fst_ref[k] * fcw_ref[k:k + 1, :]
    for k in range(FFN_CONV - 2):
        nfst_ref[k] = fst_ref[k + 1]
    nfst_ref[FFN_CONV - 2] = gate
    act = (_silu(g) * up).astype(BF16)
    x2 = x1 + _dot(act, wdown_ref[...])
    out_ref[...] = _rms_rows(x2, nfw_ref[...])


def _sample_post(x, y, xs, z, ysc, dx, gnw, wout, nw, wffn, fcw, fcb, wdown, nfw, fst):
    nb = x.shape[0]
    return pl.pallas_call(
        _sample_post_kernel,
        out_shape=[jax.ShapeDtypeStruct((nb, D_MODEL), F32),
                   jax.ShapeDtypeStruct((FFN_CONV - 1, nb, D_FF), F32)],
        compiler_params=pltpu.CompilerParams(vmem_limit_bytes=VMEM_LIMIT),
        name="sample_post",
    )(x, y, xs, z, ysc, dx, gnw, wout, nw, wffn, fcw, fcb, wdown, nfw, fst)


def kernel(x_prompt, x_sample, state_ssm, state_ssd_conv, state_short_conv, state_ffn_conv,
           norm_mix_w, w_in, ssd_conv_w, ssd_conv_b, ssd_dt_bias, ssd_a_log, ssd_d, ssd_norm_w,
           sc_conv_w, w_out, norm_ffn_w, w_ffn_in, ffn_conv_w, ffn_conv_b, w_down, norm_final_w):
    depth = w_in.shape[0]
    assert depth == 1
    win = _pack_w_in(w_in[0])
    wout = w_out[0].astype(BF16)
    wffn = w_ffn_in[0].astype(BF16)
    wdown = w_down[0].astype(BF16)
    nw = norm_mix_w[0].reshape(1, D_MODEL)
    nw2 = norm_ffn_w[0].reshape(1, D_MODEL)
    nfw = norm_final_w.reshape(1, D_MODEL)
    cw = ssd_conv_w[0]
    cb = ssd_conv_b[0].reshape(1, CONV_DIM)
    dtb = jnp.pad(ssd_dt_bias[0], (0, DT_PAD - NHEADS)).reshape(1, DT_PAD)
    a_pad = jnp.pad(-jnp.exp(ssd_a_log[0]), (0, DT_PAD - NHEADS)).reshape(1, DT_PAD)
    dx = jnp.repeat(ssd_d[0], HEAD_DIM).reshape(1, D_SSD)
    gnw = ssd_norm_w[0].reshape(1, D_SSD)
    scw = sc_conv_w[0]
    fcw = ffn_conv_w[0]
    fcb = ffn_conv_b[0].reshape(1, D_FF)
    head_of_lane = jnp.arange(D_SSD, dtype=jnp.int32) // HEAD_DIM
    expand = (jnp.arange(DT_PAD, dtype=jnp.int32)[:, None] == head_of_lane[None, :]).astype(BF16)

    x1, p_ssm, p_cst, p_scst = _mixer_prompt(x_prompt, nw, win, cw, cb, dtb, a_pad, dx, gnw, scw, wout, expand)
    y_prompt, p_fst = _ffn_prompt(x1, nw2, wffn, fcw, fcb, wdown, nfw)

    nbs = x_sample.shape[0]
    xs_in = x_sample.reshape(nbs, D_MODEL)
    cst = jnp.swapaxes(state_ssd_conv[0], 0, 1)
    scst = jnp.swapaxes(state_short_conv[0], 0, 1)
    fst = jnp.swapaxes(state_ffn_conv[0], 0, 1)
    z, xs, xdt, dax, bc, ysc, ncst, nscst = _sample_pre(xs_in, nw, win, cw, cb, dtb, a_pad, scw, expand, cst, scst)
    ntile = nbs // SAMPLE_BT
    to_cols = lambda v: jnp.transpose(v.reshape(ntile, SAMPLE_BT, D_SSD), (0, 2, 1))
    s_ssm, y_s = _sample_state(state_ssm[0], to_cols(xdt), to_cols(dax), bc)
    y_sample, nfst = _sample_post(xs_in, y_s, xs, z, ysc, dx, gnw, wout, nw2, wffn, fcw, fcb, wdown, nfw, fst)

    return (y_prompt, y_sample.reshape(nbs, 1, D_MODEL),
            p_ssm[None], p_cst[None], p_scst[None], p_fst[None],
            s_ssm[None], jnp.swapaxes(ncst, 0, 1)[None], jnp.swapaxes(nscst, 0, 1)[None],
            jnp.swapaxes(nfst, 0, 1)[None])
```

```python
import jax
import jax.numpy as jnp
from jax import lax
from jax.experimental import pallas as pl
from jax.experimental.pallas import tpu as pltpu

D_MODEL = 1024
D_SSD = 1024
D_SC = 1024
NHEADS = 16
HEAD_DIM = 64
NGROUPS = 2
NSTATE = 128
SSD_CONV = 4
CONV_DIM = D_SSD + 2 * NGROUPS * NSTATE
SC_CONV = 3
D_FF = 2816
FFN_CONV = 3
EPS = 1e-5
D_MIX = D_SSD + D_SC
D_IN_PROJ = D_SSD + CONV_DIM + NHEADS + 3 * D_SC

LANES = 128
MXU_COLS = 256
OFF_Z = 0
OFF_XBC = OFF_Z + D_SSD
OFF_DT = OFF_XBC + CONV_DIM
OFF_GB = OFF_DT + NHEADS
OFF_GC = OFF_GB + D_SC
OFF_H = OFF_GC + D_SC
DT_PAD = LANES

CHUNK = 128
CARRY = 8
SEQ_TILE = 256
VMEM_LIMIT = 56 * 1024 * 1024

F32 = jnp.float32
BF16 = jnp.bfloat16


def _dot(a, b):
    return jnp.dot(a, b, preferred_element_type=F32)


def _dot_nt(a, b_t):
    return lax.dot_general(a, b_t, (((1,), (1,)), ((), ())), preferred_element_type=F32)


def _split3(v):
    hi = v.astype(BF16)
    r = v - hi.astype(F32)
    mid = r.astype(BF16)
    lo = (r - mid.astype(F32)).astype(BF16)
    return hi, mid, lo


def _dot_sel_rhs(v, sel):
    hi, mid, lo = _split3(v)
    return (_dot(hi, sel) + _dot(mid, sel)) + _dot(lo, sel)


def _dot_sel_lhs(sel, v):
    hi, mid, lo = _split3(v)
    return (_dot(sel, hi) + _dot(sel, mid)) + _dot(sel, lo)


def _silu(v):
    return v * (1.0 / (1.0 + jnp.exp(-v)))


def _softplus(v):
    return jnp.maximum(v, 0.0) + jnp.log1p(jnp.exp(-jnp.abs(v)))


def _rms_rows(x, w):
    ms = jnp.mean(x * x, axis=-1, keepdims=True)
    return x * lax.rsqrt(ms + EPS) * w


def _mixer_prompt_kernel(x_ref, nw_ref, win_ref, cw_ref, cb_ref, dtb_ref, a_ref, dx_ref,
                         gnw_ref, scw_ref, wout_ref, e_ref,
                         x1_ref, ssm_ref, cst_ref, scst_ref,
                         xbc_ext, sc_ext, u_ref, act_ref, dt_ref, z_ref, gb_ref, s_ref, y_ref, ycat_ref):
    tile = x_ref.shape[1]
    t = pl.program_id(1)

    @pl.when(t == 0)
    def _():
        xbc_ext[0:CARRY, :] = jnp.zeros((CARRY, CONV_DIM), F32)
        sc_ext[0:CARRY, :] = jnp.zeros((CARRY, D_SC), F32)
        s_ref[...] = jnp.zeros_like(s_ref)

    u_ref[...] = _rms_rows(x_ref[0], nw_ref[...]).astype(BF16)

    def project(dst, row0, base, off, width=MXU_COLS):
        dst[row0:row0 + tile, off:off + width] = _dot_nt(u_ref[...], win_ref[base + off:base + off + width, :])

    def conv_block(off):
        conv = cb_ref[:, off:off + LANES] + (
            xbc_ext[CARRY - 3:CARRY - 3 + tile, off:off + LANES] * cw_ref[0:1, off:off + LANES])
        for k in range(1, SSD_CONV):
            conv = conv + (xbc_ext[CARRY - 3 + k:CARRY - 3 + k + tile, off:off + LANES]
                           * cw_ref[k:k + 1, off:off + LANES])
        act_ref[:, off:off + LANES] = _silu(conv)

    def sc_block(off):
        sc = sc_ext[CARRY - 2:CARRY - 2 + tile, off:off + LANES] * scw_ref[0:1, off:off + LANES]
        for k in range(1, SC_CONV):
            sc = sc + (sc_ext[CARRY - 2 + k:CARRY - 2 + k + tile, off:off + LANES]
                       * scw_ref[k:k + 1, off:off + LANES])
        ycat_ref[:, D_SSD + off:D_SSD + off + LANES] = (gb_ref[:, off:off + LANES] * sc).astype(BF16)

    for off in range(0, CONV_DIM, MXU_COLS):
        project(xbc_ext, CARRY, OFF_XBC, off)
        for sub in range(off, off + MXU_COLS, LANES):
            conv_block(sub)
    tail = xbc_ext[tile + CARRY - 3:tile + CARRY, :]
    cst_ref[0] = tail
    xbc_ext[CARRY - 3:CARRY, :] = tail
    project(dt_ref, 0, OFF_DT, 0, DT_PAD)
    for off in range(0, D_SC, MXU_COLS):
        u = u_ref[...]
        sc_ext[CARRY:CARRY + tile, off:off + MXU_COLS] = (
            _dot_nt(u, win_ref[OFF_GC + off:OFF_GC + off + MXU_COLS, :])
            * _dot_nt(u, win_ref[OFF_H + off:OFF_H + off + MXU_COLS, :]))

    fillers = []
    for k in range(D_SC // MXU_COLS):
        fillers.append(lambda k=k: project(gb_ref, 0, OFF_GB, k * MXU_COLS))
        fillers.append(lambda k=k: sc_block(k * MXU_COLS))
        fillers.append(lambda k=k: sc_block(k * MXU_COLS + LANES))
    for k in range(D_SSD // MXU_COLS):
        fillers.append(lambda k=k: project(z_ref, 0, OFF_Z, k * MXU_COLS))

    row_i = lax.broadcasted_iota(jnp.int32, (CHUNK, CHUNK), 0)
    col_i = lax.broadcasted_iota(jnp.int32, (CHUNK, CHUNK), 1)
    causal = row_i >= col_i
    tri = causal.astype(BF16)
    lane_i = lax.broadcasted_iota(jnp.int32, (1, LANES), 1)
    keeps = ((lane_i < HEAD_DIM).astype(BF16), (lane_i >= HEAD_DIM).astype(BF16))
    neg_inf = jnp.float32(-jnp.inf)
    pairs_per_group = NHEADS // NGROUPS // 2

    for c in range(tile // CHUNK):
        r0 = c * CHUNK
        dt = _softplus(dt_ref[r0:r0 + CHUNK, :] + dtb_ref[...])
        acum = _dot_sel_lhs(tri, dt * a_ref[...])
        acum_t = acum.T
        dt_t = dt.T
        ea = jnp.exp(acum)
        w_t = jnp.exp(acum_t[:, CHUNK - 1:CHUNK] - acum_t) * dt_t
        cd = jnp.broadcast_to(ea[CHUNK - 1:CHUNK, :], (8, LANES))
        cdx = _dot_sel_rhs(cd, e_ref[...])[0:1, :]
        for grp in range(NGROUPS):
            bg = act_ref[r0:r0 + CHUNK, D_SSD + grp * NSTATE:D_SSD + (grp + 1) * NSTATE]
            cg = act_ref[r0:r0 + CHUNK, D_SSD + (NGROUPS + grp) * NSTATE:D_SSD + (NGROUPS + grp + 1) * NSTATE]
            cb = lax.dot_general(cg.astype(BF16), bg.astype(BF16), (((1,), (1,)), ((), ())),
                                 preferred_element_type=F32)
            bg_t = bg.T
            for jp in range(pairs_per_group):
                j = grp * pairs_per_group + jp
                xp = act_ref[r0:r0 + CHUNK, j * LANES:(j + 1) * LANES].astype(BF16)
                sp = s_ref[:, j * LANES:(j + 1) * LANES]
                spb = sp.astype(BF16)
                lhs_parts, rhs_parts, bw_parts, x_parts = [], [], [], []
                for half, keep in enumerate(keeps):
                    h = 2 * j + half
                    xh = xp * keep
                    seg = acum[:, h:h + 1] - acum_t[h:h + 1, :]
                    decay = jnp.exp(jnp.where(causal, seg, neg_inf))
                    m = cb * decay * dt_t[h:h + 1, :]
                    eac = ea[:, h:h + 1] * cg
                    lhs_parts += [m.astype(BF16), eac.astype(BF16)]
                    rhs_parts += [xh, spb * keep]
                    bw_parts.append((bg_t * w_t[h:h + 1, :]).astype(BF16))
                    x_parts.append(xh)
                y_ref[r0:r0 + CHUNK, j * LANES:(j + 1) * LANES] = _dot(
                    jnp.concatenate(lhs_parts, axis=1), jnp.concatenate(rhs_parts, axis=0))
                snew = _dot(jnp.concatenate(bw_parts, axis=1), jnp.concatenate(x_parts, axis=0))
                s_ref[:, j * LANES:(j + 1) * LANES] = cdx[:, j * LANES:(j + 1) * LANES] * sp + snew
                if fillers:
                    fillers.pop(0)()
    while fillers:
        fillers.pop(0)()
    sc_tail = sc_ext[tile + CARRY - 2:tile + CARRY, :]
    scst_ref[0] = sc_tail
    sc_ext[CARRY - 2:CARRY, :] = sc_tail

    gw = D_SSD // NGROUPS
    for grp in range(NGROUPS):
        ssq = jnp.zeros((tile, LANES), F32)
        for off in range(grp * gw, (grp + 1) * gw, LANES):
            yb = ((y_ref[:, off:off + LANES] + dx_ref[:, off:off + LANES] * act_ref[:, off:off + LANES])
                  * _silu(z_ref[:, off:off + LANES]))
            y_ref[:, off:off + LANES] = yb
            ssq = ssq + yb * yb
        scale = lax.rsqrt(jnp.sum(ssq, axis=-1, keepdims=True) * (1.0 / gw) + EPS)
        for off in range(grp * gw, (grp + 1) * gw, LANES):
            ycat_ref[:, off:off + LANES] = (y_ref[:, off:off + LANES] * scale
                                            * gnw_ref[:, off:off + LANES]).astype(BF16)

    for off in range(0, D_MODEL, MXU_COLS):
        x1_ref[0, :, off:off + MXU_COLS] = x_ref[0, :, off:off + MXU_COLS] + _dot(
            ycat_ref[...], wout_ref[:, off:off + MXU_COLS])

    @pl.when(t == pl.num_programs(1) - 1)
    def _():
        for j in range(NHEADS // 2):
            ssm_ref[0, 2 * j:2 * j + 2] = s_ref[:, j * LANES:(j + 1) * LANES].T.reshape(2, HEAD_DIM, NSTATE)


def _const_spec(shape):
    nd = len(shape)
    return pl.BlockSpec(shape, lambda *_: (0,) * nd, pipeline_mode=pl.Buffered(1))


def _mixer_prompt(x, nw, win, cw, cb, dtb, a_pad, dx, gnw, scw, wout, expand):
    nb, seq, _ = x.shape
    tile = SEQ_TILE
    tok_spec = pl.BlockSpec((1, tile, D_MODEL), lambda b, t: (b, t, 0))
    consts = (nw, win, cw, cb, dtb, a_pad, dx, gnw, scw, wout, expand)
    return pl.pallas_call(
        _mixer_prompt_kernel,
        grid=(nb, seq // tile),
        in_specs=[tok_spec] + [_const_spec(p.shape) for p in consts],
        out_specs=[
            tok_spec,
            pl.BlockSpec((1, NHEADS, HEAD_DIM, NSTATE), lambda b, t: (b, 0, 0, 0)),
            pl.BlockSpec((1, SSD_CONV - 1, CONV_DIM), lambda b, t: (b, 0, 0)),
            pl.BlockSpec((1, SC_CONV - 1, D_SC), lambda b, t: (b, 0, 0)),
        ],
        out_shape=[
            jax.ShapeDtypeStruct((nb, seq, D_MODEL), F32),
            jax.ShapeDtypeStruct((nb, NHEADS, HEAD_DIM, NSTATE), F32),
            jax.ShapeDtypeStruct((nb, SSD_CONV - 1, CONV_DIM), F32),
            jax.ShapeDtypeStruct((nb, SC_CONV - 1, D_SC), F32),
        ],
        scratch_shapes=[
            pltpu.VMEM((tile + CARRY, CONV_DIM), F32),
            pltpu.VMEM((tile + CARRY, D_SC), F32),
            pltpu.VMEM((tile, D_MODEL), BF16),
            pltpu.VMEM((tile, CONV_DIM), F32),
            pltpu.VMEM((tile, DT_PAD), F32),
            pltpu.VMEM((tile, D_SSD), F32),
            pltpu.VMEM((tile, D_SC), F32),
            pltpu.VMEM((NSTATE, D_SSD), F32),
            pltpu.VMEM((tile, D_SSD), F32),
            pltpu.VMEM((tile, D_MIX), BF16),
        ],
        compiler_params=pltpu.CompilerParams(
            dimension_semantics=("arbitrary", "arbitrary"), vmem_limit_bytes=VMEM_LIMIT),
        name="mixer_prompt",
    )(x, *consts)


def _ffn_prompt_kernel(x1_ref, nw_ref, wffn_ref, fcw_ref, fcb_ref, wdown_ref, nfw_ref,
                       y_ref, fst_ref, gate_ext):
    tile = x1_ref.shape[1]
    t = pl.program_id(1)

    @pl.when(t == 0)
    def _():
        gate_ext[0:CARRY, :] = jnp.zeros((CARRY, D_FF), F32)

    x1 = x1_ref[0]
    u = _rms_rows(x1, nw_ref[...]).astype(BF16)
    gate_ext[CARRY:CARRY + tile, :] = _dot(u, wffn_ref[:, 0:D_FF])
    up = _dot(u, wffn_ref[:, D_FF:2 * D_FF])
    g = fcb_ref[...] + gate_ext[CARRY - 2:CARRY - 2 + tile, :] * fcw_ref[0:1, :]
    for k in range(1, FFN_CONV):
        g = g + gate_ext[CARRY - 2 + k:CARRY - 2 + k + tile, :] * fcw_ref[k:k + 1, :]
    tail = gate_ext[tile + CARRY - 2:tile + CARRY, :]
    fst_ref[0] = tail
    gate_ext[CARRY - 2:CARRY, :] = tail
    act = (_silu(g) * up).astype(BF16)
    x2 = x1 + _dot(act, wdown_ref[...])
    y_ref[0] = _rms_rows(x2, nfw_ref[...])


def _ffn_prompt(x1, nw, wffn, fcw, fcb, wdown, nfw):
    nb, seq, _ = x1.shape
    tile = SEQ_TILE
    tok_spec = pl.BlockSpec((1, tile, D_MODEL), lambda b, t: (b, t, 0))
    return pl.pallas_call(
        _ffn_prompt_kernel,
        grid=(nb, seq // tile),
        in_specs=[tok_spec] + [_const_spec(p.shape) for p in (nw, wffn, fcw, fcb, wdown, nfw)],
        out_specs=[tok_spec, pl.BlockSpec((1, FFN_CONV - 1, D_FF), lambda b, t: (b, 0, 0))],
        out_shape=[jax.ShapeDtypeStruct((nb, seq, D_MODEL), F32),
                   jax.ShapeDtypeStruct((nb, FFN_CONV - 1, D_FF), F32)],
        scratch_shapes=[pltpu.VMEM((tile + CARRY, D_FF), F32)],
        compiler_params=pltpu.CompilerParams(
            dimension_semantics=("arbitrary", "arbitrary"), vmem_limit_bytes=VMEM_LIMIT),
        name="ffn_prompt",
    )(x1, nw, wffn, fcw, fcb, wdown, nfw)


def _sample_pre_kernel(x_ref, nw_ref, win_ref, cw_ref, cb_ref, dtb_ref, a_ref, scw_ref, e_ref,
                       cst_ref, scst_ref,
                       z_ref, xs_ref, xdt_ref, dax_ref, bc_ref, ysc_ref, ncst_ref, nscst_ref):
    x = x_ref[...]
    u = _rms_rows(x, nw_ref[...]).astype(BF16)
    z_ref[...] = _dot_nt(u, win_ref[OFF_Z:OFF_Z + D_SSD, :])
    xbc = _dot_nt(u, win_ref[OFF_XBC:OFF_XBC + CONV_DIM, :])
    conv = cb_ref[...] + xbc * cw_ref[SSD_CONV - 1:SSD_CONV, :]
    for k in range(SSD_CONV - 1):
        conv = conv + cst_ref[k] * cw_ref[k:k + 1, :]
    for k in range(SSD_CONV - 2):
        ncst_ref[k] = cst_ref[k + 1]
    ncst_ref[SSD_CONV - 2] = xbc
    act = _silu(conv)
    xs = act[:, 0:D_SSD]
    xs_ref[...] = xs
    bc_ref[...] = act[:, D_SSD:CONV_DIM]
    dt = _softplus(_dot_nt(u, win_ref[OFF_DT:OFF_DT + DT_PAD, :]) + dtb_ref[...])
    xdt_ref[...] = xs * _dot_sel_rhs(dt, e_ref[...])
    dax_ref[...] = jnp.exp(_dot_sel_rhs(dt * a_ref[...], e_ref[...]))
    gc = _dot_nt(u, win_ref[OFF_GC:OFF_GC + D_SC, :])
    hh = _dot_nt(u, win_ref[OFF_H:OFF_H + D_SC, :])
    gch = gc * hh
    sc = gch * scw_ref[SC_CONV - 1:SC_CONV, :]
    for k in range(SC_CONV - 1):
        sc = sc + scst_ref[k] * scw_ref[k:k + 1, :]
    for k in range(SC_CONV - 2):
        nscst_ref[k] = scst_ref[k + 1]
    nscst_ref[SC_CONV - 2] = gch
    gb = _dot_nt(u, win_ref[OFF_GB:OFF_GB + D_SC, :])
    ysc_ref[...] = gb * sc


def _sample_pre(x, nw, win, cw, cb, dtb, a_pad, scw, expand, cst, scst):
    nb = x.shape[0]
    outs = [
        jax.ShapeDtypeStruct((nb, D_SSD), F32),
        jax.ShapeDtypeStruct((nb, D_SSD), F32),
        jax.ShapeDtypeStruct((nb, D_SSD), F32),
        jax.ShapeDtypeStruct((nb, D_SSD), F32),
        jax.ShapeDtypeStruct((nb, 2 * NGROUPS * NSTATE), F32),
        jax.ShapeDtypeStruct((nb, D_SC), F32),
        jax.ShapeDtypeStruct((SSD_CONV - 1, nb, CONV_DIM), F32),
        jax.ShapeDtypeStruct((SC_CONV - 1, nb, D_SC), F32),
    ]
    return pl.pallas_call(
        _sample_pre_kernel,
        out_shape=outs,
        compiler_params=pltpu.CompilerParams(vmem_limit_bytes=VMEM_LIMIT),
        name="sample_pre",
    )(x, nw, win, cw, cb, dtb, a_pad, scw, expand, cst, scst)


SAMPLE_BT = 8


def _sample_state_kernel(st_ref, xdt_t_ref, dax_t_ref, bc_ref, nst_ref, y_ref):
    rows = NHEADS * HEAD_DIM
    half = rows // NGROUPS
    for b in range(SAMPLE_BT):
        hst = st_ref[b].reshape(rows, NSTATE)
        dcol = dax_t_ref[0, :, b:b + 1]
        xcol = xdt_t_ref[0, :, b:b + 1]
        brow = jnp.concatenate(
            [jnp.broadcast_to(bc_ref[b:b + 1, g * NSTATE:(g + 1) * NSTATE], (half, NSTATE))
             for g in range(NGROUPS)], axis=0)
        hnew = hst * dcol + xcol * brow
        nst_ref[b] = hnew.reshape(NHEADS, HEAD_DIM, NSTATE)
        cmat = jnp.concatenate(
            [bc_ref[b:b + 1, (NGROUPS + g) * NSTATE:(NGROUPS + g + 1) * NSTATE] for g in range(NGROUPS)]
            + [jnp.zeros((8 - NGROUPS, NSTATE), F32)], axis=0)
        yr = lax.dot_general(cmat.astype(BF16), hnew.astype(BF16), (((1,), (1,)), ((), ())),
                             preferred_element_type=F32)
        lane = lax.broadcasted_iota(jnp.int32, (1, rows), 1)
        y_ref[b:b + 1, :] = jnp.where(lane < half, yr[0:1, :], yr[1:2, :])


def _sample_state(state, xdt_t, dax_t, bc):
    nb = state.shape[0]
    rows = NHEADS * HEAD_DIM
    st_spec = pl.BlockSpec((SAMPLE_BT, NHEADS, HEAD_DIM, NSTATE), lambda i: (i, 0, 0, 0))
    col_spec = pl.BlockSpec((1, rows, SAMPLE_BT), lambda i: (i, 0, 0))
    return pl.pallas_call(
        _sample_state_kernel,
        grid=(nb // SAMPLE_BT,),
        in_specs=[st_spec, col_spec, col_spec,
                  pl.BlockSpec((SAMPLE_BT, 2 * NGROUPS * NSTATE), lambda i: (i, 0))],
        out_specs=[st_spec, pl.BlockSpec((SAMPLE_BT, rows), lambda i: (i, 0))],
        out_shape=[jax.ShapeDtypeStruct(state.shape, F32), jax.ShapeDtypeStruct((nb, rows), F32)],
        compiler_params=pltpu.CompilerParams(
            dimension_semantics=("arbitrary",), vmem_limit_bytes=VMEM_LIMIT),
        name="sample_state",
    )(state, xdt_t, dax_t, bc)


def _sample_post_kernel(x_ref, y_ref, xs_ref, z_ref, ysc_ref, dx_ref, gnw_ref, wout_ref,
                        nw_ref, wffn_ref, fcw_ref, fcb_ref, wdown_ref, nfw_ref, fst_ref,
                        out_ref, nfst_ref):
    y = (y_ref[...] + dx_ref[...] * xs_ref[...]) * _silu(z_ref[...])
    gw = D_SSD // NGROUPS
    parts = [_rms_rows(y[:, g * gw:(g + 1) * gw], gnw_ref[:, g * gw:(g + 1) * gw]) for g in range(NGROUPS)]
    ycat = jnp.concatenate(parts + [ysc_ref[...]], axis=1).astype(BF16)
    x1 = x_ref[...] + _dot(ycat, wout_ref[...])
    u = _rms_rows(x1, nw_ref[...]).astype(BF16)
    gate = _dot(u, wffn_ref[:, 0:D_FF])
    up = _dot(u, wffn_ref[:, D_FF:2 * D_FF])
    g = fcb_ref[...] + gate * fcw_ref[FFN_CONV - 1:FFN_CONV, :]
    for k in range(FFN_CONV - 1):
        g = g + fst_ref[k] * fcw_ref[k:k + 1, :]
    for k in range(FFN_CONV - 2):
        nfst_ref[k] = fst_ref[k + 1]
    nfst_ref[FFN_CONV - 2] = gate
    act = (_silu(g) * up).astype(BF16)
    x2 = x1 + _dot(act, wdown_ref[...])
    out_ref[...] = _rms_rows(x2, nfw_ref[...])


def _sample_post(x, y, xs, z, ysc, dx, gnw, wout, nw, wffn, fcw, fcb, wdown, nfw, fst):
    nb = x.shape[0]
    return pl.pallas_call(
        _sample_post_kernel,
        out_shape=[jax.ShapeDtypeStruct((nb, D_MODEL), F32),
                   jax.ShapeDtypeStruct((FFN_CONV - 1, nb, D_FF), F32)],
        compiler_params=pltpu.CompilerParams(vmem_limit_bytes=VMEM_LIMIT),
        name="sample_post",
    )(x, y, xs, z, ysc, dx, gnw, wout, nw, wffn, fcw, fcb, wdown, nfw, fst)


def kernel(x_prompt, x_sample, state_ssm, state_ssd_conv, state_short_conv, state_ffn_conv,
           norm_mix_w, w_in, ssd_conv_w, ssd_conv_b, ssd_dt_bias, ssd_a_log, ssd_d, ssd_norm_w,
           sc_conv_w, w_out, norm_ffn_w, w_ffn_in, ffn_conv_w, ffn_conv_b, w_down, norm_final_w):
    depth = w_in.shape[0]
    assert depth == 1
    win = jnp.swapaxes(w_in[0], 0, 1).astype(BF16)
    wout = w_out[0].astype(BF16)
    wffn = w_ffn_in[0].astype(BF16)
    wdown = w_down[0].astype(BF16)
    nw = norm_mix_w[0].reshape(1, D_MODEL)
    nw2 = norm_ffn_w[0].reshape(1, D_MODEL)
    nfw = norm_final_w.reshape(1, D_MODEL)
    cw = ssd_conv_w[0]
    cb = ssd_conv_b[0].reshape(1, CONV_DIM)
    dtb = jnp.pad(ssd_dt_bias[0], (0, DT_PAD - NHEADS)).reshape(1, DT_PAD)
    a_pad = jnp.pad(-jnp.exp(ssd_a_log[0]), (0, DT_PAD - NHEADS)).reshape(1, DT_PAD)
    dx = jnp.repeat(ssd_d[0], HEAD_DIM).reshape(1, D_SSD)
    gnw = ssd_norm_w[0].reshape(1, D_SSD)
    scw = sc_conv_w[0]
    fcw = ffn_conv_w[0]
    fcb = ffn_conv_b[0].reshape(1, D_FF)
    head_of_lane = jnp.arange(D_SSD, dtype=jnp.int32) // HEAD_DIM
    expand = (jnp.arange(DT_PAD, dtype=jnp.int32)[:, None] == head_of_lane[None, :]).astype(BF16)

    x1, p_ssm, p_cst, p_scst = _mixer_prompt(x_prompt, nw, win, cw, cb, dtb, a_pad, dx, gnw, scw, wout, expand)
    y_prompt, p_fst = _ffn_prompt(x1, nw2, wffn, fcw, fcb, wdown, nfw)

    nbs = x_sample.shape[0]
    xs_in = x_sample.reshape(nbs, D_MODEL)
    cst = jnp.swapaxes(state_ssd_conv[0], 0, 1)
    scst = jnp.swapaxes(state_short_conv[0], 0, 1)
    fst = jnp.swapaxes(state_ffn_conv[0], 0, 1)
    z, xs, xdt, dax, bc, ysc, ncst, nscst = _sample_pre(xs_in, nw, win, cw, cb, dtb, a_pad, scw, expand, cst, scst)
    ntile = nbs // SAMPLE_BT
    to_cols = lambda v: jnp.transpose(v.reshape(ntile, SAMPLE_BT, D_SSD), (0, 2, 1))
    s_ssm, y_s = _sample_state(state_ssm[0], to_cols(xdt), to_cols(dax), bc)
    y_sample, nfst = _sample_post(xs_in, y_s, xs, z, ysc, dx, gnw, wout, nw2, wffn, fcw, fcb, wdown, nfw, fst)

    return (y_prompt, y_sample.reshape(nbs, 1, D_MODEL),
            p_ssm[None], p_cst[None], p_scst[None], p_fst[None],
            s_ssm[None], jnp.swapaxes(ncst, 0, 1)[None], jnp.swapaxes(nscst, 0, 1)[None],
            jnp.swapaxes(nfst, 0, 1)[None])
```

```python
import jax
import jax.numpy as jnp
from jax import lax
from jax.experimental import pallas as pl
from jax.experimental.pallas import tpu as pltpu

D_MODEL = 1024
D_SSD = 1024
D_SC = 1024
NHEADS = 16
HEAD_DIM = 64
NGROUPS = 2
NSTATE = 128
SSD_CONV = 4
CONV_DIM = D_SSD + 2 * NGROUPS * NSTATE
SC_CONV = 3
D_FF = 2816
FFN_CONV = 3
EPS = 1e-5
D_MIX = D_SSD + D_SC
D_IN_PROJ = D_SSD + CONV_DIM + NHEADS + 3 * D_SC

LANES = 128
MXU_COLS = 256
OFF_Z = 0
OFF_XBC = OFF_Z + D_SSD
OFF_DT = OFF_XBC + CONV_DIM
OFF_GB = OFF_DT + NHEADS
OFF_GC = OFF_GB + D_SC
OFF_H = OFF_GC + D_SC
DT_PAD = LANES

CHUNK = 128
CARRY = 8
SEQ_TILE = 512
VMEM_LIMIT = 56 * 1024 * 1024

F32 = jnp.float32
BF16 = jnp.bfloat16


def _dot(a, b):
    return jnp.dot(a, b, preferred_element_type=F32)


def _dot_nt(a, b_t):
    return lax.dot_general(a, b_t, (((1,), (1,)), ((), ())), preferred_element_type=F32)


def _split3(v):
    hi = v.astype(BF16)
    r = v - hi.astype(F32)
    mid = r.astype(BF16)
    lo = (r - mid.astype(F32)).astype(BF16)
    return hi, mid, lo


def _dot_sel_rhs(v, sel):
    hi, mid, lo = _split3(v)
    return (_dot(hi, sel) + _dot(mid, sel)) + _dot(lo, sel)


def _dot_sel_lhs(sel, v):
    hi, mid, lo = _split3(v)
    return (_dot(sel, hi) + _dot(sel, mid)) + _dot(sel, lo)


def _silu(v):
    return v * (1.0 / (1.0 + jnp.exp(-v)))


def _softplus(v):
    return jnp.maximum(v, 0.0) + jnp.log1p(jnp.exp(-jnp.abs(v)))


def _rms_rows(x, w):
    ms = jnp.mean(x * x, axis=-1, keepdims=True)
    return x * lax.rsqrt(ms + EPS) * w


def _mixer_prompt_kernel(x_ref, nw_ref, win_ref, cw_ref, cb_ref, dtb_ref, a_ref, dx_ref,
                         gnw_ref, scw_ref, wout_ref, e_ref,
                         x1_ref, ssm_ref, cst_ref, scst_ref,
                         xbc_ext, sc_ext, u_ref, act_ref, dt_ref, z_ref, gb_ref, s_ref, y_ref, ycat_ref):
    tile = x_ref.shape[1]
    t = pl.program_id(1)

    @pl.when(t == 0)
    def _():
        xbc_ext[0:CARRY, :] = jnp.zeros((CARRY, CONV_DIM), F32)
        sc_ext[0:CARRY, :] = jnp.zeros((CARRY, D_SC), F32)
        s_ref[...] = jnp.zeros_like(s_ref)

    u_ref[...] = _rms_rows(x_ref[0], nw_ref[...]).astype(BF16)

    def project(dst, row0, base, off, width=MXU_COLS):
        dst[row0:row0 + tile, off:off + width] = _dot_nt(u_ref[...], win_ref[base + off:base + off + width, :])

    def conv_block(off):
        conv = cb_ref[:, off:off + LANES] + (
            xbc_ext[CARRY - 3:CARRY - 3 + tile, off:off + LANES] * cw_ref[0:1, off:off + LANES])
        for k in range(1, SSD_CONV):
            conv = conv + (xbc_ext[CARRY - 3 + k:CARRY - 3 + k + tile, off:off + LANES]
                           * cw_ref[k:k + 1, off:off + LANES])
        act_ref[:, off:off + LANES] = _silu(conv)

    def sc_block(off):
        sc = sc_ext[CARRY - 2:CARRY - 2 + tile, off:off + LANES] * scw_ref[0:1, off:off + LANES]
        for k in range(1, SC_CONV):
            sc = sc + (sc_ext[CARRY - 2 + k:CARRY - 2 + k + tile, off:off + LANES]
                       * scw_ref[k:k + 1, off:off + LANES])
        ycat_ref[:, D_SSD + off:D_SSD + off + LANES] = (gb_ref[:, off:off + LANES] * sc).astype(BF16)

    for off in range(0, CONV_DIM, MXU_COLS):
        project(xbc_ext, CARRY, OFF_XBC, off)
        for sub in range(off, off + MXU_COLS, LANES):
            conv_block(sub)
    tail = xbc_ext[tile + CARRY - 3:tile + CARRY, :]
    cst_ref[0] = tail
    xbc_ext[CARRY - 3:CARRY, :] = tail
    project(dt_ref, 0, OFF_DT, 0, DT_PAD)
    for off in range(0, D_SC, MXU_COLS):
        u = u_ref[...]
        sc_ext[CARRY:CARRY + tile, off:off + MXU_COLS] = (
            _dot_nt(u, win_ref[OFF_GC + off:OFF_GC + off + MXU_COLS, :])
            * _dot_nt(u, win_ref[OFF_H + off:OFF_H + off + MXU_COLS, :]))

    fillers = []
    for k in range(D_SC // MXU_COLS):
        fillers.append(lambda k=k: project(gb_ref, 0, OFF_GB, k * MXU_COLS))
        fillers.append(lambda k=k: sc_block(k * MXU_COLS))
        fillers.append(lambda k=k: sc_block(k * MXU_COLS + LANES))
    for k in range(D_SSD // MXU_COLS):
        fillers.append(lambda k=k: project(z_ref, 0, OFF_Z, k * MXU_COLS))

    row_i = lax.broadcasted_iota(jnp.int32, (CHUNK, CHUNK), 0)
    col_i = lax.broadcasted_iota(jnp.int32, (CHUNK, CHUNK), 1)
    causal = row_i >= col_i
    tri = causal.astype(BF16)
    lane_i = lax.broadcasted_iota(jnp.int32, (1, LANES), 1)
    keeps = ((lane_i < HEAD_DIM).astype(BF16), (lane_i >= HEAD_DIM).astype(BF16))
    neg_inf = jnp.float32(-jnp.inf)
    pairs_per_group = NHEADS // NGROUPS // 2

    for c in range(tile // CHUNK):
        r0 = c * CHUNK
        dt = _softplus(dt_ref[r0:r0 + CHUNK, :] + dtb_ref[...])
        acum = _dot_sel_lhs(tri, dt * a_ref[...])
        acum_t = acum.T
        dt_t = dt.T
        ea = jnp.exp(acum)
        w_t = jnp.exp(acum_t[:, CHUNK - 1:CHUNK] - acum_t) * dt_t
        cd = jnp.broadcast_to(ea[CHUNK - 1:CHUNK, :], (8, LANES))
        cdx = _dot_sel_rhs(cd, e_ref[...])[0:1, :]
        for grp in range(NGROUPS):
            bg = act_ref[r0:r0 + CHUNK, D_SSD + grp * NSTATE:D_SSD + (grp + 1) * NSTATE]
            cg = act_ref[r0:r0 + CHUNK, D_SSD + (NGROUPS + grp) * NSTATE:D_SSD + (NGROUPS + grp + 1) * NSTATE]
            cb = lax.dot_general(cg.astype(BF16), bg.astype(BF16), (((1,), (1,)), ((), ())),
                                 preferred_element_type=F32)
            bg_t = bg.T
            for jp in range(pairs_per_group):
                j = grp * pairs_per_group + jp
                xp = act_ref[r0:r0 + CHUNK, j * LANES:(j + 1) * LANES].astype(BF16)
                sp = s_ref[:, j * LANES:(j + 1) * LANES]
                spb = sp.astype(BF16)
                lhs_parts, rhs_parts, bw_parts, x_parts = [], [], [], []
                for half, keep in enumerate(keeps):
                    h = 2 * j + half
                    xh = xp * keep
                    seg = acum[:, h:h + 1] - acum_t[h:h + 1, :]
                    decay = jnp.exp(jnp.where(causal, seg, neg_inf))
                    m = cb * decay * dt_t[h:h + 1, :]
                    eac = ea[:, h:h + 1] * cg
                    lhs_parts += [m.astype(BF16), eac.astype(BF16)]
                    rhs_parts += [xh, spb * keep]
                    bw_parts.append((bg_t * w_t[h:h + 1, :]).astype(BF16))
                    x_parts.append(xh)
                y_ref[r0:r0 + CHUNK, j * LANES:(j + 1) * LANES] = _dot(
                    jnp.concatenate(lhs_parts, axis=1), jnp.concatenate(rhs_parts, axis=0))
                snew = _dot(jnp.concatenate(bw_parts, axis=1), jnp.concatenate(x_parts, axis=0))
                s_ref[:, j * LANES:(j + 1) * LANES] = cdx[:, j * LANES:(j + 1) * LANES] * sp + snew
                if fillers:
                    fillers.pop(0)()
    while fillers:
        fillers.pop(0)()
    sc_tail = sc_ext[tile + CARRY - 2:tile + CARRY, :]
    scst_ref[0] = sc_tail
    sc_ext[CARRY - 2:CARRY, :] = sc_tail

    gw = D_SSD // NGROUPS
    for grp in range(NGROUPS):
        ssq = jnp.zeros((tile, LANES), F32)
        for off in range(grp * gw, (grp + 1) * gw, LANES):
            yb = ((y_ref[:, off:off + LANES] + dx_ref[:, off:off + LANES] * act_ref[:, off:off + LANES])
                  * _silu(z_ref[:, off:off + LANES]))
            y_ref[:, off:off + LANES] = yb
            ssq = ssq + yb * yb
        scale = lax.rsqrt(jnp.sum(ssq, axis=-1, keepdims=True) * (1.0 / gw) + EPS)
        for off in range(grp * gw, (grp + 1) * gw, LANES):
            ycat_ref[:, off:off + LANES] = (y_ref[:, off:off + LANES] * scale
                                            * gnw_ref[:, off:off + LANES]).astype(BF16)

    for off in range(0, D_MODEL, MXU_COLS):
        x1_ref[0, :, off:off + MXU_COLS] = x_ref[0, :, off:off + MXU_COLS] + _dot(
            ycat_ref[...], wout_ref[:, off:off + MXU_COLS])

    @pl.when(t == pl.num_programs(1) - 1)
    def _():
        for j in range(NHEADS // 2):
            ssm_ref[0, 2 * j:2 * j + 2] = s_ref[:, j * LANES:(j + 1) * LANES].T.reshape(2, HEAD_DIM, NSTATE)


def _const_spec(shape):
    nd = len(shape)
    return pl.BlockSpec(shape, lambda *_: (0,) * nd, pipeline_mode=pl.Buffered(1))


def _mixer_prompt(x, nw, win, cw, cb, dtb, a_pad, dx, gnw, scw, wout, expand):
    nb, seq, _ = x.shape
    tile = SEQ_TILE
    tok_spec = pl.BlockSpec((1, tile, D_MODEL), lambda b, t: (b, t, 0))
    consts = (nw, win, cw, cb, dtb, a_pad, dx, gnw, scw, wout, expand)
    return pl.pallas_call(
        _mixer_prompt_kernel,
        grid=(nb, seq // tile),
        in_specs=[tok_spec] + [_const_spec(p.shape) for p in consts],
        out_specs=[
            tok_spec,
            pl.BlockSpec((1, NHEADS, HEAD_DIM, NSTATE), lambda b, t: (b, 0, 0, 0)),
            pl.BlockSpec((1, SSD_CONV - 1, CONV_DIM), lambda b, t: (b, 0, 0)),
            pl.BlockSpec((1, SC_CONV - 1, D_SC), lambda b, t: (b, 0, 0)),
        ],
        out_shape=[
            jax.ShapeDtypeStruct((nb, seq, D_MODEL), F32),
            jax.ShapeDtypeStruct((nb, NHEADS, HEAD_DIM, NSTATE), F32),
            jax.ShapeDtypeStruct((nb, SSD_CONV - 1, CONV_DIM), F32),
            jax.ShapeDtypeStruct((nb, SC_CONV - 1, D_SC), F32),
        ],
        scratch_shapes=[
            pltpu.VMEM((tile + CARRY, CONV_DIM), F32),
            pltpu.VMEM((tile + CARRY, D_SC), F32),
            pltpu.VMEM((tile, D_MODEL), BF16),
            pltpu.VMEM((tile, CONV_DIM), F32),
            pltpu.VMEM((tile, DT_PAD), F32),
            pltpu.VMEM((tile, D_SSD), F32),
            pltpu.VMEM((tile, D_SC), F32),
            pltpu.VMEM((NSTATE, D_SSD), F32),
            pltpu.VMEM((tile, D_SSD), F32),
            pltpu.VMEM((tile, D_MIX), BF16),
        ],
        compiler_params=pltpu.CompilerParams(
            dimension_semantics=("arbitrary", "arbitrary"), vmem_limit_bytes=VMEM_LIMIT),
        name="mixer_prompt",
    )(x, *consts)


def _ffn_prompt_kernel(x1_ref, nw_ref, wffn_ref, fcw_ref, fcb_ref, wdown_ref, nfw_ref,
                       y_ref, fst_ref, gate_ext):
    tile = x1_ref.shape[1]
    t = pl.program_id(1)

    @pl.when(t == 0)
    def _():
        gate_ext[0:CARRY, :] = jnp.zeros((CARRY, D_FF), F32)

    x1 = x1_ref[0]
    u = _rms_rows(x1, nw_ref[...]).astype(BF16)
    gate_ext[CARRY:CARRY + tile, :] = _dot(u, wffn_ref[:, 0:D_FF])
    up = _dot(u, wffn_ref[:, D_FF:2 * D_FF])
    g = fcb_ref[...] + gate_ext[CARRY - 2:CARRY - 2 + tile, :] * fcw_ref[0:1, :]
    for k in range(1, FFN_CONV):
        g = g + gate_ext[CARRY - 2 + k:CARRY - 2 + k + tile, :] * fcw_ref[k:k + 1, :]
    tail = gate_ext[tile + CARRY - 2:tile + CARRY, :]
    fst_ref[0] = tail
    gate_ext[CARRY - 2:CARRY, :] = tail
    act = (_silu(g) * up).astype(BF16)
    x2 = x1 + _dot(act, wdown_ref[...])
    y_ref[0] = _rms_rows(x2, nfw_ref[...])


def _ffn_prompt(x1, nw, wffn, fcw, fcb, wdown, nfw):
    nb, seq, _ = x1.shape
    tile = SEQ_TILE
    tok_spec = pl.BlockSpec((1, tile, D_MODEL), lambda b, t: (b, t, 0))
    return pl.pallas_call(
        _ffn_prompt_kernel,
        grid=(nb, seq // tile),
        in_specs=[tok_spec] + [_const_spec(p.shape) for p in (nw, wffn, fcw, fcb, wdown, nfw)],
        out_specs=[tok_spec, pl.BlockSpec((1, FFN_CONV - 1, D_FF), lambda b, t: (b, 0, 0))],
        out_shape=[jax.ShapeDtypeStruct((nb, seq, D_MODEL), F32),
                   jax.ShapeDtypeStruct((nb, FFN_CONV - 1, D_FF), F32)],
        scratch_shapes=[pltpu.VMEM((tile + CARRY, D_FF), F32)],
        compiler_params=pltpu.CompilerParams(
            dimension_semantics=("arbitrary", "arbitrary"), vmem_limit_bytes=VMEM_LIMIT),
        name="ffn_prompt",
    )(x1, nw, wffn, fcw, fcb, wdown, nfw)


def _sample_pre_kernel(x_ref, nw_ref, win_ref, cw_ref, cb_ref, dtb_ref, a_ref, scw_ref, e_ref,
                       cst_ref, scst_ref,
                       z_ref, xs_ref, xdt_ref, dax_ref, bc_ref, ysc_ref, ncst_ref, nscst_ref):
    x = x_ref[...]
    u = _rms_rows(x, nw_ref[...]).astype(BF16)
    z_ref[...] = _dot_nt(u, win_ref[OFF_Z:OFF_Z + D_SSD, :])
    xbc = _dot_nt(u, win_ref[OFF_XBC:OFF_XBC + CONV_DIM, :])
    conv = cb_ref[...] + xbc * cw_ref[SSD_CONV - 1:SSD_CONV, :]
    for k in range(SSD_CONV - 1):
        conv = conv + cst_ref[k] * cw_ref[k:k + 1, :]
    for k in range(SSD_CONV - 2):
        ncst_ref[k] = cst_ref[k + 1]
    ncst_ref[SSD_CONV - 2] = xbc
    act = _silu(conv)
    xs = act[:, 0:D_SSD]
    xs_ref[...] = xs
    bc_ref[...] = act[:, D_SSD:CONV_DIM]
    dt = _softplus(_dot_nt(u, win_ref[OFF_DT:OFF_DT + DT_PAD, :]) + dtb_ref[...])
    xdt_ref[...] = xs * _dot_sel_rhs(dt, e_ref[...])
    dax_ref[...] = jnp.exp(_dot_sel_rhs(dt * a_ref[...], e_ref[...]))
    gc = _dot_nt(u, win_ref[OFF_GC:OFF_GC + D_SC, :])
    hh = _dot_nt(u, win_ref[OFF_H:OFF_H + D_SC, :])
    gch = gc * hh
    sc = gch * scw_ref[SC_CONV - 1:SC_CONV, :]
    for k in range(SC_CONV - 1):
        sc = sc + scst_ref[k] * scw_ref[k:k + 1, :]
    for k in range(SC_CONV - 2):
        nscst_ref[k] = scst_ref[k + 1]
    nscst_ref[SC_CONV - 2] = gch
    gb = _dot_nt(u, win_ref[OFF_GB:OFF_GB + D_SC, :])
    ysc_ref[...] = gb * sc


def _sample_pre(x, nw, win, cw, cb, dtb, a_pad, scw, expand, cst, scst):
    nb = x.shape[0]
    outs = [
        jax.ShapeDtypeStruct((nb, D_SSD), F32),
        jax.ShapeDtypeStruct((nb, D_SSD), F32),
        jax.ShapeDtypeStruct((nb, D_SSD), F32),
        jax.ShapeDtypeStruct((nb, D_SSD), F32),
        jax.ShapeDtypeStruct((nb, 2 * NGROUPS * NSTATE), F32),
        jax.ShapeDtypeStruct((nb, D_SC), F32),
        jax.ShapeDtypeStruct((SSD_CONV - 1, nb, CONV_DIM), F32),
        jax.ShapeDtypeStruct((SC_CONV - 1, nb, D_SC), F32),
    ]
    return pl.pallas_call(
        _sample_pre_kernel,
        out_shape=outs,
        compiler_params=pltpu.CompilerParams(vmem_limit_bytes=VMEM_LIMIT),
        name="sample_pre",
    )(x, nw, win, cw, cb, dtb, a_pad, scw, expand, cst, scst)


SAMPLE_BT = 8


def _sample_state_kernel(st_ref, xdt_t_ref, dax_t_ref, bc_ref, nst_ref, y_ref):
    rows = NHEADS * HEAD_DIM
    half = rows // NGROUPS
    for b in range(SAMPLE_BT):
        hst = st_ref[b].reshape(rows, NSTATE)
        dcol = dax_t_ref[0, :, b:b + 1]
        xcol = xdt_t_ref[0, :, b:b + 1]
        brow = jnp.concatenate(
            [jnp.broadcast_to(bc_ref[b:b + 1, g * NSTATE:(g + 1) * NSTATE], (half, NSTATE))
             for g in range(NGROUPS)], axis=0)
        hnew = hst * dcol + xcol * brow
        nst_ref[b] = hnew.reshape(NHEADS, HEAD_DIM, NSTATE)
        cmat = jnp.concatenate(
            [bc_ref[b:b + 1, (NGROUPS + g) * NSTATE:(NGROUPS + g + 1) * NSTATE] for g in range(NGROUPS)]
            + [jnp.zeros((8 - NGROUPS, NSTATE), F32)], axis=0)
        yr = lax.dot_general(cmat.astype(BF16), hnew.astype(BF16), (((1,), (1,)), ((), ())),
                             preferred_element_type=F32)
        lane = lax.broadcasted_iota(jnp.int32, (1, rows), 1)
        y_ref[b:b + 1, :] = jnp.where(lane < half, yr[0:1, :], yr[1:2, :])


def _sample_state(state, xdt_t, dax_t, bc):
    nb = state.shape[0]
    rows = NHEADS * HEAD_DIM
    st_spec = pl.BlockSpec((SAMPLE_BT, NHEADS, HEAD_DIM, NSTATE), lambda i: (i, 0, 0, 0))
    col_spec = pl.BlockSpec((1, rows, SAMPLE_BT), lambda i: (i, 0, 0))
    return pl.pallas_call(
        _sample_state_kernel,
        grid=(nb // SAMPLE_BT,),
        in_specs=[st_spec, col_spec, col_spec,
                  pl.BlockSpec((SAMPLE_BT, 2 * NGROUPS * NSTATE), lambda i: (i, 0))],
        out_specs=[st_spec, pl.BlockSpec((SAMPLE_BT, rows), lambda i: (i, 0))],
        out_shape=[jax.ShapeDtypeStruct(state.shape, F32), jax.ShapeDtypeStruct((nb, rows), F32)],
        compiler_params=pltpu.CompilerParams(
            dimension_semantics=("arbitrary",), vmem_limit_bytes=VMEM_LIMIT),
        name="sample_state",
    )(state, xdt_t, dax_t, bc)


def _sample_post_kernel(x_ref, y_ref, xs_ref, z_ref, ysc_ref, dx_ref, gnw_ref, wout_ref,
                        nw_ref, wffn_ref, fcw_ref, fcb_ref, wdown_ref, nfw_ref, fst_ref,
                        out_ref, nfst_ref):
    y = (y_ref[...] + dx_ref[...] * xs_ref[...]) * _silu(z_ref[...])
    gw = D_SSD // NGROUPS
    parts = [_rms_rows(y[:, g * gw:(g + 1) * gw], gnw_ref[:, g * gw:(g + 1) * gw]) for g in range(NGROUPS)]
    ycat = jnp.concatenate(parts + [ysc_ref[...]], axis=1).astype(BF16)
    x1 = x_ref[...] + _dot(ycat, wout_ref[...])
    u = _rms_rows(x1, nw_ref[...]).astype(BF16)
    gate = _dot(u, wffn_ref[:, 0:D_FF])
    up = _dot(u, wffn_ref[:, D_FF:2 * D_FF])
    g = fcb_ref[...] + gate * fcw_ref[FFN_CONV - 1:FFN_CONV, :]
    for k in range(FFN_CONV - 1):
        g = g + fst_ref[k] * fcw_ref[k:k + 1, :]
    for k in range(FFN_CONV - 2):
        nfst_ref[k] = fst_ref[k + 1]
    nfst_ref[FFN_CONV - 2] = gate
    act = (_silu(g) * up).astype(BF16)
    x2 = x1 + _dot(act, wdown_ref[...])
    out_ref[...] = _rms_rows(x2, nfw_ref[...])


def _sample_post(x, y, xs, z, ysc, dx, gnw, wout, nw, wffn, fcw, fcb, wdown, nfw, fst):
    nb = x.shape[0]
    return pl.pallas_call(
        _sample_post_kernel,
        out_shape=[jax.ShapeDtypeStruct((nb, D_MODEL), F32),
                   jax.ShapeDtypeStruct((FFN_CONV - 1, nb, D_FF), F32)],
        compiler_params=pltpu.CompilerParams(vmem_limit_bytes=VMEM_LIMIT),
        name="sample_post",
    )(x, y, xs, z, ysc, dx, gnw, wout, nw, wffn, fcw, fcb, wdown, nfw, fst)


def kernel(x_prompt, x_sample, state_ssm, state_ssd_conv, state_short_conv, state_ffn_conv,
           norm_mix_w, w_in, ssd_conv_w, ssd_conv_b, ssd_dt_bias, ssd_a_log, ssd_d, ssd_norm_w,
           sc_conv_w, w_out, norm_ffn_w, w_ffn_in, ffn_conv_w, ffn_conv_b, w_down, norm_final_w):
    depth = w_in.shape[0]
    assert depth == 1
    win = jnp.swapaxes(w_in[0], 0, 1).astype(BF16)
    wout = w_out[0].astype(BF16)
    wffn = w_ffn_in[0].astype(BF16)
    wdown = w_down[0].astype(BF16)
    nw = norm_mix_w[0].reshape(1, D_MODEL)
    nw2 = norm_ffn_w[0].reshape(1, D_MODEL)
    nfw = norm_final_w.reshape(1, D_MODEL)
    cw = ssd_conv_w[0]
    cb = ssd_conv_b[0].reshape(1, CONV_DIM)
    dtb = jnp.pad(ssd_dt_bias[0], (0, DT_PAD - NHEADS)).reshape(1, DT_PAD)
    a_pad = jnp.pad(-jnp.exp(ssd_a_log[0]), (0, DT_PAD - NHEADS)).reshape(1, DT_PAD)
    dx = jnp.repeat(ssd_d[0], HEAD_DIM).reshape(1, D_SSD)
    gnw = ssd_norm_w[0].reshape(1, D_SSD)
    scw = sc_conv_w[0]
    fcw = ffn_conv_w[0]
    fcb = ffn_conv_b[0].reshape(1, D_FF)
    head_of_lane = jnp.arange(D_SSD, dtype=jnp.int32) // HEAD_DIM
    expand = (jnp.arange(DT_PAD, dtype=jnp.int32)[:, None] == head_of_lane[None, :]).astype(BF16)

    x1, p_ssm, p_cst, p_scst = _mixer_prompt(x_prompt, nw, win, cw, cb, dtb, a_pad, dx, gnw, scw, wout, expand)
    y_prompt, p_fst = _ffn_prompt(x1, nw2, wffn, fcw, fcb, wdown, nfw)

    nbs = x_sample.shape[0]
    xs_in = x_sample.reshape(nbs, D_MODEL)
    cst = jnp.swapaxes(state_ssd_conv[0], 0, 1)
    scst = jnp.swapaxes(state_short_conv[0], 0, 1)
    fst = jnp.swapaxes(state_ffn_conv[0], 0, 1)
    z, xs, xdt, dax, bc, ysc, ncst, nscst = _sample_pre(xs_in, nw, win, cw, cb, dtb, a_pad, scw, expand, cst, scst)
    ntile = nbs // SAMPLE_BT
    to_cols = lambda v: jnp.transpose(v.reshape(ntile, SAMPLE_BT, D_SSD), (0, 2, 1))
    s_ssm, y_s = _sample_state(state_ssm[0], to_cols(xdt), to_cols(dax), bc)
    y_sample, nfst = _sample_post(xs_in, y_s, xs, z, ysc, dx, gnw, wout, nw2, wffn, fcw, fcb, wdown, nfw, fst)

    return (y_prompt, y_sample.reshape(nbs, 1, D_MODEL),
            p_ssm[None], p_cst[None], p_scst[None], p_fst[None],
            s_ssm[None], jnp.swapaxes(ncst, 0, 1)[None], jnp.swapaxes(nscst, 0, 1)[None],
            jnp.swapaxes(nfst, 0, 1)[None])
```

```python
import functools

import jax
import jax.numpy as jnp
from jax import lax
from jax.experimental import pallas as pl
from jax.experimental.pallas import tpu as pltpu

D_MODEL = 1024
D_SSD = 1024
D_SC = 1024
NHEADS = 16
HEAD_DIM = 64
NGROUPS = 2
NSTATE = 128
SSD_CONV = 4
CONV_DIM = D_SSD + 2 * NGROUPS * NSTATE
SC_CONV = 3
D_FF = 2816
FFN_CONV = 3
EPS = 1e-5
D_MIX = D_SSD + D_SC
D_IN_PROJ = D_SSD + CONV_DIM + NHEADS + 3 * D_SC

LANES = 128
MXU_COLS = 256
OFF_Z = 0
OFF_XBC = OFF_Z + D_SSD
OFF_DT = OFF_XBC + CONV_DIM
OFF_GB = OFF_DT + NHEADS
OFF_GC = OFF_GB + D_SC
OFF_H = OFF_GC + D_SC
DT_PAD = LANES

CHUNK = 128
CARRY = 8
SEQ_TILE = 512
VMEM_LIMIT = 56 * 1024 * 1024

F32 = jnp.float32
BF16 = jnp.bfloat16


def _dot(a, b):
    return jnp.dot(a, b, preferred_element_type=F32)


def _dot_nt(a, b_t):
    return lax.dot_general(a, b_t, (((1,), (1,)), ((), ())), preferred_element_type=F32)


def _split3(v):
    hi = v.astype(BF16)
    r = v - hi.astype(F32)
    mid = r.astype(BF16)
    lo = (r - mid.astype(F32)).astype(BF16)
    return hi, mid, lo


def _dot_sel_rhs(v, sel):
    hi, mid, lo = _split3(v)
    return (_dot(hi, sel) + _dot(mid, sel)) + _dot(lo, sel)


def _dot_sel_lhs(sel, v):
    hi, mid, lo = _split3(v)
    return (_dot(sel, hi) + _dot(sel, mid)) + _dot(sel, lo)


def _silu(v):
    h = 0.5 * v
    return h + h * jnp.tanh(h)


def _softplus(v):
    return jnp.maximum(v, 0.0) + jnp.log1p(jnp.exp(-jnp.abs(v)))


def _rms_rows(x, w):
    ms = jnp.mean(x * x, axis=-1, keepdims=True)
    return x * lax.rsqrt(ms + EPS) * w


def _mixer_prompt_kernel(tiles_per_seq,
                         x_ref, xres_ref, nw_ref, win_ref, cw_ref, cb_ref, dtb_ref, a_ref, dx_ref,
                         gnw_ref, scw_ref, wout_ref, e_ref,
                         x1_ref, ssm_ref, cst_ref, scst_ref,
                         xbc_ext, sc_ext, u_ref, act_ref, dt_ref, z_ref, gb_ref, s_ref, y_ref, ysc_ref, ycat_ref):
    tile = x_ref.shape[1]
    g = pl.program_id(0)
    last = pl.num_programs(0) - 1
    t = lax.rem(g, tiles_per_seq)

    @pl.when(g == 0)
    def _():
        ycat_ref[...] = jnp.zeros_like(ycat_ref)

    @pl.when(t == 0)
    def _():
        xbc_ext[0:CARRY, :] = jnp.zeros((CARRY, CONV_DIM), F32)
        sc_ext[0:CARRY, :] = jnp.zeros((CARRY, D_SC), F32)
        s_ref[...] = jnp.zeros_like(s_ref)

    def out_block(off):
        x1_ref[0, :, off:off + MXU_COLS] = xres_ref[0, :, off:off + MXU_COLS] + _dot(
            ycat_ref[...], wout_ref[:, off:off + MXU_COLS])

    @pl.when(g == last)
    def _():
        for off in range(0, D_MODEL, MXU_COLS):
            out_block(off)

    @pl.when(g < last)
    def _():
        _mixer_tile(tile, t == tiles_per_seq - 1, out_block,
                    x_ref, nw_ref, win_ref, cw_ref, cb_ref, dtb_ref, a_ref, dx_ref, gnw_ref, scw_ref, e_ref,
                    ssm_ref, cst_ref, scst_ref,
                    xbc_ext, sc_ext, u_ref, act_ref, dt_ref, z_ref, gb_ref, s_ref, y_ref, ysc_ref, ycat_ref)


def _mixer_tile(tile, ends_sequence, out_block,
                x_ref, nw_ref, win_ref, cw_ref, cb_ref, dtb_ref, a_ref, dx_ref, gnw_ref, scw_ref, e_ref,
                ssm_ref, cst_ref, scst_ref,
                xbc_ext, sc_ext, u_ref, act_ref, dt_ref, z_ref, gb_ref, s_ref, y_ref, ysc_ref, ycat_ref):
    u_ref[...] = _rms_rows(x_ref[0], nw_ref[...]).astype(BF16)

    def project(dst, row0, base, off, width=MXU_COLS):
        dst[row0:row0 + tile, off:off + width] = _dot_nt(u_ref[...], win_ref[base + off:base + off + width, :])

    def conv_block(off):
        conv = cb_ref[:, off:off + LANES] + (
            xbc_ext[CARRY - 3:CARRY - 3 + tile, off:off + LANES] * cw_ref[0:1, off:off + LANES])
        for k in range(1, SSD_CONV):
            conv = conv + (xbc_ext[CARRY - 3 + k:CARRY - 3 + k + tile, off:off + LANES]
                           * cw_ref[k:k + 1, off:off + LANES])
        act_ref[:, off:off + LANES] = _silu(conv)

    def sc_block(off):
        sc = sc_ext[CARRY - 2:CARRY - 2 + tile, off:off + LANES] * scw_ref[0:1, off:off + LANES]
        for k in range(1, SC_CONV):
            sc = sc + (sc_ext[CARRY - 2 + k:CARRY - 2 + k + tile, off:off + LANES]
                       * scw_ref[k:k + 1, off:off + LANES])
        ysc_ref[:, off:off + LANES] = (gb_ref[:, off:off + LANES] * sc).astype(BF16)

    for off in range(0, CONV_DIM, MXU_COLS):
        project(xbc_ext, CARRY, OFF_XBC, off)
        for sub in range(off, off + MXU_COLS, LANES):
            conv_block(sub)
    tail = xbc_ext[tile + CARRY - 3:tile + CARRY, :]
    cst_ref[0] = tail
    xbc_ext[CARRY - 3:CARRY, :] = tail
    project(dt_ref, 0, OFF_DT, 0, DT_PAD)
    for off in range(0, D_SC, MXU_COLS):
        u = u_ref[...]
        sc_ext[CARRY:CARRY + tile, off:off + MXU_COLS] = (
            _dot_nt(u, win_ref[OFF_GC + off:OFF_GC + off + MXU_COLS, :])
            * _dot_nt(u, win_ref[OFF_H + off:OFF_H + off + MXU_COLS, :]))

    fillers = []
    for k in range(D_SC // MXU_COLS):
        fillers.append(lambda k=k: out_block(k * MXU_COLS))
        fillers.append(lambda k=k: project(gb_ref, 0, OFF_GB, k * MXU_COLS))
        fillers.append(lambda k=k: sc_block(k * MXU_COLS))
        fillers.append(lambda k=k: sc_block(k * MXU_COLS + LANES))
        fillers.append(lambda k=k: project(z_ref, 0, OFF_Z, k * MXU_COLS))
    n_slots = (tile // CHUNK) * (NHEADS // 2)
    slot_of_filler = [(i * n_slots) // len(fillers) for i in range(len(fillers))]

    row_i = lax.broadcasted_iota(jnp.int32, (CHUNK, CHUNK), 0)
    col_i = lax.broadcasted_iota(jnp.int32, (CHUNK, CHUNK), 1)
    causal = row_i >= col_i
    tri = causal.astype(BF16)
    lane_i = lax.broadcasted_iota(jnp.int32, (1, LANES), 1)
    keeps = ((lane_i < HEAD_DIM).astype(BF16), (lane_i >= HEAD_DIM).astype(BF16))
    neg_inf = jnp.float32(-jnp.inf)
    pairs_per_group = NHEADS // NGROUPS // 2

    for c in range(tile // CHUNK):
        r0 = c * CHUNK
        dt = _softplus(dt_ref[r0:r0 + CHUNK, :] + dtb_ref[...])
        acum = _dot_sel_lhs(tri, dt * a_ref[...])
        acum_t = acum.T
        dt_t = dt.T
        ea = jnp.exp(acum)
        w_t = jnp.exp(acum_t[:, CHUNK - 1:CHUNK] - acum_t) * dt_t
        cd = jnp.broadcast_to(ea[CHUNK - 1:CHUNK, :], (8, LANES))
        cdx = _dot_sel_rhs(cd, e_ref[...])[0:1, :]
        for grp in range(NGROUPS):
            bg = act_ref[r0:r0 + CHUNK, D_SSD + grp * NSTATE:D_SSD + (grp + 1) * NSTATE]
            cg = act_ref[r0:r0 + CHUNK, D_SSD + (NGROUPS + grp) * NSTATE:D_SSD + (NGROUPS + grp + 1) * NSTATE]
            cb = lax.dot_general(cg.astype(BF16), bg.astype(BF16), (((1,), (1,)), ((), ())),
                                 preferred_element_type=F32)
            bg_t = bg.T
            for jp in range(pairs_per_group):
                j = grp * pairs_per_group + jp
                xp = act_ref[r0:r0 + CHUNK, j * LANES:(j + 1) * LANES].astype(BF16)
                sp = s_ref[:, j * LANES:(j + 1) * LANES]
                spb = sp.astype(BF16)
                lhs_parts, rhs_parts, bw_parts, x_parts = [], [], [], []
                for half, keep in enumerate(keeps):
                    h = 2 * j + half
                    xh = xp * keep
                    seg = acum[:, h:h + 1] - acum_t[h:h + 1, :]
                    decay = jnp.exp(jnp.where(causal, seg, neg_inf))
                    m = cb * decay * dt_t[h:h + 1, :]
                    eac = ea[:, h:h + 1] * cg
                    lhs_parts += [m.astype(BF16), eac.astype(BF16)]
                    rhs_parts += [xh, spb * keep]
                    bw_parts.append((bg_t * w_t[h:h + 1, :]).astype(BF16))
                    x_parts.append(xh)
                y_ref[r0:r0 + CHUNK, j * LANES:(j + 1) * LANES] = _dot(
                    jnp.concatenate(lhs_parts, axis=1), jnp.concatenate(rhs_parts, axis=0))
                snew = _dot(jnp.concatenate(bw_parts, axis=1), jnp.concatenate(x_parts, axis=0))
                s_ref[:, j * LANES:(j + 1) * LANES] = cdx[:, j * LANES:(j + 1) * LANES] * sp + snew
                slot = c * (NHEADS // 2) + j
                for i, filler in enumerate(fillers):
                    if slot_of_filler[i] == slot:
                        filler()
    sc_tail = sc_ext[tile + CARRY - 2:tile + CARRY, :]
    scst_ref[0] = sc_tail
    sc_ext[CARRY - 2:CARRY, :] = sc_tail

    gw = D_SSD // NGROUPS
    for grp in range(NGROUPS):
        ssq = jnp.zeros((tile, LANES), F32)
        for off in range(grp * gw, (grp + 1) * gw, LANES):
            yb = ((y_ref[:, off:off + LANES] + dx_ref[:, off:off + LANES] * act_ref[:, off:off + LANES])
                  * _silu(z_ref[:, off:off + LANES]))
            y_ref[:, off:off + LANES] = yb
            ssq = ssq + yb * yb
        scale = lax.rsqrt(jnp.sum(ssq, axis=-1, keepdims=True) * (1.0 / gw) + EPS)
        for off in range(grp * gw, (grp + 1) * gw, LANES):
            ycat_ref[:, off:off + LANES] = (y_ref[:, off:off + LANES] * scale
                                            * gnw_ref[:, off:off + LANES]).astype(BF16)
    ycat_ref[:, D_SSD:D_MIX] = ysc_ref[...]

    @pl.when(ends_sequence)
    def _():
        for j in range(NHEADS // 2):
            ssm_ref[0, 2 * j:2 * j + 2] = s_ref[:, j * LANES:(j + 1) * LANES].T.reshape(2, HEAD_DIM, NSTATE)


def _const_spec(shape):
    nd = len(shape)
    return pl.BlockSpec(shape, lambda *_: (0,) * nd, pipeline_mode=pl.Buffered(1))


def _mixer_prompt(x, nw, win, cw, cb, dtb, a_pad, dx, gnw, scw, wout, expand):
    nb, seq, _ = x.shape
    tile = SEQ_TILE
    nt = seq // tile
    total = nb * nt

    def tile_of(g, lag):
        i = jnp.clip(g - lag, 0, total - 1)
        return i // nt, i % nt

    cur_spec = pl.BlockSpec((1, tile, D_MODEL), lambda g: (*tile_of(g, 0), 0))
    prev_spec = pl.BlockSpec((1, tile, D_MODEL), lambda g: (*tile_of(g, 1), 0))
    consts = (nw, win, cw, cb, dtb, a_pad, dx, gnw, scw, wout, expand)
    return pl.pallas_call(
        functools.partial(_mixer_prompt_kernel, nt),
        grid=(total + 1,),
        in_specs=[cur_spec, prev_spec] + [_const_spec(p.shape) for p in consts],
        out_specs=[
            prev_spec,
            pl.BlockSpec((1, NHEADS, HEAD_DIM, NSTATE), lambda g: (tile_of(g, 0)[0], 0, 0, 0)),
            pl.BlockSpec((1, SSD_CONV - 1, CONV_DIM), lambda g: (tile_of(g, 0)[0], 0, 0)),
            pl.BlockSpec((1, SC_CONV - 1, D_SC), lambda g: (tile_of(g, 0)[0], 0, 0)),
        ],
        out_shape=[
            jax.ShapeDtypeStruct((nb, seq, D_MODEL), F32),
            jax.ShapeDtypeStruct((nb, NHEADS, HEAD_DIM, NSTATE), F32),
            jax.ShapeDtypeStruct((nb, SSD_CONV - 1, CONV_DIM), F32),
            jax.ShapeDtypeStruct((nb, SC_CONV - 1, D_SC), F32),
        ],
        scratch_shapes=[
            pltpu.VMEM((tile + CARRY, CONV_DIM), F32),
            pltpu.VMEM((tile + CARRY, D_SC), F32),
            pltpu.VMEM((tile, D_MODEL), BF16),
            pltpu.VMEM((tile, CONV_DIM), F32),
            pltpu.VMEM((tile, DT_PAD), F32),
            pltpu.VMEM((tile, D_SSD), F32),
            pltpu.VMEM((tile, D_SC), F32),
            pltpu.VMEM((NSTATE, D_SSD), F32),
            pltpu.VMEM((tile, D_SSD), F32),
            pltpu.VMEM((tile, D_SC), BF16),
            pltpu.VMEM((tile, D_MIX), BF16),
        ],
        compiler_params=pltpu.CompilerParams(
            dimension_semantics=("arbitrary",), vmem_limit_bytes=VMEM_LIMIT),
        name="mixer_prompt",
    )(x, x, *consts)


def _ffn_prompt_kernel(x1_ref, nw_ref, wffn_ref, fcw_ref, fcb_ref, wdown_ref, nfw_ref,
                       y_ref, fst_ref, gate_ext):
    tile = x1_ref.shape[1]
    t = pl.program_id(1)

    @pl.when(t == 0)
    def _():
        gate_ext[0:CARRY, :] = jnp.zeros((CARRY, D_FF), F32)

    x1 = x1_ref[0]
    u = _rms_rows(x1, nw_ref[...]).astype(BF16)
    gate_ext[CARRY:CARRY + tile, :] = _dot(u, wffn_ref[:, 0:D_FF])
    up = _dot(u, wffn_ref[:, D_FF:2 * D_FF])
    g = fcb_ref[...] + gate_ext[CARRY - 2:CARRY - 2 + tile, :] * fcw_ref[0:1, :]
    for k in range(1, FFN_CONV):
        g = g + gate_ext[CARRY - 2 + k:CARRY - 2 + k + tile, :] * fcw_ref[k:k + 1, :]
    tail = gate_ext[tile + CARRY - 2:tile + CARRY, :]
    fst_ref[0] = tail
    gate_ext[CARRY - 2:CARRY, :] = tail
    act = (_silu(g) * up).astype(BF16)
    x2 = x1 + _dot(act, wdown_ref[...])
    y_ref[0] = _rms_rows(x2, nfw_ref[...])


def _ffn_prompt(x1, nw, wffn, fcw, fcb, wdown, nfw):
    nb, seq, _ = x1.shape
    tile = SEQ_TILE
    tok_spec = pl.BlockSpec((1, tile, D_MODEL), lambda b, t: (b, t, 0))
    return pl.pallas_call(
        _ffn_prompt_kernel,
        grid=(nb, seq // tile),
        in_specs=[tok_spec] + [_const_spec(p.shape) for p in (nw, wffn, fcw, fcb, wdown, nfw)],
        out_specs=[tok_spec, pl.BlockSpec((1, FFN_CONV - 1, D_FF), lambda b, t: (b, 0, 0))],
        out_shape=[jax.ShapeDtypeStruct((nb, seq, D_MODEL), F32),
                   jax.ShapeDtypeStruct((nb, FFN_CONV - 1, D_FF), F32)],
        scratch_shapes=[pltpu.VMEM((tile + CARRY, D_FF), F32)],
        compiler_params=pltpu.CompilerParams(
            dimension_semantics=("arbitrary", "arbitrary"), vmem_limit_bytes=VMEM_LIMIT),
        name="ffn_prompt",
    )(x1, nw, wffn, fcw, fcb, wdown, nfw)


def _sample_pre_kernel(x_ref, nw_ref, win_ref, cw_ref, cb_ref, dtb_ref, a_ref, scw_ref, e_ref,
                       cst_ref, scst_ref,
                       z_ref, xs_ref, xdt_ref, dax_ref, bc_ref, ysc_ref, ncst_ref, nscst_ref):
    x = x_ref[...]
    u = _rms_rows(x, nw_ref[...]).astype(BF16)
    z_ref[...] = _dot_nt(u, win_ref[OFF_Z:OFF_Z + D_SSD, :])
    xbc = _dot_nt(u, win_ref[OFF_XBC:OFF_XBC + CONV_DIM, :])
    conv = cb_ref[...] + xbc * cw_ref[SSD_CONV - 1:SSD_CONV, :]
    for k in range(SSD_CONV - 1):
        conv = conv + cst_ref[k] * cw_ref[k:k + 1, :]
    for k in range(SSD_CONV - 2):
        ncst_ref[k] = cst_ref[k + 1]
    ncst_ref[SSD_CONV - 2] = xbc
    act = _silu(conv)
    xs = act[:, 0:D_SSD]
    xs_ref[...] = xs
    bc_ref[...] = act[:, D_SSD:CONV_DIM]
    dt = _softplus(_dot_nt(u, win_ref[OFF_DT:OFF_DT + DT_PAD, :]) + dtb_ref[...])
    xdt_ref[...] = xs * _dot_sel_rhs(dt, e_ref[...])
    dax_ref[...] = jnp.exp(_dot_sel_rhs(dt * a_ref[...], e_ref[...]))
    gc = _dot_nt(u, win_ref[OFF_GC:OFF_GC + D_SC, :])
    hh = _dot_nt(u, win_ref[OFF_H:OFF_H + D_SC, :])
    gch = gc * hh
    sc = gch * scw_ref[SC_CONV - 1:SC_CONV, :]
    for k in range(SC_CONV - 1):
        sc = sc + scst_ref[k] * scw_ref[k:k + 1, :]
    for k in range(SC_CONV - 2):
        nscst_ref[k] = scst_ref[k + 1]
    nscst_ref[SC_CONV - 2] = gch
    gb = _dot_nt(u, win_ref[OFF_GB:OFF_GB + D_SC, :])
    ysc_ref[...] = gb * sc


def _sample_pre(x, nw, win, cw, cb, dtb, a_pad, scw, expand, cst, scst):
    nb = x.shape[0]
    outs = [
        jax.ShapeDtypeStruct((nb, D_SSD), F32),
        jax.ShapeDtypeStruct((nb, D_SSD), F32),
        jax.ShapeDtypeStruct((nb, D_SSD), F32),
        jax.ShapeDtypeStruct((nb, D_SSD), F32),
        jax.ShapeDtypeStruct((nb, 2 * NGROUPS * NSTATE), F32),
        jax.ShapeDtypeStruct((nb, D_SC), F32),
        jax.ShapeDtypeStruct((SSD_CONV - 1, nb, CONV_DIM), F32),
        jax.ShapeDtypeStruct((SC_CONV - 1, nb, D_SC), F32),
    ]
    return pl.pallas_call(
        _sample_pre_kernel,
        out_shape=outs,
        compiler_params=pltpu.CompilerParams(vmem_limit_bytes=VMEM_LIMIT),
        name="sample_pre",
    )(x, nw, win, cw, cb, dtb, a_pad, scw, expand, cst, scst)


SAMPLE_BT = 8


def _sample_state_kernel(st_ref, xdt_t_ref, dax_t_ref, bc_ref, nst_ref, y_ref):
    rows = NHEADS * HEAD_DIM
    half = rows // NGROUPS
    for b in range(SAMPLE_BT):
        hst = st_ref[b].reshape(rows, NSTATE)
        dcol = dax_t_ref[0, :, b:b + 1]
        xcol = xdt_t_ref[0, :, b:b + 1]
        brow = jnp.concatenate(
            [jnp.broadcast_to(bc_ref[b:b + 1, g * NSTATE:(g + 1) * NSTATE], (half, NSTATE))
             for g in range(NGROUPS)], axis=0)
        hnew = hst * dcol + xcol * brow
        nst_ref[b] = hnew.reshape(NHEADS, HEAD_DIM, NSTATE)
        cmat = jnp.concatenate(
            [bc_ref[b:b + 1, (NGROUPS + g) * NSTATE:(NGROUPS + g + 1) * NSTATE] for g in range(NGROUPS)]
            + [jnp.zeros((8 - NGROUPS, NSTATE), F32)], axis=0)
        yr = lax.dot_general(cmat.astype(BF16), hnew.astype(BF16), (((1,), (1,)), ((), ())),
                             preferred_element_type=F32)
        lane = lax.broadcasted_iota(jnp.int32, (1, rows), 1)
        y_ref[b:b + 1, :] = jnp.where(lane < half, yr[0:1, :], yr[1:2, :])


def _sample_state(state, xdt_t, dax_t, bc):
    nb = state.shape[0]
    rows = NHEADS * HEAD_DIM
    st_spec = pl.BlockSpec((SAMPLE_BT, NHEADS, HEAD_DIM, NSTATE), lambda i: (i, 0, 0, 0))
    col_spec = pl.BlockSpec((1, rows, SAMPLE_BT), lambda i: (i, 0, 0))
    return pl.pallas_call(
        _sample_state_kernel,
        grid=(nb // SAMPLE_BT,),
        in_specs=[st_spec, col_spec, col_spec,
                  pl.BlockSpec((SAMPLE_BT, 2 * NGROUPS * NSTATE), lambda i: (i, 0))],
        out_specs=[st_spec, pl.BlockSpec((SAMPLE_BT, rows), lambda i: (i, 0))],
        out_shape=[jax.ShapeDtypeStruct(state.shape, F32), jax.ShapeDtypeStruct((nb, rows), F32)],
        compiler_params=pltpu.CompilerParams(
            dimension_semantics=("arbitrary",), vmem_limit_bytes=VMEM_LIMIT),
        name="sample_state",
    )(state, xdt_t, dax_t, bc)


def _sample_post_kernel(x_ref, y_ref, xs_ref, z_ref, ysc_ref, dx_ref, gnw_ref, wout_ref,
                        nw_ref, wffn_ref, fcw_ref, fcb_ref, wdown_ref, nfw_ref, fst_ref,
                        out_ref, nfst_ref):
    y = (y_ref[...] + dx_ref[...] * xs_ref[...]) * _silu(z_ref[...])
    gw = D_SSD // NGROUPS
    parts = [_rms_rows(y[:, g * gw:(g + 1) * gw], gnw_ref[:, g * gw:(g + 1) * gw]) for g in range(NGROUPS)]
    ycat = jnp.concatenate(parts + [ysc_ref[...]], axis=1).astype(BF16)
    x1 = x_ref[...] + _dot(ycat, wout_ref[...])
    u = _rms_rows(x1, nw_ref[...]).astype(BF16)
    gate = _dot(u, wffn_ref[:, 0:D_FF])
    up = _dot(u, wffn_ref[:, D_FF:2 * D_FF])
    g = fcb_ref[...] + gate * fcw_ref[FFN_CONV - 1:FFN_CONV, :]
    for k in range(FFN_CONV - 1):
        g = g + fst_ref[k] * fcw_ref[k:k + 1, :]
    for k in range(FFN_CONV - 2):
        nfst_ref[k] = fst_ref[k + 1]
    nfst_ref[FFN_CONV - 2] = gate
    act = (_silu(g) * up).astype(BF16)
    x2 = x1 + _dot(act, wdown_ref[...])
    out_ref[...] = _rms_rows(x2, nfw_ref[...])


def _sample_post(x, y, xs, z, ysc, dx, gnw, wout, nw, wffn, fcw, fcb, wdown, nfw, fst):
    nb = x.shape[0]
    return pl.pallas_call(
        _sample_post_kernel,
        out_shape=[jax.ShapeDtypeStruct((nb, D_MODEL), F32),
                   jax.ShapeDtypeStruct((FFN_CONV - 1, nb, D_FF), F32)],
        compiler_params=pltpu.CompilerParams(vmem_limit_bytes=VMEM_LIMIT),
        name="sample_post",
    )(x, y, xs, z, ysc, dx, gnw, wout, nw, wffn, fcw, fcb, wdown, nfw, fst)


def kernel(x_prompt, x_sample, state_ssm, state_ssd_conv, state_short_conv, state_ffn_conv,
           norm_mix_w, w_in, ssd_conv_w, ssd_conv_b, ssd_dt_bias, ssd_a_log, ssd_d, ssd_norm_w,
           sc_conv_w, w_out, norm_ffn_w, w_ffn_in, ffn_conv_w, ffn_conv_b, w_down, norm_final_w):
    depth = w_in.shape[0]
    assert depth == 1
    win = jnp.swapaxes(w_in[0], 0, 1).astype(BF16)
    wout = w_out[0].astype(BF16)
    wffn = w_ffn_in[0].astype(BF16)
    wdown = w_down[0].astype(BF16)
    nw = norm_mix_w[0].reshape(1, D_MODEL)
    nw2 = norm_ffn_w[0].reshape(1, D_MODEL)
    nfw = norm_final_w.reshape(1, D_MODEL)
    cw = ssd_conv_w[0]
    cb = ssd_conv_b[0].reshape(1, CONV_DIM)
    dtb = jnp.pad(ssd_dt_bias[0], (0, DT_PAD - NHEADS)).reshape(1, DT_PAD)
    a_pad = jnp.pad(-jnp.exp(ssd_a_log[0]), (0, DT_PAD - NHEADS)).reshape(1, DT_PAD)
    dx = jnp.repeat(ssd_d[0], HEAD_DIM).reshape(1, D_SSD)
    gnw = ssd_norm_w[0].reshape(1, D_SSD)
    scw = sc_conv_w[0]
    fcw = ffn_conv_w[0]
    fcb = ffn_conv_b[0].reshape(1, D_FF)
    head_of_lane = jnp.arange(D_SSD, dtype=jnp.int32) // HEAD_DIM
    expand = (jnp.arange(DT_PAD, dtype=jnp.int32)[:, None] == head_of_lane[None, :]).astype(BF16)

    x1, p_ssm, p_cst, p_scst = _mixer_prompt(x_prompt, nw, win, cw, cb, dtb, a_pad, dx, gnw, scw, wout, expand)
    y_prompt, p_fst = _ffn_prompt(x1, nw2, wffn, fcw, fcb, wdown, nfw)

    nbs = x_sample.shape[0]
    xs_in = x_sample.reshape(nbs, D_MODEL)
    cst = jnp.swapaxes(state_ssd_conv[0], 0, 1)
    scst = jnp.swapaxes(state_short_conv[0], 0, 1)
    fst = jnp.swapaxes(state_ffn_conv[0], 0, 1)
    z, xs, xdt, dax, bc, ysc, ncst, nscst = _sample_pre(xs_in, nw, win, cw, cb, dtb, a_pad, scw, expand, cst, scst)
    ntile = nbs // SAMPLE_BT
    to_cols = lambda v: jnp.transpose(v.reshape(ntile, SAMPLE_BT, D_SSD), (0, 2, 1))
    s_ssm, y_s = _sample_state(state_ssm[0], to_cols(xdt), to_cols(dax), bc)
    y_sample, nfst = _sample_post(xs_in, y_s, xs, z, ysc, dx, gnw, wout, nw2, wffn, fcw, fcb, wdown, nfw, fst)

    return (y_prompt, y_sample.reshape(nbs, 1, D_MODEL),
            p_ssm[None], p_cst[None], p_scst[None], p_fst[None],
            s_ssm[None], jnp.swapaxes(ncst, 0, 1)[None], jnp.swapaxes(nscst, 0, 1)[None],
            jnp.swapaxes(nfst, 0, 1)[None])
```

```python
import functools

import jax
import jax.numpy as jnp
from jax import lax
from jax.experimental import pallas as pl
from jax.experimental.pallas import tpu as pltpu

D_MODEL = 1024
D_SSD = 1024
D_SC = 1024
NHEADS = 16
HEAD_DIM = 64
NGROUPS = 2
NSTATE = 128
SSD_CONV = 4
CONV_DIM = D_SSD + 2 * NGROUPS * NSTATE
SC_CONV = 3
D_FF = 2816
FFN_CONV = 3
EPS = 1e-5
D_MIX = D_SSD + D_SC
D_IN_PROJ = D_SSD + CONV_DIM + NHEADS + 3 * D_SC

LANES = 128
MXU_COLS = 256
OFF_Z = 0
OFF_XBC = OFF_Z + D_SSD
OFF_DT = OFF_XBC + CONV_DIM
OFF_GB = OFF_DT + NHEADS
OFF_GC = OFF_GB + D_SC
OFF_H = OFF_GC + D_SC
DT_PAD = LANES

CHUNK = 128
CARRY = 8
SEQ_TILE = 512
VMEM_LIMIT = 56 * 1024 * 1024

F32 = jnp.float32
BF16 = jnp.bfloat16


def _dot(a, b):
    return jnp.dot(a, b, preferred_element_type=F32)


def _dot_nt(a, b_t):
    return lax.dot_general(a, b_t, (((1,), (1,)), ((), ())), preferred_element_type=F32)


def _split3(v):
    hi = v.astype(BF16)
    r = v - hi.astype(F32)
    mid = r.astype(BF16)
    lo = (r - mid.astype(F32)).astype(BF16)
    return hi, mid, lo


def _dot_sel_rhs(v, sel):
    hi, mid, lo = _split3(v)
    return (_dot(hi, sel) + _dot(mid, sel)) + _dot(lo, sel)


def _dot_sel_lhs(sel, v):
    hi, mid, lo = _split3(v)
    return (_dot(sel, hi) + _dot(sel, mid)) + _dot(sel, lo)


def _silu(v):
    h = 0.5 * v
    return h + h * jnp.tanh(h)


def _softplus(v):
    return jnp.maximum(v, 0.0) + jnp.log1p(jnp.exp(-jnp.abs(v)))


def _rms_rows(x, w):
    ms = jnp.mean(x * x, axis=-1, keepdims=True)
    return x * lax.rsqrt(ms + EPS) * w


def _mixer_prompt_kernel(tiles_per_seq,
                         x_ref, xres_ref, nw_ref, win_ref, cw_ref, cb_ref, dtb_ref, a_ref, dx_ref,
                         gnw_ref, scw_ref, wout_ref, e_ref,
                         x1_ref, ssm_ref, cst_ref, scst_ref,
                         xbc_ext, sc_ext, u_ref, act_ref, dt_ref, z_ref, gb_ref, s_ref, y_ref, ysc_ref, ycat_ref):
    tile = x_ref.shape[1]
    g = pl.program_id(0)
    last = pl.num_programs(0) - 1
    t = lax.rem(g, tiles_per_seq)

    @pl.when(g == 0)
    def _():
        ycat_ref[...] = jnp.zeros_like(ycat_ref)

    @pl.when(t == 0)
    def _():
        xbc_ext[0:CARRY, :] = jnp.zeros((CARRY, CONV_DIM), F32)
        sc_ext[0:CARRY, :] = jnp.zeros((CARRY, D_SC), F32)
        s_ref[...] = jnp.zeros_like(s_ref)

    def out_block(off):
        x1_ref[0, :, off:off + MXU_COLS] = xres_ref[0, :, off:off + MXU_COLS] + _dot(
            ycat_ref[...], wout_ref[:, off:off + MXU_COLS])

    @pl.when(g == last)
    def _():
        for off in range(0, D_MODEL, MXU_COLS):
            out_block(off)

    @pl.when(g < last)
    def _():
        _mixer_tile(tile, t == tiles_per_seq - 1, out_block,
                    x_ref, nw_ref, win_ref, cw_ref, cb_ref, dtb_ref, a_ref, dx_ref, gnw_ref, scw_ref, e_ref,
                    ssm_ref, cst_ref, scst_ref,
                    xbc_ext, sc_ext, u_ref, act_ref, dt_ref, z_ref, gb_ref, s_ref, y_ref, ysc_ref, ycat_ref)


def _mixer_tile(tile, ends_sequence, out_block,
                x_ref, nw_ref, win_ref, cw_ref, cb_ref, dtb_ref, a_ref, dx_ref, gnw_ref, scw_ref, e_ref,
                ssm_ref, cst_ref, scst_ref,
                xbc_ext, sc_ext, u_ref, act_ref, dt_ref, z_ref, gb_ref, s_ref, y_ref, ysc_ref, ycat_ref):
    u_ref[...] = _rms_rows(x_ref[0], nw_ref[...]).astype(BF16)

    def project(dst, row0, base, off, width=MXU_COLS):
        dst[row0:row0 + tile, off:off + width] = _dot_nt(u_ref[...], win_ref[base + off:base + off + width, :])

    def conv_block(off):
        conv = cb_ref[:, off:off + LANES] + (
            xbc_ext[CARRY - 3:CARRY - 3 + tile, off:off + LANES] * cw_ref[0:1, off:off + LANES])
        for k in range(1, SSD_CONV):
            conv = conv + (xbc_ext[CARRY - 3 + k:CARRY - 3 + k + tile, off:off + LANES]
                           * cw_ref[k:k + 1, off:off + LANES])
        act_ref[:, off:off + LANES] = _silu(conv)

    def sc_block(off):
        sc = sc_ext[CARRY - 2:CARRY - 2 + tile, off:off + LANES] * scw_ref[0:1, off:off + LANES]
        for k in range(1, SC_CONV):
            sc = sc + (sc_ext[CARRY - 2 + k:CARRY - 2 + k + tile, off:off + LANES]
                       * scw_ref[k:k + 1, off:off + LANES])
        ysc_ref[:, off:off + LANES] = (gb_ref[:, off:off + LANES] * sc).astype(BF16)

    for off in range(0, CONV_DIM, MXU_COLS):
        project(xbc_ext, CARRY, OFF_XBC, off)
        for sub in range(off, off + MXU_COLS, LANES):
            conv_block(sub)
    tail = xbc_ext[tile + CARRY - 3:tile + CARRY, :]
    cst_ref[0] = tail
    xbc_ext[CARRY - 3:CARRY, :] = tail
    project(dt_ref, 0, OFF_DT, 0, DT_PAD)
    for off in range(0, D_SC, MXU_COLS):
        u = u_ref[...]
        sc_ext[CARRY:CARRY + tile, off:off + MXU_COLS] = (
            _dot_nt(u, win_ref[OFF_GC + off:OFF_GC + off + MXU_COLS, :])
            * _dot_nt(u, win_ref[OFF_H + off:OFF_H + off + MXU_COLS, :]))

    fillers = []
    for k in range(D_SC // MXU_COLS):
        fillers.append(lambda k=k: out_block(k * MXU_COLS))
        fillers.append(lambda k=k: project(gb_ref, 0, OFF_GB, k * MXU_COLS))
        fillers.append(lambda k=k: sc_block(k * MXU_COLS))
        fillers.append(lambda k=k: sc_block(k * MXU_COLS + LANES))
        fillers.append(lambda k=k: project(z_ref, 0, OFF_Z, k * MXU_COLS))
    n_slots = (tile // CHUNK) * (NHEADS // 2)
    slot_of_filler = [(i * n_slots) // len(fillers) for i in range(len(fillers))]

    row_i = lax.broadcasted_iota(jnp.int32, (CHUNK, CHUNK), 0)
    col_i = lax.broadcasted_iota(jnp.int32, (CHUNK, CHUNK), 1)
    causal = row_i >= col_i
    tri = causal.astype(BF16)
    lane_i = lax.broadcasted_iota(jnp.int32, (1, LANES), 1)
    keeps = ((lane_i < HEAD_DIM).astype(BF16), (lane_i >= HEAD_DIM).astype(BF16))
    neg_inf = jnp.float32(-jnp.inf)
    pairs_per_group = NHEADS // NGROUPS // 2

    for c in range(tile // CHUNK):
        r0 = c * CHUNK
        dt = _softplus(dt_ref[r0:r0 + CHUNK, :] + dtb_ref[...])
        acum = _dot_sel_lhs(tri, dt * a_ref[...])
        acum_t = acum.T
        dt_t = dt.T
        ea = jnp.exp(acum)
        w_t = jnp.exp(acum_t[:, CHUNK - 1:CHUNK] - acum_t) * dt_t
        cd = jnp.broadcast_to(ea[CHUNK - 1:CHUNK, :], (8, LANES))
        cdx = _dot_sel_rhs(cd, e_ref[...])[0:1, :]
        for grp in range(NGROUPS):
            bg = act_ref[r0:r0 + CHUNK, D_SSD + grp * NSTATE:D_SSD + (grp + 1) * NSTATE]
            cg = act_ref[r0:r0 + CHUNK, D_SSD + (NGROUPS + grp) * NSTATE:D_SSD + (NGROUPS + grp + 1) * NSTATE]
            cb = lax.dot_general(cg.astype(BF16), bg.astype(BF16), (((1,), (1,)), ((), ())),
                                 preferred_element_type=F32)
            bg_t = bg.T
            for jp in range(pairs_per_group):
                j = grp * pairs_per_group + jp
                xp = act_ref[r0:r0 + CHUNK, j * LANES:(j + 1) * LANES].astype(BF16)
                sp = s_ref[:, j * LANES:(j + 1) * LANES]
                spb = sp.astype(BF16)
                lhs_parts, rhs_parts, bw_parts, x_parts = [], [], [], []
                for half, keep in enumerate(keeps):
                    h = 2 * j + half
                    xh = xp * keep
                    seg = acum[:, h:h + 1] - acum_t[h:h + 1, :]
                    decay = jnp.exp(jnp.where(causal, seg, neg_inf))
                    m = cb * decay * dt_t[h:h + 1, :]
                    eac = ea[:, h:h + 1] * cg
                    lhs_parts += [m.astype(BF16), eac.astype(BF16)]
                    rhs_parts += [xh, spb * keep]
                    bw_parts.append((bg_t * w_t[h:h + 1, :]).astype(BF16))
                    x_parts.append(xh)
                y_ref[r0:r0 + CHUNK, j * LANES:(j + 1) * LANES] = _dot(
                    jnp.concatenate(lhs_parts, axis=1), jnp.concatenate(rhs_parts, axis=0))
                snew = _dot(jnp.concatenate(bw_parts, axis=1), jnp.concatenate(x_parts, axis=0))
                s_ref[:, j * LANES:(j + 1) * LANES] = cdx[:, j * LANES:(j + 1) * LANES] * sp + snew
                slot = c * (NHEADS // 2) + j
                for i, filler in enumerate(fillers):
                    if slot_of_filler[i] == slot:
                        filler()
    sc_tail = sc_ext[tile + CARRY - 2:tile + CARRY, :]
    scst_ref[0] = sc_tail
    sc_ext[CARRY - 2:CARRY, :] = sc_tail

    gw = D_SSD // NGROUPS
    for grp in range(NGROUPS):
        ssq = jnp.zeros((tile, LANES), F32)
        for off in range(grp * gw, (grp + 1) * gw, LANES):
            yb = ((y_ref[:, off:off + LANES] + dx_ref[:, off:off + LANES] * act_ref[:, off:off + LANES])
                  * _silu(z_ref[:, off:off + LANES]))
            y_ref[:, off:off + LANES] = yb
            ssq = ssq + yb * yb
        scale = lax.rsqrt(jnp.sum(ssq, axis=-1, keepdims=True) * (1.0 / gw) + EPS)
        for off in range(grp * gw, (grp + 1) * gw, LANES):
            ycat_ref[:, off:off + LANES] = (y_ref[:, off:off + LANES] * scale
                                            * gnw_ref[:, off:off + LANES]).astype(BF16)
    ycat_ref[:, D_SSD:D_MIX] = ysc_ref[...]

    @pl.when(ends_sequence)
    def _():
        for j in range(NHEADS // 2):
            ssm_ref[0, 2 * j:2 * j + 2] = s_ref[:, j * LANES:(j + 1) * LANES].T.reshape(2, HEAD_DIM, NSTATE)


def _const_spec(shape):
    nd = len(shape)
    return pl.BlockSpec(shape, lambda *_: (0,) * nd, pipeline_mode=pl.Buffered(1))


def _mixer_prompt(x, nw, win, cw, cb, dtb, a_pad, dx, gnw, scw, wout, expand):
    nb, seq, _ = x.shape
    tile = SEQ_TILE
    nt = seq // tile
    total = nb * nt

    def tile_of(g, lag):
        i = jnp.clip(g - lag, 0, total - 1)
        return i // nt, i % nt

    cur_spec = pl.BlockSpec((1, tile, D_MODEL), lambda g: (*tile_of(g, 0), 0))
    prev_spec = pl.BlockSpec((1, tile, D_MODEL), lambda g: (*tile_of(g, 1), 0))
    consts = (nw, win, cw, cb, dtb, a_pad, dx, gnw, scw, wout, expand)
    return pl.pallas_call(
        functools.partial(_mixer_prompt_kernel, nt),
        grid=(total + 1,),
        in_specs=[cur_spec, prev_spec] + [_const_spec(p.shape) for p in consts],
        out_specs=[
            prev_spec,
            pl.BlockSpec((1, NHEADS, HEAD_DIM, NSTATE), lambda g: (tile_of(g, 0)[0], 0, 0, 0)),
            pl.BlockSpec((1, SSD_CONV - 1, CONV_DIM), lambda g: (tile_of(g, 0)[0], 0, 0)),
            pl.BlockSpec((1, SC_CONV - 1, D_SC), lambda g: (tile_of(g, 0)[0], 0, 0)),
        ],
        out_shape=[
            jax.ShapeDtypeStruct((nb, seq, D_MODEL), F32),
            jax.ShapeDtypeStruct((nb, NHEADS, HEAD_DIM, NSTATE), F32),
            jax.ShapeDtypeStruct((nb, SSD_CONV - 1, CONV_DIM), F32),
            jax.ShapeDtypeStruct((nb, SC_CONV - 1, D_SC), F32),
        ],
        scratch_shapes=[
            pltpu.VMEM((tile + CARRY, CONV_DIM), F32),
            pltpu.VMEM((tile + CARRY, D_SC), F32),
            pltpu.VMEM((tile, D_MODEL), BF16),
            pltpu.VMEM((tile, CONV_DIM), F32),
            pltpu.VMEM((tile, DT_PAD), F32),
            pltpu.VMEM((tile, D_SSD), F32),
            pltpu.VMEM((tile, D_SC), F32),
            pltpu.VMEM((NSTATE, D_SSD), F32),
            pltpu.VMEM((tile, D_SSD), F32),
            pltpu.VMEM((tile, D_SC), BF16),
            pltpu.VMEM((tile, D_MIX), BF16),
        ],
        compiler_params=pltpu.CompilerParams(
            dimension_semantics=("arbitrary",), vmem_limit_bytes=VMEM_LIMIT),
        name="mixer_prompt",
    )(x, x, *consts)


def _ffn_prompt_kernel(x1_ref, nw_ref, wffn_ref, fcw_ref, fcb_ref, wdown_ref, nfw_ref,
                       y_ref, fst_ref, gate_ext):
    tile = x1_ref.shape[1]
    t = pl.program_id(1)

    @pl.when(t == 0)
    def _():
        gate_ext[0:CARRY, :] = jnp.zeros((CARRY, D_FF), F32)

    x1 = x1_ref[0]
    u = _rms_rows(x1, nw_ref[...]).astype(BF16)
    gate_ext[CARRY:CARRY + tile, :] = _dot(u, wffn_ref[:, 0:D_FF])
    up = _dot(u, wffn_ref[:, D_FF:2 * D_FF])
    g = fcb_ref[...] + gate_ext[CARRY - 2:CARRY - 2 + tile, :] * fcw_ref[0:1, :]
    for k in range(1, FFN_CONV):
        g = g + gate_ext[CARRY - 2 + k:CARRY - 2 + k + tile, :] * fcw_ref[k:k + 1, :]
    tail = gate_ext[tile + CARRY - 2:tile + CARRY, :]
    fst_ref[0] = tail
    gate_ext[CARRY - 2:CARRY, :] = tail
    act = (_silu(g) * up).astype(BF16)
    x2 = x1 + _dot(act, wdown_ref[...])
    y_ref[0] = _rms_rows(x2, nfw_ref[...])


def _ffn_prompt(x1, nw, wffn, fcw, fcb, wdown, nfw):
    nb, seq, _ = x1.shape
    tile = SEQ_TILE
    tok_spec = pl.BlockSpec((1, tile, D_MODEL), lambda b, t: (b, t, 0))
    return pl.pallas_call(
        _ffn_prompt_kernel,
        grid=(nb, seq // tile),
        in_specs=[tok_spec] + [_const_spec(p.shape) for p in (nw, wffn, fcw, fcb, wdown, nfw)],
        out_specs=[tok_spec, pl.BlockSpec((1, FFN_CONV - 1, D_FF), lambda b, t: (b, 0, 0))],
        out_shape=[jax.ShapeDtypeStruct((nb, seq, D_MODEL), F32),
                   jax.ShapeDtypeStruct((nb, FFN_CONV - 1, D_FF), F32)],
        scratch_shapes=[pltpu.VMEM((tile + CARRY, D_FF), F32)],
        compiler_params=pltpu.CompilerParams(
            dimension_semantics=("arbitrary", "arbitrary"), vmem_limit_bytes=VMEM_LIMIT),
        name="ffn_prompt",
    )(x1, nw, wffn, fcw, fcb, wdown, nfw)


def _sample_pre_kernel(x_ref, nw_ref, win_ref, cw_ref, cb_ref, dtb_ref, a_ref, scw_ref, e_ref,
                       cst_ref, scst_ref,
                       z_ref, xs_ref, xdt_t_ref, da_ref, bc_ref, ysc_ref, ncst_ref, nscst_ref):
    x = x_ref[...]
    u = _rms_rows(x, nw_ref[...]).astype(BF16)
    z_ref[...] = _dot_nt(u, win_ref[OFF_Z:OFF_Z + D_SSD, :])
    xbc = _dot_nt(u, win_ref[OFF_XBC:OFF_XBC + CONV_DIM, :])
    conv = cb_ref[...] + xbc * cw_ref[SSD_CONV - 1:SSD_CONV, :]
    for k in range(SSD_CONV - 1):
        conv = conv + cst_ref[:, k, :] * cw_ref[k:k + 1, :]
    for k in range(SSD_CONV - 2):
        ncst_ref[:, k, :] = cst_ref[:, k + 1, :]
    ncst_ref[:, SSD_CONV - 2, :] = xbc
    act = _silu(conv)
    xs = act[:, 0:D_SSD]
    xs_ref[...] = xs
    for i in range(2 * NGROUPS):
        bc_ref[i] = act[:, D_SSD + i * NSTATE:D_SSD + (i + 1) * NSTATE]
    dt = _softplus(_dot_nt(u, win_ref[OFF_DT:OFF_DT + DT_PAD, :]) + dtb_ref[...])
    xdt_t_ref[...] = (xs * _dot_sel_rhs(dt, e_ref[...])).T
    da_ref[...] = jnp.exp(dt * a_ref[...])[:, 0:NHEADS]
    gc = _dot_nt(u, win_ref[OFF_GC:OFF_GC + D_SC, :])
    hh = _dot_nt(u, win_ref[OFF_H:OFF_H + D_SC, :])
    gch = gc * hh
    sc = gch * scw_ref[SC_CONV - 1:SC_CONV, :]
    for k in range(SC_CONV - 1):
        sc = sc + scst_ref[:, k, :] * scw_ref[k:k + 1, :]
    for k in range(SC_CONV - 2):
        nscst_ref[:, k, :] = scst_ref[:, k + 1, :]
    nscst_ref[:, SC_CONV - 2, :] = gch
    gb = _dot_nt(u, win_ref[OFF_GB:OFF_GB + D_SC, :])
    ysc_ref[...] = gb * sc


def _sample_pre(x, nw, win, cw, cb, dtb, a_pad, scw, expand, cst, scst):
    nb = x.shape[0]
    outs = [
        jax.ShapeDtypeStruct((nb, D_SSD), F32),
        jax.ShapeDtypeStruct((nb, D_SSD), F32),
        jax.ShapeDtypeStruct((D_SSD, nb), F32),
        jax.ShapeDtypeStruct((nb, NHEADS), F32),
        jax.ShapeDtypeStruct((2 * NGROUPS, nb, NSTATE), F32),
        jax.ShapeDtypeStruct((nb, D_SC), F32),
        jax.ShapeDtypeStruct((nb, SSD_CONV - 1, CONV_DIM), F32),
        jax.ShapeDtypeStruct((nb, SC_CONV - 1, D_SC), F32),
    ]
    return pl.pallas_call(
        _sample_pre_kernel,
        out_shape=outs,
        compiler_params=pltpu.CompilerParams(vmem_limit_bytes=VMEM_LIMIT),
        name="sample_pre",
    )(x, nw, win, cw, cb, dtb, a_pad, scw, expand, cst, scst)


SAMPLE_HEADS = 2


def _sample_state_kernel(da_ref, st_ref, xdt_t_ref, b_ref, c_ref, nst_ref, y_ref):
    nb = st_ref.shape[0]
    head0 = pl.program_id(0) * SAMPLE_HEADS
    rows = SAMPLE_HEADS * HEAD_DIM
    c_t = c_ref[0].T.astype(BF16)
    lane = lax.broadcasted_iota(jnp.int32, (rows, nb), 1)
    ycols = jnp.zeros((rows, nb), F32)
    for b in range(nb):
        upd = xdt_t_ref[:, b:b + 1] * b_ref[0, b:b + 1, :]
        parts = []
        for hh in range(SAMPLE_HEADS):
            hnew = st_ref[b, hh] * da_ref[b, head0 + hh] + upd[hh * HEAD_DIM:(hh + 1) * HEAD_DIM, :]
            nst_ref[b, hh] = hnew
            parts.append(hnew)
        prod = _dot(jnp.concatenate(parts, axis=0).astype(BF16), c_t)
        ycols = jnp.where(lane == b, prod, ycols)
    y_ref[...] = ycols.T


def _sample_state(state, da, xdt_t, bc):
    nb = state.shape[0]
    rows = SAMPLE_HEADS * HEAD_DIM
    heads_per_group = NHEADS // NGROUPS
    st_spec = pl.BlockSpec((nb, SAMPLE_HEADS, HEAD_DIM, NSTATE), lambda i: (0, i, 0, 0))
    return pl.pallas_call(
        _sample_state_kernel,
        grid=(NHEADS // SAMPLE_HEADS,),
        in_specs=[pl.BlockSpec(memory_space=pltpu.SMEM),
                  st_spec,
                  pl.BlockSpec((rows, nb), lambda i: (i, 0)),
                  pl.BlockSpec((1, nb, NSTATE), lambda i: (i * SAMPLE_HEADS // heads_per_group, 0, 0)),
                  pl.BlockSpec((1, nb, NSTATE), lambda i: (NGROUPS + i * SAMPLE_HEADS // heads_per_group, 0, 0))],
        out_specs=[st_spec, pl.BlockSpec((nb, rows), lambda i: (0, i))],
        out_shape=[jax.ShapeDtypeStruct(state.shape, F32), jax.ShapeDtypeStruct((nb, NHEADS * HEAD_DIM), F32)],
        compiler_params=pltpu.CompilerParams(
            dimension_semantics=("arbitrary",), vmem_limit_bytes=VMEM_LIMIT),
        name="sample_state",
    )(da, state, xdt_t, bc, bc)


def _sample_post_kernel(x_ref, y_ref, xs_ref, z_ref, ysc_ref, dx_ref, gnw_ref, wout_ref,
                        nw_ref, wffn_ref, fcw_ref, fcb_ref, wdown_ref, nfw_ref, fst_ref,
                        out_ref, nfst_ref):
    y = (y_ref[...] + dx_ref[...] * xs_ref[...]) * _silu(z_ref[...])
    gw = D_SSD // NGROUPS
    parts = [_rms_rows(y[:, g * gw:(g + 1) * gw], gnw_ref[:, g * gw:(g + 1) * gw]) for g in range(NGROUPS)]
    ycat = jnp.concatenate(parts + [ysc_ref[...]], axis=1).astype(BF16)
    x1 = x_ref[...] + _dot(ycat, wout_ref[...])
    u = _rms_rows(x1, nw_ref[...]).astype(BF16)
    gate = _dot(u, wffn_ref[:, 0:D_FF])
    up = _dot(u, wffn_ref[:, D_FF:2 * D_FF])
    g = fcb_ref[...] + gate * fcw_ref[FFN_CONV - 1:FFN_CONV, :]
    for k in range(FFN_CONV - 1):
        g = g + fst_ref[:, k, :] * fcw_ref[k:k + 1, :]
    for k in range(FFN_CONV - 2):
        nfst_ref[:, k, :] = fst_ref[:, k + 1, :]
    nfst_ref[:, FFN_CONV - 2, :] = gate
    act = (_silu(g) * up).astype(BF16)
    x2 = x1 + _dot(act, wdown_ref[...])
    out_ref[...] = _rms_rows(x2, nfw_ref[...])


def _sample_post(x, y, xs, z, ysc, dx, gnw, wout, nw, wffn, fcw, fcb, wdown, nfw, fst):
    nb = x.shape[0]
    return pl.pallas_call(
        _sample_post_kernel,
        out_shape=[jax.ShapeDtypeStruct((nb, D_MODEL), F32),
                   jax.ShapeDtypeStruct((nb, FFN_CONV - 1, D_FF), F32)],
        compiler_params=pltpu.CompilerParams(vmem_limit_bytes=VMEM_LIMIT),
        name="sample_post",
    )(x, y, xs, z, ysc, dx, gnw, wout, nw, wffn, fcw, fcb, wdown, nfw, fst)


def kernel(x_prompt, x_sample, state_ssm, state_ssd_conv, state_short_conv, state_ffn_conv,
           norm_mix_w, w_in, ssd_conv_w, ssd_conv_b, ssd_dt_bias, ssd_a_log, ssd_d, ssd_norm_w,
           sc_conv_w, w_out, norm_ffn_w, w_ffn_in, ffn_conv_w, ffn_conv_b, w_down, norm_final_w):
    depth = w_in.shape[0]
    assert depth == 1
    win = jnp.swapaxes(w_in[0], 0, 1).astype(BF16)
    wout = w_out[0].astype(BF16)
    wffn = w_ffn_in[0].astype(BF16)
    wdown = w_down[0].astype(BF16)
    nw = norm_mix_w[0].reshape(1, D_MODEL)
    nw2 = norm_ffn_w[0].reshape(1, D_MODEL)
    nfw = norm_final_w.reshape(1, D_MODEL)
    cw = ssd_conv_w[0]
    cb = ssd_conv_b[0].reshape(1, CONV_DIM)
    dtb = jnp.pad(ssd_dt_bias[0], (0, DT_PAD - NHEADS)).reshape(1, DT_PAD)
    a_pad = jnp.pad(-jnp.exp(ssd_a_log[0]), (0, DT_PAD - NHEADS)).reshape(1, DT_PAD)
    dx = jnp.repeat(ssd_d[0], HEAD_DIM).reshape(1, D_SSD)
    gnw = ssd_norm_w[0].reshape(1, D_SSD)
    scw = sc_conv_w[0]
    fcw = ffn_conv_w[0]
    fcb = ffn_conv_b[0].reshape(1, D_FF)
    head_of_lane = jnp.arange(D_SSD, dtype=jnp.int32) // HEAD_DIM
    expand = (jnp.arange(DT_PAD, dtype=jnp.int32)[:, None] == head_of_lane[None, :]).astype(BF16)

    x1, p_ssm, p_cst, p_scst = _mixer_prompt(x_prompt, nw, win, cw, cb, dtb, a_pad, dx, gnw, scw, wout, expand)
    y_prompt, p_fst = _ffn_prompt(x1, nw2, wffn, fcw, fcb, wdown, nfw)

    nbs = x_sample.shape[0]
    xs_in = x_sample.reshape(nbs, D_MODEL)
    cst, scst, fst = state_ssd_conv[0], state_short_conv[0], state_ffn_conv[0]
    z, xs, xdt_t, da, bc, ysc, ncst, nscst = _sample_pre(xs_in, nw, win, cw, cb, dtb, a_pad, scw, expand, cst, scst)
    s_ssm, y_s = _sample_state(state_ssm[0], da, xdt_t, bc)
    y_sample, nfst = _sample_post(xs_in, y_s, xs, z, ysc, dx, gnw, wout, nw2, wffn, fcw, fcb, wdown, nfw, fst)

    return (y_prompt, y_sample.reshape(nbs, 1, D_MODEL),
            p_ssm[None], p_cst[None], p_scst[None], p_fst[None],
            s_ssm[None], ncst[None], nscst[None], nfst[None])
```

```python
import functools

import jax
import jax.numpy as jnp
from jax import lax
from jax.experimental import pallas as pl
from jax.experimental.pallas import tpu as pltpu

D_MODEL = 1024
D_SSD = 1024
D_SC = 1024
NHEADS = 16
HEAD_DIM = 64
NGROUPS = 2
NSTATE = 128
SSD_CONV = 4
CONV_DIM = D_SSD + 2 * NGROUPS * NSTATE
SC_CONV = 3
D_FF = 2816
FFN_CONV = 3
EPS = 1e-5
D_MIX = D_SSD + D_SC
D_IN_PROJ = D_SSD + CONV_DIM + NHEADS + 3 * D_SC

LANES = 128
MXU_COLS = 256
OFF_Z = 0
OFF_XBC = OFF_Z + D_SSD
OFF_DT = OFF_XBC + CONV_DIM
OFF_GB = OFF_DT + NHEADS
OFF_GC = OFF_GB + D_SC
OFF_H = OFF_GC + D_SC
DT_PAD = LANES

CHUNK = 128
CARRY = 8
SEQ_TILE = 512
VMEM_LIMIT = 56 * 1024 * 1024

F32 = jnp.float32
BF16 = jnp.bfloat16


def _dot(a, b):
    return jnp.dot(a, b, preferred_element_type=F32)


def _dot_nt(a, b_t):
    return lax.dot_general(a, b_t, (((1,), (1,)), ((), ())), preferred_element_type=F32)


def _split3(v):
    hi = v.astype(BF16)
    r = v - hi.astype(F32)
    mid = r.astype(BF16)
    lo = (r - mid.astype(F32)).astype(BF16)
    return hi, mid, lo


def _dot_sel_rhs(v, sel):
    hi, mid, lo = _split3(v)
    return (_dot(hi, sel) + _dot(mid, sel)) + _dot(lo, sel)


def _dot_sel_lhs(sel, v):
    hi, mid, lo = _split3(v)
    return (_dot(sel, hi) + _dot(sel, mid)) + _dot(sel, lo)


def _silu(v):
    h = 0.5 * v
    return h + h * jnp.tanh(h)


def _softplus(v):
    return jnp.maximum(v, 0.0) + jnp.log1p(jnp.exp(-jnp.abs(v)))


def _rms_rows(x, w):
    ms = jnp.mean(x * x, axis=-1, keepdims=True)
    return x * lax.rsqrt(ms + EPS) * w


def _mixer_prompt_kernel(tiles_per_seq,
                         x_ref, xres_ref, nw_ref, win_ref, cw_ref, cb_ref, dtb_ref, a_ref, dx_ref,
                         gnw_ref, scw_ref, wout_ref, e_ref,
                         x1_ref, ssm_ref, cst_ref, scst_ref,
                         xbc_ext, sc_ext, u_ref, act_ref, dt_ref, z_ref, gb_ref, s_ref, y_ref, ysc_ref, ycat_ref):
    tile = x_ref.shape[1]
    g = pl.program_id(0)
    last = pl.num_programs(0) - 1
    t = lax.rem(g, tiles_per_seq)

    @pl.when(g == 0)
    def _():
        ycat_ref[...] = jnp.zeros_like(ycat_ref)

    @pl.when(t == 0)
    def _():
        xbc_ext[0:CARRY, :] = jnp.zeros((CARRY, CONV_DIM), F32)
        sc_ext[0:CARRY, :] = jnp.zeros((CARRY, D_SC), F32)
        s_ref[...] = jnp.zeros_like(s_ref)

    def out_block(off):
        x1_ref[0, :, off:off + MXU_COLS] = xres_ref[0, :, off:off + MXU_COLS] + _dot(
            ycat_ref[...], wout_ref[:, off:off + MXU_COLS])

    @pl.when(g == last)
    def _():
        for off in range(0, D_MODEL, MXU_COLS):
            out_block(off)

    @pl.when(g < last)
    def _():
        _mixer_tile(tile, t == tiles_per_seq - 1, out_block,
                    x_ref, nw_ref, win_ref, cw_ref, cb_ref, dtb_ref, a_ref, dx_ref, gnw_ref, scw_ref, e_ref,
                    ssm_ref, cst_ref, scst_ref,
                    xbc_ext, sc_ext, u_ref, act_ref, dt_ref, z_ref, gb_ref, s_ref, y_ref, ysc_ref, ycat_ref)


def _mixer_tile(tile, ends_sequence, out_block,
                x_ref, nw_ref, win_ref, cw_ref, cb_ref, dtb_ref, a_ref, dx_ref, gnw_ref, scw_ref, e_ref,
                ssm_ref, cst_ref, scst_ref,
                xbc_ext, sc_ext, u_ref, act_ref, dt_ref, z_ref, gb_ref, s_ref, y_ref, ysc_ref, ycat_ref):
    u_ref[...] = _rms_rows(x_ref[0], nw_ref[...]).astype(BF16)

    def project(dst, row0, base, off, width=MXU_COLS):
        dst[row0:row0 + tile, off:off + width] = _dot_nt(u_ref[...], win_ref[base + off:base + off + width, :])

    def conv_block(off):
        conv = cb_ref[:, off:off + LANES] + (
            xbc_ext[CARRY - 3:CARRY - 3 + tile, off:off + LANES] * cw_ref[0:1, off:off + LANES])
        for k in range(1, SSD_CONV):
            conv = conv + (xbc_ext[CARRY - 3 + k:CARRY - 3 + k + tile, off:off + LANES]
                           * cw_ref[k:k + 1, off:off + LANES])
        act_ref[:, off:off + LANES] = _silu(conv)

    def sc_block(off):
        sc = sc_ext[CARRY - 2:CARRY - 2 + tile, off:off + LANES] * scw_ref[0:1, off:off + LANES]
        for k in range(1, SC_CONV):
            sc = sc + (sc_ext[CARRY - 2 + k:CARRY - 2 + k + tile, off:off + LANES]
                       * scw_ref[k:k + 1, off:off + LANES])
        ysc_ref[:, off:off + LANES] = (gb_ref[:, off:off + LANES] * sc).astype(BF16)

    for off in range(0, CONV_DIM, MXU_COLS):
        project(xbc_ext, CARRY, OFF_XBC, off)
        for sub in range(off, off + MXU_COLS, LANES):
            conv_block(sub)
    tail = xbc_ext[tile + CARRY - 3:tile + CARRY, :]
    cst_ref[0] = tail
    xbc_ext[CARRY - 3:CARRY, :] = tail
    project(dt_ref, 0, OFF_DT, 0, DT_PAD)
    for off in range(0, D_SC, MXU_COLS):
        u = u_ref[...]
        sc_ext[CARRY:CARRY + tile, off:off + MXU_COLS] = (
            _dot_nt(u, win_ref[OFF_GC + off:OFF_GC + off + MXU_COLS, :])
            * _dot_nt(u, win_ref[OFF_H + off:OFF_H + off + MXU_COLS, :]))

    fillers = []
    for k in range(D_SC // MXU_COLS):
        fillers.append(lambda k=k: out_block(k * MXU_COLS))
        fillers.append(lambda k=k: project(gb_ref, 0, OFF_GB, k * MXU_COLS))
        fillers.append(lambda k=k: sc_block(k * MXU_COLS))
        fillers.append(lambda k=k: sc_block(k * MXU_COLS + LANES))
        fillers.append(lambda k=k: project(z_ref, 0, OFF_Z, k * MXU_COLS))
    n_slots = (tile // CHUNK) * (NHEADS // 2)
    slot_of_filler = [(i * n_slots) // len(fillers) for i in range(len(fillers))]

    row_i = lax.broadcasted_iota(jnp.int32, (CHUNK, CHUNK), 0)
    col_i = lax.broadcasted_iota(jnp.int32, (CHUNK, CHUNK), 1)
    causal = row_i >= col_i
    tri = causal.astype(BF16)
    lane_i = lax.broadcasted_iota(jnp.int32, (1, LANES), 1)
    keeps = ((lane_i < HEAD_DIM).astype(BF16), (lane_i >= HEAD_DIM).astype(BF16))
    neg_inf = jnp.float32(-jnp.inf)
    pairs_per_group = NHEADS // NGROUPS // 2

    for c in range(tile // CHUNK):
        r0 = c * CHUNK
        dt = _softplus(dt_ref[r0:r0 + CHUNK, :] + dtb_ref[...])
        acum = _dot_sel_lhs(tri, dt * a_ref[...])
        acum_t = acum.T
        dt_t = dt.T
        ea = jnp.exp(acum)
        w_t = jnp.exp(acum_t[:, CHUNK - 1:CHUNK] - acum_t) * dt_t
        cd = jnp.broadcast_to(ea[CHUNK - 1:CHUNK, :], (8, LANES))
        cdx = _dot_sel_rhs(cd, e_ref[...])[0:1, :]
        for grp in range(NGROUPS):
            bg = act_ref[r0:r0 + CHUNK, D_SSD + grp * NSTATE:D_SSD + (grp + 1) * NSTATE]
            cg = act_ref[r0:r0 + CHUNK, D_SSD + (NGROUPS + grp) * NSTATE:D_SSD + (NGROUPS + grp + 1) * NSTATE]
            cb = lax.dot_general(cg.astype(BF16), bg.astype(BF16), (((1,), (1,)), ((), ())),
                                 preferred_element_type=F32)
            bg_t = bg.T
            for jp in range(pairs_per_group):
                j = grp * pairs_per_group + jp
                xp = act_ref[r0:r0 + CHUNK, j * LANES:(j + 1) * LANES].astype(BF16)
                sp = s_ref[:, j * LANES:(j + 1) * LANES]
                spb = sp.astype(BF16)
                lhs_parts, rhs_parts, bw_parts, x_parts = [], [], [], []
                for half, keep in enumerate(keeps):
                    h = 2 * j + half
                    xh = xp * keep
                    seg = acum[:, h:h + 1] - acum_t[h:h + 1, :]
                    decay = jnp.exp(jnp.where(causal, seg, neg_inf))
                    m = cb * decay * dt_t[h:h + 1, :]
                    eac = ea[:, h:h + 1] * cg
                    lhs_parts += [m.astype(BF16), eac.astype(BF16)]
                    rhs_parts += [xh, spb * keep]
                    bw_parts.append((bg_t * w_t[h:h + 1, :]).astype(BF16))
                    x_parts.append(xh)
                y_ref[r0:r0 + CHUNK, j * LANES:(j + 1) * LANES] = _dot(
                    jnp.concatenate(lhs_parts, axis=1), jnp.concatenate(rhs_parts, axis=0))
                snew = _dot(jnp.concatenate(bw_parts, axis=1), jnp.concatenate(x_parts, axis=0))
                s_ref[:, j * LANES:(j + 1) * LANES] = cdx[:, j * LANES:(j + 1) * LANES] * sp + snew
                slot = c * (NHEADS // 2) + j
                for i, filler in enumerate(fillers):
                    if slot_of_filler[i] == slot:
                        filler()
    sc_tail = sc_ext[tile + CARRY - 2:tile + CARRY, :]
    scst_ref[0] = sc_tail
    sc_ext[CARRY - 2:CARRY, :] = sc_tail

    gw = D_SSD // NGROUPS
    for grp in range(NGROUPS):
        ssq = jnp.zeros((tile, LANES), F32)
        for off in range(grp * gw, (grp + 1) * gw, LANES):
            yb = ((y_ref[:, off:off + LANES] + dx_ref[:, off:off + LANES] * act_ref[:, off:off + LANES])
                  * _silu(z_ref[:, off:off + LANES]))
            y_ref[:, off:off + LANES] = yb
            ssq = ssq + yb * yb
        scale = lax.rsqrt(jnp.sum(ssq, axis=-1, keepdims=True) * (1.0 / gw) + EPS)
        for off in range(grp * gw, (grp + 1) * gw, LANES):
            ycat_ref[:, off:off + LANES] = (y_ref[:, off:off + LANES] * scale
                                            * gnw_ref[:, off:off + LANES]).astype(BF16)
    ycat_ref[:, D_SSD:D_MIX] = ysc_ref[...]

    @pl.when(ends_sequence)
    def _():
        for j in range(NHEADS // 2):
            ssm_ref[0, 2 * j:2 * j + 2] = s_ref[:, j * LANES:(j + 1) * LANES].T.reshape(2, HEAD_DIM, NSTATE)


def _const_spec(shape):
    nd = len(shape)
    return pl.BlockSpec(shape, lambda *_: (0,) * nd, pipeline_mode=pl.Buffered(1))


def _mixer_prompt(x, nw, win, cw, cb, dtb, a_pad, dx, gnw, scw, wout, expand):
    nb, seq, _ = x.shape
    tile = SEQ_TILE
    nt = seq // tile
    total = nb * nt

    def tile_of(g, lag):
        i = jnp.clip(g - lag, 0, total - 1)
        return i // nt, i % nt

    cur_spec = pl.BlockSpec((1, tile, D_MODEL), lambda g: (*tile_of(g, 0), 0))
    prev_spec = pl.BlockSpec((1, tile, D_MODEL), lambda g: (*tile_of(g, 1), 0))
    consts = (nw, win, cw, cb, dtb, a_pad, dx, gnw, scw, wout, expand)
    return pl.pallas_call(
        functools.partial(_mixer_prompt_kernel, nt),
        grid=(total + 1,),
        in_specs=[cur_spec, prev_spec] + [_const_spec(p.shape) for p in consts],
        out_specs=[
            prev_spec,
            pl.BlockSpec((1, NHEADS, HEAD_DIM, NSTATE), lambda g: (tile_of(g, 0)[0], 0, 0, 0)),
            pl.BlockSpec((1, SSD_CONV - 1, CONV_DIM), lambda g: (tile_of(g, 0)[0], 0, 0)),
            pl.BlockSpec((1, SC_CONV - 1, D_SC), lambda g: (tile_of(g, 0)[0], 0, 0)),
        ],
        out_shape=[
            jax.ShapeDtypeStruct((nb, seq, D_MODEL), F32),
            jax.ShapeDtypeStruct((nb, NHEADS, HEAD_DIM, NSTATE), F32),
            jax.ShapeDtypeStruct((nb, SSD_CONV - 1, CONV_DIM), F32),
            jax.ShapeDtypeStruct((nb, SC_CONV - 1, D_SC), F32),
        ],
        scratch_shapes=[
            pltpu.VMEM((tile + CARRY, CONV_DIM), F32),
            pltpu.VMEM((tile + CARRY, D_SC), F32),
            pltpu.VMEM((tile, D_MODEL), BF16),
            pltpu.VMEM((tile, CONV_DIM), F32),
            pltpu.VMEM((tile, DT_PAD), F32),
            pltpu.VMEM((tile, D_SSD), F32),
            pltpu.VMEM((tile, D_SC), F32),
            pltpu.VMEM((NSTATE, D_SSD), F32),
            pltpu.VMEM((tile, D_SSD), F32),
            pltpu.VMEM((tile, D_SC), BF16),
            pltpu.VMEM((tile, D_MIX), BF16),
        ],
        compiler_params=pltpu.CompilerParams(
            dimension_semantics=("arbitrary",), vmem_limit_bytes=VMEM_LIMIT),
        name="mixer_prompt",
    )(x, x, *consts)


def _ffn_prompt_kernel(x1_ref, nw_ref, wffn_ref, fcw_ref, fcb_ref, wdown_ref, nfw_ref,
                       y_ref, fst_ref, gate_ext):
    tile = x1_ref.shape[1]
    t = pl.program_id(1)

    @pl.when(t == 0)
    def _():
        gate_ext[0:CARRY, :] = jnp.zeros((CARRY, D_FF), F32)

    x1 = x1_ref[0]
    u = _rms_rows(x1, nw_ref[...]).astype(BF16)
    gate_ext[CARRY:CARRY + tile, :] = _dot(u, wffn_ref[:, 0:D_FF])
    up = _dot(u, wffn_ref[:, D_FF:2 * D_FF])
    g = fcb_ref[...] + gate_ext[CARRY - 2:CARRY - 2 + tile, :] * fcw_ref[0:1, :]
    for k in range(1, FFN_CONV):
        g = g + gate_ext[CARRY - 2 + k:CARRY - 2 + k + tile, :] * fcw_ref[k:k + 1, :]
    tail = gate_ext[tile + CARRY - 2:tile + CARRY, :]
    fst_ref[0] = tail
    gate_ext[CARRY - 2:CARRY, :] = tail
    act = (_silu(g) * up).astype(BF16)
    x2 = x1 + _dot(act, wdown_ref[...])
    y_ref[0] = _rms_rows(x2, nfw_ref[...])


def _ffn_prompt(x1, nw, wffn, fcw, fcb, wdown, nfw):
    nb, seq, _ = x1.shape
    tile = SEQ_TILE
    tok_spec = pl.BlockSpec((1, tile, D_MODEL), lambda b, t: (b, t, 0))
    return pl.pallas_call(
        _ffn_prompt_kernel,
        grid=(nb, seq // tile),
        in_specs=[tok_spec] + [_const_spec(p.shape) for p in (nw, wffn, fcw, fcb, wdown, nfw)],
        out_specs=[tok_spec, pl.BlockSpec((1, FFN_CONV - 1, D_FF), lambda b, t: (b, 0, 0))],
        out_shape=[jax.ShapeDtypeStruct((nb, seq, D_MODEL), F32),
                   jax.ShapeDtypeStruct((nb, FFN_CONV - 1, D_FF), F32)],
        scratch_shapes=[pltpu.VMEM((tile + CARRY, D_FF), F32)],
        compiler_params=pltpu.CompilerParams(
            dimension_semantics=("arbitrary", "arbitrary"), vmem_limit_bytes=VMEM_LIMIT),
        name="ffn_prompt",
    )(x1, nw, wffn, fcw, fcb, wdown, nfw)


def _sample_pre_kernel(x_ref, nw_ref, win_ref, cw_ref, cb_ref, dtb_ref, a_ref, scw_ref, e_ref,
                       cst_ref, scst_ref,
                       z_ref, xs_ref, xdt_t_ref, da_ref, bc_ref, ysc_ref, ncst_ref, nscst_ref):
    x = x_ref[...]
    u = _rms_rows(x, nw_ref[...]).astype(BF16)
    z_ref[...] = _dot_nt(u, win_ref[OFF_Z:OFF_Z + D_SSD, :])
    xbc = _dot_nt(u, win_ref[OFF_XBC:OFF_XBC + CONV_DIM, :])
    conv = cb_ref[...] + xbc * cw_ref[SSD_CONV - 1:SSD_CONV, :]
    for k in range(SSD_CONV - 1):
        conv = conv + cst_ref[k] * cw_ref[k:k + 1, :]
    for k in range(SSD_CONV - 2):
        ncst_ref[k] = cst_ref[k + 1]
    ncst_ref[SSD_CONV - 2] = xbc
    act = _silu(conv)
    xs = act[:, 0:D_SSD]
    xs_ref[...] = xs
    for i in range(2 * NGROUPS):
        bc_ref[i] = act[:, D_SSD + i * NSTATE:D_SSD + (i + 1) * NSTATE]
    dt = _softplus(_dot_nt(u, win_ref[OFF_DT:OFF_DT + DT_PAD, :]) + dtb_ref[...])
    xdt_t_ref[...] = (xs * _dot_sel_rhs(dt, e_ref[...])).T
    da_ref[...] = jnp.exp(dt * a_ref[...])[:, 0:NHEADS]
    gc = _dot_nt(u, win_ref[OFF_GC:OFF_GC + D_SC, :])
    hh = _dot_nt(u, win_ref[OFF_H:OFF_H + D_SC, :])
    gch = gc * hh
    sc = gch * scw_ref[SC_CONV - 1:SC_CONV, :]
    for k in range(SC_CONV - 1):
        sc = sc + scst_ref[:, k, :] * scw_ref[k:k + 1, :]
    for k in range(SC_CONV - 2):
        nscst_ref[:, k, :] = scst_ref[:, k + 1, :]
    nscst_ref[:, SC_CONV - 2, :] = gch
    gb = _dot_nt(u, win_ref[OFF_GB:OFF_GB + D_SC, :])
    ysc_ref[...] = gb * sc


def _sample_pre(x, nw, win, cw, cb, dtb, a_pad, scw, expand, cst, scst):
    nb = x.shape[0]
    outs = [
        jax.ShapeDtypeStruct((nb, D_SSD), F32),
        jax.ShapeDtypeStruct((nb, D_SSD), F32),
        jax.ShapeDtypeStruct((D_SSD, nb), F32),
        jax.ShapeDtypeStruct((nb, NHEADS), F32),
        jax.ShapeDtypeStruct((2 * NGROUPS, nb, NSTATE), F32),
        jax.ShapeDtypeStruct((nb, D_SC), F32),
        jax.ShapeDtypeStruct((SSD_CONV - 1, nb, CONV_DIM), F32),
        jax.ShapeDtypeStruct((nb, SC_CONV - 1, D_SC), F32),
    ]
    return pl.pallas_call(
        _sample_pre_kernel,
        out_shape=outs,
        compiler_params=pltpu.CompilerParams(vmem_limit_bytes=VMEM_LIMIT),
        name="sample_pre",
    )(x, nw, win, cw, cb, dtb, a_pad, scw, expand, cst, scst)


SAMPLE_HEADS = 2


def _sample_state_kernel(da_ref, st_ref, xdt_t_ref, b_ref, c_ref, nst_ref, y_ref):
    nb = st_ref.shape[0]
    head0 = pl.program_id(0) * SAMPLE_HEADS
    rows = SAMPLE_HEADS * HEAD_DIM
    c_t = c_ref[0].T.astype(BF16)
    lane = lax.broadcasted_iota(jnp.int32, (rows, nb), 1)
    ycols = jnp.zeros((rows, nb), F32)
    for b in range(nb):
        upd = xdt_t_ref[:, b:b + 1] * b_ref[0, b:b + 1, :]
        parts = []
        for hh in range(SAMPLE_HEADS):
            hnew = st_ref[b, hh] * da_ref[b, head0 + hh] + upd[hh * HEAD_DIM:(hh + 1) * HEAD_DIM, :]
            nst_ref[b, hh] = hnew
            parts.append(hnew)
        prod = _dot(jnp.concatenate(parts, axis=0).astype(BF16), c_t)
        ycols = jnp.where(lane == b, prod, ycols)
    y_ref[...] = ycols.T


def _sample_state(state, da, xdt_t, bc):
    nb = state.shape[0]
    rows = SAMPLE_HEADS * HEAD_DIM
    heads_per_group = NHEADS // NGROUPS
    st_spec = pl.BlockSpec((nb, SAMPLE_HEADS, HEAD_DIM, NSTATE), lambda i: (0, i, 0, 0))
    return pl.pallas_call(
        _sample_state_kernel,
        grid=(NHEADS // SAMPLE_HEADS,),
        in_specs=[pl.BlockSpec(memory_space=pltpu.SMEM),
                  st_spec,
                  pl.BlockSpec((rows, nb), lambda i: (i, 0)),
                  pl.BlockSpec((1, nb, NSTATE), lambda i: (i * SAMPLE_HEADS // heads_per_group, 0, 0)),
                  pl.BlockSpec((1, nb, NSTATE), lambda i: (NGROUPS + i * SAMPLE_HEADS // heads_per_group, 0, 0))],
        out_specs=[st_spec, pl.BlockSpec((nb, rows), lambda i: (0, i))],
        out_shape=[jax.ShapeDtypeStruct(state.shape, F32), jax.ShapeDtypeStruct((nb, NHEADS * HEAD_DIM), F32)],
        compiler_params=pltpu.CompilerParams(
            dimension_semantics=("arbitrary",), vmem_limit_bytes=VMEM_LIMIT),
        name="sample_state",
    )(da, state, xdt_t, bc, bc)


POST_OUT_BLOCKS = 2
POST_FFN_BLOCKS = 4
POST_DOWN_BLOCKS = 4
POST_OUT_ROWS = D_MIX // POST_OUT_BLOCKS
POST_FFN_COLS = 2 * D_FF // POST_FFN_BLOCKS
POST_DOWN_ROWS = D_FF // POST_DOWN_BLOCKS


def _sample_post_kernel(x_ref, y_ref, xs_ref, z_ref, ysc_ref, dx_ref, gnw_ref, wout_ref,
                        nw_ref, wffn_ref, fcw_ref, fcb_ref, wdown_ref, nfw_ref, fst_ref,
                        out_ref, nfst_ref, wout_bf_ref, wffn_bf_ref, wdown_bf_ref,
                        ycat_ref, x1_ref, u_ref, gu_ref, act_ref, acc_ref):
    s = pl.program_id(0)
    ffn0 = POST_OUT_BLOCKS
    down0 = ffn0 + POST_FFN_BLOCKS

    @pl.when(s == 0)
    def _():
        y = (y_ref[...] + dx_ref[...] * xs_ref[...]) * _silu(z_ref[...])
        gw = D_SSD // NGROUPS
        for g in range(NGROUPS):
            ycat_ref[:, g * gw:(g + 1) * gw] = _rms_rows(
                y[:, g * gw:(g + 1) * gw], gnw_ref[:, g * gw:(g + 1) * gw]).astype(BF16)
        ycat_ref[:, D_SSD:D_MIX] = ysc_ref[...].astype(BF16)
        x1_ref[...] = x_ref[...]

    for k in range(POST_OUT_BLOCKS):
        @pl.when(s == k)
        def _(k=k):
            w = wout_ref[...].astype(BF16)
            wout_bf_ref[...] = w
            x1_ref[...] += _dot(ycat_ref[:, k * POST_OUT_ROWS:(k + 1) * POST_OUT_ROWS], w)

    @pl.when(s == ffn0)
    def _():
        u_ref[...] = _rms_rows(x1_ref[...], nw_ref[...]).astype(BF16)

    for k in range(POST_FFN_BLOCKS):
        @pl.when(s == ffn0 + k)
        def _(k=k):
            w = wffn_ref[...].astype(BF16)
            wffn_bf_ref[...] = w
            gu_ref[:, k * POST_FFN_COLS:(k + 1) * POST_FFN_COLS] = _dot(u_ref[...], w)

    @pl.when(s == down0)
    def _():
        gate = gu_ref[:, 0:D_FF]
        g = fcb_ref[...] + gate * fcw_ref[FFN_CONV - 1:FFN_CONV, :]
        for k in range(FFN_CONV - 1):
            g = g + fst_ref[:, k, :] * fcw_ref[k:k + 1, :]
        for k in range(FFN_CONV - 2):
            nfst_ref[:, k, :] = fst_ref[:, k + 1, :]
        nfst_ref[:, FFN_CONV - 2, :] = gate
        act_ref[...] = (_silu(g) * gu_ref[:, D_FF:2 * D_FF]).astype(BF16)
        acc_ref[...] = x1_ref[...]

    for k in range(POST_DOWN_BLOCKS):
        @pl.when(s == down0 + k)
        def _(k=k):
            w = wdown_ref[...].astype(BF16)
            wdown_bf_ref[...] = w
            acc_ref[...] += _dot(act_ref[:, k * POST_DOWN_ROWS:(k + 1) * POST_DOWN_ROWS], w)

    @pl.when(s == down0 + POST_DOWN_BLOCKS - 1)
    def _():
        out_ref[...] = _rms_rows(acc_ref[...], nfw_ref[...])


def _sample_post(x, y, xs, z, ysc, dx, gnw, w_out, nw, w_ffn_in, fcw, fcb, w_down, nfw, fst):
    nb = x.shape[0]
    ffn0 = POST_OUT_BLOCKS
    down0 = ffn0 + POST_FFN_BLOCKS
    steps = down0 + POST_DOWN_BLOCKS

    def whole(a):
        nd = a.ndim
        return pl.BlockSpec(a.shape, lambda s: (0,) * nd)

    out_w = pl.BlockSpec((POST_OUT_ROWS, D_MODEL), lambda s: (jnp.clip(s, 0, POST_OUT_BLOCKS - 1), 0))
    ffn_w = pl.BlockSpec((D_MODEL, POST_FFN_COLS), lambda s: (0, jnp.clip(s - ffn0, 0, POST_FFN_BLOCKS - 1)))
    down_w = pl.BlockSpec((POST_DOWN_ROWS, D_MODEL), lambda s: (jnp.clip(s - down0, 0, POST_DOWN_BLOCKS - 1), 0))
    nfst_shape = jax.ShapeDtypeStruct((nb, FFN_CONV - 1, D_FF), F32)
    y_shape = jax.ShapeDtypeStruct((nb, D_MODEL), F32)
    return pl.pallas_call(
        _sample_post_kernel,
        grid=(steps,),
        in_specs=[_const_spec(a.shape) for a in (x, y, xs, z, ysc, dx, gnw)] + [out_w, _const_spec(nw.shape), ffn_w,
                  _const_spec(fcw.shape), _const_spec(fcb.shape), down_w, _const_spec(nfw.shape),
                  _const_spec(fst.shape)],
        out_specs=[whole(y_shape), whole(nfst_shape), out_w, ffn_w, down_w],
        out_shape=[y_shape, nfst_shape,
                   jax.ShapeDtypeStruct(w_out.shape, BF16),
                   jax.ShapeDtypeStruct(w_ffn_in.shape, BF16),
                   jax.ShapeDtypeStruct(w_down.shape, BF16)],
        scratch_shapes=[
            pltpu.VMEM((nb, D_MIX), BF16),
            pltpu.VMEM((nb, D_MODEL), F32),
            pltpu.VMEM((nb, D_MODEL), BF16),
            pltpu.VMEM((nb, 2 * D_FF), F32),
            pltpu.VMEM((nb, D_FF), BF16),
            pltpu.VMEM((nb, D_MODEL), F32),
        ],
        compiler_params=pltpu.CompilerParams(
            dimension_semantics=("arbitrary",), vmem_limit_bytes=VMEM_LIMIT),
        name="sample_post",
    )(x, y, xs, z, ysc, dx, gnw, w_out, nw, w_ffn_in, fcw, fcb, w_down, nfw, fst)


def kernel(x_prompt, x_sample, state_ssm, state_ssd_conv, state_short_conv, state_ffn_conv,
           norm_mix_w, w_in, ssd_conv_w, ssd_conv_b, ssd_dt_bias, ssd_a_log, ssd_d, ssd_norm_w,
           sc_conv_w, w_out, norm_ffn_w, w_ffn_in, ffn_conv_w, ffn_conv_b, w_down, norm_final_w):
    depth = w_in.shape[0]
    assert depth == 1
    win = jnp.swapaxes(w_in[0], 0, 1).astype(BF16)
    nw = norm_mix_w[0].reshape(1, D_MODEL)
    nw2 = norm_ffn_w[0].reshape(1, D_MODEL)
    nfw = norm_final_w.reshape(1, D_MODEL)
    cw = ssd_conv_w[0]
    cb = ssd_conv_b[0].reshape(1, CONV_DIM)
    dtb = jnp.pad(ssd_dt_bias[0], (0, DT_PAD - NHEADS)).reshape(1, DT_PAD)
    a_pad = jnp.pad(-jnp.exp(ssd_a_log[0]), (0, DT_PAD - NHEADS)).reshape(1, DT_PAD)
    dx = jnp.repeat(ssd_d[0], HEAD_DIM).reshape(1, D_SSD)
    gnw = ssd_norm_w[0].reshape(1, D_SSD)
    scw = sc_conv_w[0]
    fcw = ffn_conv_w[0]
    fcb = ffn_conv_b[0].reshape(1, D_FF)
    head_of_lane = jnp.arange(D_SSD, dtype=jnp.int32) // HEAD_DIM
    expand = (jnp.arange(DT_PAD, dtype=jnp.int32)[:, None] == head_of_lane[None, :]).astype(BF16)

    nbs = x_sample.shape[0]
    xs_in = x_sample.reshape(nbs, D_MODEL)
    cst = jnp.swapaxes(state_ssd_conv[0], 0, 1)
    scst, fst = state_short_conv[0], state_ffn_conv[0]
    z, xs, xdt_t, da, bc, ysc, ncst, nscst = _sample_pre(xs_in, nw, win, cw, cb, dtb, a_pad, scw, expand, cst, scst)
    s_ssm, y_s = _sample_state(state_ssm[0], da, xdt_t, bc)
    y_sample, nfst, wout, wffn, wdown = _sample_post(
        xs_in, y_s, xs, z, ysc, dx, gnw, w_out[0], nw2, w_ffn_in[0], fcw, fcb, w_down[0], nfw, fst)

    x1, p_ssm, p_cst, p_scst = _mixer_prompt(x_prompt, nw, win, cw, cb, dtb, a_pad, dx, gnw, scw, wout, expand)
    y_prompt, p_fst = _ffn_prompt(x1, nw2, wffn, fcw, fcb, wdown, nfw)

    return (y_prompt, y_sample.reshape(nbs, 1, D_MODEL),
            p_ssm[None], p_cst[None], p_scst[None], p_fst[None],
            s_ssm[None], jnp.swapaxes(ncst, 0, 1)[None], nscst[None], nfst[None])
```

```python
import functools

import jax
import jax.numpy as jnp
from jax import lax
from jax.experimental import pallas as pl
from jax.experimental.pallas import tpu as pltpu

D_MODEL = 1024
D_SSD = 1024
D_SC = 1024
NHEADS = 16
HEAD_DIM = 64
NGROUPS = 2
NSTATE = 128
SSD_CONV = 4
CONV_DIM = D_SSD + 2 * NGROUPS * NSTATE
SC_CONV = 3
D_FF = 2816
FFN_CONV = 3
EPS = 1e-5
D_MIX = D_SSD + D_SC
D_IN_PROJ = D_SSD + CONV_DIM + NHEADS + 3 * D_SC

LANES = 128
MXU_COLS = 256

SRC_Z = 0
SRC_XBC = SRC_Z + D_SSD
SRC_DT = SRC_XBC + CONV_DIM
SRC_GB = SRC_DT + NHEADS
SRC_GC = SRC_GB + D_SC
SRC_H = SRC_GC + D_SC
PACK_BLOCK = 512
OFF_Z = 0
OFF_XBC = OFF_Z + D_SSD
OFF_GB = OFF_XBC + CONV_DIM
OFF_GC = OFF_GB + D_SC
OFF_H = OFF_GC + D_SC
OFF_DT = OFF_H + D_SC
D_IN_PACKED = OFF_DT + PACK_BLOCK
DT_PAD = LANES

CHUNK = 128
CARRY = 8
SEQ_TILE = 512
VMEM_LIMIT = 56 * 1024 * 1024

F32 = jnp.float32
BF16 = jnp.bfloat16


def _dot(a, b):
    return jnp.dot(a, b, preferred_element_type=F32)


def _dot_nt(a, b_t):
    return lax.dot_general(a, b_t, (((1,), (1,)), ((), ())), preferred_element_type=F32)


def _split3(v):
    hi = v.astype(BF16)
    r = v - hi.astype(F32)
    mid = r.astype(BF16)
    lo = (r - mid.astype(F32)).astype(BF16)
    return hi, mid, lo


def _dot_sel_rhs(v, sel):
    hi, mid, lo = _split3(v)
    return (_dot(hi, sel) + _dot(mid, sel)) + _dot(lo, sel)


def _dot_sel_lhs(sel, v):
    hi, mid, lo = _split3(v)
    return (_dot(sel, hi) + _dot(sel, mid)) + _dot(sel, lo)


def _silu(v):
    h = 0.5 * v
    return h + h * jnp.tanh(h)


def _softplus(v):
    return jnp.maximum(v, 0.0) + jnp.log1p(jnp.exp(-jnp.abs(v)))


def _rms_rows(x, w):
    ms = jnp.mean(x * x, axis=-1, keepdims=True)
    return x * lax.rsqrt(ms + EPS) * w


def _mixer_prompt_kernel(tiles_per_seq,
                         x_ref, xres_ref, nw_ref, win_ref, cw_ref, cb_ref, dtb_ref, a_ref, dx_ref,
                         gnw_ref, scw_ref, wout_ref, e_ref,
                         x1_ref, ssm_ref, cst_ref, scst_ref,
                         xbc_ext, sc_ext, u_ref, act_ref, dt_ref, z_ref, gb_ref, s_ref, y_ref, ysc_ref, ycat_ref):
    tile = x_ref.shape[1]
    g = pl.program_id(0)
    last = pl.num_programs(0) - 1
    t = lax.rem(g, tiles_per_seq)

    @pl.when(g == 0)
    def _():
        ycat_ref[...] = jnp.zeros_like(ycat_ref)

    @pl.when(t == 0)
    def _():
        xbc_ext[0:CARRY, :] = jnp.zeros((CARRY, CONV_DIM), F32)
        sc_ext[0:CARRY, :] = jnp.zeros((CARRY, D_SC), F32)
        s_ref[...] = jnp.zeros_like(s_ref)

    def out_block(off):
        x1_ref[0, :, off:off + MXU_COLS] = xres_ref[0, :, off:off + MXU_COLS] + _dot(
            ycat_ref[...], wout_ref[:, off:off + MXU_COLS])

    @pl.when(g == last)
    def _():
        for off in range(0, D_MODEL, MXU_COLS):
            out_block(off)

    @pl.when(g < last)
    def _():
        _mixer_tile(tile, t == tiles_per_seq - 1, out_block,
                    x_ref, nw_ref, win_ref, cw_ref, cb_ref, dtb_ref, a_ref, dx_ref, gnw_ref, scw_ref, e_ref,
                    ssm_ref, cst_ref, scst_ref,
                    xbc_ext, sc_ext, u_ref, act_ref, dt_ref, z_ref, gb_ref, s_ref, y_ref, ysc_ref, ycat_ref)


def _mixer_tile(tile, ends_sequence, out_block,
                x_ref, nw_ref, win_ref, cw_ref, cb_ref, dtb_ref, a_ref, dx_ref, gnw_ref, scw_ref, e_ref,
                ssm_ref, cst_ref, scst_ref,
                xbc_ext, sc_ext, u_ref, act_ref, dt_ref, z_ref, gb_ref, s_ref, y_ref, ysc_ref, ycat_ref):
    u_ref[...] = _rms_rows(x_ref[0], nw_ref[...]).astype(BF16)

    def project(dst, row0, base, off, width=MXU_COLS):
        dst[row0:row0 + tile, off:off + width] = _dot(u_ref[...], win_ref[:, base + off:base + off + width])

    def conv_block(off):
        conv = cb_ref[:, off:off + LANES] + (
            xbc_ext[CARRY - 3:CARRY - 3 + tile, off:off + LANES] * cw_ref[0:1, off:off + LANES])
        for k in range(1, SSD_CONV):
            conv = conv + (xbc_ext[CARRY - 3 + k:CARRY - 3 + k + tile, off:off + LANES]
                           * cw_ref[k:k + 1, off:off + LANES])
        act_ref[:, off:off + LANES] = _silu(conv)

    def sc_block(off):
        sc = sc_ext[CARRY - 2:CARRY - 2 + tile, off:off + LANES] * scw_ref[0:1, off:off + LANES]
        for k in range(1, SC_CONV):
            sc = sc + (sc_ext[CARRY - 2 + k:CARRY - 2 + k + tile, off:off + LANES]
                       * scw_ref[k:k + 1, off:off + LANES])
        ysc_ref[:, off:off + LANES] = (gb_ref[:, off:off + LANES] * sc).astype(BF16)

    for off in range(0, CONV_DIM, MXU_COLS):
        project(xbc_ext, CARRY, OFF_XBC, off)
        for sub in range(off, off + MXU_COLS, LANES):
            conv_block(sub)
    tail = xbc_ext[tile + CARRY - 3:tile + CARRY, :]
    cst_ref[0] = tail
    xbc_ext[CARRY - 3:CARRY, :] = tail
    project(dt_ref, 0, OFF_DT, 0, DT_PAD)
    for off in range(0, D_SC, MXU_COLS):
        u = u_ref[...]
        sc_ext[CARRY:CARRY + tile, off:off + MXU_COLS] = (
            _dot(u, win_ref[:, OFF_GC + off:OFF_GC + off + MXU_COLS])
            * _dot(u, win_ref[:, OFF_H + off:OFF_H + off + MXU_COLS]))

    fillers = []
    for k in range(D_SC // MXU_COLS):
        fillers.append(lambda k=k: out_block(k * MXU_COLS))
        fillers.append(lambda k=k: project(gb_ref, 0, OFF_GB, k * MXU_COLS))
        fillers.append(lambda k=k: sc_block(k * MXU_COLS))
        fillers.append(lambda k=k: sc_block(k * MXU_COLS + LANES))
        fillers.append(lambda k=k: project(z_ref, 0, OFF_Z, k * MXU_COLS))
    n_slots = (tile // CHUNK) * (NHEADS // 2)
    slot_of_filler = [(i * n_slots) // len(fillers) for i in range(len(fillers))]

    row_i = lax.broadcasted_iota(jnp.int32, (CHUNK, CHUNK), 0)
    col_i = lax.broadcasted_iota(jnp.int32, (CHUNK, CHUNK), 1)
    causal = row_i >= col_i
    tri = causal.astype(BF16)
    lane_i = lax.broadcasted_iota(jnp.int32, (1, LANES), 1)
    keeps = ((lane_i < HEAD_DIM).astype(BF16), (lane_i >= HEAD_DIM).astype(BF16))
    neg_inf = jnp.float32(-jnp.inf)
    pairs_per_group = NHEADS // NGROUPS // 2

    for c in range(tile // CHUNK):
        r0 = c * CHUNK
        dt = _softplus(dt_ref[r0:r0 + CHUNK, :] + dtb_ref[...])
        acum = _dot_sel_lhs(tri, dt * a_ref[...])
        acum_t = acum.T
        dt_t = dt.T
        ea = jnp.exp(acum)
        w_t = jnp.exp(acum_t[:, CHUNK - 1:CHUNK] - acum_t) * dt_t
        cd = jnp.broadcast_to(ea[CHUNK - 1:CHUNK, :], (8, LANES))
        cdx = _dot_sel_rhs(cd, e_ref[...])[0:1, :]
        for grp in range(NGROUPS):
            bg = act_ref[r0:r0 + CHUNK, D_SSD + grp * NSTATE:D_SSD + (grp + 1) * NSTATE]
            cg = act_ref[r0:r0 + CHUNK, D_SSD + (NGROUPS + grp) * NSTATE:D_SSD + (NGROUPS + grp + 1) * NSTATE]
            cb = lax.dot_general(cg.astype(BF16), bg.astype(BF16), (((1,), (1,)), ((), ())),
                                 preferred_element_type=F32)
            bg_t = bg.T
            for jp in range(pairs_per_group):
                j = grp * pairs_per_group + jp
                xp = act_ref[r0:r0 + CHUNK, j * LANES:(j + 1) * LANES].astype(BF16)
                sp = s_ref[:, j * LANES:(j + 1) * LANES]
                spb = sp.astype(BF16)
                lhs_parts, rhs_parts, bw_parts, x_parts = [], [], [], []
                for half, keep in enumerate(keeps):
                    h = 2 * j + half
                    xh = xp * keep
                    seg = acum[:, h:h + 1] - acum_t[h:h + 1, :]
                    decay = jnp.exp(jnp.where(causal, seg, neg_inf))
                    m = cb * decay * dt_t[h:h + 1, :]
                    eac = ea[:, h:h + 1] * cg
                    lhs_parts += [m.astype(BF16), eac.astype(BF16)]
                    rhs_parts += [xh, spb * keep]
                    bw_parts.append((bg_t * w_t[h:h + 1, :]).astype(BF16))
                    x_parts.append(xh)
                y_ref[r0:r0 + CHUNK, j * LANES:(j + 1) * LANES] = _dot(
                    jnp.concatenate(lhs_parts, axis=1), jnp.concatenate(rhs_parts, axis=0))
                snew = _dot(jnp.concatenate(bw_parts, axis=1), jnp.concatenate(x_parts, axis=0))
                s_ref[:, j * LANES:(j + 1) * LANES] = cdx[:, j * LANES:(j + 1) * LANES] * sp + snew
                slot = c * (NHEADS // 2) + j
                for i, filler in enumerate(fillers):
                    if slot_of_filler[i] == slot:
                        filler()
    sc_tail = sc_ext[tile + CARRY - 2:tile + CARRY, :]
    scst_ref[0] = sc_tail
    sc_ext[CARRY - 2:CARRY, :] = sc_tail

    gw = D_SSD // NGROUPS
    for grp in range(NGROUPS):
        ssq = jnp.zeros((tile, LANES), F32)
        for off in range(grp * gw, (grp + 1) * gw, LANES):
            yb = ((y_ref[:, off:off + LANES] + dx_ref[:, off:off + LANES] * act_ref[:, off:off + LANES])
                  * _silu(z_ref[:, off:off + LANES]))
            y_ref[:, off:off + LANES] = yb
            ssq = ssq + yb * yb
        scale = lax.rsqrt(jnp.sum(ssq, axis=-1, keepdims=True) * (1.0 / gw) + EPS)
        for off in range(grp * gw, (grp + 1) * gw, LANES):
            ycat_ref[:, off:off + LANES] = (y_ref[:, off:off + LANES] * scale
                                            * gnw_ref[:, off:off + LANES]).astype(BF16)
    ycat_ref[:, D_SSD:D_MIX] = ysc_ref[...]

    @pl.when(ends_sequence)
    def _():
        for j in range(NHEADS // 2):
            ssm_ref[0, 2 * j:2 * j + 2] = s_ref[:, j * LANES:(j + 1) * LANES].T.reshape(2, HEAD_DIM, NSTATE)


def _const_spec(shape):
    nd = len(shape)
    return pl.BlockSpec(shape, lambda *_: (0,) * nd, pipeline_mode=pl.Buffered(1))


def _mixer_prompt(x, nw, win, cw, cb, dtb, a_pad, dx, gnw, scw, wout, expand):
    nb, seq, _ = x.shape
    tile = SEQ_TILE
    nt = seq // tile
    total = nb * nt

    def tile_of(g, lag):
        i = jnp.clip(g - lag, 0, total - 1)
        return i // nt, i % nt

    cur_spec = pl.BlockSpec((1, tile, D_MODEL), lambda g: (*tile_of(g, 0), 0))
    prev_spec = pl.BlockSpec((1, tile, D_MODEL), lambda g: (*tile_of(g, 1), 0))
    consts = (nw, win, cw, cb, dtb, a_pad, dx, gnw, scw, wout, expand)
    return pl.pallas_call(
        functools.partial(_mixer_prompt_kernel, nt),
        grid=(total + 1,),
        in_specs=[cur_spec, prev_spec] + [_const_spec(p.shape) for p in consts],
        out_specs=[
            prev_spec,
            pl.BlockSpec((1, NHEADS, HEAD_DIM, NSTATE), lambda g: (tile_of(g, 0)[0], 0, 0, 0)),
            pl.BlockSpec((1, SSD_CONV - 1, CONV_DIM), lambda g: (tile_of(g, 0)[0], 0, 0)),
            pl.BlockSpec((1, SC_CONV - 1, D_SC), lambda g: (tile_of(g, 0)[0], 0, 0)),
        ],
        out_shape=[
            jax.ShapeDtypeStruct((nb, seq, D_MODEL), F32),
            jax.ShapeDtypeStruct((nb, NHEADS, HEAD_DIM, NSTATE), F32),
            jax.ShapeDtypeStruct((nb, SSD_CONV - 1, CONV_DIM), F32),
            jax.ShapeDtypeStruct((nb, SC_CONV - 1, D_SC), F32),
        ],
        scratch_shapes=[
            pltpu.VMEM((tile + CARRY, CONV_DIM), F32),
            pltpu.VMEM((tile + CARRY, D_SC), F32),
            pltpu.VMEM((tile, D_MODEL), BF16),
            pltpu.VMEM((tile, CONV_DIM), F32),
            pltpu.VMEM((tile, DT_PAD), F32),
            pltpu.VMEM((tile, D_SSD), F32),
            pltpu.VMEM((tile, D_SC), F32),
            pltpu.VMEM((NSTATE, D_SSD), F32),
            pltpu.VMEM((tile, D_SSD), F32),
            pltpu.VMEM((tile, D_SC), BF16),
            pltpu.VMEM((tile, D_MIX), BF16),
        ],
        compiler_params=pltpu.CompilerParams(
            dimension_semantics=("arbitrary",), vmem_limit_bytes=VMEM_LIMIT),
        name="mixer_prompt",
    )(x, x, *consts)


def _ffn_prompt_kernel(x1_ref, nw_ref, wffn_ref, fcw_ref, fcb_ref, wdown_ref, nfw_ref,
                       y_ref, fst_ref, gate_ext):
    tile = x1_ref.shape[1]
    t = pl.program_id(1)

    @pl.when(t == 0)
    def _():
        gate_ext[0:CARRY, :] = jnp.zeros((CARRY, D_FF), F32)

    x1 = x1_ref[0]
    u = _rms_rows(x1, nw_ref[...]).astype(BF16)
    gate_ext[CARRY:CARRY + tile, :] = _dot(u, wffn_ref[:, 0:D_FF])
    up = _dot(u, wffn_ref[:, D_FF:2 * D_FF])
    g = fcb_ref[...] + gate_ext[CARRY - 2:CARRY - 2 + tile, :] * fcw_ref[0:1, :]
    for k in range(1, FFN_CONV):
        g = g + gate_ext[CARRY - 2 + k:CARRY - 2 + k + tile, :] * fcw_ref[k:k + 1, :]
    tail = gate_ext[tile + CARRY - 2:tile + CARRY, :]
    fst_ref[0] = tail
    gate_ext[CARRY - 2:CARRY, :] = tail
    act = (_silu(g) * up).astype(BF16)
    x2 = x1 + _dot(act, wdown_ref[...])
    y_ref[0] = _rms_rows(x2, nfw_ref[...])


def _ffn_prompt(x1, nw, wffn, fcw, fcb, wdown, nfw):
    nb, seq, _ = x1.shape
    tile = SEQ_TILE
    tok_spec = pl.BlockSpec((1, tile, D_MODEL), lambda b, t: (b, t, 0))
    return pl.pallas_call(
        _ffn_prompt_kernel,
        grid=(nb, seq // tile),
        in_specs=[tok_spec] + [_const_spec(p.shape) for p in (nw, wffn, fcw, fcb, wdown, nfw)],
        out_specs=[tok_spec, pl.BlockSpec((1, FFN_CONV - 1, D_FF), lambda b, t: (b, 0, 0))],
        out_shape=[jax.ShapeDtypeStruct((nb, seq, D_MODEL), F32),
                   jax.ShapeDtypeStruct((nb, FFN_CONV - 1, D_FF), F32)],
        scratch_shapes=[pltpu.VMEM((tile + CARRY, D_FF), F32)],
        compiler_params=pltpu.CompilerParams(
            dimension_semantics=("arbitrary", "arbitrary"), vmem_limit_bytes=VMEM_LIMIT),
        name="ffn_prompt",
    )(x1, nw, wffn, fcw, fcb, wdown, nfw)


PACK_STEPS = D_IN_PACKED // PACK_BLOCK


def _pack_source_row(s):
    n_before_dt = (OFF_GB - OFF_Z) // PACK_BLOCK
    n_real = (OFF_DT - OFF_Z) // PACK_BLOCK
    per = PACK_BLOCK // NHEADS
    return NHEADS * jnp.where(s < n_before_dt, s * per, jnp.where(s < n_real, s * per + 1, SRC_DT // NHEADS))


def _sample_pre_kernel(x_ref, nw_ref, wsrc_ref, cw_ref, cb_ref, dtb_ref, a_ref, scw_ref, e_ref,
                       cst_ref, scst_ref,
                       z_ref, xs_ref, xdt_t_ref, da_ref, bc_ref, ysc_ref, ncst_ref, nscst_ref, wpack_ref,
                       u_ref, proj_ref):
    s = pl.program_id(0)
    last = PACK_STEPS - 1

    @pl.when(s == 0)
    def _():
        u_ref[...] = _rms_rows(x_ref[...], nw_ref[...]).astype(BF16)

    blk = wsrc_ref[...]
    row_i = lax.broadcasted_iota(jnp.int32, blk.shape, 0)
    blk = jnp.where(jnp.logical_and(s == last, row_i >= NHEADS), 0.0, blk)
    wpack_ref[...] = blk.T.astype(BF16)
    proj_ref[s] = _dot_nt(u_ref[...], blk.astype(BF16))

    @pl.when(s == last)
    def _():
        _sample_pre_epilogue(proj_ref, cw_ref, cb_ref, dtb_ref, a_ref, scw_ref, e_ref, cst_ref, scst_ref,
                             z_ref, xs_ref, xdt_t_ref, da_ref, bc_ref, ysc_ref, ncst_ref, nscst_ref)


def _sample_pre_epilogue(proj_ref, cw_ref, cb_ref, dtb_ref, a_ref, scw_ref, e_ref, cst_ref, scst_ref,
                         z_ref, xs_ref, xdt_t_ref, da_ref, bc_ref, ysc_ref, ncst_ref, nscst_ref):
    def section(off, width):
        return jnp.concatenate([proj_ref[j] for j in range(off // PACK_BLOCK, (off + width) // PACK_BLOCK)], axis=1)

    z_ref[...] = section(OFF_Z, D_SSD)
    xbc = section(OFF_XBC, CONV_DIM)
    conv = cb_ref[...] + xbc * cw_ref[SSD_CONV - 1:SSD_CONV, :]
    for k in range(SSD_CONV - 1):
        conv = conv + cst_ref[k] * cw_ref[k:k + 1, :]
    for k in range(SSD_CONV - 2):
        ncst_ref[k] = cst_ref[k + 1]
    ncst_ref[SSD_CONV - 2] = xbc
    act = _silu(conv)
    xs = act[:, 0:D_SSD]
    xs_ref[...] = xs
    for i in range(2 * NGROUPS):
        bc_ref[i] = act[:, D_SSD + i * NSTATE:D_SSD + (i + 1) * NSTATE]
    dt = _softplus(proj_ref[OFF_DT // PACK_BLOCK][:, 0:DT_PAD] + dtb_ref[...])
    xdt_t_ref[...] = (xs * _dot_sel_rhs(dt, e_ref[...])).T
    da_ref[...] = jnp.exp(dt * a_ref[...])[:, 0:NHEADS]
    gch = section(OFF_GC, D_SC) * section(OFF_H, D_SC)
    sc = gch * scw_ref[SC_CONV - 1:SC_CONV, :]
    for k in range(SC_CONV - 1):
        sc = sc + scst_ref[:, k, :] * scw_ref[k:k + 1, :]
    for k in range(SC_CONV - 2):
        nscst_ref[:, k, :] = scst_ref[:, k + 1, :]
    nscst_ref[:, SC_CONV - 2, :] = gch
    ysc_ref[...] = section(OFF_GB, D_SC) * sc


def _sample_pre(x, nw, w_in_t, cw, cb, dtb, a_pad, scw, expand, cst, scst):
    nb = x.shape[0]
    outs = [
        jax.ShapeDtypeStruct((nb, D_SSD), F32),
        jax.ShapeDtypeStruct((nb, D_SSD), F32),
        jax.ShapeDtypeStruct((D_SSD, nb), F32),
        jax.ShapeDtypeStruct((nb, NHEADS), F32),
        jax.ShapeDtypeStruct((2 * NGROUPS, nb, NSTATE), F32),
        jax.ShapeDtypeStruct((nb, D_SC), F32),
        jax.ShapeDtypeStruct((SSD_CONV - 1, nb, CONV_DIM), F32),
        jax.ShapeDtypeStruct((nb, SC_CONV - 1, D_SC), F32),
        jax.ShapeDtypeStruct((D_MODEL, D_IN_PACKED), BF16),
    ]
    consts = (x, nw)
    params = (cw, cb, dtb, a_pad, scw, expand, cst, scst)

    def whole(a):
        nd = len(a.shape)
        return pl.BlockSpec(a.shape, lambda s: (0,) * nd)

    return pl.pallas_call(
        _sample_pre_kernel,
        grid=(PACK_STEPS,),
        in_specs=[_const_spec(a.shape) for a in consts]
        + [pl.BlockSpec((pl.Element(PACK_BLOCK), pl.Element(D_MODEL)), lambda s: (_pack_source_row(s), 0))]
        + [_const_spec(a.shape) for a in params],
        out_specs=[whole(o) for o in outs[:-1]] + [pl.BlockSpec((D_MODEL, PACK_BLOCK), lambda s: (0, s))],
        out_shape=outs,
        scratch_shapes=[pltpu.VMEM((nb, D_MODEL), BF16),
                        pltpu.VMEM((PACK_STEPS, nb, PACK_BLOCK), F32)],
        compiler_params=pltpu.CompilerParams(
            dimension_semantics=("arbitrary",), vmem_limit_bytes=VMEM_LIMIT),
        name="sample_pre",
    )(x, nw, w_in_t, cw, cb, dtb, a_pad, scw, expand, cst, scst)


SAMPLE_HEADS = 2


def _sample_state_kernel(da_ref, st_ref, xdt_t_ref, b_ref, c_ref, nst_ref, y_ref):
    nb = st_ref.shape[0]
    head0 = pl.program_id(0) * SAMPLE_HEADS
    rows = SAMPLE_HEADS * HEAD_DIM
    c_t = c_ref[0].T.astype(BF16)
    lane = lax.broadcasted_iota(jnp.int32, (rows, nb), 1)
    ycols = jnp.zeros((rows, nb), F32)
    for b in range(nb):
        upd = xdt_t_ref[:, b:b + 1] * b_ref[0, b:b + 1, :]
        parts = []
        for hh in range(SAMPLE_HEADS):
            hnew = st_ref[b, hh] * da_ref[b, head0 + hh] + upd[hh * HEAD_DIM:(hh + 1) * HEAD_DIM, :]
            nst_ref[b, hh] = hnew
            parts.append(hnew)
        prod = _dot(jnp.concatenate(parts, axis=0).astype(BF16), c_t)
        ycols = jnp.where(lane == b, prod, ycols)
    y_ref[...] = ycols.T


def _sample_state(state, da, xdt_t, bc):
    nb = state.shape[0]
    rows = SAMPLE_HEADS * HEAD_DIM
    heads_per_group = NHEADS // NGROUPS
    st_spec = pl.BlockSpec((nb, SAMPLE_HEADS, HEAD_DIM, NSTATE), lambda i: (0, i, 0, 0))
    return pl.pallas_call(
        _sample_state_kernel,
        grid=(NHEADS // SAMPLE_HEADS,),
        in_specs=[pl.BlockSpec(memory_space=pltpu.SMEM),
                  st_spec,
                  pl.BlockSpec((rows, nb), lambda i: (i, 0)),
                  pl.BlockSpec((1, nb, NSTATE), lambda i: (i * SAMPLE_HEADS // heads_per_group, 0, 0)),
                  pl.BlockSpec((1, nb, NSTATE), lambda i: (NGROUPS + i * SAMPLE_HEADS // heads_per_group, 0, 0))],
        out_specs=[st_spec, pl.BlockSpec((nb, rows), lambda i: (0, i))],
        out_shape=[jax.ShapeDtypeStruct(state.shape, F32), jax.ShapeDtypeStruct((nb, NHEADS * HEAD_DIM), F32)],
        compiler_params=pltpu.CompilerParams(
            dimension_semantics=("arbitrary",), vmem_limit_bytes=VMEM_LIMIT),
        name="sample_state",
    )(da, state, xdt_t, bc, bc)


POST_OUT_BLOCKS = 2
POST_FFN_BLOCKS = 4
POST_DOWN_BLOCKS = 4
POST_OUT_ROWS = D_MIX // POST_OUT_BLOCKS
POST_FFN_COLS = 2 * D_FF // POST_FFN_BLOCKS
POST_DOWN_ROWS = D_FF // POST_DOWN_BLOCKS


def _sample_post_kernel(x_ref, y_ref, xs_ref, z_ref, ysc_ref, dx_ref, gnw_ref, wout_ref,
                        nw_ref, wffn_ref, fcw_ref, fcb_ref, wdown_ref, nfw_ref, fst_ref,
                        out_ref, nfst_ref, wout_bf_ref, wffn_bf_ref, wdown_bf_ref,
                        ycat_ref, x1_ref, u_ref, gu_ref, act_ref, acc_ref):
    s = pl.program_id(0)
    ffn0 = POST_OUT_BLOCKS
    down0 = ffn0 + POST_FFN_BLOCKS

    @pl.when(s == 0)
    def _():
        y = (y_ref[...] + dx_ref[...] * xs_ref[...]) * _silu(z_ref[...])
        gw = D_SSD // NGROUPS
        for g in range(NGROUPS):
            ycat_ref[:, g * gw:(g + 1) * gw] = _rms_rows(
                y[:, g * gw:(g + 1) * gw], gnw_ref[:, g * gw:(g + 1) * gw]).astype(BF16)
        ycat_ref[:, D_SSD:D_MIX] = ysc_ref[...].astype(BF16)
        x1_ref[...] = x_ref[...]

    for k in range(POST_OUT_BLOCKS):
        @pl.when(s == k)
        def _(k=k):
            w = wout_ref[...].astype(BF16)
            wout_bf_ref[...] = w
            x1_ref[...] += _dot(ycat_ref[:, k * POST_OUT_ROWS:(k + 1) * POST_OUT_ROWS], w)

    @pl.when(s == ffn0)
    def _():
        u_ref[...] = _rms_rows(x1_ref[...], nw_ref[...]).astype(BF16)

    for k in range(POST_FFN_BLOCKS):
        @pl.when(s == ffn0 + k)
        def _(k=k):
            w = wffn_ref[...].astype(BF16)
            wffn_bf_ref[...] = w
            gu_ref[:, k * POST_FFN_COLS:(k + 1) * POST_FFN_COLS] = _dot(u_ref[...], w)

    @pl.when(s == down0)
    def _():
        gate = gu_ref[:, 0:D_FF]
        g = fcb_ref[...] + gate * fcw_ref[FFN_CONV - 1:FFN_CONV, :]
        for k in range(FFN_CONV - 1):
            g = g + fst_ref[:, k, :] * fcw_ref[k:k + 1, :]
        for k in range(FFN_CONV - 2):
            nfst_ref[:, k, :] = fst_ref[:, k + 1, :]
        nfst_ref[:, FFN_CONV - 2, :] = gate
        act_ref[...] = (_silu(g) * gu_ref[:, D_FF:2 * D_FF]).astype(BF16)
        acc_ref[...] = x1_ref[...]

    for k in range(POST_DOWN_BLOCKS):
        @pl.when(s == down0 + k)
        def _(k=k):
            w = wdown_ref[...].astype(BF16)
            wdown_bf_ref[...] = w
            acc_ref[...] += _dot(act_ref[:, k * POST_DOWN_ROWS:(k + 1) * POST_DOWN_ROWS], w)

    @pl.when(s == down0 + POST_DOWN_BLOCKS - 1)
    def _():
        out_ref[...] = _rms_rows(acc_ref[...], nfw_ref[...])


def _sample_post(x, y, xs, z, ysc, dx, gnw, w_out, nw, w_ffn_in, fcw, fcb, w_down, nfw, fst):
    nb = x.shape[0]
    ffn0 = POST_OUT_BLOCKS
    down0 = ffn0 + POST_FFN_BLOCKS
    steps = down0 + POST_DOWN_BLOCKS

    def whole(a):
        nd = a.ndim
        return pl.BlockSpec(a.shape, lambda s: (0,) * nd)

    out_w = pl.BlockSpec((POST_OUT_ROWS, D_MODEL), lambda s: (jnp.clip(s, 0, POST_OUT_BLOCKS - 1), 0))
    ffn_w = pl.BlockSpec((D_MODEL, POST_FFN_COLS), lambda s: (0, jnp.clip(s - ffn0, 0, POST_FFN_BLOCKS - 1)))
    down_w = pl.BlockSpec((POST_DOWN_ROWS, D_MODEL), lambda s: (jnp.clip(s - down0, 0, POST_DOWN_BLOCKS - 1), 0))
    nfst_shape = jax.ShapeDtypeStruct((nb, FFN_CONV - 1, D_FF), F32)
    y_shape = jax.ShapeDtypeStruct((nb, D_MODEL), F32)
    return pl.pallas_call(
        _sample_post_kernel,
        grid=(steps,),
        in_specs=[_const_spec(a.shape) for a in (x, y, xs, z, ysc, dx, gnw)] + [out_w, _const_spec(nw.shape), ffn_w,
                  _const_spec(fcw.shape), _const_spec(fcb.shape), down_w, _const_spec(nfw.shape),
                  _const_spec(fst.shape)],
        out_specs=[whole(y_shape), whole(nfst_shape), out_w, ffn_w, down_w],
        out_shape=[y_shape, nfst_shape,
                   jax.ShapeDtypeStruct(w_out.shape, BF16),
                   jax.ShapeDtypeStruct(w_ffn_in.shape, BF16),
                   jax.ShapeDtypeStruct(w_down.shape, BF16)],
        scratch_shapes=[
            pltpu.VMEM((nb, D_MIX), BF16),
            pltpu.VMEM((nb, D_MODEL), F32),
            pltpu.VMEM((nb, D_MODEL), BF16),
            pltpu.VMEM((nb, 2 * D_FF), F32),
            pltpu.VMEM((nb, D_FF), BF16),
            pltpu.VMEM((nb, D_MODEL), F32),
        ],
        compiler_params=pltpu.CompilerParams(
            dimension_semantics=("arbitrary",), vmem_limit_bytes=VMEM_LIMIT),
        name="sample_post",
    )(x, y, xs, z, ysc, dx, gnw, w_out, nw, w_ffn_in, fcw, fcb, w_down, nfw, fst)


def kernel(x_prompt, x_sample, state_ssm, state_ssd_conv, state_short_conv, state_ffn_conv,
           norm_mix_w, w_in, ssd_conv_w, ssd_conv_b, ssd_dt_bias, ssd_a_log, ssd_d, ssd_norm_w,
           sc_conv_w, w_out, norm_ffn_w, w_ffn_in, ffn_conv_w, ffn_conv_b, w_down, norm_final_w):
    depth = w_in.shape[0]
    assert depth == 1
    w_in_t = jnp.swapaxes(w_in[0], 0, 1)
    nw = norm_mix_w[0].reshape(1, D_MODEL)
    nw2 = norm_ffn_w[0].reshape(1, D_MODEL)
    nfw = norm_final_w.reshape(1, D_MODEL)
    cw = ssd_conv_w[0]
    cb = ssd_conv_b[0].reshape(1, CONV_DIM)
    dtb = jnp.pad(ssd_dt_bias[0], (0, DT_PAD - NHEADS)).reshape(1, DT_PAD)
    a_pad = jnp.pad(-jnp.exp(ssd_a_log[0]), (0, DT_PAD - NHEADS)).reshape(1, DT_PAD)
    dx = jnp.repeat(ssd_d[0], HEAD_DIM).reshape(1, D_SSD)
    gnw = ssd_norm_w[0].reshape(1, D_SSD)
    scw = sc_conv_w[0]
    fcw = ffn_conv_w[0]
    fcb = ffn_conv_b[0].reshape(1, D_FF)
    head_of_lane = jnp.arange(D_SSD, dtype=jnp.int32) // HEAD_DIM
    expand = (jnp.arange(DT_PAD, dtype=jnp.int32)[:, None] == head_of_lane[None, :]).astype(BF16)

    nbs = x_sample.shape[0]
    xs_in = x_sample.reshape(nbs, D_MODEL)
    cst = jnp.swapaxes(state_ssd_conv[0], 0, 1)
    scst, fst = state_short_conv[0], state_ffn_conv[0]
    z, xs, xdt_t, da, bc, ysc, ncst, nscst, win = _sample_pre(
        xs_in, nw, w_in_t, cw, cb, dtb, a_pad, scw, expand, cst, scst)
    s_ssm, y_s = _sample_state(state_ssm[0], da, xdt_t, bc)
    y_sample, nfst, wout, wffn, wdown = _sample_post(
        xs_in, y_s, xs, z, ysc, dx, gnw, w_out[0], nw2, w_ffn_in[0], fcw, fcb, w_down[0], nfw, fst)

    x1, p_ssm, p_cst, p_scst = _mixer_prompt(x_prompt, nw, win, cw, cb, dtb, a_pad, dx, gnw, scw, wout, expand)
    y_prompt, p_fst = _ffn_prompt(x1, nw2, wffn, fcw, fcb, wdown, nfw)

    return (y_prompt, y_sample.reshape(nbs, 1, D_MODEL),
            p_ssm[None], p_cst[None], p_scst[None], p_fst[None],
            s_ssm[None], jnp.swapaxes(ncst, 0, 1)[None], nscst[None], nfst[None])
```

```python
import functools

import jax
import jax.numpy as jnp
from jax import lax
from jax.experimental import pallas as pl
from jax.experimental.pallas import tpu as pltpu

D_MODEL = 1024
D_SSD = 1024
D_SC = 1024
NHEADS = 16
HEAD_DIM = 64
NGROUPS = 2
NSTATE = 128
SSD_CONV = 4
CONV_DIM = D_SSD + 2 * NGROUPS * NSTATE
SC_CONV = 3
D_FF = 2816
FFN_CONV = 3
EPS = 1e-5
D_MIX = D_SSD + D_SC
D_IN_PROJ = D_SSD + CONV_DIM + NHEADS + 3 * D_SC

LANES = 128
MXU_COLS = 256
OFF_Z = 0
OFF_XBC = OFF_Z + D_SSD
OFF_DT = OFF_XBC + CONV_DIM
OFF_GB = OFF_DT + NHEADS
OFF_GC = OFF_GB + D_SC
OFF_H = OFF_GC + D_SC
DT_PAD = LANES

CHUNK = 128
CARRY = 8
SEQ_TILE = 512
VMEM_LIMIT = 56 * 1024 * 1024

F32 = jnp.float32
BF16 = jnp.bfloat16


def _dot(a, b):
    return jnp.dot(a, b, preferred_element_type=F32)


def _dot_nt(a, b_t):
    return lax.dot_general(a, b_t, (((1,), (1,)), ((), ())), preferred_element_type=F32)


def _split3(v):
    hi = v.astype(BF16)
    r = v - hi.astype(F32)
    mid = r.astype(BF16)
    lo = (r - mid.astype(F32)).astype(BF16)
    return hi, mid, lo


def _dot_sel_rhs(v, sel):
    hi, mid, lo = _split3(v)
    return (_dot(hi, sel) + _dot(mid, sel)) + _dot(lo, sel)


def _dot_sel_lhs(sel, v):
    hi, mid, lo = _split3(v)
    return (_dot(sel, hi) + _dot(sel, mid)) + _dot(sel, lo)


def _silu(v):
    h = 0.5 * v
    return h + h * jnp.tanh(h)


def _softplus(v):
    return jnp.maximum(v, 0.0) + jnp.log1p(jnp.exp(-jnp.abs(v)))


def _rms_rows(x, w):
    ms = jnp.mean(x * x, axis=-1, keepdims=True)
    return x * lax.rsqrt(ms + EPS) * w


def _mixer_prompt_kernel(tiles_per_seq,
                         x_ref, xres_ref, nw_ref, win_ref, cw_ref, cb_ref, dtb_ref, a_ref, dx_ref,
                         gnw_ref, scw_ref, wout_ref, e_ref,
                         x1_ref, ssm_ref, cst_ref, scst_ref,
                         xbc_ext, sc_ext, u_ref, act_ref, dt_ref, z_ref, gb_ref, s_ref, y_ref, ysc_ref, ycat_ref):
    tile = x_ref.shape[1]
    g = pl.program_id(0)
    last = pl.num_programs(0) - 1
    t = lax.rem(g, tiles_per_seq)

    @pl.when(g == 0)
    def _():
        ycat_ref[...] = jnp.zeros_like(ycat_ref)

    @pl.when(t == 0)
    def _():
        xbc_ext[0:CARRY, :] = jnp.zeros((CARRY, CONV_DIM), F32)
        sc_ext[0:CARRY, :] = jnp.zeros((CARRY, D_SC), F32)
        s_ref[...] = jnp.zeros_like(s_ref)

    def out_block(off):
        x1_ref[0, :, off:off + MXU_COLS] = xres_ref[0, :, off:off + MXU_COLS] + _dot(
            ycat_ref[...], wout_ref[:, off:off + MXU_COLS])

    @pl.when(g == last)
    def _():
        for off in range(0, D_MODEL, MXU_COLS):
            out_block(off)

    @pl.when(g < last)
    def _():
        _mixer_tile(tile, t == tiles_per_seq - 1, out_block,
                    x_ref, nw_ref, win_ref, cw_ref, cb_ref, dtb_ref, a_ref, dx_ref, gnw_ref, scw_ref, e_ref,
                    ssm_ref, cst_ref, scst_ref,
                    xbc_ext, sc_ext, u_ref, act_ref, dt_ref, z_ref, gb_ref, s_ref, y_ref, ysc_ref, ycat_ref)


def _mixer_tile(tile, ends_sequence, out_block,
                x_ref, nw_ref, win_ref, cw_ref, cb_ref, dtb_ref, a_ref, dx_ref, gnw_ref, scw_ref, e_ref,
                ssm_ref, cst_ref, scst_ref,
                xbc_ext, sc_ext, u_ref, act_ref, dt_ref, z_ref, gb_ref, s_ref, y_ref, ysc_ref, ycat_ref):
    u_ref[...] = _rms_rows(x_ref[0], nw_ref[...]).astype(BF16)

    def project(dst, row0, base, off, width=MXU_COLS):
        dst[row0:row0 + tile, off:off + width] = _dot_nt(u_ref[...], win_ref[base + off:base + off + width, :])

    def conv_block(off):
        conv = cb_ref[:, off:off + LANES] + (
            xbc_ext[CARRY - 3:CARRY - 3 + tile, off:off + LANES] * cw_ref[0:1, off:off + LANES])
        for k in range(1, SSD_CONV):
            conv = conv + (xbc_ext[CARRY - 3 + k:CARRY - 3 + k + tile, off:off + LANES]
                           * cw_ref[k:k + 1, off:off + LANES])
        act_ref[:, off:off + LANES] = _silu(conv)

    def sc_block(off):
        sc = sc_ext[CARRY - 2:CARRY - 2 + tile, off:off + LANES] * scw_ref[0:1, off:off + LANES]
        for k in range(1, SC_CONV):
            sc = sc + (sc_ext[CARRY - 2 + k:CARRY - 2 + k + tile, off:off + LANES]
                       * scw_ref[k:k + 1, off:off + LANES])
        ysc_ref[:, off:off + LANES] = (gb_ref[:, off:off + LANES] * sc).astype(BF16)

    for off in range(0, CONV_DIM, MXU_COLS):
        project(xbc_ext, CARRY, OFF_XBC, off)
        for sub in range(off, off + MXU_COLS, LANES):
            conv_block(sub)
    tail = xbc_ext[tile + CARRY - 3:tile + CARRY, :]
    cst_ref[0] = tail
    xbc_ext[CARRY - 3:CARRY, :] = tail
    project(dt_ref, 0, OFF_DT, 0, DT_PAD)
    for off in range(0, D_SC, MXU_COLS):
        u = u_ref[...]
        sc_ext[CARRY:CARRY + tile, off:off + MXU_COLS] = (
            _dot_nt(u, win_ref[OFF_GC + off:OFF_GC + off + MXU_COLS, :])
            * _dot_nt(u, win_ref[OFF_H + off:OFF_H + off + MXU_COLS, :]))

    fillers = []
    for k in range(D_SC // MXU_COLS):
        fillers.append(lambda k=k: out_block(k * MXU_COLS))
        fillers.append(lambda k=k: project(gb_ref, 0, OFF_GB, k * MXU_COLS))
        fillers.append(lambda k=k: sc_block(k * MXU_COLS))
        fillers.append(lambda k=k: sc_block(k * MXU_COLS + LANES))
        fillers.append(lambda k=k: project(z_ref, 0, OFF_Z, k * MXU_COLS))
    n_slots = (tile // CHUNK) * (NHEADS // 2)
    slot_of_filler = [(i * n_slots) // len(fillers) for i in range(len(fillers))]

    row_i = lax.broadcasted_iota(jnp.int32, (CHUNK, CHUNK), 0)
    col_i = lax.broadcasted_iota(jnp.int32, (CHUNK, CHUNK), 1)
    causal = row_i >= col_i
    tri = causal.astype(BF16)
    lane_i = lax.broadcasted_iota(jnp.int32, (1, LANES), 1)
    keeps = ((lane_i < HEAD_DIM).astype(BF16), (lane_i >= HEAD_DIM).astype(BF16))
    neg_inf = jnp.float32(-jnp.inf)
    pairs_per_group = NHEADS // NGROUPS // 2

    for c in range(tile // CHUNK):
        r0 = c * CHUNK
        dt = _softplus(dt_ref[r0:r0 + CHUNK, :] + dtb_ref[...])
        acum = _dot_sel_lhs(tri, dt * a_ref[...])
        acum_t = acum.T
        dt_t = dt.T
        ea = jnp.exp(acum)
        w_t = jnp.exp(acum_t[:, CHUNK - 1:CHUNK] - acum_t) * dt_t
        cd = jnp.broadcast_to(ea[CHUNK - 1:CHUNK, :], (8, LANES))
        cdx = _dot_sel_rhs(cd, e_ref[...])[0:1, :]
        for grp in range(NGROUPS):
            bg = act_ref[r0:r0 + CHUNK, D_SSD + grp * NSTATE:D_SSD + (grp + 1) * NSTATE]
            cg = act_ref[r0:r0 + CHUNK, D_SSD + (NGROUPS + grp) * NSTATE:D_SSD + (NGROUPS + grp + 1) * NSTATE]
            cb = lax.dot_general(cg.astype(BF16), bg.astype(BF16), (((1,), (1,)), ((), ())),
                                 preferred_element_type=F32)
            bg_t = bg.T
            for jp in range(pairs_per_group):
                j = grp * pairs_per_group + jp
                xp = act_ref[r0:r0 + CHUNK, j * LANES:(j + 1) * LANES].astype(BF16)
                sp = s_ref[:, j * LANES:(j + 1) * LANES]
                spb = sp.astype(BF16)
                lhs_parts, rhs_parts, bw_parts, x_parts = [], [], [], []
                for half, keep in enumerate(keeps):
                    h = 2 * j + half
                    xh = xp * keep
                    seg = acum[:, h:h + 1] - acum_t[h:h + 1, :]
                    decay = jnp.exp(jnp.where(causal, seg, neg_inf))
                    m = cb * decay * dt_t[h:h + 1, :]
                    eac = ea[:, h:h + 1] * cg
                    lhs_parts += [m.astype(BF16), eac.astype(BF16)]
                    rhs_parts += [xh, spb * keep]
                    bw_parts.append((bg_t * w_t[h:h + 1, :]).astype(BF16))
                    x_parts.append(xh)
                y_ref[r0:r0 + CHUNK, j * LANES:(j + 1) * LANES] = _dot(
                    jnp.concatenate(lhs_parts, axis=1), jnp.concatenate(rhs_parts, axis=0))
                snew = _dot(jnp.concatenate(bw_parts, axis=1), jnp.concatenate(x_parts, axis=0))
                s_ref[:, j * LANES:(j + 1) * LANES] = cdx[:, j * LANES:(j + 1) * LANES] * sp + snew
                slot = c * (NHEADS // 2) + j
                for i, filler in enumerate(fillers):
                    if slot_of_filler[i] == slot:
                        filler()
    sc_tail = sc_ext[tile + CARRY - 2:tile + CARRY, :]
    scst_ref[0] = sc_tail
    sc_ext[CARRY - 2:CARRY, :] = sc_tail

    gw = D_SSD // NGROUPS
    for grp in range(NGROUPS):
        ssq = jnp.zeros((tile, LANES), F32)
        for off in range(grp * gw, (grp + 1) * gw, LANES):
            yb = ((y_ref[:, off:off + LANES] + dx_ref[:, off:off + LANES] * act_ref[:, off:off + LANES])
                  * _silu(z_ref[:, off:off + LANES]))
            y_ref[:, off:off + LANES] = yb
            ssq = ssq + yb * yb
        scale = lax.rsqrt(jnp.sum(ssq, axis=-1, keepdims=True) * (1.0 / gw) + EPS)
        for off in range(grp * gw, (grp + 1) * gw, LANES):
            ycat_ref[:, off:off + LANES] = (y_ref[:, off:off + LANES] * scale
                                            * gnw_ref[:, off:off + LANES]).astype(BF16)
    ycat_ref[:, D_SSD:D_MIX] = ysc_ref[...]

    @pl.when(ends_sequence)
    def _():
        for j in range(NHEADS // 2):
            ssm_ref[0, 2 * j:2 * j + 2] = s_ref[:, j * LANES:(j + 1) * LANES].T.reshape(2, HEAD_DIM, NSTATE)


def _const_spec(shape):
    nd = len(shape)
    return pl.BlockSpec(shape, lambda *_: (0,) * nd, pipeline_mode=pl.Buffered(1))


def _mixer_prompt(x, nw, win, cw, cb, dtb, a_pad, dx, gnw, scw, wout, expand):
    nb, seq, _ = x.shape
    tile = SEQ_TILE
    nt = seq // tile
    total = nb * nt

    def tile_of(g, lag):
        i = jnp.clip(g - lag, 0, total - 1)
        return i // nt, i % nt

    cur_spec = pl.BlockSpec((1, tile, D_MODEL), lambda g: (*tile_of(g, 0), 0))
    prev_spec = pl.BlockSpec((1, tile, D_MODEL), lambda g: (*tile_of(g, 1), 0))
    consts = (nw, win, cw, cb, dtb, a_pad, dx, gnw, scw, wout, expand)
    return pl.pallas_call(
        functools.partial(_mixer_prompt_kernel, nt),
        grid=(total + 1,),
        in_specs=[cur_spec, prev_spec] + [_const_spec(p.shape) for p in consts],
        out_specs=[
            prev_spec,
            pl.BlockSpec((1, NHEADS, HEAD_DIM, NSTATE), lambda g: (tile_of(g, 0)[0], 0, 0, 0)),
            pl.BlockSpec((1, SSD_CONV - 1, CONV_DIM), lambda g: (tile_of(g, 0)[0], 0, 0)),
            pl.BlockSpec((1, SC_CONV - 1, D_SC), lambda g: (tile_of(g, 0)[0], 0, 0)),
        ],
        out_shape=[
            jax.ShapeDtypeStruct((nb, seq, D_MODEL), F32),
            jax.ShapeDtypeStruct((nb, NHEADS, HEAD_DIM, NSTATE), F32),
            jax.ShapeDtypeStruct((nb, SSD_CONV - 1, CONV_DIM), F32),
            jax.ShapeDtypeStruct((nb, SC_CONV - 1, D_SC), F32),
        ],
        scratch_shapes=[
            pltpu.VMEM((tile + CARRY, CONV_DIM), F32),
            pltpu.VMEM((tile + CARRY, D_SC), F32),
            pltpu.VMEM((tile, D_MODEL), BF16),
            pltpu.VMEM((tile, CONV_DIM), F32),
            pltpu.VMEM((tile, DT_PAD), F32),
            pltpu.VMEM((tile, D_SSD), F32),
            pltpu.VMEM((tile, D_SC), F32),
            pltpu.VMEM((NSTATE, D_SSD), F32),
            pltpu.VMEM((tile, D_SSD), F32),
            pltpu.VMEM((tile, D_SC), BF16),
            pltpu.VMEM((tile, D_MIX), BF16),
        ],
        compiler_params=pltpu.CompilerParams(
            dimension_semantics=("arbitrary",), vmem_limit_bytes=VMEM_LIMIT),
        name="mixer_prompt",
    )(x, x, *consts)


def _ffn_prompt_kernel(x1_ref, nw_ref, wffn_ref, fcw_ref, fcb_ref, wdown_ref, nfw_ref,
                       y_ref, fst_ref, gate_ext):
    tile = x1_ref.shape[1]
    t = pl.program_id(1)

    @pl.when(t == 0)
    def _():
        gate_ext[0:CARRY, :] = jnp.zeros((CARRY, D_FF), F32)

    x1 = x1_ref[0]
    u = _rms_rows(x1, nw_ref[...]).astype(BF16)
    gate_ext[CARRY:CARRY + tile, :] = _dot(u, wffn_ref[:, 0:D_FF])
    up = _dot(u, wffn_ref[:, D_FF:2 * D_FF])
    g = fcb_ref[...] + gate_ext[CARRY - 2:CARRY - 2 + tile, :] * fcw_ref[0:1, :]
    for k in range(1, FFN_CONV):
        g = g + gate_ext[CARRY - 2 + k:CARRY - 2 + k + tile, :] * fcw_ref[k:k + 1, :]
    tail = gate_ext[tile + CARRY - 2:tile + CARRY, :]
    fst_ref[0] = tail
    gate_ext[CARRY - 2:CARRY, :] = tail
    act = (_silu(g) * up).astype(BF16)
    x2 = x1 + _dot(act, wdown_ref[...])
    y_ref[0] = _rms_rows(x2, nfw_ref[...])


def _ffn_prompt(x1, nw, wffn, fcw, fcb, wdown, nfw):
    nb, seq, _ = x1.shape
    tile = SEQ_TILE
    tok_spec = pl.BlockSpec((1, tile, D_MODEL), lambda b, t: (b, t, 0))
    return pl.pallas_call(
        _ffn_prompt_kernel,
        grid=(nb, seq // tile),
        in_specs=[tok_spec] + [_const_spec(p.shape) for p in (nw, wffn, fcw, fcb, wdown, nfw)],
        out_specs=[tok_spec, pl.BlockSpec((1, FFN_CONV - 1, D_FF), lambda b, t: (b, 0, 0))],
        out_shape=[jax.ShapeDtypeStruct((nb, seq, D_MODEL), F32),
                   jax.ShapeDtypeStruct((nb, FFN_CONV - 1, D_FF), F32)],
        scratch_shapes=[pltpu.VMEM((tile + CARRY, D_FF), F32)],
        compiler_params=pltpu.CompilerParams(
            dimension_semantics=("arbitrary", "arbitrary"), vmem_limit_bytes=VMEM_LIMIT),
        name="ffn_prompt",
    )(x1, nw, wffn, fcw, fcb, wdown, nfw)


CAST_ROWS = 512


def _sample_pre_kernel(x_ref, nw_ref, wf32_ref, cw_ref, cb_ref, dtb_ref, a_ref, scw_ref, e_ref,
                       cst_ref, scst_ref,
                       z_ref, xs_ref, xdt_t_ref, da_ref, bc_ref, ysc_ref, ncst_ref, nscst_ref, win_ref):
    for r0 in range(0, D_IN_PROJ, CAST_ROWS):
        r1 = min(r0 + CAST_ROWS, D_IN_PROJ)
        win_ref[r0:r1, :] = wf32_ref[r0:r1, :].astype(BF16)
    x = x_ref[...]
    u = _rms_rows(x, nw_ref[...]).astype(BF16)
    z_ref[...] = _dot_nt(u, win_ref[OFF_Z:OFF_Z + D_SSD, :])
    xbc = _dot_nt(u, win_ref[OFF_XBC:OFF_XBC + CONV_DIM, :])
    conv = cb_ref[...] + xbc * cw_ref[SSD_CONV - 1:SSD_CONV, :]
    for k in range(SSD_CONV - 1):
        conv = conv + cst_ref[k] * cw_ref[k:k + 1, :]
    for k in range(SSD_CONV - 2):
        ncst_ref[k] = cst_ref[k + 1]
    ncst_ref[SSD_CONV - 2] = xbc
    act = _silu(conv)
    xs = act[:, 0:D_SSD]
    xs_ref[...] = xs
    for i in range(2 * NGROUPS):
        bc_ref[i] = act[:, D_SSD + i * NSTATE:D_SSD + (i + 1) * NSTATE]
    dt = _softplus(_dot_nt(u, win_ref[OFF_DT:OFF_DT + DT_PAD, :]) + dtb_ref[...])
    xdt_t_ref[...] = (xs * _dot_sel_rhs(dt, e_ref[...])).T
    da_ref[...] = jnp.exp(dt * a_ref[...])[:, 0:NHEADS]
    gc = _dot_nt(u, win_ref[OFF_GC:OFF_GC + D_SC, :])
    hh = _dot_nt(u, win_ref[OFF_H:OFF_H + D_SC, :])
    gch = gc * hh
    sc = gch * scw_ref[SC_CONV - 1:SC_CONV, :]
    for k in range(SC_CONV - 1):
        sc = sc + scst_ref[:, k, :] * scw_ref[k:k + 1, :]
    for k in range(SC_CONV - 2):
        nscst_ref[:, k, :] = scst_ref[:, k + 1, :]
    nscst_ref[:, SC_CONV - 2, :] = gch
    gb = _dot_nt(u, win_ref[OFF_GB:OFF_GB + D_SC, :])
    ysc_ref[...] = gb * sc


def _sample_pre(x, nw, w_in_t, cw, cb, dtb, a_pad, scw, expand, cst, scst):
    nb = x.shape[0]
    outs = [
        jax.ShapeDtypeStruct((nb, D_SSD), F32),
        jax.ShapeDtypeStruct((nb, D_SSD), F32),
        jax.ShapeDtypeStruct((D_SSD, nb), F32),
        jax.ShapeDtypeStruct((nb, NHEADS), F32),
        jax.ShapeDtypeStruct((2 * NGROUPS, nb, NSTATE), F32),
        jax.ShapeDtypeStruct((nb, D_SC), F32),
        jax.ShapeDtypeStruct((SSD_CONV - 1, nb, CONV_DIM), F32),
        jax.ShapeDtypeStruct((nb, SC_CONV - 1, D_SC), F32),
        jax.ShapeDtypeStruct((D_IN_PROJ, D_MODEL), BF16),
    ]
    return pl.pallas_call(
        _sample_pre_kernel,
        out_shape=outs,
        compiler_params=pltpu.CompilerParams(vmem_limit_bytes=VMEM_LIMIT),
        name="sample_pre",
    )(x, nw, w_in_t, cw, cb, dtb, a_pad, scw, expand, cst, scst)


SAMPLE_HEADS = 2


def _sample_state_kernel(da_ref, st_ref, xdt_t_ref, b_ref, c_ref, nst_ref, y_ref):
    nb = st_ref.shape[0]
    head0 = pl.program_id(0) * SAMPLE_HEADS
    rows = SAMPLE_HEADS * HEAD_DIM
    c_t = c_ref[0].T.astype(BF16)
    lane = lax.broadcasted_iota(jnp.int32, (rows, nb), 1)
    ycols = jnp.zeros((rows, nb), F32)
    for b in range(nb):
        upd = xdt_t_ref[:, b:b + 1] * b_ref[0, b:b + 1, :]
        parts = []
        for hh in range(SAMPLE_HEADS):
            hnew = st_ref[b, hh] * da_ref[b, head0 + hh] + upd[hh * HEAD_DIM:(hh + 1) * HEAD_DIM, :]
            nst_ref[b, hh] = hnew
            parts.append(hnew)
        prod = _dot(jnp.concatenate(parts, axis=0).astype(BF16), c_t)
        ycols = jnp.where(lane == b, prod, ycols)
    y_ref[...] = ycols.T


def _sample_state(state, da, xdt_t, bc):
    nb = state.shape[0]
    rows = SAMPLE_HEADS * HEAD_DIM
    heads_per_group = NHEADS // NGROUPS
    st_spec = pl.BlockSpec((nb, SAMPLE_HEADS, HEAD_DIM, NSTATE), lambda i: (0, i, 0, 0))
    return pl.pallas_call(
        _sample_state_kernel,
        grid=(NHEADS // SAMPLE_HEADS,),
        in_specs=[pl.BlockSpec(memory_space=pltpu.SMEM),
                  st_spec,
                  pl.BlockSpec((rows, nb), lambda i: (i, 0)),
                  pl.BlockSpec((1, nb, NSTATE), lambda i: (i * SAMPLE_HEADS // heads_per_group, 0, 0)),
                  pl.BlockSpec((1, nb, NSTATE), lambda i: (NGROUPS + i * SAMPLE_HEADS // heads_per_group, 0, 0))],
        out_specs=[st_spec, pl.BlockSpec((nb, rows), lambda i: (0, i))],
        out_shape=[jax.ShapeDtypeStruct(state.shape, F32), jax.ShapeDtypeStruct((nb, NHEADS * HEAD_DIM), F32)],
        compiler_params=pltpu.CompilerParams(
            dimension_semantics=("arbitrary",), vmem_limit_bytes=VMEM_LIMIT),
        name="sample_state",
    )(da, state, xdt_t, bc, bc)


POST_OUT_BLOCKS = 2
POST_FFN_BLOCKS = 4
POST_DOWN_BLOCKS = 4
POST_OUT_ROWS = D_MIX // POST_OUT_BLOCKS
POST_FFN_COLS = 2 * D_FF // POST_FFN_BLOCKS
POST_DOWN_ROWS = D_FF // POST_DOWN_BLOCKS


def _sample_post_kernel(x_ref, y_ref, xs_ref, z_ref, ysc_ref, dx_ref, gnw_ref, wout_ref,
                        nw_ref, wffn_ref, fcw_ref, fcb_ref, wdown_ref, nfw_ref, fst_ref,
                        out_ref, nfst_ref, wout_bf_ref, wffn_bf_ref, wdown_bf_ref,
                        ycat_ref, x1_ref, u_ref, gu_ref, act_ref, acc_ref):
    s = pl.program_id(0)
    ffn0 = POST_OUT_BLOCKS
    down0 = ffn0 + POST_FFN_BLOCKS

    @pl.when(s == 0)
    def _():
        y = (y_ref[...] + dx_ref[...] * xs_ref[...]) * _silu(z_ref[...])
        gw = D_SSD // NGROUPS
        for g in range(NGROUPS):
            ycat_ref[:, g * gw:(g + 1) * gw] = _rms_rows(
                y[:, g * gw:(g + 1) * gw], gnw_ref[:, g * gw:(g + 1) * gw]).astype(BF16)
        ycat_ref[:, D_SSD:D_MIX] = ysc_ref[...].astype(BF16)
        x1_ref[...] = x_ref[...]

    for k in range(POST_OUT_BLOCKS):
        @pl.when(s == k)
        def _(k=k):
            w = wout_ref[...].astype(BF16)
            wout_bf_ref[...] = w
            x1_ref[...] += _dot(ycat_ref[:, k * POST_OUT_ROWS:(k + 1) * POST_OUT_ROWS], w)

    @pl.when(s == ffn0)
    def _():
        u_ref[...] = _rms_rows(x1_ref[...], nw_ref[...]).astype(BF16)

    for k in range(POST_FFN_BLOCKS):
        @pl.when(s == ffn0 + k)
        def _(k=k):
            w = wffn_ref[...].astype(BF16)
            wffn_bf_ref[...] = w
            gu_ref[:, k * POST_FFN_COLS:(k + 1) * POST_FFN_COLS] = _dot(u_ref[...], w)

    @pl.when(s == down0)
    def _():
        gate = gu_ref[:, 0:D_FF]
        g = fcb_ref[...] + gate * fcw_ref[FFN_CONV - 1:FFN_CONV, :]
        for k in range(FFN_CONV - 1):
            g = g + fst_ref[:, k, :] * fcw_ref[k:k + 1, :]
        for k in range(FFN_CONV - 2):
            nfst_ref[:, k, :] = fst_ref[:, k + 1, :]
        nfst_ref[:, FFN_CONV - 2, :] = gate
        act_ref[...] = (_silu(g) * gu_ref[:, D_FF:2 * D_FF]).astype(BF16)
        acc_ref[...] = x1_ref[...]

    for k in range(POST_DOWN_BLOCKS):
        @pl.when(s == down0 + k)
        def _(k=k):
            w = wdown_ref[...].astype(BF16)
            wdown_bf_ref[...] = w
            acc_ref[...] += _dot(act_ref[:, k * POST_DOWN_ROWS:(k + 1) * POST_DOWN_ROWS], w)

    @pl.when(s == down0 + POST_DOWN_BLOCKS - 1)
    def _():
        out_ref[...] = _rms_rows(acc_ref[...], nfw_ref[...])


def _sample_post(x, y, xs, z, ysc, dx, gnw, w_out, nw, w_ffn_in, fcw, fcb, w_down, nfw, fst):
    nb = x.shape[0]
    ffn0 = POST_OUT_BLOCKS
    down0 = ffn0 + POST_FFN_BLOCKS
    steps = down0 + POST_DOWN_BLOCKS

    def whole(a):
        nd = a.ndim
        return pl.BlockSpec(a.shape, lambda s: (0,) * nd)

    out_w = pl.BlockSpec((POST_OUT_ROWS, D_MODEL), lambda s: (jnp.clip(s, 0, POST_OUT_BLOCKS - 1), 0))
    ffn_w = pl.BlockSpec((D_MODEL, POST_FFN_COLS), lambda s: (0, jnp.clip(s - ffn0, 0, POST_FFN_BLOCKS - 1)))
    down_w = pl.BlockSpec((POST_DOWN_ROWS, D_MODEL), lambda s: (jnp.clip(s - down0, 0, POST_DOWN_BLOCKS - 1), 0))
    nfst_shape = jax.ShapeDtypeStruct((nb, FFN_CONV - 1, D_FF), F32)
    y_shape = jax.ShapeDtypeStruct((nb, D_MODEL), F32)
    return pl.pallas_call(
        _sample_post_kernel,
        grid=(steps,),
        in_specs=[_const_spec(a.shape) for a in (x, y, xs, z, ysc, dx, gnw)] + [out_w, _const_spec(nw.shape), ffn_w,
                  _const_spec(fcw.shape), _const_spec(fcb.shape), down_w, _const_spec(nfw.shape),
                  _const_spec(fst.shape)],
        out_specs=[whole(y_shape), whole(nfst_shape), out_w, ffn_w, down_w],
        out_shape=[y_shape, nfst_shape,
                   jax.ShapeDtypeStruct(w_out.shape, BF16),
                   jax.ShapeDtypeStruct(w_ffn_in.shape, BF16),
                   jax.ShapeDtypeStruct(w_down.shape, BF16)],
        scratch_shapes=[
            pltpu.VMEM((nb, D_MIX), BF16),
            pltpu.VMEM((nb, D_MODEL), F32),
            pltpu.VMEM((nb, D_MODEL), BF16),
            pltpu.VMEM((nb, 2 * D_FF), F32),
            pltpu.VMEM((nb, D_FF), BF16),
            pltpu.VMEM((nb, D_MODEL), F32),
        ],
        compiler_params=pltpu.CompilerParams(
            dimension_semantics=("arbitrary",), vmem_limit_bytes=VMEM_LIMIT),
        name="sample_post",
    )(x, y, xs, z, ysc, dx, gnw, w_out, nw, w_ffn_in, fcw, fcb, w_down, nfw, fst)


def kernel(x_prompt, x_sample, state_ssm, state_ssd_conv, state_short_conv, state_ffn_conv,
           norm_mix_w, w_in, ssd_conv_w, ssd_conv_b, ssd_dt_bias, ssd_a_log, ssd_d, ssd_norm_w,
           sc_conv_w, w_out, norm_ffn_w, w_ffn_in, ffn_conv_w, ffn_conv_b, w_down, norm_final_w):
    depth = w_in.shape[0]
    assert depth == 1
    w_in_t = jnp.swapaxes(w_in[0], 0, 1)
    nw = norm_mix_w[0].reshape(1, D_MODEL)
    nw2 = norm_ffn_w[0].reshape(1, D_MODEL)
    nfw = norm_final_w.reshape(1, D_MODEL)
    cw = ssd_conv_w[0]
    cb = ssd_conv_b[0].reshape(1, CONV_DIM)
    dtb = jnp.pad(ssd_dt_bias[0], (0, DT_PAD - NHEADS)).reshape(1, DT_PAD)
    a_pad = jnp.pad(-jnp.exp(ssd_a_log[0]), (0, DT_PAD - NHEADS)).reshape(1, DT_PAD)
    dx = jnp.repeat(ssd_d[0], HEAD_DIM).reshape(1, D_SSD)
    gnw = ssd_norm_w[0].reshape(1, D_SSD)
    scw = sc_conv_w[0]
    fcw = ffn_conv_w[0]
    fcb = ffn_conv_b[0].reshape(1, D_FF)
    head_of_lane = jnp.arange(D_SSD, dtype=jnp.int32) // HEAD_DIM
    expand = (jnp.arange(DT_PAD, dtype=jnp.int32)[:, None] == head_of_lane[None, :]).astype(BF16)

    nbs = x_sample.shape[0]
    xs_in = x_sample.reshape(nbs, D_MODEL)
    cst = jnp.swapaxes(state_ssd_conv[0], 0, 1)
    scst, fst = state_short_conv[0], state_ffn_conv[0]
    z, xs, xdt_t, da, bc, ysc, ncst, nscst, win = _sample_pre(
        xs_in, nw, w_in_t, cw, cb, dtb, a_pad, scw, expand, cst, scst)
    s_ssm, y_s = _sample_state(state_ssm[0], da, xdt_t, bc)
    y_sample, nfst, wout, wffn, wdown = _sample_post(
        xs_in, y_s, xs, z, ysc, dx, gnw, w_out[0], nw2, w_ffn_in[0], fcw, fcb, w_down[0], nfw, fst)

    x1, p_ssm, p_cst, p_scst = _mixer_prompt(x_prompt, nw, win, cw, cb, dtb, a_pad, dx, gnw, scw, wout, expand)
    y_prompt, p_fst = _ffn_prompt(x1, nw2, wffn, fcw, fcb, wdown, nfw)

    return (y_prompt, y_sample.reshape(nbs, 1, D_MODEL),
            p_ssm[None], p_cst[None], p_scst[None], p_fst[None],
            s_ssm[None], jnp.swapaxes(ncst, 0, 1)[None], nscst[None], nfst[None])
```

```python
import functools

import jax
import jax.numpy as jnp
from jax import lax
from jax.experimental import pallas as pl
from jax.experimental.pallas import tpu as pltpu

D_MODEL = 1024
D_SSD = 1024
D_SC = 1024
NHEADS = 16
HEAD_DIM = 64
NGROUPS = 2
NSTATE = 128
SSD_CONV = 4
CONV_DIM = D_SSD + 2 * NGROUPS * NSTATE
SC_CONV = 3
D_FF = 2816
FFN_CONV = 3
EPS = 1e-5
D_MIX = D_SSD + D_SC
D_IN_PROJ = D_SSD + CONV_DIM + NHEADS + 3 * D_SC

LANES = 128
MXU_COLS = 256
OFF_Z = 0
OFF_XBC = OFF_Z + D_SSD
OFF_DT = OFF_XBC + CONV_DIM
OFF_GB = OFF_DT + NHEADS
OFF_GC = OFF_GB + D_SC
OFF_H = OFF_GC + D_SC
DT_PAD = LANES

CHUNK = 128
CARRY = 8
SEQ_TILE = 512
VMEM_LIMIT = 56 * 1024 * 1024

F32 = jnp.float32
BF16 = jnp.bfloat16


def _dot(a, b):
    return jnp.dot(a, b, preferred_element_type=F32)


def _dot_nt(a, b_t):
    return lax.dot_general(a, b_t, (((1,), (1,)), ((), ())), preferred_element_type=F32)


def _split3(v):
    hi = v.astype(BF16)
    r = v - hi.astype(F32)
    mid = r.astype(BF16)
    lo = (r - mid.astype(F32)).astype(BF16)
    return hi, mid, lo


def _dot_sel_rhs(v, sel):
    hi, mid, lo = _split3(v)
    return (_dot(hi, sel) + _dot(mid, sel)) + _dot(lo, sel)


def _dot_sel_lhs(sel, v):
    hi, mid, lo = _split3(v)
    return (_dot(sel, hi) + _dot(sel, mid)) + _dot(sel, lo)


def _silu(v):
    h = 0.5 * v
    return h + h * jnp.tanh(h)


def _softplus(v):
    return jnp.maximum(v, 0.0) + jnp.log1p(jnp.exp(-jnp.abs(v)))


def _rms_rows(x, w):
    ms = jnp.mean(x * x, axis=-1, keepdims=True)
    return x * lax.rsqrt(ms + EPS) * w


def _mixer_prompt_kernel(tiles_per_seq,
                         x_ref, xres_ref, wffn_f32_ref, wdown_f32_ref,
                         nw_ref, win_ref, cw_ref, cb_ref, dtb_ref, a_ref, dx_ref,
                         gnw_ref, scw_ref, wout_ref, e_ref,
                         x1_ref, ssm_ref, cst_ref, scst_ref, wffn_bf_ref, wdown_bf_ref,
                         xbc_ext, sc_ext, u_ref, act_ref, dt_ref, z_ref, gb_ref, s_ref, y_ref, ysc_ref, ycat_ref):
    tile = x_ref.shape[1]
    g = pl.program_id(0)
    last = pl.num_programs(0) - 1
    t = lax.rem(g, tiles_per_seq)

    @pl.when(g < CAST_FFN_STEPS)
    def _():
        wffn_bf_ref[...] = wffn_f32_ref[...].astype(BF16)

    @pl.when(g < CAST_DOWN_STEPS)
    def _():
        wdown_bf_ref[...] = wdown_f32_ref[...].astype(BF16)

    @pl.when(g == 0)
    def _():
        ycat_ref[...] = jnp.zeros_like(ycat_ref)

    @pl.when(t == 0)
    def _():
        xbc_ext[0:CARRY, :] = jnp.zeros((CARRY, CONV_DIM), F32)
        sc_ext[0:CARRY, :] = jnp.zeros((CARRY, D_SC), F32)
        s_ref[...] = jnp.zeros_like(s_ref)

    def out_block(off):
        x1_ref[0, :, off:off + MXU_COLS] = xres_ref[0, :, off:off + MXU_COLS] + _dot(
            ycat_ref[...], wout_ref[:, off:off + MXU_COLS])

    @pl.when(g == last)
    def _():
        for off in range(0, D_MODEL, MXU_COLS):
            out_block(off)

    @pl.when(g < last)
    def _():
        _mixer_tile(tile, t == tiles_per_seq - 1, out_block,
                    x_ref, nw_ref, win_ref, cw_ref, cb_ref, dtb_ref, a_ref, dx_ref, gnw_ref, scw_ref, e_ref,
                    ssm_ref, cst_ref, scst_ref,
                    xbc_ext, sc_ext, u_ref, act_ref, dt_ref, z_ref, gb_ref, s_ref, y_ref, ysc_ref, ycat_ref)


def _mixer_tile(tile, ends_sequence, out_block,
                x_ref, nw_ref, win_ref, cw_ref, cb_ref, dtb_ref, a_ref, dx_ref, gnw_ref, scw_ref, e_ref,
                ssm_ref, cst_ref, scst_ref,
                xbc_ext, sc_ext, u_ref, act_ref, dt_ref, z_ref, gb_ref, s_ref, y_ref, ysc_ref, ycat_ref):
    u_ref[...] = _rms_rows(x_ref[0], nw_ref[...]).astype(BF16)

    def project(dst, row0, base, off, width=MXU_COLS):
        dst[row0:row0 + tile, off:off + width] = _dot_nt(u_ref[...], win_ref[base + off:base + off + width, :])

    def conv_block(off):
        conv = cb_ref[:, off:off + LANES] + (
            xbc_ext[CARRY - 3:CARRY - 3 + tile, off:off + LANES] * cw_ref[0:1, off:off + LANES])
        for k in range(1, SSD_CONV):
            conv = conv + (xbc_ext[CARRY - 3 + k:CARRY - 3 + k + tile, off:off + LANES]
                           * cw_ref[k:k + 1, off:off + LANES])
        act_ref[:, off:off + LANES] = _silu(conv)

    def sc_block(off):
        sc = sc_ext[CARRY - 2:CARRY - 2 + tile, off:off + LANES] * scw_ref[0:1, off:off + LANES]
        for k in range(1, SC_CONV):
            sc = sc + (sc_ext[CARRY - 2 + k:CARRY - 2 + k + tile, off:off + LANES]
                       * scw_ref[k:k + 1, off:off + LANES])
        ysc_ref[:, off:off + LANES] = (gb_ref[:, off:off + LANES] * sc).astype(BF16)

    for off in range(0, CONV_DIM, MXU_COLS):
        project(xbc_ext, CARRY, OFF_XBC, off)
        for sub in range(off, off + MXU_COLS, LANES):
            conv_block(sub)
    tail = xbc_ext[tile + CARRY - 3:tile + CARRY, :]
    cst_ref[0] = tail
    xbc_ext[CARRY - 3:CARRY, :] = tail
    project(dt_ref, 0, OFF_DT, 0, DT_PAD)
    for off in range(0, D_SC, MXU_COLS):
        u = u_ref[...]
        sc_ext[CARRY:CARRY + tile, off:off + MXU_COLS] = (
            _dot_nt(u, win_ref[OFF_GC + off:OFF_GC + off + MXU_COLS, :])
            * _dot_nt(u, win_ref[OFF_H + off:OFF_H + off + MXU_COLS, :]))

    fillers = []
    for k in range(D_SC // MXU_COLS):
        fillers.append(lambda k=k: out_block(k * MXU_COLS))
        fillers.append(lambda k=k: project(gb_ref, 0, OFF_GB, k * MXU_COLS))
        fillers.append(lambda k=k: sc_block(k * MXU_COLS))
        fillers.append(lambda k=k: sc_block(k * MXU_COLS + LANES))
        fillers.append(lambda k=k: project(z_ref, 0, OFF_Z, k * MXU_COLS))
    n_slots = (tile // CHUNK) * (NHEADS // 2)
    slot_of_filler = [(i * n_slots) // len(fillers) for i in range(len(fillers))]

    row_i = lax.broadcasted_iota(jnp.int32, (CHUNK, CHUNK), 0)
    col_i = lax.broadcasted_iota(jnp.int32, (CHUNK, CHUNK), 1)
    causal = row_i >= col_i
    tri = causal.astype(BF16)
    lane_i = lax.broadcasted_iota(jnp.int32, (1, LANES), 1)
    keeps = ((lane_i < HEAD_DIM).astype(BF16), (lane_i >= HEAD_DIM).astype(BF16))
    neg_inf = jnp.float32(-jnp.inf)
    pairs_per_group = NHEADS // NGROUPS // 2

    for c in range(tile // CHUNK):
        r0 = c * CHUNK
        dt = _softplus(dt_ref[r0:r0 + CHUNK, :] + dtb_ref[...])
        acum = _dot_sel_lhs(tri, dt * a_ref[...])
        acum_t = acum.T
        dt_t = dt.T
        ea = jnp.exp(acum)
        w_t = jnp.exp(acum_t[:, CHUNK - 1:CHUNK] - acum_t) * dt_t
        cd = jnp.broadcast_to(ea[CHUNK - 1:CHUNK, :], (8, LANES))
        cdx = _dot_sel_rhs(cd, e_ref[...])[0:1, :]
        for grp in range(NGROUPS):
            bg = act_ref[r0:r0 + CHUNK, D_SSD + grp * NSTATE:D_SSD + (grp + 1) * NSTATE]
            cg = act_ref[r0:r0 + CHUNK, D_SSD + (NGROUPS + grp) * NSTATE:D_SSD + (NGROUPS + grp + 1) * NSTATE]
            cb = lax.dot_general(cg.astype(BF16), bg.astype(BF16), (((1,), (1,)), ((), ())),
                                 preferred_element_type=F32)
            bg_t = bg.T
            for jp in range(pairs_per_group):
                j = grp * pairs_per_group + jp
                xp = act_ref[r0:r0 + CHUNK, j * LANES:(j + 1) * LANES].astype(BF16)
                sp = s_ref[:, j * LANES:(j + 1) * LANES]
                spb = sp.astype(BF16)
                lhs_parts, rhs_parts, bw_parts, x_parts = [], [], [], []
                for half, keep in enumerate(keeps):
                    h = 2 * j + half
                    xh = xp * keep
                    seg = acum[:, h:h + 1] - acum_t[h:h + 1, :]
                    decay = jnp.exp(jnp.where(causal, seg, neg_inf))
                    m = cb * decay * dt_t[h:h + 1, :]
                    eac = ea[:, h:h + 1] * cg
                    lhs_parts += [m.astype(BF16), eac.astype(BF16)]
                    rhs_parts += [xh, spb * keep]
                    bw_parts.append((bg_t * w_t[h:h + 1, :]).astype(BF16))
                    x_parts.append(xh)
                y_ref[r0:r0 + CHUNK, j * LANES:(j + 1) * LANES] = _dot(
                    jnp.concatenate(lhs_parts, axis=1), jnp.concatenate(rhs_parts, axis=0))
                snew = _dot(jnp.concatenate(bw_parts, axis=1), jnp.concatenate(x_parts, axis=0))
                s_ref[:, j * LANES:(j + 1) * LANES] = cdx[:, j * LANES:(j + 1) * LANES] * sp + snew
                slot = c * (NHEADS // 2) + j
                for i, filler in enumerate(fillers):
                    if slot_of_filler[i] == slot:
                        filler()
    sc_tail = sc_ext[tile + CARRY - 2:tile + CARRY, :]
    scst_ref[0] = sc_tail
    sc_ext[CARRY - 2:CARRY, :] = sc_tail

    gw = D_SSD // NGROUPS
    for grp in range(NGROUPS):
        ssq = jnp.zeros((tile, LANES), F32)
        for off in range(grp * gw, (grp + 1) * gw, LANES):
            yb = ((y_ref[:, off:off + LANES] + dx_ref[:, off:off + LANES] * act_ref[:, off:off + LANES])
                  * _silu(z_ref[:, off:off + LANES]))
            y_ref[:, off:off + LANES] = yb
            ssq = ssq + yb * yb
        scale = lax.rsqrt(jnp.sum(ssq, axis=-1, keepdims=True) * (1.0 / gw) + EPS)
        for off in range(grp * gw, (grp + 1) * gw, LANES):
            ycat_ref[:, off:off + LANES] = (y_ref[:, off:off + LANES] * scale
                                            * gnw_ref[:, off:off + LANES]).astype(BF16)
    ycat_ref[:, D_SSD:D_MIX] = ysc_ref[...]

    @pl.when(ends_sequence)
    def _():
        for j in range(NHEADS // 2):
            ssm_ref[0, 2 * j:2 * j + 2] = s_ref[:, j * LANES:(j + 1) * LANES].T.reshape(2, HEAD_DIM, NSTATE)


def _const_spec(shape):
    nd = len(shape)
    return pl.BlockSpec(shape, lambda *_: (0,) * nd, pipeline_mode=pl.Buffered(1))


CAST_FFN_STEPS = 32
CAST_DOWN_STEPS = 16


def _mixer_prompt(x, nw, win, cw, cb, dtb, a_pad, dx, gnw, scw, wout, expand, w_ffn_in, w_down):
    nb, seq, _ = x.shape
    tile = SEQ_TILE
    nt = seq // tile
    total = nb * nt
    assert total >= CAST_FFN_STEPS and total >= CAST_DOWN_STEPS
    ffn_rows = D_MODEL // CAST_FFN_STEPS
    down_rows = D_FF // CAST_DOWN_STEPS
    ffn_spec = pl.BlockSpec((ffn_rows, 2 * D_FF), lambda g: (jnp.minimum(g, CAST_FFN_STEPS - 1), 0))
    down_spec = pl.BlockSpec((down_rows, D_MODEL), lambda g: (jnp.minimum(g, CAST_DOWN_STEPS - 1), 0))

    def tile_of(g, lag):
        i = jnp.clip(g - lag, 0, total - 1)
        return i // nt, i % nt

    cur_spec = pl.BlockSpec((1, tile, D_MODEL), lambda g: (*tile_of(g, 0), 0))
    prev_spec = pl.BlockSpec((1, tile, D_MODEL), lambda g: (*tile_of(g, 1), 0))
    consts = (nw, win, cw, cb, dtb, a_pad, dx, gnw, scw, wout, expand)
    return pl.pallas_call(
        functools.partial(_mixer_prompt_kernel, nt),
        grid=(total + 1,),
        in_specs=[cur_spec, prev_spec, ffn_spec, down_spec] + [_const_spec(p.shape) for p in consts],
        out_specs=[
            prev_spec,
            pl.BlockSpec((1, NHEADS, HEAD_DIM, NSTATE), lambda g: (tile_of(g, 0)[0], 0, 0, 0)),
            pl.BlockSpec((1, SSD_CONV - 1, CONV_DIM), lambda g: (tile_of(g, 0)[0], 0, 0)),
            pl.BlockSpec((1, SC_CONV - 1, D_SC), lambda g: (tile_of(g, 0)[0], 0, 0)),
            ffn_spec,
            down_spec,
        ],
        out_shape=[
            jax.ShapeDtypeStruct((nb, seq, D_MODEL), F32),
            jax.ShapeDtypeStruct((nb, NHEADS, HEAD_DIM, NSTATE), F32),
            jax.ShapeDtypeStruct((nb, SSD_CONV - 1, CONV_DIM), F32),
            jax.ShapeDtypeStruct((nb, SC_CONV - 1, D_SC), F32),
            jax.ShapeDtypeStruct(w_ffn_in.shape, BF16),
            jax.ShapeDtypeStruct(w_down.shape, BF16),
        ],
        scratch_shapes=[
            pltpu.VMEM((tile + CARRY, CONV_DIM), F32),
            pltpu.VMEM((tile + CARRY, D_SC), F32),
            pltpu.VMEM((tile, D_MODEL), BF16),
            pltpu.VMEM((tile, CONV_DIM), F32),
            pltpu.VMEM((tile, DT_PAD), F32),
            pltpu.VMEM((tile, D_SSD), F32),
            pltpu.VMEM((tile, D_SC), F32),
            pltpu.VMEM((NSTATE, D_SSD), F32),
            pltpu.VMEM((tile, D_SSD), F32),
            pltpu.VMEM((tile, D_SC), BF16),
            pltpu.VMEM((tile, D_MIX), BF16),
        ],
        compiler_params=pltpu.CompilerParams(
            dimension_semantics=("arbitrary",), vmem_limit_bytes=VMEM_LIMIT),
        name="mixer_prompt",
    )(x, x, w_ffn_in, w_down, *consts)


def _ffn_prompt_kernel(x1_ref, nw_ref, wffn_ref, fcw_ref, fcb_ref, wdown_ref, nfw_ref,
                       y_ref, fst_ref, gate_ext):
    tile = x1_ref.shape[1]
    t = pl.program_id(1)

    @pl.when(t == 0)
    def _():
        gate_ext[0:CARRY, :] = jnp.zeros((CARRY, D_FF), F32)

    x1 = x1_ref[0]
    u = _rms_rows(x1, nw_ref[...]).astype(BF16)
    gate_ext[CARRY:CARRY + tile, :] = _dot(u, wffn_ref[:, 0:D_FF])
    up = _dot(u, wffn_ref[:, D_FF:2 * D_FF])
    g = fcb_ref[...] + gate_ext[CARRY - 2:CARRY - 2 + tile, :] * fcw_ref[0:1, :]
    for k in range(1, FFN_CONV):
        g = g + gate_ext[CARRY - 2 + k:CARRY - 2 + k + tile, :] * fcw_ref[k:k + 1, :]
    tail = gate_ext[tile + CARRY - 2:tile + CARRY, :]
    fst_ref[0] = tail
    gate_ext[CARRY - 2:CARRY, :] = tail
    act = (_silu(g) * up).astype(BF16)
    x2 = x1 + _dot(act, wdown_ref[...])
    y_ref[0] = _rms_rows(x2, nfw_ref[...])


def _ffn_prompt(x1, nw, wffn, fcw, fcb, wdown, nfw):
    nb, seq, _ = x1.shape
    tile = SEQ_TILE
    tok_spec = pl.BlockSpec((1, tile, D_MODEL), lambda b, t: (b, t, 0))
    return pl.pallas_call(
        _ffn_prompt_kernel,
        grid=(nb, seq // tile),
        in_specs=[tok_spec] + [_const_spec(p.shape) for p in (nw, wffn, fcw, fcb, wdown, nfw)],
        out_specs=[tok_spec, pl.BlockSpec((1, FFN_CONV - 1, D_FF), lambda b, t: (b, 0, 0))],
        out_shape=[jax.ShapeDtypeStruct((nb, seq, D_MODEL), F32),
                   jax.ShapeDtypeStruct((nb, FFN_CONV - 1, D_FF), F32)],
        scratch_shapes=[pltpu.VMEM((tile + CARRY, D_FF), F32)],
        compiler_params=pltpu.CompilerParams(
            dimension_semantics=("arbitrary", "arbitrary"), vmem_limit_bytes=VMEM_LIMIT),
        name="ffn_prompt",
    )(x1, nw, wffn, fcw, fcb, wdown, nfw)


CAST_ROWS = 512


def _sample_pre_kernel(x_ref, nw_ref, wf32_ref, cw_ref, cb_ref, dtb_ref, a_ref, scw_ref, e_ref,
                       cst_ref, scst_ref,
                       z_ref, xs_ref, xdt_t_ref, da_ref, bc_ref, ysc_ref, ncst_ref, nscst_ref, win_ref):
    for r0 in range(0, D_IN_PROJ, CAST_ROWS):
        r1 = min(r0 + CAST_ROWS, D_IN_PROJ)
        win_ref[r0:r1, :] = wf32_ref[r0:r1, :].astype(BF16)
    x = x_ref[...]
    u = _rms_rows(x, nw_ref[...]).astype(BF16)
    z_ref[...] = _dot_nt(u, win_ref[OFF_Z:OFF_Z + D_SSD, :])
    xbc = _dot_nt(u, win_ref[OFF_XBC:OFF_XBC + CONV_DIM, :])
    conv = cb_ref[...] + xbc * cw_ref[SSD_CONV - 1:SSD_CONV, :]
    for k in range(SSD_CONV - 1):
        conv = conv + cst_ref[k] * cw_ref[k:k + 1, :]
    for k in range(SSD_CONV - 2):
        ncst_ref[k] = cst_ref[k + 1]
    ncst_ref[SSD_CONV - 2] = xbc
    act = _silu(conv)
    xs = act[:, 0:D_SSD]
    xs_ref[...] = xs
    for i in range(2 * NGROUPS):
        bc_ref[i] = act[:, D_SSD + i * NSTATE:D_SSD + (i + 1) * NSTATE]
    dt = _softplus(_dot_nt(u, win_ref[OFF_DT:OFF_DT + DT_PAD, :]) + dtb_ref[...])
    xdt_t_ref[...] = (xs * _dot_sel_rhs(dt, e_ref[...])).T
    da_ref[...] = jnp.exp(dt * a_ref[...])[:, 0:NHEADS]
    gc = _dot_nt(u, win_ref[OFF_GC:OFF_GC + D_SC, :])
    hh = _dot_nt(u, win_ref[OFF_H:OFF_H + D_SC, :])
    gch = gc * hh
    sc = gch * scw_ref[SC_CONV - 1:SC_CONV, :]
    for k in range(SC_CONV - 1):
        sc = sc + scst_ref[:, k, :] * scw_ref[k:k + 1, :]
    for k in range(SC_CONV - 2):
        nscst_ref[:, k, :] = scst_ref[:, k + 1, :]
    nscst_ref[:, SC_CONV - 2, :] = gch
    gb = _dot_nt(u, win_ref[OFF_GB:OFF_GB + D_SC, :])
    ysc_ref[...] = gb * sc


def _sample_pre(x, nw, w_in_t, cw, cb, dtb, a_pad, scw, expand, cst, scst):
    nb = x.shape[0]
    outs = [
        jax.ShapeDtypeStruct((nb, D_SSD), F32),
        jax.ShapeDtypeStruct((nb, D_SSD), F32),
        jax.ShapeDtypeStruct((D_SSD, nb), F32),
        jax.ShapeDtypeStruct((nb, NHEADS), F32),
        jax.ShapeDtypeStruct((2 * NGROUPS, nb, NSTATE), F32),
        jax.ShapeDtypeStruct((nb, D_SC), F32),
        jax.ShapeDtypeStruct((SSD_CONV - 1, nb, CONV_DIM), F32),
        jax.ShapeDtypeStruct((nb, SC_CONV - 1, D_SC), F32),
        jax.ShapeDtypeStruct((D_IN_PROJ, D_MODEL), BF16),
    ]
    return pl.pallas_call(
        _sample_pre_kernel,
        out_shape=outs,
        compiler_params=pltpu.CompilerParams(vmem_limit_bytes=VMEM_LIMIT),
        name="sample_pre",
    )(x, nw, w_in_t, cw, cb, dtb, a_pad, scw, expand, cst, scst)


SAMPLE_HEADS = 2


def _sample_state_kernel(da_ref, st_ref, xdt_t_ref, b_ref, c_ref, nst_ref, y_ref):
    nb = st_ref.shape[0]
    head0 = pl.program_id(0) * SAMPLE_HEADS
    rows = SAMPLE_HEADS * HEAD_DIM
    c_t = c_ref[0].T.astype(BF16)
    lane = lax.broadcasted_iota(jnp.int32, (rows, nb), 1)
    ycols = jnp.zeros((rows, nb), F32)
    for b in range(nb):
        upd = xdt_t_ref[:, b:b + 1] * b_ref[0, b:b + 1, :]
        parts = []
        for hh in range(SAMPLE_HEADS):
            hnew = st_ref[b, hh] * da_ref[b, head0 + hh] + upd[hh * HEAD_DIM:(hh + 1) * HEAD_DIM, :]
            nst_ref[b, hh] = hnew
            parts.append(hnew)
        prod = _dot(jnp.concatenate(parts, axis=0).astype(BF16), c_t)
        ycols = jnp.where(lane == b, prod, ycols)
    y_ref[...] = ycols.T


def _sample_state(state, da, xdt_t, bc):
    nb = state.shape[0]
    rows = SAMPLE_HEADS * HEAD_DIM
    heads_per_group = NHEADS // NGROUPS
    st_spec = pl.BlockSpec((nb, SAMPLE_HEADS, HEAD_DIM, NSTATE), lambda i: (0, i, 0, 0))
    return pl.pallas_call(
        _sample_state_kernel,
        grid=(NHEADS // SAMPLE_HEADS,),
        in_specs=[pl.BlockSpec(memory_space=pltpu.SMEM),
                  st_spec,
                  pl.BlockSpec((rows, nb), lambda i: (i, 0)),
                  pl.BlockSpec((1, nb, NSTATE), lambda i: (i * SAMPLE_HEADS // heads_per_group, 0, 0)),
                  pl.BlockSpec((1, nb, NSTATE), lambda i: (NGROUPS + i * SAMPLE_HEADS // heads_per_group, 0, 0))],
        out_specs=[st_spec, pl.BlockSpec((nb, rows), lambda i: (0, i))],
        out_shape=[jax.ShapeDtypeStruct(state.shape, F32), jax.ShapeDtypeStruct((nb, NHEADS * HEAD_DIM), F32)],
        compiler_params=pltpu.CompilerParams(
            dimension_semantics=("arbitrary",), vmem_limit_bytes=VMEM_LIMIT),
        name="sample_state",
    )(da, state, xdt_t, bc, bc)


POST_OUT_BLOCKS = 1
POST_FFN_BLOCKS = 2
POST_DOWN_BLOCKS = 1
POST_OUT_ROWS = D_MIX // POST_OUT_BLOCKS
POST_FFN_COLS = 2 * D_FF // POST_FFN_BLOCKS
POST_DOWN_ROWS = D_FF // POST_DOWN_BLOCKS


def _sample_post_kernel(x_ref, y_ref, xs_ref, z_ref, ysc_ref, dx_ref, gnw_ref, wout_ref,
                        nw_ref, wffn_ref, fcw_ref, fcb_ref, wdown_ref, nfw_ref, fst_ref,
                        out_ref, nfst_ref,
                        ycat_ref, x1_ref, u_ref, gu_ref, act_ref, acc_ref):
    s = pl.program_id(0)
    ffn0 = POST_OUT_BLOCKS
    down0 = ffn0 + POST_FFN_BLOCKS

    @pl.when(s == 0)
    def _():
        y = (y_ref[...] + dx_ref[...] * xs_ref[...]) * _silu(z_ref[...])
        gw = D_SSD // NGROUPS
        for g in range(NGROUPS):
            ycat_ref[:, g * gw:(g + 1) * gw] = _rms_rows(
                y[:, g * gw:(g + 1) * gw], gnw_ref[:, g * gw:(g + 1) * gw]).astype(BF16)
        ycat_ref[:, D_SSD:D_MIX] = ysc_ref[...].astype(BF16)
        x1_ref[...] = x_ref[...]

    for k in range(POST_OUT_BLOCKS):
        @pl.when(s == k)
        def _(k=k):
            x1_ref[...] += _dot(ycat_ref[:, k * POST_OUT_ROWS:(k + 1) * POST_OUT_ROWS], wout_ref[...])

    @pl.when(s == ffn0)
    def _():
        u_ref[...] = _rms_rows(x1_ref[...], nw_ref[...]).astype(BF16)

    for k in range(POST_FFN_BLOCKS):
        @pl.when(s == ffn0 + k)
        def _(k=k):
            gu_ref[:, k * POST_FFN_COLS:(k + 1) * POST_FFN_COLS] = _dot(u_ref[...], wffn_ref[...])

    @pl.when(s == down0)
    def _():
        gate = gu_ref[:, 0:D_FF]
        g = fcb_ref[...] + gate * fcw_ref[FFN_CONV - 1:FFN_CONV, :]
        for k in range(FFN_CONV - 1):
            g = g + fst_ref[:, k, :] * fcw_ref[k:k + 1, :]
        for k in range(FFN_CONV - 2):
            nfst_ref[:, k, :] = fst_ref[:, k + 1, :]
        nfst_ref[:, FFN_CONV - 2, :] = gate
        act_ref[...] = (_silu(g) * gu_ref[:, D_FF:2 * D_FF]).astype(BF16)
        acc_ref[...] = x1_ref[...]

    for k in range(POST_DOWN_BLOCKS):
        @pl.when(s == down0 + k)
        def _(k=k):
            acc_ref[...] += _dot(act_ref[:, k * POST_DOWN_ROWS:(k + 1) * POST_DOWN_ROWS], wdown_ref[...])

    @pl.when(s == down0 + POST_DOWN_BLOCKS - 1)
    def _():
        out_ref[...] = _rms_rows(acc_ref[...], nfw_ref[...])


def _sample_post(x, y, xs, z, ysc, dx, gnw, wout, nw, wffn, fcw, fcb, wdown, nfw, fst):
    nb = x.shape[0]
    ffn0 = POST_OUT_BLOCKS
    down0 = ffn0 + POST_FFN_BLOCKS
    steps = down0 + POST_DOWN_BLOCKS

    def whole(a):
        nd = a.ndim
        return pl.BlockSpec(a.shape, lambda s: (0,) * nd)

    out_w = pl.BlockSpec((POST_OUT_ROWS, D_MODEL), lambda s: (jnp.clip(s, 0, POST_OUT_BLOCKS - 1), 0))
    ffn_w = pl.BlockSpec((D_MODEL, POST_FFN_COLS), lambda s: (0, jnp.clip(s - ffn0, 0, POST_FFN_BLOCKS - 1)))
    down_w = pl.BlockSpec((POST_DOWN_ROWS, D_MODEL), lambda s: (jnp.clip(s - down0, 0, POST_DOWN_BLOCKS - 1), 0))
    nfst_shape = jax.ShapeDtypeStruct((nb, FFN_CONV - 1, D_FF), F32)
    y_shape = jax.ShapeDtypeStruct((nb, D_MODEL), F32)
    return pl.pallas_call(
        _sample_post_kernel,
        grid=(steps,),
        in_specs=[_const_spec(a.shape) for a in (x, y, xs, z, ysc, dx, gnw)] + [out_w, _const_spec(nw.shape), ffn_w,
                  _const_spec(fcw.shape), _const_spec(fcb.shape), down_w, _const_spec(nfw.shape),
                  _const_spec(fst.shape)],
        out_specs=[whole(y_shape), whole(nfst_shape)],
        out_shape=[y_shape, nfst_shape],
        scratch_shapes=[
            pltpu.VMEM((nb, D_MIX), BF16),
            pltpu.VMEM((nb, D_MODEL), F32),
            pltpu.VMEM((nb, D_MODEL), BF16),
            pltpu.VMEM((nb, 2 * D_FF), F32),
            pltpu.VMEM((nb, D_FF), BF16),
            pltpu.VMEM((nb, D_MODEL), F32),
        ],
        compiler_params=pltpu.CompilerParams(
            dimension_semantics=("arbitrary",), vmem_limit_bytes=VMEM_LIMIT),
        name="sample_post",
    )(x, y, xs, z, ysc, dx, gnw, wout, nw, wffn, fcw, fcb, wdown, nfw, fst)


def kernel(x_prompt, x_sample, state_ssm, state_ssd_conv, state_short_conv, state_ffn_conv,
           norm_mix_w, w_in, ssd_conv_w, ssd_conv_b, ssd_dt_bias, ssd_a_log, ssd_d, ssd_norm_w,
           sc_conv_w, w_out, norm_ffn_w, w_ffn_in, ffn_conv_w, ffn_conv_b, w_down, norm_final_w):
    depth = w_in.shape[0]
    assert depth == 1
    w_in_t = jnp.swapaxes(w_in[0], 0, 1)
    nw = norm_mix_w[0].reshape(1, D_MODEL)
    nw2 = norm_ffn_w[0].reshape(1, D_MODEL)
    nfw = norm_final_w.reshape(1, D_MODEL)
    cw = ssd_conv_w[0]
    cb = ssd_conv_b[0].reshape(1, CONV_DIM)
    dtb = jnp.pad(ssd_dt_bias[0], (0, DT_PAD - NHEADS)).reshape(1, DT_PAD)
    a_pad = jnp.pad(-jnp.exp(ssd_a_log[0]), (0, DT_PAD - NHEADS)).reshape(1, DT_PAD)
    dx = jnp.repeat(ssd_d[0], HEAD_DIM).reshape(1, D_SSD)
    gnw = ssd_norm_w[0].reshape(1, D_SSD)
    scw = sc_conv_w[0]
    fcw = ffn_conv_w[0]
    fcb = ffn_conv_b[0].reshape(1, D_FF)
    head_of_lane = jnp.arange(D_SSD, dtype=jnp.int32) // HEAD_DIM
    expand = (jnp.arange(DT_PAD, dtype=jnp.int32)[:, None] == head_of_lane[None, :]).astype(BF16)

    nbs = x_sample.shape[0]
    xs_in = x_sample.reshape(nbs, D_MODEL)
    cst = jnp.swapaxes(state_ssd_conv[0], 0, 1)
    scst, fst = state_short_conv[0], state_ffn_conv[0]
    z, xs, xdt_t, da, bc, ysc, ncst, nscst, win = _sample_pre(
        xs_in, nw, w_in_t, cw, cb, dtb, a_pad, scw, expand, cst, scst)
    s_ssm, y_s = _sample_state(state_ssm[0], da, xdt_t, bc)

    wout = w_out[0].astype(BF16)
    x1, p_ssm, p_cst, p_scst, wffn, wdown = _mixer_prompt(
        x_prompt, nw, win, cw, cb, dtb, a_pad, dx, gnw, scw, wout, expand, w_ffn_in[0], w_down[0])
    y_sample, nfst = _sample_post(xs_in, y_s, xs, z, ysc, dx, gnw, wout, nw2, wffn, fcw, fcb, wdown, nfw, fst)
    y_prompt, p_fst = _ffn_prompt(x1, nw2, wffn, fcw, fcb, wdown, nfw)

    return (y_prompt, y_sample.reshape(nbs, 1, D_MODEL),
            p_ssm[None], p_cst[None], p_scst[None], p_fst[None],
            s_ssm[None], jnp.swapaxes(ncst, 0, 1)[None], nscst[None], nfst[None])
```

```python
import functools

import jax
import jax.numpy as jnp
from jax import lax
from jax.experimental import pallas as pl
from jax.experimental.pallas import tpu as pltpu

D_MODEL = 1024
D_SSD = 1024
D_SC = 1024
NHEADS = 16
HEAD_DIM = 64
NGROUPS = 2
NSTATE = 128
SSD_CONV = 4
CONV_DIM = D_SSD + 2 * NGROUPS * NSTATE
SC_CONV = 3
D_FF = 2816
FFN_CONV = 3
EPS = 1e-5
D_MIX = D_SSD + D_SC
D_IN_PROJ = D_SSD + CONV_DIM + NHEADS + 3 * D_SC

LANES = 128
MXU_COLS = 256
OFF_Z = 0
OFF_XBC = OFF_Z + D_SSD
OFF_DT = OFF_XBC + CONV_DIM
OFF_GB = OFF_DT + NHEADS
OFF_GC = OFF_GB + D_SC
OFF_H = OFF_GC + D_SC
DT_PAD = LANES

CHUNK = 128
CARRY = 8
SEQ_TILE = 512
VMEM_LIMIT = 56 * 1024 * 1024

F32 = jnp.float32
BF16 = jnp.bfloat16


def _dot(a, b):
    return jnp.dot(a, b, preferred_element_type=F32)


def _dot_nt(a, b_t):
    return lax.dot_general(a, b_t, (((1,), (1,)), ((), ())), preferred_element_type=F32)


def _split3(v):
    hi = v.astype(BF16)
    r = v - hi.astype(F32)
    mid = r.astype(BF16)
    lo = (r - mid.astype(F32)).astype(BF16)
    return hi, mid, lo


def _dot_sel_rhs(v, sel):
    hi, mid, lo = _split3(v)
    return (_dot(hi, sel) + _dot(mid, sel)) + _dot(lo, sel)


def _dot_sel_lhs(sel, v):
    hi, mid, lo = _split3(v)
    return (_dot(sel, hi) + _dot(sel, mid)) + _dot(sel, lo)


def _silu(v):
    h = 0.5 * v
    return h + h * jnp.tanh(h)


def _softplus(v):
    return jnp.maximum(v, 0.0) + jnp.log1p(jnp.exp(-jnp.abs(v)))


def _rms_rows(x, w):
    ms = jnp.mean(x * x, axis=-1, keepdims=True)
    return x * lax.rsqrt(ms + EPS) * w


def _mixer_prompt_kernel(tiles_per_seq,
                         x_ref, xres_ref, wffn_f32_ref, wdown_f32_ref,
                         nw_ref, win_ref, cw_ref, cb_ref, dtb_ref, a_ref, dx_ref,
                         gnw_ref, scw_ref, wout_ref, e_ref,
                         x1_ref, ssm_ref, cst_ref, scst_ref, wffn_bf_ref, wdown_bf_ref,
                         xbc_ext, sc_ext, u_ref, act_ref, dt_ref, z_ref, gb_ref, s_ref, y_ref, ysc_ref, ycat_ref):
    tile = x_ref.shape[1]
    g = pl.program_id(0)
    last = pl.num_programs(0) - 1
    t = lax.rem(g, tiles_per_seq)

    @pl.when(g < CAST_FFN_STEPS)
    def _():
        wffn_bf_ref[...] = wffn_f32_ref[...].astype(BF16)

    @pl.when(g < CAST_DOWN_STEPS)
    def _():
        wdown_bf_ref[...] = wdown_f32_ref[...].astype(BF16)

    @pl.when(g == 0)
    def _():
        ycat_ref[...] = jnp.zeros_like(ycat_ref)

    @pl.when(t == 0)
    def _():
        xbc_ext[0:CARRY, :] = jnp.zeros((CARRY, CONV_DIM), F32)
        sc_ext[0:CARRY, :] = jnp.zeros((CARRY, D_SC), F32)
        s_ref[...] = jnp.zeros_like(s_ref)

    def out_block(off):
        x1_ref[0, :, off:off + MXU_COLS] = xres_ref[0, :, off:off + MXU_COLS] + _dot(
            ycat_ref[...], wout_ref[:, off:off + MXU_COLS])

    @pl.when(g == last)
    def _():
        for off in range(0, D_MODEL, MXU_COLS):
            out_block(off)

    @pl.when(g < last)
    def _():
        _mixer_tile(tile, t == tiles_per_seq - 1, out_block,
                    x_ref, nw_ref, win_ref, cw_ref, cb_ref, dtb_ref, a_ref, dx_ref, gnw_ref, scw_ref, e_ref,
                    ssm_ref, cst_ref, scst_ref,
                    xbc_ext, sc_ext, u_ref, act_ref, dt_ref, z_ref, gb_ref, s_ref, y_ref, ysc_ref, ycat_ref)


def _mixer_tile(tile, ends_sequence, out_block,
                x_ref, nw_ref, win_ref, cw_ref, cb_ref, dtb_ref, a_ref, dx_ref, gnw_ref, scw_ref, e_ref,
                ssm_ref, cst_ref, scst_ref,
                xbc_ext, sc_ext, u_ref, act_ref, dt_ref, z_ref, gb_ref, s_ref, y_ref, ysc_ref, ycat_ref):
    u_ref[...] = _rms_rows(x_ref[0], nw_ref[...]).astype(BF16)

    def project(dst, row0, base, off, width=MXU_COLS):
        dst[row0:row0 + tile, off:off + width] = _dot_nt(u_ref[...], win_ref[base + off:base + off + width, :])

    def conv_block(off):
        conv = cb_ref[:, off:off + LANES] + (
            xbc_ext[CARRY - 3:CARRY - 3 + tile, off:off + LANES] * cw_ref[0:1, off:off + LANES])
        for k in range(1, SSD_CONV):
            conv = conv + (xbc_ext[CARRY - 3 + k:CARRY - 3 + k + tile, off:off + LANES]
                           * cw_ref[k:k + 1, off:off + LANES])
        act_ref[:, off:off + LANES] = _silu(conv)

    def sc_block(off):
        sc = sc_ext[CARRY - 2:CARRY - 2 + tile, off:off + LANES] * scw_ref[0:1, off:off + LANES]
        for k in range(1, SC_CONV):
            sc = sc + (sc_ext[CARRY - 2 + k:CARRY - 2 + k + tile, off:off + LANES]
                       * scw_ref[k:k + 1, off:off + LANES])
        ysc_ref[:, off:off + LANES] = (gb_ref[:, off:off + LANES] * sc).astype(BF16)

    for off in range(0, CONV_DIM, MXU_COLS):
        project(xbc_ext, CARRY, OFF_XBC, off)
        for sub in range(off, off + MXU_COLS, LANES):
            conv_block(sub)
    tail = xbc_ext[tile + CARRY - 3:tile + CARRY, :]
    cst_ref[0] = tail
    xbc_ext[CARRY - 3:CARRY, :] = tail
    project(dt_ref, 0, OFF_DT, 0, DT_PAD)
    for off in range(0, D_SC, MXU_COLS):
        u = u_ref[...]
        sc_ext[CARRY:CARRY + tile, off:off + MXU_COLS] = (
            _dot_nt(u, win_ref[OFF_GC + off:OFF_GC + off + MXU_COLS, :])
            * _dot_nt(u, win_ref[OFF_H + off:OFF_H + off + MXU_COLS, :]))

    fillers = []
    for k in range(D_SC // MXU_COLS):
        fillers.append(lambda k=k: out_block(k * MXU_COLS))
        fillers.append(lambda k=k: project(gb_ref, 0, OFF_GB, k * MXU_COLS))
        fillers.append(lambda k=k: sc_block(k * MXU_COLS))
        fillers.append(lambda k=k: sc_block(k * MXU_COLS + LANES))
        fillers.append(lambda k=k: project(z_ref, 0, OFF_Z, k * MXU_COLS))
    n_slots = (tile // CHUNK) * (NHEADS // 2)
    slot_of_filler = [(i * n_slots) // len(fillers) for i in range(len(fillers))]

    row_i = lax.broadcasted_iota(jnp.int32, (CHUNK, CHUNK), 0)
    col_i = lax.broadcasted_iota(jnp.int32, (CHUNK, CHUNK), 1)
    causal = row_i >= col_i
    tri = causal.astype(BF16)
    lane_i = lax.broadcasted_iota(jnp.int32, (1, LANES), 1)
    keeps = ((lane_i < HEAD_DIM).astype(BF16), (lane_i >= HEAD_DIM).astype(BF16))
    neg_inf = jnp.float32(-jnp.inf)
    pairs_per_group = NHEADS // NGROUPS // 2

    for c in range(tile // CHUNK):
        r0 = c * CHUNK
        dt = _softplus(dt_ref[r0:r0 + CHUNK, :] + dtb_ref[...])
        acum = _dot_sel_lhs(tri, dt * a_ref[...])
        acum_t = acum.T
        dt_t = dt.T
        ea = jnp.exp(acum)
        w_t = jnp.exp(acum_t[:, CHUNK - 1:CHUNK] - acum_t) * dt_t
        cd = jnp.broadcast_to(ea[CHUNK - 1:CHUNK, :], (8, LANES))
        cdx = _dot_sel_rhs(cd, e_ref[...])[0:1, :]
        for grp in range(NGROUPS):
            bg = act_ref[r0:r0 + CHUNK, D_SSD + grp * NSTATE:D_SSD + (grp + 1) * NSTATE]
            cg = act_ref[r0:r0 + CHUNK, D_SSD + (NGROUPS + grp) * NSTATE:D_SSD + (NGROUPS + grp + 1) * NSTATE]
            cb = lax.dot_general(cg.astype(BF16), bg.astype(BF16), (((1,), (1,)), ((), ())),
                                 preferred_element_type=F32)
            bg_t = bg.T
            for jp in range(pairs_per_group):
                j = grp * pairs_per_group + jp
                xp = act_ref[r0:r0 + CHUNK, j * LANES:(j + 1) * LANES].astype(BF16)
                sp = s_ref[:, j * LANES:(j + 1) * LANES]
                spb = sp.astype(BF16)
                lhs_parts, rhs_parts, bw_parts, x_parts = [], [], [], []
                for half, keep in enumerate(keeps):
                    h = 2 * j + half
                    xh = xp * keep
                    seg = acum[:, h:h + 1] - acum_t[h:h + 1, :]
                    decay = jnp.exp(jnp.where(causal, seg, neg_inf))
                    m = cb * decay * dt_t[h:h + 1, :]
                    eac = ea[:, h:h + 1] * cg
                    lhs_parts += [m.astype(BF16), eac.astype(BF16)]
                    rhs_parts += [xh, spb * keep]
                    bw_parts.append((bg_t * w_t[h:h + 1, :]).astype(BF16))
                    x_parts.append(xh)
                y_ref[r0:r0 + CHUNK, j * LANES:(j + 1) * LANES] = _dot(
                    jnp.concatenate(lhs_parts, axis=1), jnp.concatenate(rhs_parts, axis=0))
                snew = _dot(jnp.concatenate(bw_parts, axis=1), jnp.concatenate(x_parts, axis=0))
                s_ref[:, j * LANES:(j + 1) * LANES] = cdx[:, j * LANES:(j + 1) * LANES] * sp + snew
                slot = c * (NHEADS // 2) + j
                for i, filler in enumerate(fillers):
                    if slot_of_filler[i] == slot:
                        filler()
    sc_tail = sc_ext[tile + CARRY - 2:tile + CARRY, :]
    scst_ref[0] = sc_tail
    sc_ext[CARRY - 2:CARRY, :] = sc_tail

    gw = D_SSD // NGROUPS
    for grp in range(NGROUPS):
        ssq = jnp.zeros((tile, LANES), F32)
        for off in range(grp * gw, (grp + 1) * gw, LANES):
            yb = ((y_ref[:, off:off + LANES] + dx_ref[:, off:off + LANES] * act_ref[:, off:off + LANES])
                  * _silu(z_ref[:, off:off + LANES]))
            y_ref[:, off:off + LANES] = yb
            ssq = ssq + yb * yb
        scale = lax.rsqrt(jnp.sum(ssq, axis=-1, keepdims=True) * (1.0 / gw) + EPS)
        for off in range(grp * gw, (grp + 1) * gw, LANES):
            ycat_ref[:, off:off + LANES] = (y_ref[:, off:off + LANES] * scale
                                            * gnw_ref[:, off:off + LANES]).astype(BF16)
    ycat_ref[:, D_SSD:D_MIX] = ysc_ref[...]

    @pl.when(ends_sequence)
    def _():
        for j in range(NHEADS // 2):
            ssm_ref[0, 2 * j:2 * j + 2] = s_ref[:, j * LANES:(j + 1) * LANES].T.reshape(2, HEAD_DIM, NSTATE)


def _const_spec(shape):
    nd = len(shape)
    return pl.BlockSpec(shape, lambda *_: (0,) * nd, pipeline_mode=pl.Buffered(1))


CAST_FFN_STEPS = 32
CAST_DOWN_STEPS = 16


def _mixer_prompt(x, nw, win, cw, cb, dtb, a_pad, dx, gnw, scw, wout, expand, w_ffn_in, w_down):
    nb, seq, _ = x.shape
    tile = SEQ_TILE
    nt = seq // tile
    total = nb * nt
    assert total >= CAST_FFN_STEPS and total >= CAST_DOWN_STEPS
    ffn_rows = D_MODEL // CAST_FFN_STEPS
    down_rows = D_FF // CAST_DOWN_STEPS
    ffn_spec = pl.BlockSpec((ffn_rows, 2 * D_FF), lambda g: (jnp.minimum(g, CAST_FFN_STEPS - 1), 0))
    down_spec = pl.BlockSpec((down_rows, D_MODEL), lambda g: (jnp.minimum(g, CAST_DOWN_STEPS - 1), 0))

    def tile_of(g, lag):
        i = jnp.clip(g - lag, 0, total - 1)
        return i // nt, i % nt

    cur_spec = pl.BlockSpec((1, tile, D_MODEL), lambda g: (*tile_of(g, 0), 0))
    prev_spec = pl.BlockSpec((1, tile, D_MODEL), lambda g: (*tile_of(g, 1), 0))
    consts = (nw, win, cw, cb, dtb, a_pad, dx, gnw, scw, wout, expand)
    return pl.pallas_call(
        functools.partial(_mixer_prompt_kernel, nt),
        grid=(total + 1,),
        in_specs=[cur_spec, prev_spec, ffn_spec, down_spec] + [_const_spec(p.shape) for p in consts],
        out_specs=[
            prev_spec,
            pl.BlockSpec((1, NHEADS, HEAD_DIM, NSTATE), lambda g: (tile_of(g, 0)[0], 0, 0, 0)),
            pl.BlockSpec((1, SSD_CONV - 1, CONV_DIM), lambda g: (tile_of(g, 0)[0], 0, 0)),
            pl.BlockSpec((1, SC_CONV - 1, D_SC), lambda g: (tile_of(g, 0)[0], 0, 0)),
            ffn_spec,
            down_spec,
        ],
        out_shape=[
            jax.ShapeDtypeStruct((nb, seq, D_MODEL), F32),
            jax.ShapeDtypeStruct((nb, NHEADS, HEAD_DIM, NSTATE), F32),
            jax.ShapeDtypeStruct((nb, SSD_CONV - 1, CONV_DIM), F32),
            jax.ShapeDtypeStruct((nb, SC_CONV - 1, D_SC), F32),
            jax.ShapeDtypeStruct(w_ffn_in.shape, BF16),
            jax.ShapeDtypeStruct(w_down.shape, BF16),
        ],
        scratch_shapes=[
            pltpu.VMEM((tile + CARRY, CONV_DIM), F32),
            pltpu.VMEM((tile + CARRY, D_SC), F32),
            pltpu.VMEM((tile, D_MODEL), BF16),
            pltpu.VMEM((tile, CONV_DIM), F32),
            pltpu.VMEM((tile, DT_PAD), F32),
            pltpu.VMEM((tile, D_SSD), F32),
            pltpu.VMEM((tile, D_SC), F32),
            pltpu.VMEM((NSTATE, D_SSD), F32),
            pltpu.VMEM((tile, D_SSD), F32),
            pltpu.VMEM((tile, D_SC), BF16),
            pltpu.VMEM((tile, D_MIX), BF16),
        ],
        compiler_params=pltpu.CompilerParams(
            dimension_semantics=("arbitrary",), vmem_limit_bytes=VMEM_LIMIT),
        name="mixer_prompt",
    )(x, x, w_ffn_in, w_down, *consts)


def _ffn_prompt_kernel(x1_ref, nw_ref, wffn_ref, fcw_ref, fcb_ref, wdown_ref, nfw_ref,
                       y_ref, fst_ref, gate_ext):
    tile = x1_ref.shape[1]
    t = pl.program_id(1)

    @pl.when(t == 0)
    def _():
        gate_ext[0:CARRY, :] = jnp.zeros((CARRY, D_FF), F32)

    x1 = x1_ref[0]
    u = _rms_rows(x1, nw_ref[...]).astype(BF16)
    gate_ext[CARRY:CARRY + tile, :] = _dot(u, wffn_ref[:, 0:D_FF])
    up = _dot(u, wffn_ref[:, D_FF:2 * D_FF])
    g = fcb_ref[...] + gate_ext[CARRY - 2:CARRY - 2 + tile, :] * fcw_ref[0:1, :]
    for k in range(1, FFN_CONV):
        g = g + gate_ext[CARRY - 2 + k:CARRY - 2 + k + tile, :] * fcw_ref[k:k + 1, :]
    tail = gate_ext[tile + CARRY - 2:tile + CARRY, :]
    fst_ref[0] = tail
    gate_ext[CARRY - 2:CARRY, :] = tail
    act = (_silu(g) * up).astype(BF16)
    x2 = x1 + _dot(act, wdown_ref[...])
    y_ref[0] = _rms_rows(x2, nfw_ref[...])


def _ffn_prompt(x1, nw, wffn, fcw, fcb, wdown, nfw):
    nb, seq, _ = x1.shape
    tile = SEQ_TILE
    tok_spec = pl.BlockSpec((1, tile, D_MODEL), lambda b, t: (b, t, 0))
    return pl.pallas_call(
        _ffn_prompt_kernel,
        grid=(nb, seq // tile),
        in_specs=[tok_spec] + [_const_spec(p.shape) for p in (nw, wffn, fcw, fcb, wdown, nfw)],
        out_specs=[tok_spec, pl.BlockSpec((1, FFN_CONV - 1, D_FF), lambda b, t: (b, 0, 0))],
        out_shape=[jax.ShapeDtypeStruct((nb, seq, D_MODEL), F32),
                   jax.ShapeDtypeStruct((nb, FFN_CONV - 1, D_FF), F32)],
        scratch_shapes=[pltpu.VMEM((tile + CARRY, D_FF), F32)],
        compiler_params=pltpu.CompilerParams(
            dimension_semantics=("arbitrary", "arbitrary"), vmem_limit_bytes=VMEM_LIMIT),
        name="ffn_prompt",
    )(x1, nw, wffn, fcw, fcb, wdown, nfw)


CAST_ROWS = 512


def _sample_pre_kernel(x_ref, nw_ref, wf32_ref, cw_ref, cb_ref, dtb_ref, a_ref, scw_ref, e_ref,
                       cst_ref, scst_ref,
                       z_ref, xs_ref, xdt_t_ref, da_ref, bc_ref, ysc_ref, ncst_ref, nscst_ref, win_ref):
    for r0 in range(0, D_IN_PROJ, CAST_ROWS):
        r1 = min(r0 + CAST_ROWS, D_IN_PROJ)
        win_ref[r0:r1, :] = wf32_ref[r0:r1, :].astype(BF16)
    x = x_ref[...]
    u = _rms_rows(x, nw_ref[...]).astype(BF16)
    z_ref[...] = _dot_nt(u, win_ref[OFF_Z:OFF_Z + D_SSD, :])
    xbc = _dot_nt(u, win_ref[OFF_XBC:OFF_XBC + CONV_DIM, :])
    conv = cb_ref[...] + xbc * cw_ref[SSD_CONV - 1:SSD_CONV, :]
    for k in range(SSD_CONV - 1):
        conv = conv + cst_ref[k] * cw_ref[k:k + 1, :]
    for k in range(SSD_CONV - 2):
        ncst_ref[k] = cst_ref[k + 1]
    ncst_ref[SSD_CONV - 2] = xbc
    act = _silu(conv)
    xs = act[:, 0:D_SSD]
    xs_ref[...] = xs
    for i in range(2 * NGROUPS):
        bc_ref[i] = act[:, D_SSD + i * NSTATE:D_SSD + (i + 1) * NSTATE]
    dt = _softplus(_dot_nt(u, win_ref[OFF_DT:OFF_DT + DT_PAD, :]) + dtb_ref[...])
    xdt_t_ref[...] = (xs * _dot_sel_rhs(dt, e_ref[...])).T
    da_ref[...] = jnp.exp(dt * a_ref[...])[:, 0:NHEADS]
    gc = _dot_nt(u, win_ref[OFF_GC:OFF_GC + D_SC, :])
    hh = _dot_nt(u, win_ref[OFF_H:OFF_H + D_SC, :])
    gch = gc * hh
    sc = gch * scw_ref[SC_CONV - 1:SC_CONV, :]
    for k in range(SC_CONV - 1):
        sc = sc + scst_ref[:, k, :] * scw_ref[k:k + 1, :]
    for k in range(SC_CONV - 2):
        nscst_ref[:, k, :] = scst_ref[:, k + 1, :]
    nscst_ref[:, SC_CONV - 2, :] = gch
    gb = _dot_nt(u, win_ref[OFF_GB:OFF_GB + D_SC, :])
    ysc_ref[...] = gb * sc


def _sample_pre(x, nw, w_in_t, cw, cb, dtb, a_pad, scw, expand, cst, scst):
    nb = x.shape[0]
    outs = [
        jax.ShapeDtypeStruct((nb, D_SSD), F32),
        jax.ShapeDtypeStruct((nb, D_SSD), F32),
        jax.ShapeDtypeStruct((D_SSD, nb), F32),
        jax.ShapeDtypeStruct((nb, NHEADS), F32),
        jax.ShapeDtypeStruct((2 * NGROUPS, nb, NSTATE), F32),
        jax.ShapeDtypeStruct((nb, D_SC), F32),
        jax.ShapeDtypeStruct((SSD_CONV - 1, nb, CONV_DIM), F32),
        jax.ShapeDtypeStruct((nb, SC_CONV - 1, D_SC), F32),
        jax.ShapeDtypeStruct((D_IN_PROJ, D_MODEL), BF16),
    ]
    return pl.pallas_call(
        _sample_pre_kernel,
        out_shape=outs,
        compiler_params=pltpu.CompilerParams(vmem_limit_bytes=VMEM_LIMIT),
        name="sample_pre",
    )(x, nw, w_in_t, cw, cb, dtb, a_pad, scw, expand, cst, scst)


SAMPLE_HEADS = 2


def _sample_state_kernel(da_ref, st_ref, xdt_t_ref, b_ref, c_ref, nst_ref, y_ref):
    nb = st_ref.shape[0]
    head0 = pl.program_id(0) * SAMPLE_HEADS
    rows = SAMPLE_HEADS * HEAD_DIM
    c_t = c_ref[0].T.astype(BF16)
    lane = lax.broadcasted_iota(jnp.int32, (rows, nb), 1)
    ycols = jnp.zeros((rows, nb), F32)
    for b in range(nb):
        upd = xdt_t_ref[:, b:b + 1] * b_ref[0, b:b + 1, :]
        parts = []
        for hh in range(SAMPLE_HEADS):
            hnew = st_ref[b, hh] * da_ref[b, head0 + hh] + upd[hh * HEAD_DIM:(hh + 1) * HEAD_DIM, :]
            nst_ref[b, hh] = hnew
            parts.append(hnew)
        prod = _dot(jnp.concatenate(parts, axis=0).astype(BF16), c_t)
        ycols = jnp.where(lane == b, prod, ycols)
    y_ref[...] = ycols.T


def _sample_state(state, da, xdt_t, bc):
    nb = state.shape[0]
    rows = SAMPLE_HEADS * HEAD_DIM
    heads_per_group = NHEADS // NGROUPS
    st_spec = pl.BlockSpec((nb, SAMPLE_HEADS, HEAD_DIM, NSTATE), lambda i: (0, i, 0, 0))
    return pl.pallas_call(
        _sample_state_kernel,
        grid=(NHEADS // SAMPLE_HEADS,),
        in_specs=[pl.BlockSpec(memory_space=pltpu.SMEM),
                  st_spec,
                  pl.BlockSpec((rows, nb), lambda i: (i, 0)),
                  pl.BlockSpec((1, nb, NSTATE), lambda i: (i * SAMPLE_HEADS // heads_per_group, 0, 0)),
                  pl.BlockSpec((1, nb, NSTATE), lambda i: (NGROUPS + i * SAMPLE_HEADS // heads_per_group, 0, 0))],
        out_specs=[st_spec, pl.BlockSpec((nb, rows), lambda i: (0, i))],
        out_shape=[jax.ShapeDtypeStruct(state.shape, F32), jax.ShapeDtypeStruct((nb, NHEADS * HEAD_DIM), F32)],
        compiler_params=pltpu.CompilerParams(
            dimension_semantics=("arbitrary",), vmem_limit_bytes=VMEM_LIMIT),
        name="sample_state",
    )(da, state, xdt_t, bc, bc)


def _sample_post_kernel(x_ref, y_ref, xs_ref, z_ref, ysc_ref, dx_ref, gnw_ref, wout_ref,
                        nw_ref, wffn_ref, fcw_ref, fcb_ref, wdown_ref, nfw_ref, fst_ref,
                        out_ref, nfst_ref):
    y = (y_ref[...] + dx_ref[...] * xs_ref[...]) * _silu(z_ref[...])
    gw = D_SSD // NGROUPS
    parts = [_rms_rows(y[:, g * gw:(g + 1) * gw], gnw_ref[:, g * gw:(g + 1) * gw]) for g in range(NGROUPS)]
    ycat = jnp.concatenate(parts + [ysc_ref[...]], axis=1).astype(BF16)
    x1 = x_ref[...] + _dot(ycat, wout_ref[...])
    u = _rms_rows(x1, nw_ref[...]).astype(BF16)
    gate = _dot(u, wffn_ref[:, 0:D_FF])
    up = _dot(u, wffn_ref[:, D_FF:2 * D_FF])
    g = fcb_ref[...] + gate * fcw_ref[FFN_CONV - 1:FFN_CONV, :]
    for k in range(FFN_CONV - 1):
        g = g + fst_ref[:, k, :] * fcw_ref[k:k + 1, :]
    for k in range(FFN_CONV - 2):
        nfst_ref[:, k, :] = fst_ref[:, k + 1, :]
    nfst_ref[:, FFN_CONV - 2, :] = gate
    act = (_silu(g) * up).astype(BF16)
    x2 = x1 + _dot(act, wdown_ref[...])
    out_ref[...] = _rms_rows(x2, nfw_ref[...])


def _sample_post(x, y, xs, z, ysc, dx, gnw, wout, nw, wffn, fcw, fcb, wdown, nfw, fst):
    nb = x.shape[0]
    return pl.pallas_call(
        _sample_post_kernel,
        out_shape=[jax.ShapeDtypeStruct((nb, D_MODEL), F32),
                   jax.ShapeDtypeStruct((nb, FFN_CONV - 1, D_FF), F32)],
        compiler_params=pltpu.CompilerParams(vmem_limit_bytes=VMEM_LIMIT),
        name="sample_post",
    )(x, y, xs, z, ysc, dx, gnw, wout, nw, wffn, fcw, fcb, wdown, nfw, fst)


def kernel(x_prompt, x_sample, state_ssm, state_ssd_conv, state_short_conv, state_ffn_conv,
           norm_mix_w, w_in, ssd_conv_w, ssd_conv_b, ssd_dt_bias, ssd_a_log, ssd_d, ssd_norm_w,
           sc_conv_w, w_out, norm_ffn_w, w_ffn_in, ffn_conv_w, ffn_conv_b, w_down, norm_final_w):
    depth = w_in.shape[0]
    assert depth == 1
    w_in_t = jnp.swapaxes(w_in[0], 0, 1)
    nw = norm_mix_w[0].reshape(1, D_MODEL)
    nw2 = norm_ffn_w[0].reshape(1, D_MODEL)
    nfw = norm_final_w.reshape(1, D_MODEL)
    cw = ssd_conv_w[0]
    cb = ssd_conv_b[0].reshape(1, CONV_DIM)
    dtb = jnp.pad(ssd_dt_bias[0], (0, DT_PAD - NHEADS)).reshape(1, DT_PAD)
    a_pad = jnp.pad(-jnp.exp(ssd_a_log[0]), (0, DT_PAD - NHEADS)).reshape(1, DT_PAD)
    dx = jnp.repeat(ssd_d[0], HEAD_DIM).reshape(1, D_SSD)
    gnw = ssd_norm_w[0].reshape(1, D_SSD)
    scw = sc_conv_w[0]
    fcw = ffn_conv_w[0]
    fcb = ffn_conv_b[0].reshape(1, D_FF)
    head_of_lane = jnp.arange(D_SSD, dtype=jnp.int32) // HEAD_DIM
    expand = (jnp.arange(DT_PAD, dtype=jnp.int32)[:, None] == head_of_lane[None, :]).astype(BF16)

    nbs = x_sample.shape[0]
    xs_in = x_sample.reshape(nbs, D_MODEL)
    cst = jnp.swapaxes(state_ssd_conv[0], 0, 1)
    scst, fst = state_short_conv[0], state_ffn_conv[0]
    z, xs, xdt_t, da, bc, ysc, ncst, nscst, win = _sample_pre(
        xs_in, nw, w_in_t, cw, cb, dtb, a_pad, scw, expand, cst, scst)
    s_ssm, y_s = _sample_state(state_ssm[0], da, xdt_t, bc)

    wout = w_out[0].astype(BF16)
    x1, p_ssm, p_cst, p_scst, wffn, wdown = _mixer_prompt(
        x_prompt, nw, win, cw, cb, dtb, a_pad, dx, gnw, scw, wout, expand, w_ffn_in[0], w_down[0])
    y_sample, nfst = _sample_post(xs_in, y_s, xs, z, ysc, dx, gnw, wout, nw2, wffn, fcw, fcb, wdown, nfw, fst)
    y_prompt, p_fst = _ffn_prompt(x1, nw2, wffn, fcw, fcb, wdown, nfw)

    return (y_prompt, y_sample.reshape(nbs, 1, D_MODEL),
            p_ssm[None], p_cst[None], p_scst[None], p_fst[None],
            s_ssm[None], jnp.swapaxes(ncst, 0, 1)[None], nscst[None], nfst[None])
```

```python
import functools

import jax
import jax.numpy as jnp
from jax import lax
from jax.experimental import pallas as pl
from jax.experimental.pallas import tpu as pltpu

D_MODEL = 1024
D_SSD = 1024
D_SC = 1024
NHEADS = 16
HEAD_DIM = 64
NGROUPS = 2
NSTATE = 128
SSD_CONV = 4
CONV_DIM = D_SSD + 2 * NGROUPS * NSTATE
SC_CONV = 3
D_FF = 2816
FFN_CONV = 3
EPS = 1e-5
D_MIX = D_SSD + D_SC
D_IN_PROJ = D_SSD + CONV_DIM + NHEADS + 3 * D_SC

LANES = 128
MXU_COLS = 256
OFF_Z = 0
OFF_XBC = OFF_Z + D_SSD
OFF_DT = OFF_XBC + CONV_DIM
OFF_GB = OFF_DT + NHEADS
OFF_GC = OFF_GB + D_SC
OFF_H = OFF_GC + D_SC
DT_PAD = LANES

CHUNK = 128
CARRY = 8
SEQ_TILE = 512
VMEM_LIMIT = 56 * 1024 * 1024

F32 = jnp.float32
BF16 = jnp.bfloat16


def _dot(a, b):
    return jnp.dot(a, b, preferred_element_type=F32)


def _dot_nt(a, b_t):
    return lax.dot_general(a, b_t, (((1,), (1,)), ((), ())), preferred_element_type=F32)


def _split3(v):
    hi = v.astype(BF16)
    r = v - hi.astype(F32)
    mid = r.astype(BF16)
    lo = (r - mid.astype(F32)).astype(BF16)
    return hi, mid, lo


def _dot_sel_rhs(v, sel):
    hi, mid, lo = _split3(v)
    return (_dot(hi, sel) + _dot(mid, sel)) + _dot(lo, sel)


def _dot_sel_lhs(sel, v):
    hi, mid, lo = _split3(v)
    return (_dot(sel, hi) + _dot(sel, mid)) + _dot(sel, lo)


def _silu(v):
    h = 0.5 * v
    return h + h * jnp.tanh(h)


def _softplus(v):
    return jnp.maximum(v, 0.0) + jnp.log1p(jnp.exp(-jnp.abs(v)))


def _rms_rows(x, w):
    ms = jnp.mean(x * x, axis=-1, keepdims=True)
    return x * lax.rsqrt(ms + EPS) * w


def _mixer_prompt_kernel(tiles_per_seq,
                         x_ref, xres_ref, wffn_f32_ref, wdown_f32_ref,
                         nw_ref, win_ref, cw_ref, cb_ref, dtb_ref, a_ref, dx_ref,
                         gnw_ref, scw_ref, wout_ref, e_ref,
                         x1_ref, ssm_ref, cst_ref, scst_ref, wffn_bf_ref, wdown_bf_ref,
                         xbc_ext, sc_ext, u_ref, act_ref, dt_ref, z_ref, gb_ref, s_ref, y_ref, ysc_ref, ycat_ref):
    tile = x_ref.shape[1]
    g = pl.program_id(0)
    last = pl.num_programs(0) - 1
    t = lax.rem(g, tiles_per_seq)

    @pl.when(g < CAST_FFN_STEPS)
    def _():
        wffn_bf_ref[...] = wffn_f32_ref[...].astype(BF16)

    @pl.when(g < CAST_DOWN_STEPS)
    def _():
        wdown_bf_ref[...] = wdown_f32_ref[...].astype(BF16)

    @pl.when(g == 0)
    def _():
        ycat_ref[...] = jnp.zeros_like(ycat_ref)

    @pl.when(t == 0)
    def _():
        xbc_ext[0:CARRY, :] = jnp.zeros((CARRY, CONV_DIM), F32)
        sc_ext[0:CARRY, :] = jnp.zeros((CARRY, D_SC), F32)
        s_ref[...] = jnp.zeros_like(s_ref)

    def out_block(off):
        x1_ref[0, :, off:off + MXU_COLS] = xres_ref[0, :, off:off + MXU_COLS] + _dot(
            ycat_ref[...], wout_ref[:, off:off + MXU_COLS])

    @pl.when(g == last)
    def _():
        for off in range(0, D_MODEL, MXU_COLS):
            out_block(off)

    @pl.when(g < last)
    def _():
        _mixer_tile(tile, t == tiles_per_seq - 1, out_block,
                    x_ref, nw_ref, win_ref, cw_ref, cb_ref, dtb_ref, a_ref, dx_ref, gnw_ref, scw_ref, e_ref,
                    ssm_ref, cst_ref, scst_ref,
                    xbc_ext, sc_ext, u_ref, act_ref, dt_ref, z_ref, gb_ref, s_ref, y_ref, ysc_ref, ycat_ref)


def _mixer_tile(tile, ends_sequence, out_block,
                x_ref, nw_ref, win_ref, cw_ref, cb_ref, dtb_ref, a_ref, dx_ref, gnw_ref, scw_ref, e_ref,
                ssm_ref, cst_ref, scst_ref,
                xbc_ext, sc_ext, u_ref, act_ref, dt_ref, z_ref, gb_ref, s_ref, y_ref, ysc_ref, ycat_ref):
    u_ref[...] = _rms_rows(x_ref[0], nw_ref[...]).astype(BF16)

    def project(dst, row0, base, off, width=MXU_COLS):
        dst[row0:row0 + tile, off:off + width] = _dot_nt(u_ref[...], win_ref[base + off:base + off + width, :])

    def conv_block(off):
        conv = cb_ref[:, off:off + LANES] + (
            xbc_ext[CARRY - 3:CARRY - 3 + tile, off:off + LANES] * cw_ref[0:1, off:off + LANES])
        for k in range(1, SSD_CONV):
            conv = conv + (xbc_ext[CARRY - 3 + k:CARRY - 3 + k + tile, off:off + LANES]
                           * cw_ref[k:k + 1, off:off + LANES])
        act_ref[:, off:off + LANES] = _silu(conv)

    def sc_block(off):
        sc = sc_ext[CARRY - 2:CARRY - 2 + tile, off:off + LANES] * scw_ref[0:1, off:off + LANES]
        for k in range(1, SC_CONV):
            sc = sc + (sc_ext[CARRY - 2 + k:CARRY - 2 + k + tile, off:off + LANES]
                       * scw_ref[k:k + 1, off:off + LANES])
        ysc_ref[:, off:off + LANES] = (gb_ref[:, off:off + LANES] * sc).astype(BF16)

    for off in range(0, CONV_DIM, MXU_COLS):
        project(xbc_ext, CARRY, OFF_XBC, off)
        for sub in range(off, off + MXU_COLS, LANES):
            conv_block(sub)
    tail = xbc_ext[tile + CARRY - 3:tile + CARRY, :]
    cst_ref[0] = tail
    xbc_ext[CARRY - 3:CARRY, :] = tail
    project(dt_ref, 0, OFF_DT, 0, DT_PAD)
    for off in range(0, D_SC, MXU_COLS):
        u = u_ref[...]
        sc_ext[CARRY:CARRY + tile, off:off + MXU_COLS] = (
            _dot_nt(u, win_ref[OFF_GC + off:OFF_GC + off + MXU_COLS, :])
            * _dot_nt(u, win_ref[OFF_H + off:OFF_H + off + MXU_COLS, :]))

    fillers = []
    for k in range(D_SC // MXU_COLS):
        fillers.append(lambda k=k: out_block(k * MXU_COLS))
        fillers.append(lambda k=k: project(gb_ref, 0, OFF_GB, k * MXU_COLS))
        fillers.append(lambda k=k: sc_block(k * MXU_COLS))
        fillers.append(lambda k=k: sc_block(k * MXU_COLS + LANES))
        fillers.append(lambda k=k: project(z_ref, 0, OFF_Z, k * MXU_COLS))
    n_slots = (tile // CHUNK) * (NHEADS // 2)
    slot_of_filler = [(i * n_slots) // len(fillers) for i in range(len(fillers))]

    row_i = lax.broadcasted_iota(jnp.int32, (CHUNK, CHUNK), 0)
    col_i = lax.broadcasted_iota(jnp.int32, (CHUNK, CHUNK), 1)
    causal = row_i >= col_i
    tri = causal.astype(BF16)
    lane_i = lax.broadcasted_iota(jnp.int32, (1, LANES), 1)
    keeps = ((lane_i < HEAD_DIM).astype(BF16), (lane_i >= HEAD_DIM).astype(BF16))
    neg_inf = jnp.float32(-jnp.inf)
    pairs_per_group = NHEADS // NGROUPS // 2

    for c in range(tile // CHUNK):
        r0 = c * CHUNK
        dt = _softplus(dt_ref[r0:r0 + CHUNK, :] + dtb_ref[...])
        acum = _dot_sel_lhs(tri, dt * a_ref[...])
        acum_t = acum.T
        dt_t = dt.T
        ea = jnp.exp(acum)
        w_t = jnp.exp(acum_t[:, CHUNK - 1:CHUNK] - acum_t) * dt_t
        cd = jnp.broadcast_to(ea[CHUNK - 1:CHUNK, :], (8, LANES))
        cdx = _dot_sel_rhs(cd, e_ref[...])[0:1, :]
        for grp in range(NGROUPS):
            bg = act_ref[r0:r0 + CHUNK, D_SSD + grp * NSTATE:D_SSD + (grp + 1) * NSTATE]
            cg = act_ref[r0:r0 + CHUNK, D_SSD + (NGROUPS + grp) * NSTATE:D_SSD + (NGROUPS + grp + 1) * NSTATE]
            cb = lax.dot_general(cg.astype(BF16), bg.astype(BF16), (((1,), (1,)), ((), ())),
                                 preferred_element_type=F32)
            bg_t = bg.T
            for jp in range(pairs_per_group):
                j = grp * pairs_per_group + jp
                xp = act_ref[r0:r0 + CHUNK, j * LANES:(j + 1) * LANES].astype(BF16)
                sp = s_ref[:, j * LANES:(j + 1) * LANES]
                spb = sp.astype(BF16)
                lhs_parts, rhs_parts, bw_parts, x_parts = [], [], [], []
                for half, keep in enumerate(keeps):
                    h = 2 * j + half
                    xh = xp * keep
                    seg = acum[:, h:h + 1] - acum_t[h:h + 1, :]
                    decay = jnp.exp(jnp.where(causal, seg, neg_inf))
                    m = cb * decay * dt_t[h:h + 1, :]
                    eac = ea[:, h:h + 1] * cg
                    lhs_parts += [m.astype(BF16), eac.astype(BF16)]
                    rhs_parts += [xh, spb * keep]
                    bw_parts.append((bg_t * w_t[h:h + 1, :]).astype(BF16))
                    x_parts.append(xh)
                y_ref[r0:r0 + CHUNK, j * LANES:(j + 1) * LANES] = _dot(
                    jnp.concatenate(lhs_parts, axis=1), jnp.concatenate(rhs_parts, axis=0))
                snew = _dot(jnp.concatenate(bw_parts, axis=1), jnp.concatenate(x_parts, axis=0))
                s_ref[:, j * LANES:(j + 1) * LANES] = cdx[:, j * LANES:(j + 1) * LANES] * sp + snew
                slot = c * (NHEADS // 2) + j
                for i, filler in enumerate(fillers):
                    if slot_of_filler[i] == slot:
                        filler()
    sc_tail = sc_ext[tile + CARRY - 2:tile + CARRY, :]
    scst_ref[0] = sc_tail
    sc_ext[CARRY - 2:CARRY, :] = sc_tail

    gw = D_SSD // NGROUPS
    for grp in range(NGROUPS):
        ssq = jnp.zeros((tile, LANES), F32)
        for off in range(grp * gw, (grp + 1) * gw, LANES):
            yb = ((y_ref[:, off:off + LANES] + dx_ref[:, off:off + LANES] * act_ref[:, off:off + LANES])
                  * _silu(z_ref[:, off:off + LANES]))
            y_ref[:, off:off + LANES] = yb
            ssq = ssq + yb * yb
        scale = lax.rsqrt(jnp.sum(ssq, axis=-1, keepdims=True) * (1.0 / gw) + EPS)
        for off in range(grp * gw, (grp + 1) * gw, LANES):
            ycat_ref[:, off:off + LANES] = (y_ref[:, off:off + LANES] * scale
                                            * gnw_ref[:, off:off + LANES]).astype(BF16)
    ycat_ref[:, D_SSD:D_MIX] = ysc_ref[...]

    @pl.when(ends_sequence)
    def _():
        for j in range(NHEADS // 2):
            ssm_ref[0, 2 * j:2 * j + 2] = s_ref[:, j * LANES:(j + 1) * LANES].T.reshape(2, HEAD_DIM, NSTATE)


def _const_spec(shape):
    nd = len(shape)
    return pl.BlockSpec(shape, lambda *_: (0,) * nd, pipeline_mode=pl.Buffered(1))


CAST_FFN_STEPS = 32
CAST_DOWN_STEPS = 16


def _mixer_prompt(x, nw, win, cw, cb, dtb, a_pad, dx, gnw, scw, wout, expand, w_ffn_in, w_down):
    nb, seq, _ = x.shape
    tile = SEQ_TILE
    nt = seq // tile
    total = nb * nt
    assert total >= CAST_FFN_STEPS and total >= CAST_DOWN_STEPS
    ffn_rows = D_MODEL // CAST_FFN_STEPS
    down_rows = D_FF // CAST_DOWN_STEPS
    ffn_spec = pl.BlockSpec((ffn_rows, 2 * D_FF), lambda g: (jnp.minimum(g, CAST_FFN_STEPS - 1), 0))
    down_spec = pl.BlockSpec((down_rows, D_MODEL), lambda g: (jnp.minimum(g, CAST_DOWN_STEPS - 1), 0))

    def tile_of(g, lag):
        i = jnp.clip(g - lag, 0, total - 1)
        return i // nt, i % nt

    cur_spec = pl.BlockSpec((1, tile, D_MODEL), lambda g: (*tile_of(g, 0), 0))
    prev_spec = pl.BlockSpec((1, tile, D_MODEL), lambda g: (*tile_of(g, 1), 0))
    consts = (nw, win, cw, cb, dtb, a_pad, dx, gnw, scw, wout, expand)
    return pl.pallas_call(
        functools.partial(_mixer_prompt_kernel, nt),
        grid=(total + 1,),
        in_specs=[cur_spec, prev_spec, ffn_spec, down_spec] + [_const_spec(p.shape) for p in consts],
        out_specs=[
            prev_spec,
            pl.BlockSpec((1, NHEADS, HEAD_DIM, NSTATE), lambda g: (tile_of(g, 0)[0], 0, 0, 0)),
            pl.BlockSpec((1, SSD_CONV - 1, CONV_DIM), lambda g: (tile_of(g, 0)[0], 0, 0)),
            pl.BlockSpec((1, SC_CONV - 1, D_SC), lambda g: (tile_of(g, 0)[0], 0, 0)),
            ffn_spec,
            down_spec,
        ],
        out_shape=[
            jax.ShapeDtypeStruct((nb, seq, D_MODEL), F32),
            jax.ShapeDtypeStruct((nb, NHEADS, HEAD_DIM, NSTATE), F32),
            jax.ShapeDtypeStruct((nb, SSD_CONV - 1, CONV_DIM), F32),
            jax.ShapeDtypeStruct((nb, SC_CONV - 1, D_SC), F32),
            jax.ShapeDtypeStruct(w_ffn_in.shape, BF16),
            jax.ShapeDtypeStruct(w_down.shape, BF16),
        ],
        scratch_shapes=[
            pltpu.VMEM((tile + CARRY, CONV_DIM), F32),
            pltpu.VMEM((tile + CARRY, D_SC), F32),
            pltpu.VMEM((tile, D_MODEL), BF16),
            pltpu.VMEM((tile, CONV_DIM), F32),
            pltpu.VMEM((tile, DT_PAD), F32),
            pltpu.VMEM((tile, D_SSD), F32),
            pltpu.VMEM((tile, D_SC), F32),
            pltpu.VMEM((NSTATE, D_SSD), F32),
            pltpu.VMEM((tile, D_SSD), F32),
            pltpu.VMEM((tile, D_SC), BF16),
            pltpu.VMEM((tile, D_MIX), BF16),
        ],
        compiler_params=pltpu.CompilerParams(
            dimension_semantics=("arbitrary",), vmem_limit_bytes=VMEM_LIMIT),
        name="mixer_prompt",
    )(x, x, w_ffn_in, w_down, *consts)


STATE_HEADS = 2
STATE_SEQS = 64
STATE_ROWS = STATE_HEADS * HEAD_DIM


def _state_block_update(da_ref, step, st_ref, xdt_t_ref, b_ref, c_ref, nst_ref, yt_ref):
    seq_blocks = da_ref.shape[0] // STATE_SEQS
    seq0 = lax.rem(step, seq_blocks) * STATE_SEQS
    head0 = (step // seq_blocks) * STATE_HEADS
    c_blk = c_ref[0].astype(BF16)
    lane = lax.broadcasted_iota(jnp.int32, (STATE_ROWS, STATE_SEQS), 1)
    ycols = jnp.zeros((STATE_ROWS, STATE_SEQS), F32)
    for i in range(STATE_SEQS):
        upd = xdt_t_ref[0, :, i:i + 1] * b_ref[0, i:i + 1, :]
        parts = []
        for hh in range(STATE_HEADS):
            hnew = st_ref[i, hh] * da_ref[seq0 + i, head0 + hh] + upd[hh * HEAD_DIM:(hh + 1) * HEAD_DIM, :]
            nst_ref[i, hh] = hnew
            parts.append(hnew)
        prod = _dot_nt(jnp.concatenate(parts, axis=0).astype(BF16), c_blk)
        ycols = jnp.where(lane == i, prod, ycols)
    yt_ref[0] = ycols


def _ffn_prompt_kernel(state_steps,
                       da_ref, x1_ref, st_ref, xdt_t_ref, b_ref, c_ref,
                       nw_ref, wffn_ref, fcw_ref, fcb_ref, wdown_ref, nfw_ref,
                       y_ref, fst_ref, nst_ref, yt_ref, gate_ext):
    tile = x1_ref.shape[1]
    t = pl.program_id(1)
    step = pl.program_id(0) * pl.num_programs(1) + t

    @pl.when(step < state_steps)
    def _():
        _state_block_update(da_ref, step, st_ref, xdt_t_ref, b_ref, c_ref, nst_ref, yt_ref)

    @pl.when(t == 0)
    def _():
        gate_ext[0:CARRY, :] = jnp.zeros((CARRY, D_FF), F32)

    x1 = x1_ref[0]
    u = _rms_rows(x1, nw_ref[...]).astype(BF16)
    gate_ext[CARRY:CARRY + tile, :] = _dot(u, wffn_ref[:, 0:D_FF])
    up = _dot(u, wffn_ref[:, D_FF:2 * D_FF])
    g = fcb_ref[...] + gate_ext[CARRY - 2:CARRY - 2 + tile, :] * fcw_ref[0:1, :]
    for k in range(1, FFN_CONV):
        g = g + gate_ext[CARRY - 2 + k:CARRY - 2 + k + tile, :] * fcw_ref[k:k + 1, :]
    tail = gate_ext[tile + CARRY - 2:tile + CARRY, :]
    fst_ref[0] = tail
    gate_ext[CARRY - 2:CARRY, :] = tail
    act = (_silu(g) * up).astype(BF16)
    x2 = x1 + _dot(act, wdown_ref[...])
    y_ref[0] = _rms_rows(x2, nfw_ref[...])


def _ffn_prompt(x1, nw, wffn, fcw, fcb, wdown, nfw, state, da, xdt_t, bc):
    nb, seq, _ = x1.shape
    tile = SEQ_TILE
    nt = seq // tile
    nseq = state.shape[0]
    seq_blocks = nseq // STATE_SEQS
    state_steps = seq_blocks * (NHEADS // STATE_HEADS)
    assert nb * nt >= state_steps
    heads_per_group = NHEADS // NGROUPS

    def block_of(b, t):
        s = jnp.minimum(b * nt + t, state_steps - 1)
        return s % seq_blocks, s // seq_blocks

    def st_map(b, t):
        sb, hb = block_of(b, t)
        return (sb, hb, 0, 0)

    def col_map(b, t):
        sb, hb = block_of(b, t)
        return (sb, hb, 0)

    def b_map(b, t):
        sb, hb = block_of(b, t)
        return (hb * STATE_HEADS // heads_per_group, sb, 0)

    def c_map(b, t):
        sb, hb = block_of(b, t)
        return (NGROUPS + hb * STATE_HEADS // heads_per_group, sb, 0)

    tok_spec = pl.BlockSpec((1, tile, D_MODEL), lambda b, t: (b, t, 0))
    st_spec = pl.BlockSpec((STATE_SEQS, STATE_HEADS, HEAD_DIM, NSTATE), st_map)
    col_spec = pl.BlockSpec((1, STATE_ROWS, STATE_SEQS), col_map)
    return pl.pallas_call(
        functools.partial(_ffn_prompt_kernel, state_steps),
        grid=(nb, nt),
        in_specs=[pl.BlockSpec(memory_space=pltpu.SMEM), tok_spec, st_spec, col_spec,
                  pl.BlockSpec((1, STATE_SEQS, NSTATE), b_map), pl.BlockSpec((1, STATE_SEQS, NSTATE), c_map)]
        + [_const_spec(p.shape) for p in (nw, wffn, fcw, fcb, wdown, nfw)],
        out_specs=[tok_spec, pl.BlockSpec((1, FFN_CONV - 1, D_FF), lambda b, t: (b, 0, 0)), st_spec, col_spec],
        out_shape=[jax.ShapeDtypeStruct((nb, seq, D_MODEL), F32),
                   jax.ShapeDtypeStruct((nb, FFN_CONV - 1, D_FF), F32),
                   jax.ShapeDtypeStruct(state.shape, F32),
                   jax.ShapeDtypeStruct(xdt_t.shape, F32)],
        scratch_shapes=[pltpu.VMEM((tile + CARRY, D_FF), F32)],
        compiler_params=pltpu.CompilerParams(
            dimension_semantics=("arbitrary", "arbitrary"), vmem_limit_bytes=VMEM_LIMIT),
        name="ffn_prompt",
    )(da, x1, state, xdt_t, bc, bc, nw, wffn, fcw, fcb, wdown, nfw)


CAST_ROWS = 512


def _sample_pre_kernel(x_ref, nw_ref, wf32_ref, cw_ref, cb_ref, dtb_ref, a_ref, scw_ref, e_ref,
                       cst_ref, scst_ref,
                       z_ref, xs_ref, xdt_t_ref, da_ref, bc_ref, ysc_ref, ncst_ref, nscst_ref, win_ref):
    for r0 in range(0, D_IN_PROJ, CAST_ROWS):
        r1 = min(r0 + CAST_ROWS, D_IN_PROJ)
        win_ref[r0:r1, :] = wf32_ref[r0:r1, :].astype(BF16)
    x = x_ref[...]
    u = _rms_rows(x, nw_ref[...]).astype(BF16)
    z_ref[...] = _dot_nt(u, win_ref[OFF_Z:OFF_Z + D_SSD, :])
    xbc = _dot_nt(u, win_ref[OFF_XBC:OFF_XBC + CONV_DIM, :])
    conv = cb_ref[...] + xbc * cw_ref[SSD_CONV - 1:SSD_CONV, :]
    for k in range(SSD_CONV - 1):
        conv = conv + cst_ref[k] * cw_ref[k:k + 1, :]
    for k in range(SSD_CONV - 2):
        ncst_ref[k] = cst_ref[k + 1]
    ncst_ref[SSD_CONV - 2] = xbc
    act = _silu(conv)
    xs = act[:, 0:D_SSD]
    xs_ref[...] = xs
    for i in range(2 * NGROUPS):
        bc_ref[i] = act[:, D_SSD + i * NSTATE:D_SSD + (i + 1) * NSTATE]
    dt = _softplus(_dot_nt(u, win_ref[OFF_DT:OFF_DT + DT_PAD, :]) + dtb_ref[...])
    xdt_t = (xs * _dot_sel_rhs(dt, e_ref[...])).T
    for sb in range(xdt_t_ref.shape[0]):
        xdt_t_ref[sb] = xdt_t[:, sb * STATE_SEQS:(sb + 1) * STATE_SEQS]
    da_ref[...] = jnp.exp(dt * a_ref[...])[:, 0:NHEADS]
    gc = _dot_nt(u, win_ref[OFF_GC:OFF_GC + D_SC, :])
    hh = _dot_nt(u, win_ref[OFF_H:OFF_H + D_SC, :])
    gch = gc * hh
    sc = gch * scw_ref[SC_CONV - 1:SC_CONV, :]
    for k in range(SC_CONV - 1):
        sc = sc + scst_ref[:, k, :] * scw_ref[k:k + 1, :]
    for k in range(SC_CONV - 2):
        nscst_ref[:, k, :] = scst_ref[:, k + 1, :]
    nscst_ref[:, SC_CONV - 2, :] = gch
    gb = _dot_nt(u, win_ref[OFF_GB:OFF_GB + D_SC, :])
    ysc_ref[...] = gb * sc


def _sample_pre(x, nw, w_in_t, cw, cb, dtb, a_pad, scw, expand, cst, scst):
    nb = x.shape[0]
    outs = [
        jax.ShapeDtypeStruct((nb, D_SSD), F32),
        jax.ShapeDtypeStruct((nb, D_SSD), F32),
        jax.ShapeDtypeStruct((nb // STATE_SEQS, D_SSD, STATE_SEQS), F32),
        jax.ShapeDtypeStruct((nb, NHEADS), F32),
        jax.ShapeDtypeStruct((2 * NGROUPS, nb, NSTATE), F32),
        jax.ShapeDtypeStruct((nb, D_SC), F32),
        jax.ShapeDtypeStruct((SSD_CONV - 1, nb, CONV_DIM), F32),
        jax.ShapeDtypeStruct((nb, SC_CONV - 1, D_SC), F32),
        jax.ShapeDtypeStruct((D_IN_PROJ, D_MODEL), BF16),
    ]
    return pl.pallas_call(
        _sample_pre_kernel,
        out_shape=outs,
        compiler_params=pltpu.CompilerParams(vmem_limit_bytes=VMEM_LIMIT),
        name="sample_pre",
    )(x, nw, w_in_t, cw, cb, dtb, a_pad, scw, expand, cst, scst)


def _sample_post_kernel(x_ref, yt_ref, xs_ref, z_ref, ysc_ref, dx_ref, gnw_ref, wout_ref,
                        nw_ref, wffn_ref, fcw_ref, fcb_ref, wdown_ref, nfw_ref, fst_ref,
                        out_ref, nfst_ref):
    y_ssd = jnp.concatenate([yt_ref[sb] for sb in range(yt_ref.shape[0])], axis=1).T
    y = (y_ssd + dx_ref[...] * xs_ref[...]) * _silu(z_ref[...])
    gw = D_SSD // NGROUPS
    parts = [_rms_rows(y[:, g * gw:(g + 1) * gw], gnw_ref[:, g * gw:(g + 1) * gw]) for g in range(NGROUPS)]
    ycat = jnp.concatenate(parts + [ysc_ref[...]], axis=1).astype(BF16)
    x1 = x_ref[...] + _dot(ycat, wout_ref[...])
    u = _rms_rows(x1, nw_ref[...]).astype(BF16)
    gate = _dot(u, wffn_ref[:, 0:D_FF])
    up = _dot(u, wffn_ref[:, D_FF:2 * D_FF])
    g = fcb_ref[...] + gate * fcw_ref[FFN_CONV - 1:FFN_CONV, :]
    for k in range(FFN_CONV - 1):
        g = g + fst_ref[:, k, :] * fcw_ref[k:k + 1, :]
    for k in range(FFN_CONV - 2):
        nfst_ref[:, k, :] = fst_ref[:, k + 1, :]
    nfst_ref[:, FFN_CONV - 2, :] = gate
    act = (_silu(g) * up).astype(BF16)
    x2 = x1 + _dot(act, wdown_ref[...])
    out_ref[...] = _rms_rows(x2, nfw_ref[...])


def _sample_post(x, y, xs, z, ysc, dx, gnw, wout, nw, wffn, fcw, fcb, wdown, nfw, fst):
    nb = x.shape[0]
    return pl.pallas_call(
        _sample_post_kernel,
        out_shape=[jax.ShapeDtypeStruct((nb, D_MODEL), F32),
                   jax.ShapeDtypeStruct((nb, FFN_CONV - 1, D_FF), F32)],
        compiler_params=pltpu.CompilerParams(vmem_limit_bytes=VMEM_LIMIT),
        name="sample_post",
    )(x, y, xs, z, ysc, dx, gnw, wout, nw, wffn, fcw, fcb, wdown, nfw, fst)


def kernel(x_prompt, x_sample, state_ssm, state_ssd_conv, state_short_conv, state_ffn_conv,
           norm_mix_w, w_in, ssd_conv_w, ssd_conv_b, ssd_dt_bias, ssd_a_log, ssd_d, ssd_norm_w,
           sc_conv_w, w_out, norm_ffn_w, w_ffn_in, ffn_conv_w, ffn_conv_b, w_down, norm_final_w):
    depth = w_in.shape[0]
    assert depth == 1
    w_in_t = jnp.swapaxes(w_in[0], 0, 1)
    nw = norm_mix_w[0].reshape(1, D_MODEL)
    nw2 = norm_ffn_w[0].reshape(1, D_MODEL)
    nfw = norm_final_w.reshape(1, D_MODEL)
    cw = ssd_conv_w[0]
    cb = ssd_conv_b[0].reshape(1, CONV_DIM)
    dtb = jnp.pad(ssd_dt_bias[0], (0, DT_PAD - NHEADS)).reshape(1, DT_PAD)
    a_pad = jnp.pad(-jnp.exp(ssd_a_log[0]), (0, DT_PAD - NHEADS)).reshape(1, DT_PAD)
    dx = jnp.repeat(ssd_d[0], HEAD_DIM).reshape(1, D_SSD)
    gnw = ssd_norm_w[0].reshape(1, D_SSD)
    scw = sc_conv_w[0]
    fcw = ffn_conv_w[0]
    fcb = ffn_conv_b[0].reshape(1, D_FF)
    head_of_lane = jnp.arange(D_SSD, dtype=jnp.int32) // HEAD_DIM
    expand = (jnp.arange(DT_PAD, dtype=jnp.int32)[:, None] == head_of_lane[None, :]).astype(BF16)

    nbs = x_sample.shape[0]
    xs_in = x_sample.reshape(nbs, D_MODEL)
    cst = jnp.swapaxes(state_ssd_conv[0], 0, 1)
    scst, fst = state_short_conv[0], state_ffn_conv[0]
    z, xs, xdt_t, da, bc, ysc, ncst, nscst, win = _sample_pre(
        xs_in, nw, w_in_t, cw, cb, dtb, a_pad, scw, expand, cst, scst)

    wout = w_out[0].astype(BF16)
    x1, p_ssm, p_cst, p_scst, wffn, wdown = _mixer_prompt(
        x_prompt, nw, win, cw, cb, dtb, a_pad, dx, gnw, scw, wout, expand, w_ffn_in[0], w_down[0])
    y_prompt, p_fst, s_ssm, y_t = _ffn_prompt(x1, nw2, wffn, fcw, fcb, wdown, nfw, state_ssm[0], da, xdt_t, bc)
    y_sample, nfst = _sample_post(xs_in, y_t, xs, z, ysc, dx, gnw, wout, nw2, wffn, fcw, fcb, wdown, nfw, fst)

    return (y_prompt, y_sample.reshape(nbs, 1, D_MODEL),
            p_ssm[None], p_cst[None], p_scst[None], p_fst[None],
            s_ssm[None], jnp.swapaxes(ncst, 0, 1)[None], nscst[None], nfst[None])
```

```python
import functools

import jax
import jax.numpy as jnp
import numpy as np
from jax import lax
from jax.experimental import pallas as pl
from jax.experimental.pallas import tpu as pltpu

D_MODEL = 1024
D_SSD = 1024
D_SC = 1024
NHEADS = 16
HEAD_DIM = 64
NGROUPS = 2
NSTATE = 128
SSD_CONV = 4
CONV_DIM = D_SSD + 2 * NGROUPS * NSTATE
SC_CONV = 3
D_FF = 2816
FFN_CONV = 3
EPS = 1e-5
D_MIX = D_SSD + D_SC
D_IN_PROJ = D_SSD + CONV_DIM + NHEADS + 3 * D_SC

LANES = 128
MXU_COLS = 256
OFF_Z = 0
OFF_XBC = OFF_Z + D_SSD
OFF_DT = OFF_XBC + CONV_DIM
OFF_GB = OFF_DT + NHEADS
OFF_GC = OFF_GB + D_SC
OFF_H = OFF_GC + D_SC
DT_PAD = LANES

CHUNK = 128
CARRY = 8
SEQ_TILE = 512
VMEM_LIMIT = 56 * 1024 * 1024

F32 = jnp.float32
BF16 = jnp.bfloat16


def _dot(a, b):
    return jnp.dot(a, b, preferred_element_type=F32)


def _dot_nt(a, b_t):
    return lax.dot_general(a, b_t, (((1,), (1,)), ((), ())), preferred_element_type=F32)


def _split3(v):
    hi = v.astype(BF16)
    r = v - hi.astype(F32)
    mid = r.astype(BF16)
    lo = (r - mid.astype(F32)).astype(BF16)
    return hi, mid, lo


def _dot_sel_rhs(v, sel):
    hi, mid, lo = _split3(v)
    return (_dot(hi, sel) + _dot(mid, sel)) + _dot(lo, sel)


def _dot_sel_lhs(sel, v):
    hi, mid, lo = _split3(v)
    return (_dot(sel, hi) + _dot(sel, mid)) + _dot(sel, lo)


def _silu(v):
    h = 0.5 * v
    return h + h * jnp.tanh(h)


def _softplus(v):
    return jnp.maximum(v, 0.0) + jnp.log1p(jnp.exp(-jnp.abs(v)))


def _rms_rows(x, w):
    ms = jnp.mean(x * x, axis=-1, keepdims=True)
    return x * lax.rsqrt(ms + EPS) * w


def _mixer_prompt_kernel(tiles_per_seq,
                         x_ref, xres_ref, wffn_f32_ref, wdown_f32_ref,
                         nw_ref, win_ref, cw_ref, cb_ref, dtb_ref, a_ref, dx_ref,
                         gnw_ref, scw_ref, wout_ref, e_ref,
                         x1_ref, ssm_ref, cst_ref, scst_ref, wffn_bf_ref, wdown_bf_ref,
                         xbc_ext, sc_ext, u_ref, act_ref, dt_ref, z_ref, gb_ref, s_ref, y_ref, ysc_ref, ycat_ref):
    tile = x_ref.shape[1]
    g = pl.program_id(0)
    last = pl.num_programs(0) - 1
    t = lax.rem(g, tiles_per_seq)

    @pl.when(g < CAST_FFN_STEPS)
    def _():
        wffn_bf_ref[...] = wffn_f32_ref[...].astype(BF16)

    @pl.when(g < CAST_DOWN_STEPS)
    def _():
        wdown_bf_ref[...] = wdown_f32_ref[...].astype(BF16)

    @pl.when(g == 0)
    def _():
        ycat_ref[...] = jnp.zeros_like(ycat_ref)

    @pl.when(t == 0)
    def _():
        xbc_ext[0:CARRY, :] = jnp.zeros((CARRY, CONV_DIM), F32)
        sc_ext[0:CARRY, :] = jnp.zeros((CARRY, D_SC), F32)
        s_ref[...] = jnp.zeros_like(s_ref)

    def out_block(off):
        x1_ref[0, :, off:off + MXU_COLS] = xres_ref[0, :, off:off + MXU_COLS] + _dot(
            ycat_ref[...], wout_ref[:, off:off + MXU_COLS])

    @pl.when(g == last)
    def _():
        for off in range(0, D_MODEL, MXU_COLS):
            out_block(off)

    @pl.when(g < last)
    def _():
        _mixer_tile(tile, t == tiles_per_seq - 1, out_block,
                    x_ref, nw_ref, win_ref, cw_ref, cb_ref, dtb_ref, a_ref, dx_ref, gnw_ref, scw_ref, e_ref,
                    ssm_ref, cst_ref, scst_ref,
                    xbc_ext, sc_ext, u_ref, act_ref, dt_ref, z_ref, gb_ref, s_ref, y_ref, ysc_ref, ycat_ref)


def _mixer_tile(tile, ends_sequence, out_block,
                x_ref, nw_ref, win_ref, cw_ref, cb_ref, dtb_ref, a_ref, dx_ref, gnw_ref, scw_ref, e_ref,
                ssm_ref, cst_ref, scst_ref,
                xbc_ext, sc_ext, u_ref, act_ref, dt_ref, z_ref, gb_ref, s_ref, y_ref, ysc_ref, ycat_ref):
    u_ref[...] = _rms_rows(x_ref[0], nw_ref[...]).astype(BF16)

    def project(dst, row0, base, off, width=MXU_COLS):
        dst[row0:row0 + tile, off:off + width] = _dot_nt(u_ref[...], win_ref[base + off:base + off + width, :])

    def conv_block(off):
        conv = cb_ref[:, off:off + LANES] + (
            xbc_ext[CARRY - 3:CARRY - 3 + tile, off:off + LANES] * cw_ref[0:1, off:off + LANES])
        for k in range(1, SSD_CONV):
            conv = conv + (xbc_ext[CARRY - 3 + k:CARRY - 3 + k + tile, off:off + LANES]
                           * cw_ref[k:k + 1, off:off + LANES])
        act_ref[:, off:off + LANES] = _silu(conv)

    def sc_block(off):
        sc = sc_ext[CARRY - 2:CARRY - 2 + tile, off:off + LANES] * scw_ref[0:1, off:off + LANES]
        for k in range(1, SC_CONV):
            sc = sc + (sc_ext[CARRY - 2 + k:CARRY - 2 + k + tile, off:off + LANES]
                       * scw_ref[k:k + 1, off:off + LANES])
        ysc_ref[:, off:off + LANES] = (gb_ref[:, off:off + LANES] * sc).astype(BF16)

    for off in range(0, CONV_DIM, MXU_COLS):
        project(xbc_ext, CARRY, OFF_XBC, off)
        for sub in range(off, off + MXU_COLS, LANES):
            conv_block(sub)
    tail = xbc_ext[tile + CARRY - 3:tile + CARRY, :]
    cst_ref[0] = tail
    xbc_ext[CARRY - 3:CARRY, :] = tail
    project(dt_ref, 0, OFF_DT, 0, DT_PAD)
    for off in range(0, D_SC, MXU_COLS):
        u = u_ref[...]
        sc_ext[CARRY:CARRY + tile, off:off + MXU_COLS] = (
            _dot_nt(u, win_ref[OFF_GC + off:OFF_GC + off + MXU_COLS, :])
            * _dot_nt(u, win_ref[OFF_H + off:OFF_H + off + MXU_COLS, :]))

    fillers = []
    for k in range(D_SC // MXU_COLS):
        fillers.append(lambda k=k: out_block(k * MXU_COLS))
        fillers.append(lambda k=k: project(gb_ref, 0, OFF_GB, k * MXU_COLS))
        fillers.append(lambda k=k: sc_block(k * MXU_COLS))
        fillers.append(lambda k=k: sc_block(k * MXU_COLS + LANES))
        fillers.append(lambda k=k: project(z_ref, 0, OFF_Z, k * MXU_COLS))
    n_slots = (tile // CHUNK) * (NHEADS // 2)
    slot_of_filler = [(i * n_slots) // len(fillers) for i in range(len(fillers))]

    row_i = lax.broadcasted_iota(jnp.int32, (CHUNK, CHUNK), 0)
    col_i = lax.broadcasted_iota(jnp.int32, (CHUNK, CHUNK), 1)
    causal = row_i >= col_i
    tri = causal.astype(BF16)
    lane_i = lax.broadcasted_iota(jnp.int32, (1, LANES), 1)
    keeps = ((lane_i < HEAD_DIM).astype(BF16), (lane_i >= HEAD_DIM).astype(BF16))
    neg_inf = jnp.float32(-jnp.inf)
    pairs_per_group = NHEADS // NGROUPS // 2

    for c in range(tile // CHUNK):
        r0 = c * CHUNK
        dt = _softplus(dt_ref[r0:r0 + CHUNK, :] + dtb_ref[...])
        acum = _dot_sel_lhs(tri, dt * a_ref[...])
        acum_t = acum.T
        dt_t = dt.T
        ea = jnp.exp(acum)
        w_t = jnp.exp(acum_t[:, CHUNK - 1:CHUNK] - acum_t) * dt_t
        cd = jnp.broadcast_to(ea[CHUNK - 1:CHUNK, :], (8, LANES))
        cdx = _dot_sel_rhs(cd, e_ref[...])[0:1, :]
        for grp in range(NGROUPS):
            bg = act_ref[r0:r0 + CHUNK, D_SSD + grp * NSTATE:D_SSD + (grp + 1) * NSTATE]
            cg = act_ref[r0:r0 + CHUNK, D_SSD + (NGROUPS + grp) * NSTATE:D_SSD + (NGROUPS + grp + 1) * NSTATE]
            cb = lax.dot_general(cg.astype(BF16), bg.astype(BF16), (((1,), (1,)), ((), ())),
                                 preferred_element_type=F32)
            bg_t = bg.T
            for jp in range(pairs_per_group):
                j = grp * pairs_per_group + jp
                xp = act_ref[r0:r0 + CHUNK, j * LANES:(j + 1) * LANES].astype(BF16)
                sp = s_ref[:, j * LANES:(j + 1) * LANES]
                spb = sp.astype(BF16)
                lhs_parts, rhs_parts, bw_parts, x_parts = [], [], [], []
                for half, keep in enumerate(keeps):
                    h = 2 * j + half
                    xh = xp * keep
                    seg = acum[:, h:h + 1] - acum_t[h:h + 1, :]
                    decay = jnp.exp(jnp.where(causal, seg, neg_inf))
                    m = cb * decay * dt_t[h:h + 1, :]
                    eac = ea[:, h:h + 1] * cg
                    lhs_parts += [m.astype(BF16), eac.astype(BF16)]
                    rhs_parts += [xh, spb * keep]
                    bw_parts.append((bg_t * w_t[h:h + 1, :]).astype(BF16))
                    x_parts.append(xh)
                y_ref[r0:r0 + CHUNK, j * LANES:(j + 1) * LANES] = _dot(
                    jnp.concatenate(lhs_parts, axis=1), jnp.concatenate(rhs_parts, axis=0))
                snew = _dot(jnp.concatenate(bw_parts, axis=1), jnp.concatenate(x_parts, axis=0))
                s_ref[:, j * LANES:(j + 1) * LANES] = cdx[:, j * LANES:(j + 1) * LANES] * sp + snew
                slot = c * (NHEADS // 2) + j
                for i, filler in enumerate(fillers):
                    if slot_of_filler[i] == slot:
                        filler()
    sc_tail = sc_ext[tile + CARRY - 2:tile + CARRY, :]
    scst_ref[0] = sc_tail
    sc_ext[CARRY - 2:CARRY, :] = sc_tail

    gw = D_SSD // NGROUPS
    for grp in range(NGROUPS):
        ssq = jnp.zeros((tile, LANES), F32)
        for off in range(grp * gw, (grp + 1) * gw, LANES):
            yb = ((y_ref[:, off:off + LANES] + dx_ref[:, off:off + LANES] * act_ref[:, off:off + LANES])
                  * _silu(z_ref[:, off:off + LANES]))
            y_ref[:, off:off + LANES] = yb
            ssq = ssq + yb * yb
        scale = lax.rsqrt(jnp.sum(ssq, axis=-1, keepdims=True) * (1.0 / gw) + EPS)
        for off in range(grp * gw, (grp + 1) * gw, LANES):
            ycat_ref[:, off:off + LANES] = (y_ref[:, off:off + LANES] * scale
                                            * gnw_ref[:, off:off + LANES]).astype(BF16)
    ycat_ref[:, D_SSD:D_MIX] = ysc_ref[...]

    @pl.when(ends_sequence)
    def _():
        for j in range(NHEADS // 2):
            ssm_ref[0, 2 * j:2 * j + 2] = s_ref[:, j * LANES:(j + 1) * LANES].T.reshape(2, HEAD_DIM, NSTATE)


def _const_spec(shape):
    nd = len(shape)
    return pl.BlockSpec(shape, lambda *_: (0,) * nd, pipeline_mode=pl.Buffered(1))


CAST_FFN_STEPS = 32
CAST_DOWN_STEPS = 16


def _mixer_prompt(x, nw, win, cw, cb, dtb, a_pad, dx, gnw, scw, wout, expand, w_ffn_in, w_down):
    nb, seq, _ = x.shape
    tile = SEQ_TILE
    nt = seq // tile
    total = nb * nt
    assert total >= CAST_FFN_STEPS and total >= CAST_DOWN_STEPS
    ffn_rows = D_MODEL // CAST_FFN_STEPS
    down_rows = D_FF // CAST_DOWN_STEPS
    ffn_spec = pl.BlockSpec((ffn_rows, 2 * D_FF), lambda g: (jnp.minimum(g, CAST_FFN_STEPS - 1), 0))
    down_spec = pl.BlockSpec((down_rows, D_MODEL), lambda g: (jnp.minimum(g, CAST_DOWN_STEPS - 1), 0))

    def tile_of(g, lag):
        i = jnp.clip(g - lag, 0, total - 1)
        return i // nt, i % nt

    cur_spec = pl.BlockSpec((1, tile, D_MODEL), lambda g: (*tile_of(g, 0), 0))
    prev_spec = pl.BlockSpec((1, tile, D_MODEL), lambda g: (*tile_of(g, 1), 0))
    consts = (nw, win, cw, cb, dtb, a_pad, dx, gnw, scw, wout, expand)
    return pl.pallas_call(
        functools.partial(_mixer_prompt_kernel, nt),
        grid=(total + 1,),
        in_specs=[cur_spec, prev_spec, ffn_spec, down_spec] + [_const_spec(p.shape) for p in consts],
        out_specs=[
            prev_spec,
            pl.BlockSpec((1, NHEADS, HEAD_DIM, NSTATE), lambda g: (tile_of(g, 0)[0], 0, 0, 0)),
            pl.BlockSpec((1, SSD_CONV - 1, CONV_DIM), lambda g: (tile_of(g, 0)[0], 0, 0)),
            pl.BlockSpec((1, SC_CONV - 1, D_SC), lambda g: (tile_of(g, 0)[0], 0, 0)),
            ffn_spec,
            down_spec,
        ],
        out_shape=[
            jax.ShapeDtypeStruct((nb, seq, D_MODEL), F32),
            jax.ShapeDtypeStruct((nb, NHEADS, HEAD_DIM, NSTATE), F32),
            jax.ShapeDtypeStruct((nb, SSD_CONV - 1, CONV_DIM), F32),
            jax.ShapeDtypeStruct((nb, SC_CONV - 1, D_SC), F32),
            jax.ShapeDtypeStruct(w_ffn_in.shape, BF16),
            jax.ShapeDtypeStruct(w_down.shape, BF16),
        ],
        scratch_shapes=[
            pltpu.VMEM((tile + CARRY, CONV_DIM), F32),
            pltpu.VMEM((tile + CARRY, D_SC), F32),
            pltpu.VMEM((tile, D_MODEL), BF16),
            pltpu.VMEM((tile, CONV_DIM), F32),
            pltpu.VMEM((tile, DT_PAD), F32),
            pltpu.VMEM((tile, D_SSD), F32),
            pltpu.VMEM((tile, D_SC), F32),
            pltpu.VMEM((NSTATE, D_SSD), F32),
            pltpu.VMEM((tile, D_SSD), F32),
            pltpu.VMEM((tile, D_SC), BF16),
            pltpu.VMEM((tile, D_MIX), BF16),
        ],
        compiler_params=pltpu.CompilerParams(
            dimension_semantics=("arbitrary",), vmem_limit_bytes=VMEM_LIMIT),
        name="mixer_prompt",
    )(x, x, w_ffn_in, w_down, *consts)


STATE_HEADS = 2
STATE_SEQS = 64
STATE_ROWS = STATE_HEADS * HEAD_DIM


def _state_block_update(da_ref, step, st_ref, xdt_t_ref, b_ref, c_ref, nst_ref, yt_ref):
    seq_blocks = da_ref.shape[0] // STATE_SEQS
    seq0 = lax.rem(step, seq_blocks) * STATE_SEQS
    head0 = (step // seq_blocks) * STATE_HEADS
    c_blk = c_ref[0].astype(BF16)
    lane = lax.broadcasted_iota(jnp.int32, (STATE_ROWS, STATE_SEQS), 1)
    ycols = jnp.zeros((STATE_ROWS, STATE_SEQS), F32)
    for i in range(STATE_SEQS):
        upd = xdt_t_ref[0, :, i:i + 1] * b_ref[0, i:i + 1, :]
        parts = []
        for hh in range(STATE_HEADS):
            hnew = st_ref[i, hh] * da_ref[seq0 + i, head0 + hh] + upd[hh * HEAD_DIM:(hh + 1) * HEAD_DIM, :]
            nst_ref[i, hh] = hnew
            parts.append(hnew)
        prod = _dot_nt(jnp.concatenate(parts, axis=0).astype(BF16), c_blk)
        ycols = jnp.where(lane == i, prod, ycols)
    yt_ref[0] = ycols


def _ffn_prompt_kernel(state_steps,
                       da_ref, x1_ref, st_ref, xdt_t_ref, b_ref, c_ref,
                       nw_ref, wffn_ref, fcw_ref, fcb_ref, wdown_ref, nfw_ref,
                       y_ref, fst_ref, nst_ref, yt_ref, gate_ext):
    tile = x1_ref.shape[1]
    t = pl.program_id(1)
    step = pl.program_id(0) * pl.num_programs(1) + t

    @pl.when(step < state_steps)
    def _():
        _state_block_update(da_ref, step, st_ref, xdt_t_ref, b_ref, c_ref, nst_ref, yt_ref)

    @pl.when(t == 0)
    def _():
        gate_ext[0:CARRY, :] = jnp.zeros((CARRY, D_FF), F32)

    x1 = x1_ref[0]
    u = _rms_rows(x1, nw_ref[...]).astype(BF16)
    gate_ext[CARRY:CARRY + tile, :] = _dot(u, wffn_ref[:, 0:D_FF])
    up = _dot(u, wffn_ref[:, D_FF:2 * D_FF])
    g = fcb_ref[...] + gate_ext[CARRY - 2:CARRY - 2 + tile, :] * fcw_ref[0:1, :]
    for k in range(1, FFN_CONV):
        g = g + gate_ext[CARRY - 2 + k:CARRY - 2 + k + tile, :] * fcw_ref[k:k + 1, :]
    tail = gate_ext[tile + CARRY - 2:tile + CARRY, :]
    fst_ref[0] = tail
    gate_ext[CARRY - 2:CARRY, :] = tail
    act = (_silu(g) * up).astype(BF16)
    x2 = x1 + _dot(act, wdown_ref[...])
    y_ref[0] = _rms_rows(x2, nfw_ref[...])


def _ffn_prompt(x1, nw, wffn, fcw, fcb, wdown, nfw, state, da, xdt_t, bc):
    nb, seq, _ = x1.shape
    tile = SEQ_TILE
    nt = seq // tile
    nseq = state.shape[0]
    seq_blocks = nseq // STATE_SEQS
    state_steps = seq_blocks * (NHEADS // STATE_HEADS)
    assert nb * nt >= state_steps
    heads_per_group = NHEADS // NGROUPS

    def block_of(b, t):
        s = jnp.minimum(b * nt + t, state_steps - 1)
        return s % seq_blocks, s // seq_blocks

    def st_map(b, t):
        sb, hb = block_of(b, t)
        return (sb, hb, 0, 0)

    def col_map(b, t):
        sb, hb = block_of(b, t)
        return (sb, hb, 0)

    def b_map(b, t):
        sb, hb = block_of(b, t)
        return (hb * STATE_HEADS // heads_per_group, sb, 0)

    def c_map(b, t):
        sb, hb = block_of(b, t)
        return (NGROUPS + hb * STATE_HEADS // heads_per_group, sb, 0)

    tok_spec = pl.BlockSpec((1, tile, D_MODEL), lambda b, t: (b, t, 0))
    st_spec = pl.BlockSpec((STATE_SEQS, STATE_HEADS, HEAD_DIM, NSTATE), st_map)
    col_spec = pl.BlockSpec((1, STATE_ROWS, STATE_SEQS), col_map)
    return pl.pallas_call(
        functools.partial(_ffn_prompt_kernel, state_steps),
        grid=(nb, nt),
        in_specs=[pl.BlockSpec(memory_space=pltpu.SMEM), tok_spec, st_spec, col_spec,
                  pl.BlockSpec((1, STATE_SEQS, NSTATE), b_map), pl.BlockSpec((1, STATE_SEQS, NSTATE), c_map)]
        + [_const_spec(p.shape) for p in (nw, wffn, fcw, fcb, wdown, nfw)],
        out_specs=[tok_spec, pl.BlockSpec((1, FFN_CONV - 1, D_FF), lambda b, t: (b, 0, 0)), st_spec, col_spec],
        out_shape=[jax.ShapeDtypeStruct((nb, seq, D_MODEL), F32),
                   jax.ShapeDtypeStruct((nb, FFN_CONV - 1, D_FF), F32),
                   jax.ShapeDtypeStruct(state.shape, F32),
                   jax.ShapeDtypeStruct(xdt_t.shape, F32)],
        scratch_shapes=[pltpu.VMEM((tile + CARRY, D_FF), F32)],
        compiler_params=pltpu.CompilerParams(
            dimension_semantics=("arbitrary", "arbitrary"), vmem_limit_bytes=VMEM_LIMIT),
        name="ffn_prompt",
    )(da, x1, state, xdt_t, bc, bc, nw, wffn, fcw, fcb, wdown, nfw)


CAST_ROWS = 512


def _sample_pre_kernel(x_ref, nw_ref, wf32_ref, cw_ref, cb_ref, dt_bias_ref, a_log_ref, d_ref, scw_ref, e_ref,
                       cst_ref, scst_ref,
                       z_ref, xs_ref, xdt_t_ref, da_ref, bc_ref, ysc_ref, ncst_ref, nscst_ref, win_ref,
                       dtb_ref, a_ref, dx_ref):
    pad = jnp.zeros((1, DT_PAD - NHEADS), F32)
    dtb_ref[...] = jnp.concatenate([dt_bias_ref[...], pad], axis=1)
    a_ref[...] = jnp.concatenate([-jnp.exp(a_log_ref[...]), pad], axis=1)
    d_pad = jnp.broadcast_to(jnp.concatenate([d_ref[...], pad], axis=1), (8, DT_PAD))
    dx_ref[...] = _dot_sel_rhs(d_pad, e_ref[...])[0:1, :]
    for r0 in range(0, D_IN_PROJ, CAST_ROWS):
        r1 = min(r0 + CAST_ROWS, D_IN_PROJ)
        win_ref[r0:r1, :] = wf32_ref[r0:r1, :].astype(BF16)
    x = x_ref[...]
    u = _rms_rows(x, nw_ref[...]).astype(BF16)
    z_ref[...] = _dot_nt(u, win_ref[OFF_Z:OFF_Z + D_SSD, :])
    xbc = _dot_nt(u, win_ref[OFF_XBC:OFF_XBC + CONV_DIM, :])
    conv = cb_ref[...] + xbc * cw_ref[SSD_CONV - 1:SSD_CONV, :]
    for k in range(SSD_CONV - 1):
        conv = conv + cst_ref[k] * cw_ref[k:k + 1, :]
    for k in range(SSD_CONV - 2):
        ncst_ref[k] = cst_ref[k + 1]
    ncst_ref[SSD_CONV - 2] = xbc
    act = _silu(conv)
    xs = act[:, 0:D_SSD]
    xs_ref[...] = xs
    for i in range(2 * NGROUPS):
        bc_ref[i] = act[:, D_SSD + i * NSTATE:D_SSD + (i + 1) * NSTATE]
    dt = _softplus(_dot_nt(u, win_ref[OFF_DT:OFF_DT + DT_PAD, :]) + dtb_ref[...])
    xdt_t = (xs * _dot_sel_rhs(dt, e_ref[...])).T
    for sb in range(xdt_t_ref.shape[0]):
        xdt_t_ref[sb] = xdt_t[:, sb * STATE_SEQS:(sb + 1) * STATE_SEQS]
    da_ref[...] = jnp.exp(dt * a_ref[...])[:, 0:NHEADS]
    gc = _dot_nt(u, win_ref[OFF_GC:OFF_GC + D_SC, :])
    hh = _dot_nt(u, win_ref[OFF_H:OFF_H + D_SC, :])
    gch = gc * hh
    sc = gch * scw_ref[SC_CONV - 1:SC_CONV, :]
    for k in range(SC_CONV - 1):
        sc = sc + scst_ref[:, k, :] * scw_ref[k:k + 1, :]
    for k in range(SC_CONV - 2):
        nscst_ref[:, k, :] = scst_ref[:, k + 1, :]
    nscst_ref[:, SC_CONV - 2, :] = gch
    gb = _dot_nt(u, win_ref[OFF_GB:OFF_GB + D_SC, :])
    ysc_ref[...] = gb * sc


def _sample_pre(x, nw, w_in_t, cw, cb, dt_bias, a_log, d, scw, expand, cst, scst):
    nb = x.shape[0]
    outs = [
        jax.ShapeDtypeStruct((nb, D_SSD), F32),
        jax.ShapeDtypeStruct((nb, D_SSD), F32),
        jax.ShapeDtypeStruct((nb // STATE_SEQS, D_SSD, STATE_SEQS), F32),
        jax.ShapeDtypeStruct((nb, NHEADS), F32),
        jax.ShapeDtypeStruct((2 * NGROUPS, nb, NSTATE), F32),
        jax.ShapeDtypeStruct((nb, D_SC), F32),
        jax.ShapeDtypeStruct((SSD_CONV - 1, nb, CONV_DIM), F32),
        jax.ShapeDtypeStruct((nb, SC_CONV - 1, D_SC), F32),
        jax.ShapeDtypeStruct((D_IN_PROJ, D_MODEL), BF16),
        jax.ShapeDtypeStruct((1, DT_PAD), F32),
        jax.ShapeDtypeStruct((1, DT_PAD), F32),
        jax.ShapeDtypeStruct((1, D_SSD), F32),
    ]
    return pl.pallas_call(
        _sample_pre_kernel,
        out_shape=outs,
        compiler_params=pltpu.CompilerParams(vmem_limit_bytes=VMEM_LIMIT),
        name="sample_pre",
    )(x, nw, w_in_t, cw, cb, dt_bias, a_log, d, scw, expand, cst, scst)


def _sample_post_kernel(x_ref, yt_ref, xs_ref, z_ref, ysc_ref, dx_ref, gnw_ref, wout_ref,
                        nw_ref, wffn_ref, fcw_ref, fcb_ref, wdown_ref, nfw_ref, fst_ref,
                        out_ref, nfst_ref):
    y_ssd = jnp.concatenate([yt_ref[sb] for sb in range(yt_ref.shape[0])], axis=1).T
    y = (y_ssd + dx_ref[...] * xs_ref[...]) * _silu(z_ref[...])
    gw = D_SSD // NGROUPS
    parts = [_rms_rows(y[:, g * gw:(g + 1) * gw], gnw_ref[:, g * gw:(g + 1) * gw]) for g in range(NGROUPS)]
    ycat = jnp.concatenate(parts + [ysc_ref[...]], axis=1).astype(BF16)
    x1 = x_ref[...] + _dot(ycat, wout_ref[...])
    u = _rms_rows(x1, nw_ref[...]).astype(BF16)
    gate = _dot(u, wffn_ref[:, 0:D_FF])
    up = _dot(u, wffn_ref[:, D_FF:2 * D_FF])
    g = fcb_ref[...] + gate * fcw_ref[FFN_CONV - 1:FFN_CONV, :]
    for k in range(FFN_CONV - 1):
        g = g + fst_ref[:, k, :] * fcw_ref[k:k + 1, :]
    for k in range(FFN_CONV - 2):
        nfst_ref[:, k, :] = fst_ref[:, k + 1, :]
    nfst_ref[:, FFN_CONV - 2, :] = gate
    act = (_silu(g) * up).astype(BF16)
    x2 = x1 + _dot(act, wdown_ref[...])
    out_ref[...] = _rms_rows(x2, nfw_ref[...])


def _sample_post(x, y, xs, z, ysc, dx, gnw, wout, nw, wffn, fcw, fcb, wdown, nfw, fst):
    nb = x.shape[0]
    return pl.pallas_call(
        _sample_post_kernel,
        out_shape=[jax.ShapeDtypeStruct((nb, D_MODEL), F32),
                   jax.ShapeDtypeStruct((nb, FFN_CONV - 1, D_FF), F32)],
        compiler_params=pltpu.CompilerParams(vmem_limit_bytes=VMEM_LIMIT),
        name="sample_post",
    )(x, y, xs, z, ysc, dx, gnw, wout, nw, wffn, fcw, fcb, wdown, nfw, fst)


def kernel(x_prompt, x_sample, state_ssm, state_ssd_conv, state_short_conv, state_ffn_conv,
           norm_mix_w, w_in, ssd_conv_w, ssd_conv_b, ssd_dt_bias, ssd_a_log, ssd_d, ssd_norm_w,
           sc_conv_w, w_out, norm_ffn_w, w_ffn_in, ffn_conv_w, ffn_conv_b, w_down, norm_final_w):
    depth = w_in.shape[0]
    assert depth == 1
    w_in_t = jnp.swapaxes(w_in[0], 0, 1)
    nw = norm_mix_w[0].reshape(1, D_MODEL)
    nw2 = norm_ffn_w[0].reshape(1, D_MODEL)
    nfw = norm_final_w.reshape(1, D_MODEL)
    cw = ssd_conv_w[0]
    cb = ssd_conv_b[0].reshape(1, CONV_DIM)
    gnw = ssd_norm_w[0].reshape(1, D_SSD)
    scw = sc_conv_w[0]
    fcw = ffn_conv_w[0]
    fcb = ffn_conv_b[0].reshape(1, D_FF)
    expand = jnp.asarray(np.arange(DT_PAD)[:, None] == np.arange(D_SSD)[None, :] // HEAD_DIM, dtype=BF16)

    nbs = x_sample.shape[0]
    xs_in = x_sample.reshape(nbs, D_MODEL)
    cst = jnp.swapaxes(state_ssd_conv[0], 0, 1)
    scst, fst = state_short_conv[0], state_ffn_conv[0]
    z, xs, xdt_t, da, bc, ysc, ncst, nscst, win, dtb, a_pad, dx = _sample_pre(
        xs_in, nw, w_in_t, cw, cb, ssd_dt_bias, ssd_a_log, ssd_d, scw, expand, cst, scst)

    wout = w_out[0].astype(BF16)
    x1, p_ssm, p_cst, p_scst, wffn, wdown = _mixer_prompt(
        x_prompt, nw, win, cw, cb, dtb, a_pad, dx, gnw, scw, wout, expand, w_ffn_in[0], w_down[0])
    y_prompt, p_fst, s_ssm, y_t = _ffn_prompt(x1, nw2, wffn, fcw, fcb, wdown, nfw, state_ssm[0], da, xdt_t, bc)
    y_sample, nfst = _sample_post(xs_in, y_t, xs, z, ysc, dx, gnw, wout, nw2, wffn, fcw, fcb, wdown, nfw, fst)

    return (y_prompt, y_sample.reshape(nbs, 1, D_MODEL),
            p_ssm[None], p_cst[None], p_scst[None], p_fst[None],
            s_ssm[None], jnp.swapaxes(ncst, 0, 1)[None], nscst[None], nfst[None])
```

```python
import functools

import jax
import jax.numpy as jnp
import numpy as np
from jax import lax
from jax.experimental import pallas as pl
from jax.experimental.pallas import tpu as pltpu

D_MODEL = 1024
D_SSD = 1024
D_SC = 1024
NHEADS = 16
HEAD_DIM = 64
NGROUPS = 2
NSTATE = 128
SSD_CONV = 4
CONV_DIM = D_SSD + 2 * NGROUPS * NSTATE
SC_CONV = 3
D_FF = 2816
FFN_CONV = 3
EPS = 1e-5
D_MIX = D_SSD + D_SC
D_IN_PROJ = D_SSD + CONV_DIM + NHEADS + 3 * D_SC

LANES = 128
MXU_COLS = 256
OFF_Z = 0
OFF_XBC = OFF_Z + D_SSD
OFF_DT = OFF_XBC + CONV_DIM
OFF_GB = OFF_DT + NHEADS
OFF_GC = OFF_GB + D_SC
OFF_H = OFF_GC + D_SC
DT_PAD = LANES

CHUNK = 128
CARRY = 8
SEQ_TILE = 512
VMEM_LIMIT = 56 * 1024 * 1024

F32 = jnp.float32
BF16 = jnp.bfloat16


def _dot(a, b):
    return jnp.dot(a, b, preferred_element_type=F32)


def _dot_nt(a, b_t):
    return lax.dot_general(a, b_t, (((1,), (1,)), ((), ())), preferred_element_type=F32)


def _split3(v):
    hi = v.astype(BF16)
    r = v - hi.astype(F32)
    mid = r.astype(BF16)
    lo = (r - mid.astype(F32)).astype(BF16)
    return hi, mid, lo


def _dot_sel_rhs(v, sel):
    hi, mid, lo = _split3(v)
    return (_dot(hi, sel) + _dot(mid, sel)) + _dot(lo, sel)


def _dot_sel_lhs(sel, v):
    hi, mid, lo = _split3(v)
    return (_dot(sel, hi) + _dot(sel, mid)) + _dot(sel, lo)


def _silu(v):
    h = 0.5 * v
    return h + h * jnp.tanh(h)


def _softplus(v):
    return jnp.maximum(v, 0.0) + jnp.log1p(jnp.exp(-jnp.abs(v)))


def _rms_rows(x, w):
    ms = jnp.mean(x * x, axis=-1, keepdims=True)
    return x * lax.rsqrt(ms + EPS) * w


def _mixer_prompt_kernel(tiles_per_seq,
                         x_ref, xres_ref, wffn_f32_ref, wdown_f32_ref,
                         nw_ref, win_ref, cw_ref, cb_ref, dtb_ref, a_ref, dx_ref,
                         gnw_ref, scw_ref, wout_ref, e_ref,
                         x1_ref, ssm_ref, cst_ref, scst_ref, wffn_bf_ref, wdown_bf_ref,
                         xbc_ext, sc_ext, u_ref, act_ref, dt_ref, z_ref, gb_ref, s_ref, y_ref, ysc_ref, ycat_ref):
    tile = x_ref.shape[1]
    g = pl.program_id(0)
    last = pl.num_programs(0) - 1
    t = lax.rem(g, tiles_per_seq)

    @pl.when(g < CAST_FFN_STEPS)
    def _():
        wffn_bf_ref[...] = wffn_f32_ref[...].astype(BF16)

    @pl.when(g < CAST_DOWN_STEPS)
    def _():
        wdown_bf_ref[...] = wdown_f32_ref[...].astype(BF16)

    @pl.when(g == 0)
    def _():
        ycat_ref[...] = jnp.zeros_like(ycat_ref)

    @pl.when(t == 0)
    def _():
        xbc_ext[0:CARRY, :] = jnp.zeros((CARRY, CONV_DIM), F32)
        sc_ext[0:CARRY, :] = jnp.zeros((CARRY, D_SC), F32)
        s_ref[...] = jnp.zeros_like(s_ref)

    def out_block(off):
        x1_ref[0, :, off:off + MXU_COLS] = xres_ref[0, :, off:off + MXU_COLS] + _dot(
            ycat_ref[...], wout_ref[:, off:off + MXU_COLS])

    @pl.when(g == last)
    def _():
        for off in range(0, D_MODEL, MXU_COLS):
            out_block(off)

    @pl.when(g < last)
    def _():
        _mixer_tile(tile, t == tiles_per_seq - 1, out_block,
                    x_ref, nw_ref, win_ref, cw_ref, cb_ref, dtb_ref, a_ref, dx_ref, gnw_ref, scw_ref, e_ref,
                    ssm_ref, cst_ref, scst_ref,
                    xbc_ext, sc_ext, u_ref, act_ref, dt_ref, z_ref, gb_ref, s_ref, y_ref, ysc_ref, ycat_ref)


def _mixer_tile(tile, ends_sequence, out_block,
                x_ref, nw_ref, win_ref, cw_ref, cb_ref, dtb_ref, a_ref, dx_ref, gnw_ref, scw_ref, e_ref,
                ssm_ref, cst_ref, scst_ref,
                xbc_ext, sc_ext, u_ref, act_ref, dt_ref, z_ref, gb_ref, s_ref, y_ref, ysc_ref, ycat_ref):
    u_ref[...] = _rms_rows(x_ref[0], nw_ref[...]).astype(BF16)

    def project(dst, row0, base, off, width=MXU_COLS):
        dst[row0:row0 + tile, off:off + width] = _dot_nt(u_ref[...], win_ref[base + off:base + off + width, :])

    def conv_block(off):
        conv = cb_ref[:, off:off + LANES] + (
            xbc_ext[CARRY - 3:CARRY - 3 + tile, off:off + LANES] * cw_ref[0:1, off:off + LANES])
        for k in range(1, SSD_CONV):
            conv = conv + (xbc_ext[CARRY - 3 + k:CARRY - 3 + k + tile, off:off + LANES]
                           * cw_ref[k:k + 1, off:off + LANES])
        act_ref[:, off:off + LANES] = _silu(conv)

    def sc_block(off):
        sc = sc_ext[CARRY - 2:CARRY - 2 + tile, off:off + LANES] * scw_ref[0, 0:1, off:off + LANES]
        for k in range(1, SC_CONV):
            sc = sc + (sc_ext[CARRY - 2 + k:CARRY - 2 + k + tile, off:off + LANES]
                       * scw_ref[0, k:k + 1, off:off + LANES])
        ysc_ref[:, off:off + LANES] = (gb_ref[:, off:off + LANES] * sc).astype(BF16)

    for off in range(0, CONV_DIM, MXU_COLS):
        project(xbc_ext, CARRY, OFF_XBC, off)
        for sub in range(off, off + MXU_COLS, LANES):
            conv_block(sub)
    tail = xbc_ext[tile + CARRY - 3:tile + CARRY, :]
    cst_ref[0] = tail
    xbc_ext[CARRY - 3:CARRY, :] = tail
    project(dt_ref, 0, OFF_DT, 0, DT_PAD)
    for off in range(0, D_SC, MXU_COLS):
        u = u_ref[...]
        sc_ext[CARRY:CARRY + tile, off:off + MXU_COLS] = (
            _dot_nt(u, win_ref[OFF_GC + off:OFF_GC + off + MXU_COLS, :])
            * _dot_nt(u, win_ref[OFF_H + off:OFF_H + off + MXU_COLS, :]))

    fillers = []
    for k in range(D_SC // MXU_COLS):
        fillers.append(lambda k=k: out_block(k * MXU_COLS))
        fillers.append(lambda k=k: project(gb_ref, 0, OFF_GB, k * MXU_COLS))
        fillers.append(lambda k=k: sc_block(k * MXU_COLS))
        fillers.append(lambda k=k: sc_block(k * MXU_COLS + LANES))
        fillers.append(lambda k=k: project(z_ref, 0, OFF_Z, k * MXU_COLS))
    n_slots = (tile // CHUNK) * (NHEADS // 2)
    slot_of_filler = [(i * n_slots) // len(fillers) for i in range(len(fillers))]

    row_i = lax.broadcasted_iota(jnp.int32, (CHUNK, CHUNK), 0)
    col_i = lax.broadcasted_iota(jnp.int32, (CHUNK, CHUNK), 1)
    causal = row_i >= col_i
    tri = causal.astype(BF16)
    lane_i = lax.broadcasted_iota(jnp.int32, (1, LANES), 1)
    keeps = ((lane_i < HEAD_DIM).astype(BF16), (lane_i >= HEAD_DIM).astype(BF16))
    neg_inf = jnp.float32(-jnp.inf)
    pairs_per_group = NHEADS // NGROUPS // 2

    for c in range(tile // CHUNK):
        r0 = c * CHUNK
        dt = _softplus(dt_ref[r0:r0 + CHUNK, :] + dtb_ref[...])
        acum = _dot_sel_lhs(tri, dt * a_ref[...])
        acum_t = acum.T
        dt_t = dt.T
        ea = jnp.exp(acum)
        w_t = jnp.exp(acum_t[:, CHUNK - 1:CHUNK] - acum_t) * dt_t
        cd = jnp.broadcast_to(ea[CHUNK - 1:CHUNK, :], (8, LANES))
        cdx = _dot_sel_rhs(cd, e_ref[...])[0:1, :]
        for grp in range(NGROUPS):
            bg = act_ref[r0:r0 + CHUNK, D_SSD + grp * NSTATE:D_SSD + (grp + 1) * NSTATE]
            cg = act_ref[r0:r0 + CHUNK, D_SSD + (NGROUPS + grp) * NSTATE:D_SSD + (NGROUPS + grp + 1) * NSTATE]
            cb = lax.dot_general(cg.astype(BF16), bg.astype(BF16), (((1,), (1,)), ((), ())),
                                 preferred_element_type=F32)
            bg_t = bg.T
            for jp in range(pairs_per_group):
                j = grp * pairs_per_group + jp
                xp = act_ref[r0:r0 + CHUNK, j * LANES:(j + 1) * LANES].astype(BF16)
                sp = s_ref[:, j * LANES:(j + 1) * LANES]
                spb = sp.astype(BF16)
                lhs_parts, rhs_parts, bw_parts, x_parts = [], [], [], []
                for half, keep in enumerate(keeps):
                    h = 2 * j + half
                    xh = xp * keep
                    seg = acum[:, h:h + 1] - acum_t[h:h + 1, :]
                    decay = jnp.exp(jnp.where(causal, seg, neg_inf))
                    m = cb * decay * dt_t[h:h + 1, :]
                    eac = ea[:, h:h + 1] * cg
                    lhs_parts += [m.astype(BF16), eac.astype(BF16)]
                    rhs_parts += [xh, spb * keep]
                    bw_parts.append((bg_t * w_t[h:h + 1, :]).astype(BF16))
                    x_parts.append(xh)
                y_ref[r0:r0 + CHUNK, j * LANES:(j + 1) * LANES] = _dot(
                    jnp.concatenate(lhs_parts, axis=1), jnp.concatenate(rhs_parts, axis=0))
                snew = _dot(jnp.concatenate(bw_parts, axis=1), jnp.concatenate(x_parts, axis=0))
                s_ref[:, j * LANES:(j + 1) * LANES] = cdx[:, j * LANES:(j + 1) * LANES] * sp + snew
                slot = c * (NHEADS // 2) + j
                for i, filler in enumerate(fillers):
                    if slot_of_filler[i] == slot:
                        filler()
    sc_tail = sc_ext[tile + CARRY - 2:tile + CARRY, :]
    scst_ref[0] = sc_tail
    sc_ext[CARRY - 2:CARRY, :] = sc_tail

    gw = D_SSD // NGROUPS
    for grp in range(NGROUPS):
        ssq = jnp.zeros((tile, LANES), F32)
        for off in range(grp * gw, (grp + 1) * gw, LANES):
            yb = ((y_ref[:, off:off + LANES] + dx_ref[:, off:off + LANES] * act_ref[:, off:off + LANES])
                  * _silu(z_ref[:, off:off + LANES]))
            y_ref[:, off:off + LANES] = yb
            ssq = ssq + yb * yb
        scale = lax.rsqrt(jnp.sum(ssq, axis=-1, keepdims=True) * (1.0 / gw) + EPS)
        for off in range(grp * gw, (grp + 1) * gw, LANES):
            ycat_ref[:, off:off + LANES] = (y_ref[:, off:off + LANES] * scale
                                            * gnw_ref[:, off:off + LANES]).astype(BF16)
    ycat_ref[:, D_SSD:D_MIX] = ysc_ref[...]

    @pl.when(ends_sequence)
    def _():
        for j in range(NHEADS // 2):
            ssm_ref[0, 2 * j:2 * j + 2] = s_ref[:, j * LANES:(j + 1) * LANES].T.reshape(2, HEAD_DIM, NSTATE)


def _const_spec(shape):
    nd = len(shape)
    return pl.BlockSpec(shape, lambda *_: (0,) * nd, pipeline_mode=pl.Buffered(1))


CAST_FFN_STEPS = 32
CAST_DOWN_STEPS = 16


def _mixer_prompt(x, nw, win, cw, cb, dtb, a_pad, dx, gnw, scw, wout, expand, w_ffn_in, w_down):
    nb, seq, _ = x.shape
    tile = SEQ_TILE
    nt = seq // tile
    total = nb * nt
    assert total >= CAST_FFN_STEPS and total >= CAST_DOWN_STEPS
    ffn_rows = D_MODEL // CAST_FFN_STEPS
    down_rows = D_FF // CAST_DOWN_STEPS
    ffn_spec = pl.BlockSpec((ffn_rows, 2 * D_FF), lambda g: (jnp.minimum(g, CAST_FFN_STEPS - 1), 0))
    down_spec = pl.BlockSpec((down_rows, D_MODEL), lambda g: (jnp.minimum(g, CAST_DOWN_STEPS - 1), 0))

    def tile_of(g, lag):
        i = jnp.clip(g - lag, 0, total - 1)
        return i // nt, i % nt

    cur_spec = pl.BlockSpec((1, tile, D_MODEL), lambda g: (*tile_of(g, 0), 0))
    prev_spec = pl.BlockSpec((1, tile, D_MODEL), lambda g: (*tile_of(g, 1), 0))
    consts = (nw, win, cw, cb, dtb, a_pad, dx, gnw, scw, wout, expand)
    return pl.pallas_call(
        functools.partial(_mixer_prompt_kernel, nt),
        grid=(total + 1,),
        in_specs=[cur_spec, prev_spec, ffn_spec, down_spec] + [_const_spec(p.shape) for p in consts],
        out_specs=[
            prev_spec,
            pl.BlockSpec((1, NHEADS, HEAD_DIM, NSTATE), lambda g: (tile_of(g, 0)[0], 0, 0, 0)),
            pl.BlockSpec((1, SSD_CONV - 1, CONV_DIM), lambda g: (tile_of(g, 0)[0], 0, 0)),
            pl.BlockSpec((1, SC_CONV - 1, D_SC), lambda g: (tile_of(g, 0)[0], 0, 0)),
            ffn_spec,
            down_spec,
        ],
        out_shape=[
            jax.ShapeDtypeStruct((nb, seq, D_MODEL), F32),
            jax.ShapeDtypeStruct((nb, NHEADS, HEAD_DIM, NSTATE), F32),
            jax.ShapeDtypeStruct((nb, SSD_CONV - 1, CONV_DIM), F32),
            jax.ShapeDtypeStruct((nb, SC_CONV - 1, D_SC), F32),
            jax.ShapeDtypeStruct(w_ffn_in.shape, BF16),
            jax.ShapeDtypeStruct(w_down.shape, BF16),
        ],
        scratch_shapes=[
            pltpu.VMEM((tile + CARRY, CONV_DIM), F32),
            pltpu.VMEM((tile + CARRY, D_SC), F32),
            pltpu.VMEM((tile, D_MODEL), BF16),
            pltpu.VMEM((tile, CONV_DIM), F32),
            pltpu.VMEM((tile, DT_PAD), F32),
            pltpu.VMEM((tile, D_SSD), F32),
            pltpu.VMEM((tile, D_SC), F32),
            pltpu.VMEM((NSTATE, D_SSD), F32),
            pltpu.VMEM((tile, D_SSD), F32),
            pltpu.VMEM((tile, D_SC), BF16),
            pltpu.VMEM((tile, D_MIX), BF16),
        ],
        compiler_params=pltpu.CompilerParams(
            dimension_semantics=("arbitrary",), vmem_limit_bytes=VMEM_LIMIT),
        name="mixer_prompt",
    )(x, x, w_ffn_in, w_down, *consts)


STATE_HEADS = 2
STATE_SEQS = 64
STATE_ROWS = STATE_HEADS * HEAD_DIM


def _state_block_update(da_ref, step, st_ref, xdt_t_ref, b_ref, c_ref, nst_ref, yt_ref):
    seq_blocks = da_ref.shape[0] // STATE_SEQS
    seq0 = lax.rem(step, seq_blocks) * STATE_SEQS
    head0 = (step // seq_blocks) * STATE_HEADS
    c_blk = c_ref[0].astype(BF16)
    lane = lax.broadcasted_iota(jnp.int32, (STATE_ROWS, STATE_SEQS), 1)
    ycols = jnp.zeros((STATE_ROWS, STATE_SEQS), F32)
    for i in range(STATE_SEQS):
        upd = xdt_t_ref[0, :, i:i + 1] * b_ref[0, i:i + 1, :]
        parts = []
        for hh in range(STATE_HEADS):
            hnew = st_ref[i, hh] * da_ref[seq0 + i, head0 + hh] + upd[hh * HEAD_DIM:(hh + 1) * HEAD_DIM, :]
            nst_ref[i, hh] = hnew
            parts.append(hnew)
        prod = _dot_nt(jnp.concatenate(parts, axis=0).astype(BF16), c_blk)
        ycols = jnp.where(lane == i, prod, ycols)
    yt_ref[0] = ycols


def _ffn_prompt_kernel(state_steps,
                       da_ref, x1_ref, st_ref, xdt_t_ref, b_ref, c_ref,
                       nw_ref, wffn_ref, fcw_ref, fcb_ref, wdown_ref, nfw_ref,
                       y_ref, fst_ref, nst_ref, yt_ref, gate_ext):
    tile = x1_ref.shape[1]
    t = pl.program_id(1)
    step = pl.program_id(0) * pl.num_programs(1) + t

    @pl.when(step < state_steps)
    def _():
        _state_block_update(da_ref, step, st_ref, xdt_t_ref, b_ref, c_ref, nst_ref, yt_ref)

    @pl.when(t == 0)
    def _():
        gate_ext[0:CARRY, :] = jnp.zeros((CARRY, D_FF), F32)

    x1 = x1_ref[0]
    u = _rms_rows(x1, nw_ref[...]).astype(BF16)
    gate_ext[CARRY:CARRY + tile, :] = _dot(u, wffn_ref[:, 0:D_FF])
    up = _dot(u, wffn_ref[:, D_FF:2 * D_FF])
    g = fcb_ref[...] + gate_ext[CARRY - 2:CARRY - 2 + tile, :] * fcw_ref[0, 0:1, :]
    for k in range(1, FFN_CONV):
        g = g + gate_ext[CARRY - 2 + k:CARRY - 2 + k + tile, :] * fcw_ref[0, k:k + 1, :]
    tail = gate_ext[tile + CARRY - 2:tile + CARRY, :]
    fst_ref[0] = tail
    gate_ext[CARRY - 2:CARRY, :] = tail
    act = (_silu(g) * up).astype(BF16)
    x2 = x1 + _dot(act, wdown_ref[...])
    y_ref[0] = _rms_rows(x2, nfw_ref[...])


def _ffn_prompt(x1, nw, wffn, fcw, fcb, wdown, nfw, state, da, xdt_t, bc):
    nb, seq, _ = x1.shape
    tile = SEQ_TILE
    nt = seq // tile
    nseq = state.shape[0]
    seq_blocks = nseq // STATE_SEQS
    state_steps = seq_blocks * (NHEADS // STATE_HEADS)
    assert nb * nt >= state_steps
    heads_per_group = NHEADS // NGROUPS

    def block_of(b, t):
        s = jnp.minimum(b * nt + t, state_steps - 1)
        return s % seq_blocks, s // seq_blocks

    def st_map(b, t):
        sb, hb = block_of(b, t)
        return (sb, hb, 0, 0)

    def col_map(b, t):
        sb, hb = block_of(b, t)
        return (sb, hb, 0)

    def b_map(b, t):
        sb, hb = block_of(b, t)
        return (hb * STATE_HEADS // heads_per_group, sb, 0)

    def c_map(b, t):
        sb, hb = block_of(b, t)
        return (NGROUPS + hb * STATE_HEADS // heads_per_group, sb, 0)

    tok_spec = pl.BlockSpec((1, tile, D_MODEL), lambda b, t: (b, t, 0))
    st_spec = pl.BlockSpec((STATE_SEQS, STATE_HEADS, HEAD_DIM, NSTATE), st_map)
    col_spec = pl.BlockSpec((1, STATE_ROWS, STATE_SEQS), col_map)
    return pl.pallas_call(
        functools.partial(_ffn_prompt_kernel, state_steps),
        grid=(nb, nt),
        in_specs=[pl.BlockSpec(memory_space=pltpu.SMEM), tok_spec, st_spec, col_spec,
                  pl.BlockSpec((1, STATE_SEQS, NSTATE), b_map), pl.BlockSpec((1, STATE_SEQS, NSTATE), c_map)]
        + [_const_spec(p.shape) for p in (nw, wffn, fcw, fcb, wdown, nfw)],
        out_specs=[tok_spec, pl.BlockSpec((1, FFN_CONV - 1, D_FF), lambda b, t: (b, 0, 0)), st_spec, col_spec],
        out_shape=[jax.ShapeDtypeStruct((nb, seq, D_MODEL), F32),
                   jax.ShapeDtypeStruct((nb, FFN_CONV - 1, D_FF), F32),
                   jax.ShapeDtypeStruct(state.shape, F32),
                   jax.ShapeDtypeStruct(xdt_t.shape, F32)],
        scratch_shapes=[pltpu.VMEM((tile + CARRY, D_FF), F32)],
        compiler_params=pltpu.CompilerParams(
            dimension_semantics=("arbitrary", "arbitrary"), vmem_limit_bytes=VMEM_LIMIT),
        name="ffn_prompt",
    )(da, x1, state, xdt_t, bc, bc, nw, wffn, fcw, fcb, wdown, nfw)


CAST_ROWS = 512


def _sample_pre_kernel(x_ref, nw_ref, wf32_ref, cw_ref, cb_ref, dt_bias_ref, a_log_ref, d_ref, scw_ref, e_ref,
                       cst_ref, scst_ref,
                       z_ref, xs_ref, xdt_t_ref, da_ref, bc_ref, ysc_ref, ncst_ref, nscst_ref, win_ref,
                       dtb_ref, a_ref, dx_ref):
    pad = jnp.zeros((1, DT_PAD - NHEADS), F32)
    dtb_ref[...] = jnp.concatenate([dt_bias_ref[...], pad], axis=1)
    a_ref[...] = jnp.concatenate([-jnp.exp(a_log_ref[...]), pad], axis=1)
    d_pad = jnp.broadcast_to(jnp.concatenate([d_ref[...], pad], axis=1), (8, DT_PAD))
    dx_ref[...] = _dot_sel_rhs(d_pad, e_ref[...])[0:1, :]
    for r0 in range(0, D_IN_PROJ, CAST_ROWS):
        r1 = min(r0 + CAST_ROWS, D_IN_PROJ)
        win_ref[r0:r1, :] = wf32_ref[r0:r1, :].astype(BF16)
    x = x_ref[0]
    u = _rms_rows(x, nw_ref[...]).astype(BF16)
    z_ref[...] = _dot_nt(u, win_ref[OFF_Z:OFF_Z + D_SSD, :])
    xbc = _dot_nt(u, win_ref[OFF_XBC:OFF_XBC + CONV_DIM, :])
    conv = cb_ref[...] + xbc * cw_ref[SSD_CONV - 1:SSD_CONV, :]
    for k in range(SSD_CONV - 1):
        conv = conv + cst_ref[k] * cw_ref[k:k + 1, :]
    for k in range(SSD_CONV - 2):
        ncst_ref[k] = cst_ref[k + 1]
    ncst_ref[SSD_CONV - 2] = xbc
    act = _silu(conv)
    xs = act[:, 0:D_SSD]
    xs_ref[...] = xs
    for i in range(2 * NGROUPS):
        bc_ref[i] = act[:, D_SSD + i * NSTATE:D_SSD + (i + 1) * NSTATE]
    dt = _softplus(_dot_nt(u, win_ref[OFF_DT:OFF_DT + DT_PAD, :]) + dtb_ref[...])
    xdt_t = (xs * _dot_sel_rhs(dt, e_ref[...])).T
    for sb in range(xdt_t_ref.shape[0]):
        xdt_t_ref[sb] = xdt_t[:, sb * STATE_SEQS:(sb + 1) * STATE_SEQS]
    da_ref[...] = jnp.exp(dt * a_ref[...])[:, 0:NHEADS]
    gc = _dot_nt(u, win_ref[OFF_GC:OFF_GC + D_SC, :])
    hh = _dot_nt(u, win_ref[OFF_H:OFF_H + D_SC, :])
    gch = gc * hh
    sc = gch * scw_ref[0, SC_CONV - 1:SC_CONV, :]
    for k in range(SC_CONV - 1):
        sc = sc + scst_ref[:, k, :] * scw_ref[0, k:k + 1, :]
    for k in range(SC_CONV - 2):
        nscst_ref[:, k, :] = scst_ref[:, k + 1, :]
    nscst_ref[:, SC_CONV - 2, :] = gch
    gb = _dot_nt(u, win_ref[OFF_GB:OFF_GB + D_SC, :])
    ysc_ref[...] = gb * sc


def _sample_pre(x, nw, w_in_t, cw, cb, dt_bias, a_log, d, scw, expand, cst, scst):
    nb = x.shape[1]
    outs = [
        jax.ShapeDtypeStruct((nb, D_SSD), F32),
        jax.ShapeDtypeStruct((nb, D_SSD), F32),
        jax.ShapeDtypeStruct((nb // STATE_SEQS, D_SSD, STATE_SEQS), F32),
        jax.ShapeDtypeStruct((nb, NHEADS), F32),
        jax.ShapeDtypeStruct((2 * NGROUPS, nb, NSTATE), F32),
        jax.ShapeDtypeStruct((nb, D_SC), F32),
        jax.ShapeDtypeStruct((SSD_CONV - 1, nb, CONV_DIM), F32),
        jax.ShapeDtypeStruct((nb, SC_CONV - 1, D_SC), F32),
        jax.ShapeDtypeStruct((D_IN_PROJ, D_MODEL), BF16),
        jax.ShapeDtypeStruct((1, DT_PAD), F32),
        jax.ShapeDtypeStruct((1, DT_PAD), F32),
        jax.ShapeDtypeStruct((1, D_SSD), F32),
    ]
    return pl.pallas_call(
        _sample_pre_kernel,
        out_shape=outs,
        compiler_params=pltpu.CompilerParams(vmem_limit_bytes=VMEM_LIMIT),
        name="sample_pre",
    )(x, nw, w_in_t, cw, cb, dt_bias, a_log, d, scw, expand, cst, scst)


def _sample_post_kernel(x_ref, yt_ref, xs_ref, z_ref, ysc_ref, dx_ref, gnw_ref, wout_ref,
                        nw_ref, wffn_ref, fcw_ref, fcb_ref, wdown_ref, nfw_ref, fst_ref,
                        out_ref, nfst_ref):
    y_ssd = jnp.concatenate([yt_ref[sb] for sb in range(yt_ref.shape[0])], axis=1).T
    y = (y_ssd + dx_ref[...] * xs_ref[...]) * _silu(z_ref[...])
    gw = D_SSD // NGROUPS
    parts = [_rms_rows(y[:, g * gw:(g + 1) * gw], gnw_ref[:, g * gw:(g + 1) * gw]) for g in range(NGROUPS)]
    ycat = jnp.concatenate(parts + [ysc_ref[...]], axis=1).astype(BF16)
    x1 = x_ref[0] + _dot(ycat, wout_ref[...])
    u = _rms_rows(x1, nw_ref[...]).astype(BF16)
    gate = _dot(u, wffn_ref[:, 0:D_FF])
    up = _dot(u, wffn_ref[:, D_FF:2 * D_FF])
    g = fcb_ref[...] + gate * fcw_ref[0, FFN_CONV - 1:FFN_CONV, :]
    for k in range(FFN_CONV - 1):
        g = g + fst_ref[:, k, :] * fcw_ref[0, k:k + 1, :]
    for k in range(FFN_CONV - 2):
        nfst_ref[:, k, :] = fst_ref[:, k + 1, :]
    nfst_ref[:, FFN_CONV - 2, :] = gate
    act = (_silu(g) * up).astype(BF16)
    x2 = x1 + _dot(act, wdown_ref[...])
    out_ref[:, 0, :] = _rms_rows(x2, nfw_ref[...])


def _sample_post(x, y, xs, z, ysc, dx, gnw, wout, nw, wffn, fcw, fcb, wdown, nfw, fst):
    nb = x.shape[1]
    return pl.pallas_call(
        _sample_post_kernel,
        out_shape=[jax.ShapeDtypeStruct((nb, 1, D_MODEL), F32),
                   jax.ShapeDtypeStruct((nb, FFN_CONV - 1, D_FF), F32)],
        compiler_params=pltpu.CompilerParams(vmem_limit_bytes=VMEM_LIMIT),
        name="sample_post",
    )(x, y, xs, z, ysc, dx, gnw, wout, nw, wffn, fcw, fcb, wdown, nfw, fst)


def kernel(x_prompt, x_sample, state_ssm, state_ssd_conv, state_short_conv, state_ffn_conv,
           norm_mix_w, w_in, ssd_conv_w, ssd_conv_b, ssd_dt_bias, ssd_a_log, ssd_d, ssd_norm_w,
           sc_conv_w, w_out, norm_ffn_w, w_ffn_in, ffn_conv_w, ffn_conv_b, w_down, norm_final_w):
    depth = w_in.shape[0]
    assert depth == 1
    w_in_t = jnp.swapaxes(w_in[0], 0, 1)
    nw = norm_mix_w[0].reshape(1, D_MODEL)
    nw2 = norm_ffn_w[0].reshape(1, D_MODEL)
    nfw = norm_final_w.reshape(1, D_MODEL)
    cw = ssd_conv_w[0]
    cb = ssd_conv_b[0].reshape(1, CONV_DIM)
    gnw = ssd_norm_w[0].reshape(1, D_SSD)
    scw = sc_conv_w
    fcw = ffn_conv_w
    fcb = ffn_conv_b[0].reshape(1, D_FF)
    expand = jnp.asarray(np.arange(DT_PAD)[:, None] == np.arange(D_SSD)[None, :] // HEAD_DIM, dtype=BF16)

    xs_in = jnp.swapaxes(x_sample, 0, 1)
    cst = jnp.swapaxes(state_ssd_conv[0], 0, 1)
    scst, fst = state_short_conv[0], state_ffn_conv[0]
    z, xs, xdt_t, da, bc, ysc, ncst, nscst, win, dtb, a_pad, dx = _sample_pre(
        xs_in, nw, w_in_t, cw, cb, ssd_dt_bias, ssd_a_log, ssd_d, scw, expand, cst, scst)

    wout = w_out[0].astype(BF16)
    x1, p_ssm, p_cst, p_scst, wffn, wdown = _mixer_prompt(
        x_prompt, nw, win, cw, cb, dtb, a_pad, dx, gnw, scw, wout, expand, w_ffn_in[0], w_down[0])
    y_prompt, p_fst, s_ssm, y_t = _ffn_prompt(x1, nw2, wffn, fcw, fcb, wdown, nfw, state_ssm[0], da, xdt_t, bc)
    y_sample, nfst = _sample_post(xs_in, y_t, xs, z, ysc, dx, gnw, wout, nw2, wffn, fcw, fcb, wdown, nfw, fst)

    return (y_prompt, y_sample,
            p_ssm[None], p_cst[None], p_scst[None], p_fst[None],
            s_ssm[None], jnp.swapaxes(ncst, 0, 1)[None], nscst[None], nfst[None])
```

```python
import functools

import jax
import jax.numpy as jnp
import numpy as np
from jax import lax
from jax.experimental import pallas as pl
from jax.experimental.pallas import tpu as pltpu

D_MODEL = 1024
D_SSD = 1024
D_SC = 1024
NHEADS = 16
HEAD_DIM = 64
NGROUPS = 2
NSTATE = 128
SSD_CONV = 4
CONV_DIM = D_SSD + 2 * NGROUPS * NSTATE
SC_CONV = 3
D_FF = 2816
FFN_CONV = 3
EPS = 1e-5
D_MIX = D_SSD + D_SC
D_IN_PROJ = D_SSD + CONV_DIM + NHEADS + 3 * D_SC

LANES = 128
MXU_COLS = 256
OFF_Z = 0
OFF_XBC = OFF_Z + D_SSD
OFF_DT = OFF_XBC + CONV_DIM
OFF_GB = OFF_DT + NHEADS
OFF_GC = OFF_GB + D_SC
OFF_H = OFF_GC + D_SC
DT_PAD = LANES

CHUNK = 128
CARRY = 8
SEQ_TILE = 512
VMEM_LIMIT = 56 * 1024 * 1024

F32 = jnp.float32
BF16 = jnp.bfloat16


def _dot(a, b):
    return jnp.dot(a, b, preferred_element_type=F32)


def _dot_nt(a, b_t):
    return lax.dot_general(a, b_t, (((1,), (1,)), ((), ())), preferred_element_type=F32)


def _split3(v):
    hi = v.astype(BF16)
    r = v - hi.astype(F32)
    mid = r.astype(BF16)
    lo = (r - mid.astype(F32)).astype(BF16)
    return hi, mid, lo


def _dot_sel_rhs(v, sel):
    hi, mid, lo = _split3(v)
    return (_dot(hi, sel) + _dot(mid, sel)) + _dot(lo, sel)


def _dot_sel_lhs(sel, v):
    hi, mid, lo = _split3(v)
    return (_dot(sel, hi) + _dot(sel, mid)) + _dot(sel, lo)


def _silu(v):
    h = 0.5 * v
    return h + h * jnp.tanh(h)


def _softplus(v):
    return jnp.maximum(v, 0.0) + jnp.log1p(jnp.exp(-jnp.abs(v)))


def _rms_rows(x, w):
    ms = jnp.mean(x * x, axis=-1, keepdims=True)
    return x * lax.rsqrt(ms + EPS) * w


def _mixer_prompt_kernel(tiles_per_seq,
                         x_ref, xres_ref, wffn_f32_ref, wdown_f32_ref,
                         nw_ref, win_ref, cw_ref, cb_ref, dtb_ref, a_ref, dx_ref,
                         gnw_ref, scw_ref, wout_ref, e_ref,
                         x1_ref, ssm_ref, cst_ref, scst_ref, wffn_bf_ref, wdown_bf_ref,
                         xbc_ext, sc_ext, u_ref, act_ref, dt_ref, z_ref, gb_ref, s_ref, y_ref, ysc_ref, ycat_ref):
    tile = x_ref.shape[1]
    g = pl.program_id(0)
    last = pl.num_programs(0) - 1
    t = lax.rem(g, tiles_per_seq)

    @pl.when(g < CAST_FFN_STEPS)
    def _():
        wffn_bf_ref[...] = wffn_f32_ref[...].astype(BF16)

    @pl.when(g < CAST_DOWN_STEPS)
    def _():
        wdown_bf_ref[...] = wdown_f32_ref[...].astype(BF16)

    @pl.when(g == 0)
    def _():
        ycat_ref[...] = jnp.zeros_like(ycat_ref)

    @pl.when(t == 0)
    def _():
        xbc_ext[0:CARRY, :] = jnp.zeros((CARRY, CONV_DIM), F32)
        sc_ext[0:CARRY, :] = jnp.zeros((CARRY, D_SC), F32)
        s_ref[...] = jnp.zeros_like(s_ref)

    def out_block(off):
        x1_ref[0, :, off:off + MXU_COLS] = xres_ref[0, :, off:off + MXU_COLS] + _dot(
            ycat_ref[...], wout_ref[:, off:off + MXU_COLS])

    @pl.when(g == last)
    def _():
        for off in range(0, D_MODEL, MXU_COLS):
            out_block(off)

    @pl.when(g < last)
    def _():
        _mixer_tile(tile, t == tiles_per_seq - 1, out_block,
                    x_ref, nw_ref, win_ref, cw_ref, cb_ref, dtb_ref, a_ref, dx_ref, gnw_ref, scw_ref, e_ref,
                    ssm_ref, cst_ref, scst_ref,
                    xbc_ext, sc_ext, u_ref, act_ref, dt_ref, z_ref, gb_ref, s_ref, y_ref, ysc_ref, ycat_ref)


def _mixer_tile(tile, ends_sequence, out_block,
                x_ref, nw_ref, win_ref, cw_ref, cb_ref, dtb_ref, a_ref, dx_ref, gnw_ref, scw_ref, e_ref,
                ssm_ref, cst_ref, scst_ref,
                xbc_ext, sc_ext, u_ref, act_ref, dt_ref, z_ref, gb_ref, s_ref, y_ref, ysc_ref, ycat_ref):
    u_ref[...] = _rms_rows(x_ref[0], nw_ref[...]).astype(BF16)

    def project(dst, row0, base, off, width=MXU_COLS):
        dst[row0:row0 + tile, off:off + width] = _dot_nt(u_ref[...], win_ref[base + off:base + off + width, :])

    def conv_block(off):
        conv = cb_ref[:, off:off + LANES] + (
            xbc_ext[CARRY - 3:CARRY - 3 + tile, off:off + LANES] * cw_ref[0:1, off:off + LANES])
        for k in range(1, SSD_CONV):
            conv = conv + (xbc_ext[CARRY - 3 + k:CARRY - 3 + k + tile, off:off + LANES]
                           * cw_ref[k:k + 1, off:off + LANES])
        act_ref[:, off:off + LANES] = _silu(conv)

    def sc_block(off):
        sc = sc_ext[CARRY - 2:CARRY - 2 + tile, off:off + LANES] * scw_ref[0, :, off:off + LANES]
        for k in range(1, SC_CONV):
            sc = sc + (sc_ext[CARRY - 2 + k:CARRY - 2 + k + tile, off:off + LANES]
                       * scw_ref[k, :, off:off + LANES])
        ysc_ref[:, off:off + LANES] = (gb_ref[:, off:off + LANES] * sc).astype(BF16)

    for off in range(0, CONV_DIM, MXU_COLS):
        project(xbc_ext, CARRY, OFF_XBC, off)
        for sub in range(off, off + MXU_COLS, LANES):
            conv_block(sub)
    tail = xbc_ext[tile + CARRY - 3:tile + CARRY, :]
    cst_ref[0] = tail
    xbc_ext[CARRY - 3:CARRY, :] = tail
    project(dt_ref, 0, OFF_DT, 0, DT_PAD)
    for off in range(0, D_SC, MXU_COLS):
        u = u_ref[...]
        sc_ext[CARRY:CARRY + tile, off:off + MXU_COLS] = (
            _dot_nt(u, win_ref[OFF_GC + off:OFF_GC + off + MXU_COLS, :])
            * _dot_nt(u, win_ref[OFF_H + off:OFF_H + off + MXU_COLS, :]))

    fillers = []
    for k in range(D_SC // MXU_COLS):
        fillers.append(lambda k=k: out_block(k * MXU_COLS))
        fillers.append(lambda k=k: project(gb_ref, 0, OFF_GB, k * MXU_COLS))
        fillers.append(lambda k=k: sc_block(k * MXU_COLS))
        fillers.append(lambda k=k: sc_block(k * MXU_COLS + LANES))
        fillers.append(lambda k=k: project(z_ref, 0, OFF_Z, k * MXU_COLS))
    n_slots = (tile // CHUNK) * (NHEADS // 2)
    slot_of_filler = [(i * n_slots) // len(fillers) for i in range(len(fillers))]

    row_i = lax.broadcasted_iota(jnp.int32, (CHUNK, CHUNK), 0)
    col_i = lax.broadcasted_iota(jnp.int32, (CHUNK, CHUNK), 1)
    causal = row_i >= col_i
    tri = causal.astype(BF16)
    lane_i = lax.broadcasted_iota(jnp.int32, (1, LANES), 1)
    keeps = ((lane_i < HEAD_DIM).astype(BF16), (lane_i >= HEAD_DIM).astype(BF16))
    neg_inf = jnp.float32(-jnp.inf)
    pairs_per_group = NHEADS // NGROUPS // 2

    for c in range(tile // CHUNK):
        r0 = c * CHUNK
        dt = _softplus(dt_ref[r0:r0 + CHUNK, :] + dtb_ref[...])
        acum = _dot_sel_lhs(tri, dt * a_ref[...])
        acum_t = acum.T
        dt_t = dt.T
        ea = jnp.exp(acum)
        w_t = jnp.exp(acum_t[:, CHUNK - 1:CHUNK] - acum_t) * dt_t
        cd = jnp.broadcast_to(ea[CHUNK - 1:CHUNK, :], (8, LANES))
        cdx = _dot_sel_rhs(cd, e_ref[...])[0:1, :]
        for grp in range(NGROUPS):
            bg = act_ref[r0:r0 + CHUNK, D_SSD + grp * NSTATE:D_SSD + (grp + 1) * NSTATE]
            cg = act_ref[r0:r0 + CHUNK, D_SSD + (NGROUPS + grp) * NSTATE:D_SSD + (NGROUPS + grp + 1) * NSTATE]
            cb = lax.dot_general(cg.astype(BF16), bg.astype(BF16), (((1,), (1,)), ((), ())),
                                 preferred_element_type=F32)
            bg_t = bg.T
            for jp in range(pairs_per_group):
                j = grp * pairs_per_group + jp
                xp = act_ref[r0:r0 + CHUNK, j * LANES:(j + 1) * LANES].astype(BF16)
                sp = s_ref[:, j * LANES:(j + 1) * LANES]
                spb = sp.astype(BF16)
                lhs_parts, rhs_parts, bw_parts, x_parts = [], [], [], []
                for half, keep in enumerate(keeps):
                    h = 2 * j + half
                    xh = xp * keep
                    seg = acum[:, h:h + 1] - acum_t[h:h + 1, :]
                    decay = jnp.exp(jnp.where(causal, seg, neg_inf))
                    m = cb * decay * dt_t[h:h + 1, :]
                    eac = ea[:, h:h + 1] * cg
                    lhs_parts += [m.astype(BF16), eac.astype(BF16)]
                    rhs_parts += [xh, spb * keep]
                    bw_parts.append((bg_t * w_t[h:h + 1, :]).astype(BF16))
                    x_parts.append(xh)
                y_ref[r0:r0 + CHUNK, j * LANES:(j + 1) * LANES] = _dot(
                    jnp.concatenate(lhs_parts, axis=1), jnp.concatenate(rhs_parts, axis=0))
                snew = _dot(jnp.concatenate(bw_parts, axis=1), jnp.concatenate(x_parts, axis=0))
                s_ref[:, j * LANES:(j + 1) * LANES] = cdx[:, j * LANES:(j + 1) * LANES] * sp + snew
                slot = c * (NHEADS // 2) + j
                for i, filler in enumerate(fillers):
                    if slot_of_filler[i] == slot:
                        filler()
    sc_tail = sc_ext[tile + CARRY - 2:tile + CARRY, :]
    scst_ref[0] = sc_tail
    sc_ext[CARRY - 2:CARRY, :] = sc_tail

    gw = D_SSD // NGROUPS
    for grp in range(NGROUPS):
        ssq = jnp.zeros((tile, LANES), F32)
        for off in range(grp * gw, (grp + 1) * gw, LANES):
            yb = ((y_ref[:, off:off + LANES] + dx_ref[:, off:off + LANES] * act_ref[:, off:off + LANES])
                  * _silu(z_ref[:, off:off + LANES]))
            y_ref[:, off:off + LANES] = yb
            ssq = ssq + yb * yb
        scale = lax.rsqrt(jnp.sum(ssq, axis=-1, keepdims=True) * (1.0 / gw) + EPS)
        for off in range(grp * gw, (grp + 1) * gw, LANES):
            ycat_ref[:, off:off + LANES] = (y_ref[:, off:off + LANES] * scale
                                            * gnw_ref[:, off:off + LANES]).astype(BF16)
    ycat_ref[:, D_SSD:D_MIX] = ysc_ref[...]

    @pl.when(ends_sequence)
    def _():
        for j in range(NHEADS // 2):
            ssm_ref[0, 2 * j:2 * j + 2] = s_ref[:, j * LANES:(j + 1) * LANES].T.reshape(2, HEAD_DIM, NSTATE)


def _const_spec(shape):
    nd = len(shape)
    return pl.BlockSpec(shape, lambda *_: (0,) * nd, pipeline_mode=pl.Buffered(1))


CAST_FFN_STEPS = 32
CAST_DOWN_STEPS = 16


def _mixer_prompt(x, nw, win, cw, cb, dtb, a_pad, dx, gnw, scw, wout, expand, w_ffn_in, w_down):
    nb, seq, _ = x.shape
    tile = SEQ_TILE
    nt = seq // tile
    total = nb * nt
    assert total >= CAST_FFN_STEPS and total >= CAST_DOWN_STEPS
    ffn_rows = D_MODEL // CAST_FFN_STEPS
    down_rows = D_FF // CAST_DOWN_STEPS
    ffn_spec = pl.BlockSpec((ffn_rows, 2 * D_FF), lambda g: (jnp.minimum(g, CAST_FFN_STEPS - 1), 0))
    down_spec = pl.BlockSpec((down_rows, D_MODEL), lambda g: (jnp.minimum(g, CAST_DOWN_STEPS - 1), 0))

    def tile_of(g, lag):
        i = jnp.clip(g - lag, 0, total - 1)
        return i // nt, i % nt

    cur_spec = pl.BlockSpec((1, tile, D_MODEL), lambda g: (*tile_of(g, 0), 0))
    prev_spec = pl.BlockSpec((1, tile, D_MODEL), lambda g: (*tile_of(g, 1), 0))
    consts = (nw, win, cw, cb, dtb, a_pad, dx, gnw, scw, wout, expand)
    return pl.pallas_call(
        functools.partial(_mixer_prompt_kernel, nt),
        grid=(total + 1,),
        in_specs=[cur_spec, prev_spec, ffn_spec, down_spec] + [_const_spec(p.shape) for p in consts],
        out_specs=[
            prev_spec,
            pl.BlockSpec((1, NHEADS, HEAD_DIM, NSTATE), lambda g: (tile_of(g, 0)[0], 0, 0, 0)),
            pl.BlockSpec((1, SSD_CONV - 1, CONV_DIM), lambda g: (tile_of(g, 0)[0], 0, 0)),
            pl.BlockSpec((1, SC_CONV - 1, D_SC), lambda g: (tile_of(g, 0)[0], 0, 0)),
            ffn_spec,
            down_spec,
        ],
        out_shape=[
            jax.ShapeDtypeStruct((nb, seq, D_MODEL), F32),
            jax.ShapeDtypeStruct((nb, NHEADS, HEAD_DIM, NSTATE), F32),
            jax.ShapeDtypeStruct((nb, SSD_CONV - 1, CONV_DIM), F32),
            jax.ShapeDtypeStruct((nb, SC_CONV - 1, D_SC), F32),
            jax.ShapeDtypeStruct(w_ffn_in.shape, BF16),
            jax.ShapeDtypeStruct(w_down.shape, BF16),
        ],
        scratch_shapes=[
            pltpu.VMEM((tile + CARRY, CONV_DIM), F32),
            pltpu.VMEM((tile + CARRY, D_SC), F32),
            pltpu.VMEM((tile, D_MODEL), BF16),
            pltpu.VMEM((tile, CONV_DIM), F32),
            pltpu.VMEM((tile, DT_PAD), F32),
            pltpu.VMEM((tile, D_SSD), F32),
            pltpu.VMEM((tile, D_SC), F32),
            pltpu.VMEM((NSTATE, D_SSD), F32),
            pltpu.VMEM((tile, D_SSD), F32),
            pltpu.VMEM((tile, D_SC), BF16),
            pltpu.VMEM((tile, D_MIX), BF16),
        ],
        compiler_params=pltpu.CompilerParams(
            dimension_semantics=("arbitrary",), vmem_limit_bytes=VMEM_LIMIT),
        name="mixer_prompt",
    )(x, x, w_ffn_in, w_down, *consts)


STATE_HEADS = 2
STATE_SEQS = 64
STATE_ROWS = STATE_HEADS * HEAD_DIM


def _state_block_update(da_ref, step, st_ref, xdt_t_ref, b_ref, c_ref, nst_ref, yt_ref):
    seq_blocks = da_ref.shape[0] // STATE_SEQS
    seq0 = lax.rem(step, seq_blocks) * STATE_SEQS
    head0 = (step // seq_blocks) * STATE_HEADS
    c_blk = c_ref[0].astype(BF16)
    lane = lax.broadcasted_iota(jnp.int32, (STATE_ROWS, STATE_SEQS), 1)
    ycols = jnp.zeros((STATE_ROWS, STATE_SEQS), F32)
    for i in range(STATE_SEQS):
        upd = xdt_t_ref[0, :, i:i + 1] * b_ref[0, i:i + 1, :]
        parts = []
        for hh in range(STATE_HEADS):
            hnew = st_ref[i, hh] * da_ref[seq0 + i, head0 + hh] + upd[hh * HEAD_DIM:(hh + 1) * HEAD_DIM, :]
            nst_ref[i, hh] = hnew
            parts.append(hnew)
        prod = _dot_nt(jnp.concatenate(parts, axis=0).astype(BF16), c_blk)
        ycols = jnp.where(lane == i, prod, ycols)
    yt_ref[0] = ycols


def _ffn_prompt_kernel(state_steps,
                       da_ref, x1_ref, st_ref, xdt_t_ref, b_ref, c_ref,
                       nw_ref, wffn_ref, fcw_ref, fcb_ref, wdown_ref, nfw_ref,
                       y_ref, fst_ref, nst_ref, yt_ref, gate_ext):
    tile = x1_ref.shape[1]
    t = pl.program_id(1)
    step = pl.program_id(0) * pl.num_programs(1) + t

    @pl.when(step < state_steps)
    def _():
        _state_block_update(da_ref, step, st_ref, xdt_t_ref, b_ref, c_ref, nst_ref, yt_ref)

    @pl.when(t == 0)
    def _():
        gate_ext[0:CARRY, :] = jnp.zeros((CARRY, D_FF), F32)

    x1 = x1_ref[0]
    u = _rms_rows(x1, nw_ref[...]).astype(BF16)
    gate_ext[CARRY:CARRY + tile, :] = _dot(u, wffn_ref[:, 0:D_FF])
    up = _dot(u, wffn_ref[:, D_FF:2 * D_FF])
    g = fcb_ref[...] + gate_ext[CARRY - 2:CARRY - 2 + tile, :] * fcw_ref[0]
    for k in range(1, FFN_CONV):
        g = g + gate_ext[CARRY - 2 + k:CARRY - 2 + k + tile, :] * fcw_ref[k]
    tail = gate_ext[tile + CARRY - 2:tile + CARRY, :]
    fst_ref[0] = tail
    gate_ext[CARRY - 2:CARRY, :] = tail
    act = (_silu(g) * up).astype(BF16)
    x2 = x1 + _dot(act, wdown_ref[...])
    y_ref[0] = _rms_rows(x2, nfw_ref[...])


def _ffn_prompt(x1, nw, wffn, fcw, fcb, wdown, nfw, state, da, xdt_t, bc):
    nb, seq, _ = x1.shape
    tile = SEQ_TILE
    nt = seq // tile
    nseq = state.shape[0]
    seq_blocks = nseq // STATE_SEQS
    state_steps = seq_blocks * (NHEADS // STATE_HEADS)
    assert nb * nt >= state_steps
    heads_per_group = NHEADS // NGROUPS

    def block_of(b, t):
        s = jnp.minimum(b * nt + t, state_steps - 1)
        return s % seq_blocks, s // seq_blocks

    def st_map(b, t):
        sb, hb = block_of(b, t)
        return (sb, hb, 0, 0)

    def col_map(b, t):
        sb, hb = block_of(b, t)
        return (sb, hb, 0)

    def b_map(b, t):
        sb, hb = block_of(b, t)
        return (hb * STATE_HEADS // heads_per_group, sb, 0)

    def c_map(b, t):
        sb, hb = block_of(b, t)
        return (NGROUPS + hb * STATE_HEADS // heads_per_group, sb, 0)

    tok_spec = pl.BlockSpec((1, tile, D_MODEL), lambda b, t: (b, t, 0))
    st_spec = pl.BlockSpec((STATE_SEQS, STATE_HEADS, HEAD_DIM, NSTATE), st_map)
    col_spec = pl.BlockSpec((1, STATE_ROWS, STATE_SEQS), col_map)
    return pl.pallas_call(
        functools.partial(_ffn_prompt_kernel, state_steps),
        grid=(nb, nt),
        in_specs=[pl.BlockSpec(memory_space=pltpu.SMEM), tok_spec, st_spec, col_spec,
                  pl.BlockSpec((1, STATE_SEQS, NSTATE), b_map), pl.BlockSpec((1, STATE_SEQS, NSTATE), c_map)]
        + [_const_spec(p.shape) for p in (nw, wffn, fcw, fcb, wdown, nfw)],
        out_specs=[tok_spec, pl.BlockSpec((1, FFN_CONV - 1, D_FF), lambda b, t: (b, 0, 0)), st_spec, col_spec],
        out_shape=[jax.ShapeDtypeStruct((nb, seq, D_MODEL), F32),
                   jax.ShapeDtypeStruct((nb, FFN_CONV - 1, D_FF), F32),
                   jax.ShapeDtypeStruct(state.shape, F32),
                   jax.ShapeDtypeStruct(xdt_t.shape, F32)],
        scratch_shapes=[pltpu.VMEM((tile + CARRY, D_FF), F32)],
        compiler_params=pltpu.CompilerParams(
            dimension_semantics=("arbitrary", "arbitrary"), vmem_limit_bytes=VMEM_LIMIT),
        name="ffn_prompt",
    )(da, x1, state, xdt_t, bc, bc, nw, wffn, fcw, fcb, wdown, nfw)


CAST_ROWS = 512


def _sample_pre_kernel(x_ref, nw_ref, wf32_ref, cw_ref, cb_ref, dt_bias_ref, a_log_ref, d_ref, scw_ref, e_ref,
                       cst_ref, scst_ref,
                       z_ref, xs_ref, xdt_t_ref, da_ref, bc_ref, ysc_ref, ncst_ref, nscst_ref, win_ref,
                       dtb_ref, a_ref, dx_ref):
    pad = jnp.zeros((1, DT_PAD - NHEADS), F32)
    dtb_ref[...] = jnp.concatenate([dt_bias_ref[...], pad], axis=1)
    a_ref[...] = jnp.concatenate([-jnp.exp(a_log_ref[...]), pad], axis=1)
    d_pad = jnp.broadcast_to(jnp.concatenate([d_ref[...], pad], axis=1), (8, DT_PAD))
    dx_ref[...] = _dot_sel_rhs(d_pad, e_ref[...])[0:1, :]
    for r0 in range(0, D_IN_PROJ, CAST_ROWS):
        r1 = min(r0 + CAST_ROWS, D_IN_PROJ)
        win_ref[r0:r1, :] = wf32_ref[r0:r1, :].astype(BF16)
    x = x_ref[:, 0, :]
    u = _rms_rows(x, nw_ref[...]).astype(BF16)
    z_ref[...] = _dot_nt(u, win_ref[OFF_Z:OFF_Z + D_SSD, :])
    xbc = _dot_nt(u, win_ref[OFF_XBC:OFF_XBC + CONV_DIM, :])
    conv = cb_ref[...] + xbc * cw_ref[SSD_CONV - 1:SSD_CONV, :]
    for k in range(SSD_CONV - 1):
        conv = conv + cst_ref[k] * cw_ref[k:k + 1, :]
    for k in range(SSD_CONV - 2):
        ncst_ref[k] = cst_ref[k + 1]
    ncst_ref[SSD_CONV - 2] = xbc
    act = _silu(conv)
    xs = act[:, 0:D_SSD]
    xs_ref[...] = xs
    for i in range(2 * NGROUPS):
        bc_ref[i] = act[:, D_SSD + i * NSTATE:D_SSD + (i + 1) * NSTATE]
    dt = _softplus(_dot_nt(u, win_ref[OFF_DT:OFF_DT + DT_PAD, :]) + dtb_ref[...])
    xdt_t = (xs * _dot_sel_rhs(dt, e_ref[...])).T
    for sb in range(xdt_t_ref.shape[0]):
        xdt_t_ref[sb] = xdt_t[:, sb * STATE_SEQS:(sb + 1) * STATE_SEQS]
    da_ref[...] = jnp.exp(dt * a_ref[...])[:, 0:NHEADS]
    gc = _dot_nt(u, win_ref[OFF_GC:OFF_GC + D_SC, :])
    hh = _dot_nt(u, win_ref[OFF_H:OFF_H + D_SC, :])
    gch = gc * hh
    sc = gch * scw_ref[SC_CONV - 1]
    for k in range(SC_CONV - 1):
        sc = sc + scst_ref[:, k, :] * scw_ref[k]
    for k in range(SC_CONV - 2):
        nscst_ref[:, k, :] = scst_ref[:, k + 1, :]
    nscst_ref[:, SC_CONV - 2, :] = gch
    gb = _dot_nt(u, win_ref[OFF_GB:OFF_GB + D_SC, :])
    ysc_ref[...] = gb * sc


def _sample_pre(x, nw, w_in_t, cw, cb, dt_bias, a_log, d, scw, expand, cst, scst):
    nb = x.shape[0]
    outs = [
        jax.ShapeDtypeStruct((nb, D_SSD), F32),
        jax.ShapeDtypeStruct((nb, D_SSD), F32),
        jax.ShapeDtypeStruct((nb // STATE_SEQS, D_SSD, STATE_SEQS), F32),
        jax.ShapeDtypeStruct((nb, NHEADS), F32),
        jax.ShapeDtypeStruct((2 * NGROUPS, nb, NSTATE), F32),
        jax.ShapeDtypeStruct((nb, D_SC), F32),
        jax.ShapeDtypeStruct((SSD_CONV - 1, nb, CONV_DIM), F32),
        jax.ShapeDtypeStruct((nb, SC_CONV - 1, D_SC), F32),
        jax.ShapeDtypeStruct((D_IN_PROJ, D_MODEL), BF16),
        jax.ShapeDtypeStruct((1, DT_PAD), F32),
        jax.ShapeDtypeStruct((1, DT_PAD), F32),
        jax.ShapeDtypeStruct((1, D_SSD), F32),
    ]
    return pl.pallas_call(
        _sample_pre_kernel,
        out_shape=outs,
        compiler_params=pltpu.CompilerParams(vmem_limit_bytes=VMEM_LIMIT),
        name="sample_pre",
    )(x, nw, w_in_t, cw, cb, dt_bias, a_log, d, scw, expand, cst, scst)


def _sample_post_kernel(x_ref, yt_ref, xs_ref, z_ref, ysc_ref, dx_ref, gnw_ref, wout_ref,
                        nw_ref, wffn_ref, fcw_ref, fcb_ref, wdown_ref, nfw_ref, fst_ref,
                        out_ref, nfst_ref):
    y_ssd = jnp.concatenate([yt_ref[sb] for sb in range(yt_ref.shape[0])], axis=1).T
    y = (y_ssd + dx_ref[...] * xs_ref[...]) * _silu(z_ref[...])
    gw = D_SSD // NGROUPS
    parts = [_rms_rows(y[:, g * gw:(g + 1) * gw], gnw_ref[:, g * gw:(g + 1) * gw]) for g in range(NGROUPS)]
    ycat = jnp.concatenate(parts + [ysc_ref[...]], axis=1).astype(BF16)
    x1 = x_ref[:, 0, :] + _dot(ycat, wout_ref[...])
    u = _rms_rows(x1, nw_ref[...]).astype(BF16)
    gate = _dot(u, wffn_ref[:, 0:D_FF])
    up = _dot(u, wffn_ref[:, D_FF:2 * D_FF])
    g = fcb_ref[...] + gate * fcw_ref[FFN_CONV - 1]
    for k in range(FFN_CONV - 1):
        g = g + fst_ref[:, k, :] * fcw_ref[k]
    for k in range(FFN_CONV - 2):
        nfst_ref[:, k, :] = fst_ref[:, k + 1, :]
    nfst_ref[:, FFN_CONV - 2, :] = gate
    act = (_silu(g) * up).astype(BF16)
    x2 = x1 + _dot(act, wdown_ref[...])
    out_ref[:, 0, :] = _rms_rows(x2, nfw_ref[...])


def _sample_post(x, y, xs, z, ysc, dx, gnw, wout, nw, wffn, fcw, fcb, wdown, nfw, fst):
    nb = x.shape[0]
    return pl.pallas_call(
        _sample_post_kernel,
        out_shape=[jax.ShapeDtypeStruct((nb, 1, D_MODEL), F32),
                   jax.ShapeDtypeStruct((nb, FFN_CONV - 1, D_FF), F32)],
        compiler_params=pltpu.CompilerParams(vmem_limit_bytes=VMEM_LIMIT),
        name="sample_post",
    )(x, y, xs, z, ysc, dx, gnw, wout, nw, wffn, fcw, fcb, wdown, nfw, fst)


def kernel(x_prompt, x_sample, state_ssm, state_ssd_conv, state_short_conv, state_ffn_conv,
           norm_mix_w, w_in, ssd_conv_w, ssd_conv_b, ssd_dt_bias, ssd_a_log, ssd_d, ssd_norm_w,
           sc_conv_w, w_out, norm_ffn_w, w_ffn_in, ffn_conv_w, ffn_conv_b, w_down, norm_final_w):
    depth = w_in.shape[0]
    assert depth == 1
    w_in_t = jnp.swapaxes(w_in[0], 0, 1)
    nw = norm_mix_w[0].reshape(1, D_MODEL)
    nw2 = norm_ffn_w[0].reshape(1, D_MODEL)
    nfw = norm_final_w.reshape(1, D_MODEL)
    cw = ssd_conv_w[0]
    cb = ssd_conv_b[0].reshape(1, CONV_DIM)
    gnw = ssd_norm_w[0].reshape(1, D_SSD)
    scw = jnp.swapaxes(sc_conv_w, 0, 1)
    fcw = jnp.swapaxes(ffn_conv_w, 0, 1)
    fcb = ffn_conv_b[0].reshape(1, D_FF)
    expand = jnp.asarray(np.arange(DT_PAD)[:, None] == np.arange(D_SSD)[None, :] // HEAD_DIM, dtype=BF16)

    xs_in = x_sample
    cst = jnp.swapaxes(state_ssd_conv[0], 0, 1)
    scst, fst = state_short_conv[0], state_ffn_conv[0]
    z, xs, xdt_t, da, bc, ysc, ncst, nscst, win, dtb, a_pad, dx = _sample_pre(
        xs_in, nw, w_in_t, cw, cb, ssd_dt_bias, ssd_a_log, ssd_d, scw, expand, cst, scst)

    wout = w_out[0].astype(BF16)
    x1, p_ssm, p_cst, p_scst, wffn, wdown = _mixer_prompt(
        x_prompt, nw, win, cw, cb, dtb, a_pad, dx, gnw, scw, wout, expand, w_ffn_in[0], w_down[0])
    y_prompt, p_fst, s_ssm, y_t = _ffn_prompt(x1, nw2, wffn, fcw, fcb, wdown, nfw, state_ssm[0], da, xdt_t, bc)
    y_sample, nfst = _sample_post(xs_in, y_t, xs, z, ysc, dx, gnw, wout, nw2, wffn, fcw, fcb, wdown, nfw, fst)

    return (y_prompt, y_sample,
            p_ssm[None], p_cst[None], p_scst[None], p_fst[None],
            s_ssm[None], jnp.swapaxes(ncst, 0, 1)[None], nscst[None], nfst[None])
```

```python
import functools

import jax
import jax.numpy as jnp
import numpy as np
from jax import lax
from jax.experimental import pallas as pl
from jax.experimental.pallas import tpu as pltpu

D_MODEL = 1024
D_SSD = 1024
D_SC = 1024
NHEADS = 16
HEAD_DIM = 64
NGROUPS = 2
NSTATE = 128
SSD_CONV = 4
CONV_DIM = D_SSD + 2 * NGROUPS * NSTATE
SC_CONV = 3
D_FF = 2816
FFN_CONV = 3
EPS = 1e-5
D_MIX = D_SSD + D_SC
D_IN_PROJ = D_SSD + CONV_DIM + NHEADS + 3 * D_SC

LANES = 128
MXU_COLS = 256
OFF_Z = 0
OFF_XBC = OFF_Z + D_SSD
OFF_DT = OFF_XBC + CONV_DIM
OFF_GB = OFF_DT + NHEADS
OFF_GC = OFF_GB + D_SC
OFF_H = OFF_GC + D_SC
DT_PAD = LANES

CHUNK = 128
CARRY = 8
SEQ_TILE = 512
VMEM_LIMIT = 56 * 1024 * 1024

F32 = jnp.float32
BF16 = jnp.bfloat16


def _dot(a, b):
    return jnp.dot(a, b, preferred_element_type=F32)


def _dot_nt(a, b_t):
    return lax.dot_general(a, b_t, (((1,), (1,)), ((), ())), preferred_element_type=F32)


def _split3(v):
    hi = v.astype(BF16)
    r = v - hi.astype(F32)
    mid = r.astype(BF16)
    lo = (r - mid.astype(F32)).astype(BF16)
    return hi, mid, lo


def _dot_sel_rhs(v, sel):
    hi, mid, lo = _split3(v)
    return (_dot(hi, sel) + _dot(mid, sel)) + _dot(lo, sel)


def _dot_sel_lhs(sel, v):
    hi, mid, lo = _split3(v)
    return (_dot(sel, hi) + _dot(sel, mid)) + _dot(sel, lo)


def _silu(v):
    h = 0.5 * v
    return h + h * jnp.tanh(h)


def _softplus(v):
    return jnp.maximum(v, 0.0) + jnp.log1p(jnp.exp(-jnp.abs(v)))


def _rms_rows(x, w):
    ms = jnp.mean(x * x, axis=-1, keepdims=True)
    return x * lax.rsqrt(ms + EPS) * w


def _mixer_prompt_kernel(tiles_per_seq,
                         x_ref, xres_ref, wffn_f32_ref, wdown_f32_ref,
                         nw_ref, win_ref, cw_ref, cb_ref, dtb_ref, a_ref, dx_ref,
                         gnw_ref, scw_ref, wout_ref, e_ref,
                         x1_ref, ssm_ref, cst_ref, scst_ref, wffn_bf_ref, wdown_bf_ref,
                         xbc_ext, sc_ext, u_ref, act_ref, dt_ref, z_ref, gb_ref, s_ref, y_ref, ysc_ref, ycat_ref):
    tile = x_ref.shape[1]
    g = pl.program_id(0)
    last = pl.num_programs(0) - 1
    t = lax.rem(g, tiles_per_seq)

    @pl.when(g < CAST_FFN_STEPS)
    def _():
        wffn_bf_ref[...] = wffn_f32_ref[...].astype(BF16)

    @pl.when(g < CAST_DOWN_STEPS)
    def _():
        wdown_bf_ref[...] = wdown_f32_ref[...].astype(BF16)

    @pl.when(g == 0)
    def _():
        ycat_ref[...] = jnp.zeros_like(ycat_ref)

    @pl.when(t == 0)
    def _():
        xbc_ext[0:CARRY, :] = jnp.zeros((CARRY, CONV_DIM), F32)
        sc_ext[0:CARRY, :] = jnp.zeros((CARRY, D_SC), F32)
        s_ref[...] = jnp.zeros_like(s_ref)

    def out_block(off):
        x1_ref[0, :, off:off + MXU_COLS] = xres_ref[0, :, off:off + MXU_COLS] + _dot(
            ycat_ref[...], wout_ref[:, off:off + MXU_COLS])

    @pl.when(g == last)
    def _():
        for off in range(0, D_MODEL, MXU_COLS):
            out_block(off)

    @pl.when(g < last)
    def _():
        _mixer_tile(tile, g // tiles_per_seq, t == tiles_per_seq - 1, out_block,
                    x_ref, nw_ref, win_ref, cw_ref, cb_ref, dtb_ref, a_ref, dx_ref, gnw_ref, scw_ref, e_ref,
                    ssm_ref, cst_ref, scst_ref,
                    xbc_ext, sc_ext, u_ref, act_ref, dt_ref, z_ref, gb_ref, s_ref, y_ref, ysc_ref, ycat_ref)


def _mixer_tile(tile, seq, ends_sequence, out_block,
                x_ref, nw_ref, win_ref, cw_ref, cb_ref, dtb_ref, a_ref, dx_ref, gnw_ref, scw_ref, e_ref,
                ssm_ref, cst_ref, scst_ref,
                xbc_ext, sc_ext, u_ref, act_ref, dt_ref, z_ref, gb_ref, s_ref, y_ref, ysc_ref, ycat_ref):
    u_ref[...] = _rms_rows(x_ref[0], nw_ref[...]).astype(BF16)

    def project(dst, row0, base, off, width=MXU_COLS):
        dst[row0:row0 + tile, off:off + width] = _dot_nt(u_ref[...], win_ref[base + off:base + off + width, :])

    def conv_block(off):
        conv = cb_ref[:, off:off + LANES] + (
            xbc_ext[CARRY - 3:CARRY - 3 + tile, off:off + LANES] * cw_ref[0:1, off:off + LANES])
        for k in range(1, SSD_CONV):
            conv = conv + (xbc_ext[CARRY - 3 + k:CARRY - 3 + k + tile, off:off + LANES]
                           * cw_ref[k:k + 1, off:off + LANES])
        act_ref[:, off:off + LANES] = _silu(conv)

    def sc_block(off):
        sc = sc_ext[CARRY - 2:CARRY - 2 + tile, off:off + LANES] * scw_ref[0, :, off:off + LANES]
        for k in range(1, SC_CONV):
            sc = sc + (sc_ext[CARRY - 2 + k:CARRY - 2 + k + tile, off:off + LANES]
                       * scw_ref[k, :, off:off + LANES])
        ysc_ref[:, off:off + LANES] = (gb_ref[:, off:off + LANES] * sc).astype(BF16)

    for off in range(0, CONV_DIM, MXU_COLS):
        project(xbc_ext, CARRY, OFF_XBC, off)
        for sub in range(off, off + MXU_COLS, LANES):
            conv_block(sub)
    tail = xbc_ext[tile + CARRY - 3:tile + CARRY, :]
    for k in range(SSD_CONV - 1):
        cst_ref[k, pl.ds(seq, 1), :] = tail[k:k + 1, :]
    xbc_ext[CARRY - 3:CARRY, :] = tail
    project(dt_ref, 0, OFF_DT, 0, DT_PAD)
    for off in range(0, D_SC, MXU_COLS):
        u = u_ref[...]
        sc_ext[CARRY:CARRY + tile, off:off + MXU_COLS] = (
            _dot_nt(u, win_ref[OFF_GC + off:OFF_GC + off + MXU_COLS, :])
            * _dot_nt(u, win_ref[OFF_H + off:OFF_H + off + MXU_COLS, :]))

    fillers = []
    for k in range(D_SC // MXU_COLS):
        fillers.append(lambda k=k: out_block(k * MXU_COLS))
        fillers.append(lambda k=k: project(gb_ref, 0, OFF_GB, k * MXU_COLS))
        fillers.append(lambda k=k: sc_block(k * MXU_COLS))
        fillers.append(lambda k=k: sc_block(k * MXU_COLS + LANES))
        fillers.append(lambda k=k: project(z_ref, 0, OFF_Z, k * MXU_COLS))
    n_slots = (tile // CHUNK) * (NHEADS // 2)
    slot_of_filler = [(i * n_slots) // len(fillers) for i in range(len(fillers))]

    row_i = lax.broadcasted_iota(jnp.int32, (CHUNK, CHUNK), 0)
    col_i = lax.broadcasted_iota(jnp.int32, (CHUNK, CHUNK), 1)
    causal = row_i >= col_i
    tri = causal.astype(BF16)
    lane_i = lax.broadcasted_iota(jnp.int32, (1, LANES), 1)
    keeps = ((lane_i < HEAD_DIM).astype(BF16), (lane_i >= HEAD_DIM).astype(BF16))
    neg_inf = jnp.float32(-jnp.inf)
    pairs_per_group = NHEADS // NGROUPS // 2

    for c in range(tile // CHUNK):
        r0 = c * CHUNK
        dt = _softplus(dt_ref[r0:r0 + CHUNK, :] + dtb_ref[...])
        acum = _dot_sel_lhs(tri, dt * a_ref[...])
        acum_t = acum.T
        dt_t = dt.T
        ea = jnp.exp(acum)
        w_t = jnp.exp(acum_t[:, CHUNK - 1:CHUNK] - acum_t) * dt_t
        cd = jnp.broadcast_to(ea[CHUNK - 1:CHUNK, :], (8, LANES))
        cdx = _dot_sel_rhs(cd, e_ref[...])[0:1, :]
        for grp in range(NGROUPS):
            bg = act_ref[r0:r0 + CHUNK, D_SSD + grp * NSTATE:D_SSD + (grp + 1) * NSTATE]
            cg = act_ref[r0:r0 + CHUNK, D_SSD + (NGROUPS + grp) * NSTATE:D_SSD + (NGROUPS + grp + 1) * NSTATE]
            cb = lax.dot_general(cg.astype(BF16), bg.astype(BF16), (((1,), (1,)), ((), ())),
                                 preferred_element_type=F32)
            bg_t = bg.T
            for jp in range(pairs_per_group):
                j = grp * pairs_per_group + jp
                xp = act_ref[r0:r0 + CHUNK, j * LANES:(j + 1) * LANES].astype(BF16)
                sp = s_ref[:, j * LANES:(j + 1) * LANES]
                spb = sp.astype(BF16)
                lhs_parts, rhs_parts, bw_parts, x_parts = [], [], [], []
                for half, keep in enumerate(keeps):
                    h = 2 * j + half
                    xh = xp * keep
                    seg = acum[:, h:h + 1] - acum_t[h:h + 1, :]
                    decay = jnp.exp(jnp.where(causal, seg, neg_inf))
                    m = cb * decay * dt_t[h:h + 1, :]
                    eac = ea[:, h:h + 1] * cg
                    lhs_parts += [m.astype(BF16), eac.astype(BF16)]
                    rhs_parts += [xh, spb * keep]
                    bw_parts.append((bg_t * w_t[h:h + 1, :]).astype(BF16))
                    x_parts.append(xh)
                y_ref[r0:r0 + CHUNK, j * LANES:(j + 1) * LANES] = _dot(
                    jnp.concatenate(lhs_parts, axis=1), jnp.concatenate(rhs_parts, axis=0))
                snew = _dot(jnp.concatenate(bw_parts, axis=1), jnp.concatenate(x_parts, axis=0))
                s_ref[:, j * LANES:(j + 1) * LANES] = cdx[:, j * LANES:(j + 1) * LANES] * sp + snew
                slot = c * (NHEADS // 2) + j
                for i, filler in enumerate(fillers):
                    if slot_of_filler[i] == slot:
                        filler()
    sc_tail = sc_ext[tile + CARRY - 2:tile + CARRY, :]
    scst_ref[0] = sc_tail
    sc_ext[CARRY - 2:CARRY, :] = sc_tail

    gw = D_SSD // NGROUPS
    for grp in range(NGROUPS):
        ssq = jnp.zeros((tile, LANES), F32)
        for off in range(grp * gw, (grp + 1) * gw, LANES):
            yb = ((y_ref[:, off:off + LANES] + dx_ref[:, off:off + LANES] * act_ref[:, off:off + LANES])
                  * _silu(z_ref[:, off:off + LANES]))
            y_ref[:, off:off + LANES] = yb
            ssq = ssq + yb * yb
        scale = lax.rsqrt(jnp.sum(ssq, axis=-1, keepdims=True) * (1.0 / gw) + EPS)
        for off in range(grp * gw, (grp + 1) * gw, LANES):
            ycat_ref[:, off:off + LANES] = (y_ref[:, off:off + LANES] * scale
                                            * gnw_ref[:, off:off + LANES]).astype(BF16)
    ycat_ref[:, D_SSD:D_MIX] = ysc_ref[...]

    @pl.when(ends_sequence)
    def _():
        for j in range(NHEADS // 2):
            ssm_ref[0, 2 * j:2 * j + 2] = s_ref[:, j * LANES:(j + 1) * LANES].T.reshape(2, HEAD_DIM, NSTATE)


def _const_spec(shape):
    nd = len(shape)
    return pl.BlockSpec(shape, lambda *_: (0,) * nd, pipeline_mode=pl.Buffered(1))


CAST_FFN_STEPS = 32
CAST_DOWN_STEPS = 16


def _mixer_prompt(x, nw, win, cw, cb, dtb, a_pad, dx, gnw, scw, wout, expand, w_ffn_in, w_down):
    nb, seq, _ = x.shape
    tile = SEQ_TILE
    nt = seq // tile
    total = nb * nt
    assert total >= CAST_FFN_STEPS and total >= CAST_DOWN_STEPS
    ffn_rows = D_MODEL // CAST_FFN_STEPS
    down_rows = D_FF // CAST_DOWN_STEPS
    ffn_spec = pl.BlockSpec((ffn_rows, 2 * D_FF), lambda g: (jnp.minimum(g, CAST_FFN_STEPS - 1), 0))
    down_spec = pl.BlockSpec((down_rows, D_MODEL), lambda g: (jnp.minimum(g, CAST_DOWN_STEPS - 1), 0))

    def tile_of(g, lag):
        i = jnp.clip(g - lag, 0, total - 1)
        return i // nt, i % nt

    cur_spec = pl.BlockSpec((1, tile, D_MODEL), lambda g: (*tile_of(g, 0), 0))
    prev_spec = pl.BlockSpec((1, tile, D_MODEL), lambda g: (*tile_of(g, 1), 0))
    consts = (nw, win, cw, cb, dtb, a_pad, dx, gnw, scw, wout, expand)
    return pl.pallas_call(
        functools.partial(_mixer_prompt_kernel, nt),
        grid=(total + 1,),
        in_specs=[cur_spec, prev_spec, ffn_spec, down_spec] + [_const_spec(p.shape) for p in consts],
        out_specs=[
            prev_spec,
            pl.BlockSpec((1, NHEADS, HEAD_DIM, NSTATE), lambda g: (tile_of(g, 0)[0], 0, 0, 0)),
            pl.BlockSpec((SSD_CONV - 1, nb, CONV_DIM), lambda g: (0, 0, 0)),
            pl.BlockSpec((1, SC_CONV - 1, D_SC), lambda g: (tile_of(g, 0)[0], 0, 0)),
            ffn_spec,
            down_spec,
        ],
        out_shape=[
            jax.ShapeDtypeStruct((nb, seq, D_MODEL), F32),
            jax.ShapeDtypeStruct((nb, NHEADS, HEAD_DIM, NSTATE), F32),
            jax.ShapeDtypeStruct((SSD_CONV - 1, nb, CONV_DIM), F32),
            jax.ShapeDtypeStruct((nb, SC_CONV - 1, D_SC), F32),
            jax.ShapeDtypeStruct(w_ffn_in.shape, BF16),
            jax.ShapeDtypeStruct(w_down.shape, BF16),
        ],
        scratch_shapes=[
            pltpu.VMEM((tile + CARRY, CONV_DIM), F32),
            pltpu.VMEM((tile + CARRY, D_SC), F32),
            pltpu.VMEM((tile, D_MODEL), BF16),
            pltpu.VMEM((tile, CONV_DIM), F32),
            pltpu.VMEM((tile, DT_PAD), F32),
            pltpu.VMEM((tile, D_SSD), F32),
            pltpu.VMEM((tile, D_SC), F32),
            pltpu.VMEM((NSTATE, D_SSD), F32),
            pltpu.VMEM((tile, D_SSD), F32),
            pltpu.VMEM((tile, D_SC), BF16),
            pltpu.VMEM((tile, D_MIX), BF16),
        ],
        compiler_params=pltpu.CompilerParams(
            dimension_semantics=("arbitrary",), vmem_limit_bytes=VMEM_LIMIT),
        name="mixer_prompt",
    )(x, x, w_ffn_in, w_down, *consts)


STATE_HEADS = 2
STATE_SEQS = 64
STATE_ROWS = STATE_HEADS * HEAD_DIM


def _state_block_update(da_ref, step, st_ref, xdt_t_ref, b_ref, c_ref, nst_ref, yt_ref):
    seq_blocks = da_ref.shape[0] // STATE_SEQS
    seq0 = lax.rem(step, seq_blocks) * STATE_SEQS
    head0 = (step // seq_blocks) * STATE_HEADS
    c_blk = c_ref[0].astype(BF16)
    lane = lax.broadcasted_iota(jnp.int32, (STATE_ROWS, STATE_SEQS), 1)
    ycols = jnp.zeros((STATE_ROWS, STATE_SEQS), F32)
    for i in range(STATE_SEQS):
        upd = xdt_t_ref[0, :, i:i + 1] * b_ref[0, i:i + 1, :]
        parts = []
        for hh in range(STATE_HEADS):
            hnew = st_ref[i, hh] * da_ref[seq0 + i, head0 + hh] + upd[hh * HEAD_DIM:(hh + 1) * HEAD_DIM, :]
            nst_ref[i, hh] = hnew
            parts.append(hnew)
        prod = _dot_nt(jnp.concatenate(parts, axis=0).astype(BF16), c_blk)
        ycols = jnp.where(lane == i, prod, ycols)
    yt_ref[0] = ycols


def _ffn_prompt_kernel(state_steps,
                       da_ref, x1_ref, st_ref, xdt_t_ref, b_ref, c_ref,
                       nw_ref, wffn_ref, fcw_ref, fcb_ref, wdown_ref, nfw_ref,
                       y_ref, fst_ref, nst_ref, yt_ref, gate_ext):
    tile = x1_ref.shape[1]
    t = pl.program_id(1)
    step = pl.program_id(0) * pl.num_programs(1) + t

    @pl.when(step < state_steps)
    def _():
        _state_block_update(da_ref, step, st_ref, xdt_t_ref, b_ref, c_ref, nst_ref, yt_ref)

    @pl.when(t == 0)
    def _():
        gate_ext[0:CARRY, :] = jnp.zeros((CARRY, D_FF), F32)

    x1 = x1_ref[0]
    u = _rms_rows(x1, nw_ref[...]).astype(BF16)
    gate_ext[CARRY:CARRY + tile, :] = _dot(u, wffn_ref[:, 0:D_FF])
    up = _dot(u, wffn_ref[:, D_FF:2 * D_FF])
    g = fcb_ref[...] + gate_ext[CARRY - 2:CARRY - 2 + tile, :] * fcw_ref[0]
    for k in range(1, FFN_CONV):
        g = g + gate_ext[CARRY - 2 + k:CARRY - 2 + k + tile, :] * fcw_ref[k]
    tail = gate_ext[tile + CARRY - 2:tile + CARRY, :]
    fst_ref[0] = tail
    gate_ext[CARRY - 2:CARRY, :] = tail
    act = (_silu(g) * up).astype(BF16)
    x2 = x1 + _dot(act, wdown_ref[...])
    y_ref[0] = _rms_rows(x2, nfw_ref[...])


def _ffn_prompt(x1, nw, wffn, fcw, fcb, wdown, nfw, state, da, xdt_t, bc):
    nb, seq, _ = x1.shape
    tile = SEQ_TILE
    nt = seq // tile
    nseq = state.shape[0]
    seq_blocks = nseq // STATE_SEQS
    state_steps = seq_blocks * (NHEADS // STATE_HEADS)
    assert nb * nt >= state_steps
    heads_per_group = NHEADS // NGROUPS

    def block_of(b, t):
        s = jnp.minimum(b * nt + t, state_steps - 1)
        return s % seq_blocks, s // seq_blocks

    def st_map(b, t):
        sb, hb = block_of(b, t)
        return (sb, hb, 0, 0)

    def col_map(b, t):
        sb, hb = block_of(b, t)
        return (sb, hb, 0)

    def b_map(b, t):
        sb, hb = block_of(b, t)
        return (hb * STATE_HEADS // heads_per_group, sb, 0)

    def c_map(b, t):
        sb, hb = block_of(b, t)
        return (NGROUPS + hb * STATE_HEADS // heads_per_group, sb, 0)

    tok_spec = pl.BlockSpec((1, tile, D_MODEL), lambda b, t: (b, t, 0))
    st_spec = pl.BlockSpec((STATE_SEQS, STATE_HEADS, HEAD_DIM, NSTATE), st_map)
    col_spec = pl.BlockSpec((1, STATE_ROWS, STATE_SEQS), col_map)
    return pl.pallas_call(
        functools.partial(_ffn_prompt_kernel, state_steps),
        grid=(nb, nt),
        in_specs=[pl.BlockSpec(memory_space=pltpu.SMEM), tok_spec, st_spec, col_spec,
                  pl.BlockSpec((1, STATE_SEQS, NSTATE), b_map), pl.BlockSpec((1, STATE_SEQS, NSTATE), c_map)]
        + [_const_spec(p.shape) for p in (nw, wffn, fcw, fcb, wdown, nfw)],
        out_specs=[tok_spec, pl.BlockSpec((1, FFN_CONV - 1, D_FF), lambda b, t: (b, 0, 0)), st_spec, col_spec],
        out_shape=[jax.ShapeDtypeStruct((nb, seq, D_MODEL), F32),
                   jax.ShapeDtypeStruct((nb, FFN_CONV - 1, D_FF), F32),
                   jax.ShapeDtypeStruct(state.shape, F32),
                   jax.ShapeDtypeStruct(xdt_t.shape, F32)],
        scratch_shapes=[pltpu.VMEM((tile + CARRY, D_FF), F32)],
        compiler_params=pltpu.CompilerParams(
            dimension_semantics=("arbitrary", "arbitrary"), vmem_limit_bytes=VMEM_LIMIT),
        name="ffn_prompt",
    )(da, x1, state, xdt_t, bc, bc, nw, wffn, fcw, fcb, wdown, nfw)


CAST_ROWS = 512


def _sample_pre_kernel(x_ref, nw_ref, wf32_ref, cw_ref, cb_ref, dt_bias_ref, a_log_ref, d_ref, scw_ref, e_ref,
                       cst_ref, scst_ref,
                       z_ref, xs_ref, xdt_t_ref, da_ref, bc_ref, ysc_ref, ncst_ref, nscst_ref, win_ref,
                       dtb_ref, a_ref, dx_ref):
    pad = jnp.zeros((1, DT_PAD - NHEADS), F32)
    dtb_ref[...] = jnp.concatenate([dt_bias_ref[...], pad], axis=1)
    a_ref[...] = jnp.concatenate([-jnp.exp(a_log_ref[...]), pad], axis=1)
    d_pad = jnp.broadcast_to(jnp.concatenate([d_ref[...], pad], axis=1), (8, DT_PAD))
    dx_ref[...] = _dot_sel_rhs(d_pad, e_ref[...])[0:1, :]
    for r0 in range(0, D_IN_PROJ, CAST_ROWS):
        r1 = min(r0 + CAST_ROWS, D_IN_PROJ)
        win_ref[r0:r1, :] = wf32_ref[r0:r1, :].astype(BF16)
    x = x_ref[:, 0, :]
    u = _rms_rows(x, nw_ref[...]).astype(BF16)
    z_ref[...] = _dot_nt(u, win_ref[OFF_Z:OFF_Z + D_SSD, :])
    xbc = _dot_nt(u, win_ref[OFF_XBC:OFF_XBC + CONV_DIM, :])
    conv = cb_ref[...] + xbc * cw_ref[SSD_CONV - 1:SSD_CONV, :]
    for k in range(SSD_CONV - 1):
        conv = conv + cst_ref[k] * cw_ref[k:k + 1, :]
    for k in range(SSD_CONV - 2):
        ncst_ref[k] = cst_ref[k + 1]
    ncst_ref[SSD_CONV - 2] = xbc
    act = _silu(conv)
    xs = act[:, 0:D_SSD]
    xs_ref[...] = xs
    for i in range(2 * NGROUPS):
        bc_ref[i] = act[:, D_SSD + i * NSTATE:D_SSD + (i + 1) * NSTATE]
    dt = _softplus(_dot_nt(u, win_ref[OFF_DT:OFF_DT + DT_PAD, :]) + dtb_ref[...])
    xdt_t = (xs * _dot_sel_rhs(dt, e_ref[...])).T
    for sb in range(xdt_t_ref.shape[0]):
        xdt_t_ref[sb] = xdt_t[:, sb * STATE_SEQS:(sb + 1) * STATE_SEQS]
    da_ref[...] = jnp.exp(dt * a_ref[...])[:, 0:NHEADS]
    gc = _dot_nt(u, win_ref[OFF_GC:OFF_GC + D_SC, :])
    hh = _dot_nt(u, win_ref[OFF_H:OFF_H + D_SC, :])
    gch = gc * hh
    sc = gch * scw_ref[SC_CONV - 1]
    for k in range(SC_CONV - 1):
        sc = sc + scst_ref[:, k, :] * scw_ref[k]
    for k in range(SC_CONV - 2):
        nscst_ref[:, k, :] = scst_ref[:, k + 1, :]
    nscst_ref[:, SC_CONV - 2, :] = gch
    gb = _dot_nt(u, win_ref[OFF_GB:OFF_GB + D_SC, :])
    ysc_ref[...] = gb * sc


def _sample_pre(x, nw, w_in_t, cw, cb, dt_bias, a_log, d, scw, expand, cst, scst):
    nb = x.shape[0]
    outs = [
        jax.ShapeDtypeStruct((nb, D_SSD), F32),
        jax.ShapeDtypeStruct((nb, D_SSD), F32),
        jax.ShapeDtypeStruct((nb // STATE_SEQS, D_SSD, STATE_SEQS), F32),
        jax.ShapeDtypeStruct((nb, NHEADS), F32),
        jax.ShapeDtypeStruct((2 * NGROUPS, nb, NSTATE), F32),
        jax.ShapeDtypeStruct((nb, D_SC), F32),
        jax.ShapeDtypeStruct((SSD_CONV - 1, nb, CONV_DIM), F32),
        jax.ShapeDtypeStruct((nb, SC_CONV - 1, D_SC), F32),
        jax.ShapeDtypeStruct((D_IN_PROJ, D_MODEL), BF16),
        jax.ShapeDtypeStruct((1, DT_PAD), F32),
        jax.ShapeDtypeStruct((1, DT_PAD), F32),
        jax.ShapeDtypeStruct((1, D_SSD), F32),
    ]
    return pl.pallas_call(
        _sample_pre_kernel,
        out_shape=outs,
        compiler_params=pltpu.CompilerParams(vmem_limit_bytes=VMEM_LIMIT),
        name="sample_pre",
    )(x, nw, w_in_t, cw, cb, dt_bias, a_log, d, scw, expand, cst, scst)


def _sample_post_kernel(x_ref, yt_ref, xs_ref, z_ref, ysc_ref, dx_ref, gnw_ref, wout_ref,
                        nw_ref, wffn_ref, fcw_ref, fcb_ref, wdown_ref, nfw_ref, fst_ref,
                        out_ref, nfst_ref):
    y_ssd = jnp.concatenate([yt_ref[sb] for sb in range(yt_ref.shape[0])], axis=1).T
    y = (y_ssd + dx_ref[...] * xs_ref[...]) * _silu(z_ref[...])
    gw = D_SSD // NGROUPS
    parts = [_rms_rows(y[:, g * gw:(g + 1) * gw], gnw_ref[:, g * gw:(g + 1) * gw]) for g in range(NGROUPS)]
    ycat = jnp.concatenate(parts + [ysc_ref[...]], axis=1).astype(BF16)
    x1 = x_ref[:, 0, :] + _dot(ycat, wout_ref[...])
    u = _rms_rows(x1, nw_ref[...]).astype(BF16)
    gate = _dot(u, wffn_ref[:, 0:D_FF])
    up = _dot(u, wffn_ref[:, D_FF:2 * D_FF])
    g = fcb_ref[...] + gate * fcw_ref[FFN_CONV - 1]
    for k in range(FFN_CONV - 1):
        g = g + fst_ref[:, k, :] * fcw_ref[k]
    for k in range(FFN_CONV - 2):
        nfst_ref[:, k, :] = fst_ref[:, k + 1, :]
    nfst_ref[:, FFN_CONV - 2, :] = gate
    act = (_silu(g) * up).astype(BF16)
    x2 = x1 + _dot(act, wdown_ref[...])
    out_ref[:, 0, :] = _rms_rows(x2, nfw_ref[...])


def _sample_post(x, y, xs, z, ysc, dx, gnw, wout, nw, wffn, fcw, fcb, wdown, nfw, fst):
    nb = x.shape[0]
    return pl.pallas_call(
        _sample_post_kernel,
        out_shape=[jax.ShapeDtypeStruct((nb, 1, D_MODEL), F32),
                   jax.ShapeDtypeStruct((nb, FFN_CONV - 1, D_FF), F32)],
        compiler_params=pltpu.CompilerParams(vmem_limit_bytes=VMEM_LIMIT),
        name="sample_post",
    )(x, y, xs, z, ysc, dx, gnw, wout, nw, wffn, fcw, fcb, wdown, nfw, fst)


def kernel(x_prompt, x_sample, state_ssm, state_ssd_conv, state_short_conv, state_ffn_conv,
           norm_mix_w, w_in, ssd_conv_w, ssd_conv_b, ssd_dt_bias, ssd_a_log, ssd_d, ssd_norm_w,
           sc_conv_w, w_out, norm_ffn_w, w_ffn_in, ffn_conv_w, ffn_conv_b, w_down, norm_final_w):
    depth = w_in.shape[0]
    assert depth == 1
    w_in_t = jnp.swapaxes(w_in[0], 0, 1)
    nw = norm_mix_w[0].reshape(1, D_MODEL)
    nw2 = norm_ffn_w[0].reshape(1, D_MODEL)
    nfw = norm_final_w.reshape(1, D_MODEL)
    cw = ssd_conv_w[0]
    cb = ssd_conv_b[0].reshape(1, CONV_DIM)
    gnw = ssd_norm_w[0].reshape(1, D_SSD)
    scw = jnp.swapaxes(sc_conv_w, 0, 1)
    fcw = jnp.swapaxes(ffn_conv_w, 0, 1)
    fcb = ffn_conv_b[0].reshape(1, D_FF)
    expand = jnp.asarray(np.arange(DT_PAD)[:, None] == np.arange(D_SSD)[None, :] // HEAD_DIM, dtype=BF16)

    xs_in = x_sample
    cst = jnp.swapaxes(state_ssd_conv[0], 0, 1)
    scst, fst = state_short_conv[0], state_ffn_conv[0]
    z, xs, xdt_t, da, bc, ysc, ncst, nscst, win, dtb, a_pad, dx = _sample_pre(
        xs_in, nw, w_in_t, cw, cb, ssd_dt_bias, ssd_a_log, ssd_d, scw, expand, cst, scst)

    wout = w_out[0].astype(BF16)
    x1, p_ssm, p_cst, p_scst, wffn, wdown = _mixer_prompt(
        x_prompt, nw, win, cw, cb, dtb, a_pad, dx, gnw, scw, wout, expand, w_ffn_in[0], w_down[0])
    y_prompt, p_fst, s_ssm, y_t = _ffn_prompt(x1, nw2, wffn, fcw, fcb, wdown, nfw, state_ssm[0], da, xdt_t, bc)
    y_sample, nfst = _sample_post(xs_in, y_t, xs, z, ysc, dx, gnw, wout, nw2, wffn, fcw, fcb, wdown, nfw, fst)

    return (y_prompt, y_sample,
            p_ssm[None], jnp.swapaxes(p_cst, 0, 1)[None], p_scst[None], p_fst[None],
            s_ssm[None], jnp.swapaxes(ncst, 0, 1)[None], nscst[None], nfst[None])
```

```python
import functools

import jax
import jax.numpy as jnp
import numpy as np
from jax import lax
from jax.experimental import pallas as pl
from jax.experimental.pallas import tpu as pltpu

D_MODEL = 1024
D_SSD = 1024
D_SC = 1024
NHEADS = 16
HEAD_DIM = 64
NGROUPS = 2
NSTATE = 128
SSD_CONV = 4
CONV_DIM = D_SSD + 2 * NGROUPS * NSTATE
SC_CONV = 3
D_FF = 2816
FFN_CONV = 3
EPS = 1e-5
D_MIX = D_SSD + D_SC
D_IN_PROJ = D_SSD + CONV_DIM + NHEADS + 3 * D_SC

LANES = 128
MXU_COLS = 256
OFF_Z = 0
OFF_XBC = OFF_Z + D_SSD
OFF_DT = OFF_XBC + CONV_DIM
OFF_GB = OFF_DT + NHEADS
OFF_GC = OFF_GB + D_SC
OFF_H = OFF_GC + D_SC
DT_PAD = LANES

CHUNK = 128
CARRY = 8
SEQ_TILE = 512
VMEM_LIMIT = 56 * 1024 * 1024

F32 = jnp.float32
BF16 = jnp.bfloat16


def _dot(a, b):
    return jnp.dot(a, b, preferred_element_type=F32)


def _dot_nt(a, b_t):
    return lax.dot_general(a, b_t, (((1,), (1,)), ((), ())), preferred_element_type=F32)


def _split3(v):
    hi = v.astype(BF16)
    r = v - hi.astype(F32)
    mid = r.astype(BF16)
    lo = (r - mid.astype(F32)).astype(BF16)
    return hi, mid, lo


def _dot_sel_rhs(v, sel):
    hi, mid, lo = _split3(v)
    return (_dot(hi, sel) + _dot(mid, sel)) + _dot(lo, sel)


def _dot_sel_lhs(sel, v):
    hi, mid, lo = _split3(v)
    return (_dot(sel, hi) + _dot(sel, mid)) + _dot(sel, lo)


def _silu(v):
    h = 0.5 * v
    return h + h * jnp.tanh(h)


def _softplus(v):
    return jnp.maximum(v, 0.0) + jnp.log1p(jnp.exp(-jnp.abs(v)))


def _rms_rows(x, w):
    ms = jnp.mean(x * x, axis=-1, keepdims=True)
    return x * lax.rsqrt(ms + EPS) * w


def _mixer_prompt_kernel(tiles_per_seq,
                         x_ref, xres_ref, wffn_f32_ref, wdown_f32_ref,
                         nw_ref, win_ref, cw_ref, cb_ref, dtb_ref, a_ref, dx_ref,
                         gnw_ref, scw_ref, wout_ref, e_ref,
                         x1_ref, ssm_ref, cst_ref, scst_ref, wffn_bf_ref, wdown_bf_ref,
                         xbc_ext, sc_ext, u_ref, act_ref, dt_ref, z_ref, gb_ref, s_ref, y_ref, ysc_ref, ycat_ref):
    tile = x_ref.shape[1]
    g = pl.program_id(0)
    last = pl.num_programs(0) - 1
    t = lax.rem(g, tiles_per_seq)

    @pl.when(g < CAST_FFN_STEPS)
    def _():
        wffn_bf_ref[...] = wffn_f32_ref[...].astype(BF16)

    @pl.when(g < CAST_DOWN_STEPS)
    def _():
        wdown_bf_ref[...] = wdown_f32_ref[...].astype(BF16)

    @pl.when(g == 0)
    def _():
        ycat_ref[...] = jnp.zeros_like(ycat_ref)

    @pl.when(t == 0)
    def _():
        xbc_ext[0:CARRY, :] = jnp.zeros((CARRY, CONV_DIM), F32)
        sc_ext[0:CARRY, :] = jnp.zeros((CARRY, D_SC), F32)
        s_ref[...] = jnp.zeros_like(s_ref)

    def out_block(off):
        x1_ref[0, :, off:off + MXU_COLS] = xres_ref[0, :, off:off + MXU_COLS] + _dot(
            ycat_ref[...], wout_ref[:, off:off + MXU_COLS])

    @pl.when(g == last)
    def _():
        for off in range(0, D_MODEL, MXU_COLS):
            out_block(off)

    @pl.when(g < last)
    def _():
        _mixer_tile(tile, t == tiles_per_seq - 1, out_block,
                    x_ref, nw_ref, win_ref, cw_ref, cb_ref, dtb_ref, a_ref, dx_ref, gnw_ref, scw_ref, e_ref,
                    ssm_ref, cst_ref, scst_ref,
                    xbc_ext, sc_ext, u_ref, act_ref, dt_ref, z_ref, gb_ref, s_ref, y_ref, ysc_ref, ycat_ref)


def _mixer_tile(tile, ends_sequence, out_block,
                x_ref, nw_ref, win_ref, cw_ref, cb_ref, dtb_ref, a_ref, dx_ref, gnw_ref, scw_ref, e_ref,
                ssm_ref, cst_ref, scst_ref,
                xbc_ext, sc_ext, u_ref, act_ref, dt_ref, z_ref, gb_ref, s_ref, y_ref, ysc_ref, ycat_ref):
    u_ref[...] = _rms_rows(x_ref[0], nw_ref[...]).astype(BF16)

    def project(dst, row0, base, off, width=MXU_COLS):
        dst[row0:row0 + tile, off:off + width] = _dot_nt(u_ref[...], win_ref[base + off:base + off + width, :])

    def conv_block(off):
        conv = cb_ref[:, off:off + LANES] + (
            xbc_ext[CARRY - 3:CARRY - 3 + tile, off:off + LANES] * cw_ref[0:1, off:off + LANES])
        for k in range(1, SSD_CONV):
            conv = conv + (xbc_ext[CARRY - 3 + k:CARRY - 3 + k + tile, off:off + LANES]
                           * cw_ref[k:k + 1, off:off + LANES])
        act_ref[:, off:off + LANES] = _silu(conv)

    def sc_block(off):
        sc = sc_ext[CARRY - 2:CARRY - 2 + tile, off:off + LANES] * scw_ref[0, :, off:off + LANES]
        for k in range(1, SC_CONV):
            sc = sc + (sc_ext[CARRY - 2 + k:CARRY - 2 + k + tile, off:off + LANES]
                       * scw_ref[k, :, off:off + LANES])
        ysc_ref[:, off:off + LANES] = (gb_ref[:, off:off + LANES] * sc).astype(BF16)

    for off in range(0, CONV_DIM, MXU_COLS):
        project(xbc_ext, CARRY, OFF_XBC, off)
        for sub in range(off, off + MXU_COLS, LANES):
            conv_block(sub)
    tail = xbc_ext[tile + CARRY - 3:tile + CARRY, :]
    cst_ref[0] = tail
    xbc_ext[CARRY - 3:CARRY, :] = tail
    project(dt_ref, 0, OFF_DT, 0, DT_PAD)
    for off in range(0, D_SC, MXU_COLS):
        u = u_ref[...]
        sc_ext[CARRY:CARRY + tile, off:off + MXU_COLS] = (
            _dot_nt(u, win_ref[OFF_GC + off:OFF_GC + off + MXU_COLS, :])
            * _dot_nt(u, win_ref[OFF_H + off:OFF_H + off + MXU_COLS, :]))

    fillers = []
    for k in range(D_SC // MXU_COLS):
        fillers.append(lambda k=k: out_block(k * MXU_COLS))
        fillers.append(lambda k=k: project(gb_ref, 0, OFF_GB, k * MXU_COLS))
        fillers.append(lambda k=k: sc_block(k * MXU_COLS))
        fillers.append(lambda k=k: sc_block(k * MXU_COLS + LANES))
        fillers.append(lambda k=k: project(z_ref, 0, OFF_Z, k * MXU_COLS))
    n_slots = (tile // CHUNK) * (NHEADS // 2)
    slot_of_filler = [(i * n_slots) // len(fillers) for i in range(len(fillers))]

    row_i = lax.broadcasted_iota(jnp.int32, (CHUNK, CHUNK), 0)
    col_i = lax.broadcasted_iota(jnp.int32, (CHUNK, CHUNK), 1)
    causal = row_i >= col_i
    tri = causal.astype(BF16)
    lane_i = lax.broadcasted_iota(jnp.int32, (1, LANES), 1)
    keeps = ((lane_i < HEAD_DIM).astype(BF16), (lane_i >= HEAD_DIM).astype(BF16))
    neg_inf = jnp.float32(-jnp.inf)
    pairs_per_group = NHEADS // NGROUPS // 2

    for c in range(tile // CHUNK):
        r0 = c * CHUNK
        dt = _softplus(dt_ref[r0:r0 + CHUNK, :] + dtb_ref[...])
        acum = _dot_sel_lhs(tri, dt * a_ref[...])
        acum_t = acum.T
        dt_t = dt.T
        ea = jnp.exp(acum)
        w_t = jnp.exp(acum_t[:, CHUNK - 1:CHUNK] - acum_t) * dt_t
        cd = jnp.broadcast_to(ea[CHUNK - 1:CHUNK, :], (8, LANES))
        cdx = _dot_sel_rhs(cd, e_ref[...])[0:1, :]
        for grp in range(NGROUPS):
            bg = act_ref[r0:r0 + CHUNK, D_SSD + grp * NSTATE:D_SSD + (grp + 1) * NSTATE]
            cg = act_ref[r0:r0 + CHUNK, D_SSD + (NGROUPS + grp) * NSTATE:D_SSD + (NGROUPS + grp + 1) * NSTATE]
            cb = lax.dot_general(cg.astype(BF16), bg.astype(BF16), (((1,), (1,)), ((), ())),
                                 preferred_element_type=F32)
            bg_t = bg.T
            for jp in range(pairs_per_group):
                j = grp * pairs_per_group + jp
                xp = act_ref[r0:r0 + CHUNK, j * LANES:(j + 1) * LANES].astype(BF16)
                sp = s_ref[:, j * LANES:(j + 1) * LANES]
                spb = sp.astype(BF16)
                lhs_parts, rhs_parts, bw_parts, x_parts = [], [], [], []
                for half, keep in enumerate(keeps):
                    h = 2 * j + half
                    xh = xp * keep
                    seg = acum[:, h:h + 1] - acum_t[h:h + 1, :]
                    decay = jnp.exp(jnp.where(causal, seg, neg_inf))
                    m = cb * decay * dt_t[h:h + 1, :]
                    eac = ea[:, h:h + 1] * cg
                    lhs_parts += [m.astype(BF16), eac.astype(BF16)]
                    rhs_parts += [xh, spb * keep]
                    bw_parts.append((bg_t * w_t[h:h + 1, :]).astype(BF16))
                    x_parts.append(xh)
                y_ref[r0:r0 + CHUNK, j * LANES:(j + 1) * LANES] = _dot(
                    jnp.concatenate(lhs_parts, axis=1), jnp.concatenate(rhs_parts, axis=0))
                snew = _dot(jnp.concatenate(bw_parts, axis=1), jnp.concatenate(x_parts, axis=0))
                s_ref[:, j * LANES:(j + 1) * LANES] = cdx[:, j * LANES:(j + 1) * LANES] * sp + snew
                slot = c * (NHEADS // 2) + j
                for i, filler in enumerate(fillers):
                    if slot_of_filler[i] == slot:
                        filler()
    sc_tail = sc_ext[tile + CARRY - 2:tile + CARRY, :]
    scst_ref[0] = sc_tail
    sc_ext[CARRY - 2:CARRY, :] = sc_tail

    gw = D_SSD // NGROUPS
    for grp in range(NGROUPS):
        ssq = jnp.zeros((tile, LANES), F32)
        for off in range(grp * gw, (grp + 1) * gw, LANES):
            yb = ((y_ref[:, off:off + LANES] + dx_ref[:, off:off + LANES] * act_ref[:, off:off + LANES])
                  * _silu(z_ref[:, off:off + LANES]))
            y_ref[:, off:off + LANES] = yb
            ssq = ssq + yb * yb
        scale = lax.rsqrt(jnp.sum(ssq, axis=-1, keepdims=True) * (1.0 / gw) + EPS)
        for off in range(grp * gw, (grp + 1) * gw, LANES):
            ycat_ref[:, off:off + LANES] = (y_ref[:, off:off + LANES] * scale
                                            * gnw_ref[:, off:off + LANES]).astype(BF16)
    ycat_ref[:, D_SSD:D_MIX] = ysc_ref[...]

    @pl.when(ends_sequence)
    def _():
        for j in range(NHEADS // 2):
            ssm_ref[0, 2 * j:2 * j + 2] = s_ref[:, j * LANES:(j + 1) * LANES].T.reshape(2, HEAD_DIM, NSTATE)


def _const_spec(shape):
    nd = len(shape)
    return pl.BlockSpec(shape, lambda *_: (0,) * nd, pipeline_mode=pl.Buffered(1))


CAST_FFN_STEPS = 32
CAST_DOWN_STEPS = 16


def _mixer_prompt(x, nw, win, cw, cb, dtb, a_pad, dx, gnw, scw, wout, expand, w_ffn_in, w_down):
    nb, seq, _ = x.shape
    tile = SEQ_TILE
    nt = seq // tile
    total = nb * nt
    assert total >= CAST_FFN_STEPS and total >= CAST_DOWN_STEPS
    ffn_rows = D_MODEL // CAST_FFN_STEPS
    down_rows = D_FF // CAST_DOWN_STEPS
    ffn_spec = pl.BlockSpec((ffn_rows, 2 * D_FF), lambda g: (jnp.minimum(g, CAST_FFN_STEPS - 1), 0))
    down_spec = pl.BlockSpec((down_rows, D_MODEL), lambda g: (jnp.minimum(g, CAST_DOWN_STEPS - 1), 0))

    def tile_of(g, lag):
        i = jnp.clip(g - lag, 0, total - 1)
        return i // nt, i % nt

    cur_spec = pl.BlockSpec((1, tile, D_MODEL), lambda g: (*tile_of(g, 0), 0))
    prev_spec = pl.BlockSpec((1, tile, D_MODEL), lambda g: (*tile_of(g, 1), 0))
    consts = (nw, win, cw, cb, dtb, a_pad, dx, gnw, scw, wout, expand)
    return pl.pallas_call(
        functools.partial(_mixer_prompt_kernel, nt),
        grid=(total + 1,),
        in_specs=[cur_spec, prev_spec, ffn_spec, down_spec] + [_const_spec(p.shape) for p in consts],
        out_specs=[
            prev_spec,
            pl.BlockSpec((1, NHEADS, HEAD_DIM, NSTATE), lambda g: (tile_of(g, 0)[0], 0, 0, 0)),
            pl.BlockSpec((1, SSD_CONV - 1, CONV_DIM), lambda g: (tile_of(g, 0)[0], 0, 0)),
            pl.BlockSpec((1, SC_CONV - 1, D_SC), lambda g: (tile_of(g, 0)[0], 0, 0)),
            ffn_spec,
            down_spec,
        ],
        out_shape=[
            jax.ShapeDtypeStruct((nb, seq, D_MODEL), F32),
            jax.ShapeDtypeStruct((nb, NHEADS, HEAD_DIM, NSTATE), F32),
            jax.ShapeDtypeStruct((nb, SSD_CONV - 1, CONV_DIM), F32),
            jax.ShapeDtypeStruct((nb, SC_CONV - 1, D_SC), F32),
            jax.ShapeDtypeStruct(w_ffn_in.shape, BF16),
            jax.ShapeDtypeStruct(w_down.shape, BF16),
        ],
        scratch_shapes=[
            pltpu.VMEM((tile + CARRY, CONV_DIM), F32),
            pltpu.VMEM((tile + CARRY, D_SC), F32),
            pltpu.VMEM((tile, D_MODEL), BF16),
            pltpu.VMEM((tile, CONV_DIM), F32),
            pltpu.VMEM((tile, DT_PAD), F32),
            pltpu.VMEM((tile, D_SSD), F32),
            pltpu.VMEM((tile, D_SC), F32),
            pltpu.VMEM((NSTATE, D_SSD), F32),
            pltpu.VMEM((tile, D_SSD), F32),
            pltpu.VMEM((tile, D_SC), BF16),
            pltpu.VMEM((tile, D_MIX), BF16),
        ],
        compiler_params=pltpu.CompilerParams(
            dimension_semantics=("arbitrary",), vmem_limit_bytes=VMEM_LIMIT),
        name="mixer_prompt",
    )(x, x, w_ffn_in, w_down, *consts)


STATE_HEADS = 2
STATE_SEQS = 64
STATE_ROWS = STATE_HEADS * HEAD_DIM


def _state_block_update(da_ref, step, st_ref, xdt_t_ref, b_ref, c_ref, nst_ref, yt_ref):
    seq_blocks = da_ref.shape[0] // STATE_SEQS
    seq0 = lax.rem(step, seq_blocks) * STATE_SEQS
    head0 = (step // seq_blocks) * STATE_HEADS
    c_blk = c_ref[0].astype(BF16)
    lane = lax.broadcasted_iota(jnp.int32, (STATE_ROWS, STATE_SEQS), 1)
    ycols = jnp.zeros((STATE_ROWS, STATE_SEQS), F32)
    for i in range(STATE_SEQS):
        upd = xdt_t_ref[0, :, i:i + 1] * b_ref[0, i:i + 1, :]
        parts = []
        for hh in range(STATE_HEADS):
            hnew = st_ref[i, hh] * da_ref[seq0 + i, head0 + hh] + upd[hh * HEAD_DIM:(hh + 1) * HEAD_DIM, :]
            nst_ref[i, hh] = hnew
            parts.append(hnew)
        prod = _dot_nt(jnp.concatenate(parts, axis=0).astype(BF16), c_blk)
        ycols = jnp.where(lane == i, prod, ycols)
    yt_ref[0] = ycols


def _ffn_prompt_kernel(state_steps,
                       da_ref, x1_ref, st_ref, xdt_t_ref, b_ref, c_ref,
                       nw_ref, wffn_ref, fcw_ref, fcb_ref, wdown_ref, nfw_ref,
                       y_ref, fst_ref, nst_ref, yt_ref, gate_ext):
    tile = x1_ref.shape[1]
    t = pl.program_id(1)
    step = pl.program_id(0) * pl.num_programs(1) + t

    @pl.when(step < state_steps)
    def _():
        _state_block_update(da_ref, step, st_ref, xdt_t_ref, b_ref, c_ref, nst_ref, yt_ref)

    @pl.when(t == 0)
    def _():
        gate_ext[0:CARRY, :] = jnp.zeros((CARRY, D_FF), F32)

    x1 = x1_ref[0]
    u = _rms_rows(x1, nw_ref[...]).astype(BF16)
    gate_ext[CARRY:CARRY + tile, :] = _dot(u, wffn_ref[:, 0:D_FF])
    up = _dot(u, wffn_ref[:, D_FF:2 * D_FF])
    g = fcb_ref[...] + gate_ext[CARRY - 2:CARRY - 2 + tile, :] * fcw_ref[0]
    for k in range(1, FFN_CONV):
        g = g + gate_ext[CARRY - 2 + k:CARRY - 2 + k + tile, :] * fcw_ref[k]
    tail = gate_ext[tile + CARRY - 2:tile + CARRY, :]
    fst_ref[0] = tail
    gate_ext[CARRY - 2:CARRY, :] = tail
    act = (_silu(g) * up).astype(BF16)
    x2 = x1 + _dot(act, wdown_ref[...])
    y_ref[0] = _rms_rows(x2, nfw_ref[...])


def _ffn_prompt(x1, nw, wffn, fcw, fcb, wdown, nfw, state, da, xdt_t, bc):
    nb, seq, _ = x1.shape
    tile = SEQ_TILE
    nt = seq // tile
    nseq = state.shape[0]
    seq_blocks = nseq // STATE_SEQS
    state_steps = seq_blocks * (NHEADS // STATE_HEADS)
    assert nb * nt >= state_steps
    heads_per_group = NHEADS // NGROUPS

    def block_of(b, t):
        s = jnp.minimum(b * nt + t, state_steps - 1)
        return s % seq_blocks, s // seq_blocks

    def st_map(b, t):
        sb, hb = block_of(b, t)
        return (sb, hb, 0, 0)

    def col_map(b, t):
        sb, hb = block_of(b, t)
        return (sb, hb, 0)

    def b_map(b, t):
        sb, hb = block_of(b, t)
        return (hb * STATE_HEADS // heads_per_group, sb, 0)

    def c_map(b, t):
        sb, hb = block_of(b, t)
        return (NGROUPS + hb * STATE_HEADS // heads_per_group, sb, 0)

    tok_spec = pl.BlockSpec((1, tile, D_MODEL), lambda b, t: (b, t, 0))
    st_spec = pl.BlockSpec((STATE_SEQS, STATE_HEADS, HEAD_DIM, NSTATE), st_map)
    col_spec = pl.BlockSpec((1, STATE_ROWS, STATE_SEQS), col_map)
    return pl.pallas_call(
        functools.partial(_ffn_prompt_kernel, state_steps),
        grid=(nb, nt),
        in_specs=[pl.BlockSpec(memory_space=pltpu.SMEM), tok_spec, st_spec, col_spec,
                  pl.BlockSpec((1, STATE_SEQS, NSTATE), b_map), pl.BlockSpec((1, STATE_SEQS, NSTATE), c_map)]
        + [_const_spec(p.shape) for p in (nw, wffn, fcw, fcb, wdown, nfw)],
        out_specs=[tok_spec, pl.BlockSpec((1, FFN_CONV - 1, D_FF), lambda b, t: (b, 0, 0)), st_spec, col_spec],
        out_shape=[jax.ShapeDtypeStruct((nb, seq, D_MODEL), F32),
                   jax.ShapeDtypeStruct((nb, FFN_CONV - 1, D_FF), F32),
                   jax.ShapeDtypeStruct(state.shape, F32),
                   jax.ShapeDtypeStruct(xdt_t.shape, F32)],
        scratch_shapes=[pltpu.VMEM((tile + CARRY, D_FF), F32)],
        compiler_params=pltpu.CompilerParams(
            dimension_semantics=("arbitrary", "arbitrary"), vmem_limit_bytes=VMEM_LIMIT),
        name="ffn_prompt",
    )(da, x1, state, xdt_t, bc, bc, nw, wffn, fcw, fcb, wdown, nfw)


CAST_ROWS = 1280
CAST_STEPS = -(-D_IN_PROJ // CAST_ROWS)
CAST_STARTS = tuple(min(s * CAST_ROWS, D_IN_PROJ - CAST_ROWS) for s in range(CAST_STEPS))
WEIGHT_ROW_ALIGN = 16


def _cast_block_start(s):
    assert all(r % WEIGHT_ROW_ALIGN == 0 for r in CAST_STARTS)
    per = CAST_ROWS // WEIGHT_ROW_ALIGN
    return WEIGHT_ROW_ALIGN * jnp.where(s < CAST_STEPS - 1, s * per, CAST_STARTS[-1] // WEIGHT_ROW_ALIGN)


def _sample_pre_kernel(x_ref, nw_ref, wf32_ref, cw_ref, cb_ref, dt_bias_ref, a_log_ref, d_ref, scw_ref, e_ref,
                       cst_ref, scst_ref,
                       z_ref, xs_ref, xdt_t_ref, da_ref, bc_ref, ysc_ref, ncst_ref, nscst_ref, wbf_ref,
                       dtb_ref, a_ref, dx_ref, win_ref):
    s = pl.program_id(0)
    blk = wf32_ref[...].astype(BF16)
    wbf_ref[...] = blk
    for k, r0 in enumerate(CAST_STARTS):
        @pl.when(s == k)
        def _(r0=r0):
            win_ref[r0:r0 + CAST_ROWS, :] = blk

    @pl.when(s == CAST_STEPS - 1)
    def _():
        _sample_pre_compute(x_ref, nw_ref, win_ref, cw_ref, cb_ref, dt_bias_ref, a_log_ref, d_ref, scw_ref, e_ref,
                            cst_ref, scst_ref, z_ref, xs_ref, xdt_t_ref, da_ref, bc_ref, ysc_ref, ncst_ref,
                            nscst_ref, dtb_ref, a_ref, dx_ref)


def _sample_pre_compute(x_ref, nw_ref, win_ref, cw_ref, cb_ref, dt_bias_ref, a_log_ref, d_ref, scw_ref, e_ref,
                        cst_ref, scst_ref, z_ref, xs_ref, xdt_t_ref, da_ref, bc_ref, ysc_ref, ncst_ref,
                        nscst_ref, dtb_ref, a_ref, dx_ref):
    pad = jnp.zeros((1, DT_PAD - NHEADS), F32)
    dtb_ref[...] = jnp.concatenate([dt_bias_ref[...], pad], axis=1)
    a_ref[...] = jnp.concatenate([-jnp.exp(a_log_ref[...]), pad], axis=1)
    d_pad = jnp.broadcast_to(jnp.concatenate([d_ref[...], pad], axis=1), (8, DT_PAD))
    dx_ref[...] = _dot_sel_rhs(d_pad, e_ref[...])[0:1, :]
    x = x_ref[:, 0, :]
    u = _rms_rows(x, nw_ref[...]).astype(BF16)
    z_ref[...] = _dot_nt(u, win_ref[OFF_Z:OFF_Z + D_SSD, :])
    xbc = _dot_nt(u, win_ref[OFF_XBC:OFF_XBC + CONV_DIM, :])
    conv = cb_ref[...] + xbc * cw_ref[SSD_CONV - 1:SSD_CONV, :]
    for k in range(SSD_CONV - 1):
        conv = conv + cst_ref[k] * cw_ref[k:k + 1, :]
    for k in range(SSD_CONV - 2):
        ncst_ref[k] = cst_ref[k + 1]
    ncst_ref[SSD_CONV - 2] = xbc
    act = _silu(conv)
    xs = act[:, 0:D_SSD]
    xs_ref[...] = xs
    for i in range(2 * NGROUPS):
        bc_ref[i] = act[:, D_SSD + i * NSTATE:D_SSD + (i + 1) * NSTATE]
    dt = _softplus(_dot_nt(u, win_ref[OFF_DT:OFF_DT + DT_PAD, :]) + dtb_ref[...])
    xdt_t = (xs * _dot_sel_rhs(dt, e_ref[...])).T
    for sb in range(xdt_t_ref.shape[0]):
        xdt_t_ref[sb] = xdt_t[:, sb * STATE_SEQS:(sb + 1) * STATE_SEQS]
    da_ref[...] = jnp.exp(dt * a_ref[...])[:, 0:NHEADS]
    gc = _dot_nt(u, win_ref[OFF_GC:OFF_GC + D_SC, :])
    hh = _dot_nt(u, win_ref[OFF_H:OFF_H + D_SC, :])
    gch = gc * hh
    sc = gch * scw_ref[SC_CONV - 1]
    for k in range(SC_CONV - 1):
        sc = sc + scst_ref[:, k, :] * scw_ref[k]
    for k in range(SC_CONV - 2):
        nscst_ref[:, k, :] = scst_ref[:, k + 1, :]
    nscst_ref[:, SC_CONV - 2, :] = gch
    gb = _dot_nt(u, win_ref[OFF_GB:OFF_GB + D_SC, :])
    ysc_ref[...] = gb * sc


def _sample_pre(x, nw, w_in_t, cw, cb, dt_bias, a_log, d, scw, expand, cst, scst):
    nb = x.shape[0]
    outs = [
        jax.ShapeDtypeStruct((nb, D_SSD), F32),
        jax.ShapeDtypeStruct((nb, D_SSD), F32),
        jax.ShapeDtypeStruct((nb // STATE_SEQS, D_SSD, STATE_SEQS), F32),
        jax.ShapeDtypeStruct((nb, NHEADS), F32),
        jax.ShapeDtypeStruct((2 * NGROUPS, nb, NSTATE), F32),
        jax.ShapeDtypeStruct((nb, D_SC), F32),
        jax.ShapeDtypeStruct((SSD_CONV - 1, nb, CONV_DIM), F32),
        jax.ShapeDtypeStruct((nb, SC_CONV - 1, D_SC), F32),
        jax.ShapeDtypeStruct((D_IN_PROJ, D_MODEL), BF16),
        jax.ShapeDtypeStruct((1, DT_PAD), F32),
        jax.ShapeDtypeStruct((1, DT_PAD), F32),
        jax.ShapeDtypeStruct((1, D_SSD), F32),
    ]
    def whole(a):
        nd = len(a.shape)
        return pl.BlockSpec(a.shape, lambda s: (0,) * nd)

    def weight_rows():
        return pl.BlockSpec((pl.Element(CAST_ROWS), pl.Element(D_MODEL)), lambda s: (_cast_block_start(s), 0))

    weight_out = 8
    return pl.pallas_call(
        _sample_pre_kernel,
        grid=(CAST_STEPS,),
        in_specs=[_const_spec(x.shape), _const_spec(nw.shape), weight_rows()]
        + [_const_spec(a.shape) for a in (cw, cb, dt_bias, a_log, d, scw, expand, cst, scst)],
        out_specs=[weight_rows() if i == weight_out else whole(o) for i, o in enumerate(outs)],
        out_shape=outs,
        scratch_shapes=[pltpu.VMEM((D_IN_PROJ, D_MODEL), BF16)],
        compiler_params=pltpu.CompilerParams(
            dimension_semantics=("arbitrary",), vmem_limit_bytes=VMEM_LIMIT),
        name="sample_pre",
    )(x, nw, w_in_t, cw, cb, dt_bias, a_log, d, scw, expand, cst, scst)


def _sample_post_kernel(x_ref, yt_ref, xs_ref, z_ref, ysc_ref, dx_ref, gnw_ref, wout_ref,
                        nw_ref, wffn_ref, fcw_ref, fcb_ref, wdown_ref, nfw_ref, fst_ref,
                        out_ref, nfst_ref):
    y_ssd = jnp.concatenate([yt_ref[sb] for sb in range(yt_ref.shape[0])], axis=1).T
    y = (y_ssd + dx_ref[...] * xs_ref[...]) * _silu(z_ref[...])
    gw = D_SSD // NGROUPS
    parts = [_rms_rows(y[:, g * gw:(g + 1) * gw], gnw_ref[:, g * gw:(g + 1) * gw]) for g in range(NGROUPS)]
    ycat = jnp.concatenate(parts + [ysc_ref[...]], axis=1).astype(BF16)
    x1 = x_ref[:, 0, :] + _dot(ycat, wout_ref[...])
    u = _rms_rows(x1, nw_ref[...]).astype(BF16)
    gate = _dot(u, wffn_ref[:, 0:D_FF])
    up = _dot(u, wffn_ref[:, D_FF:2 * D_FF])
    g = fcb_ref[...] + gate * fcw_ref[FFN_CONV - 1]
    for k in range(FFN_CONV - 1):
        g = g + fst_ref[:, k, :] * fcw_ref[k]
    for k in range(FFN_CONV - 2):
        nfst_ref[:, k, :] = fst_ref[:, k + 1, :]
    nfst_ref[:, FFN_CONV - 2, :] = gate
    act = (_silu(g) * up).astype(BF16)
    x2 = x1 + _dot(act, wdown_ref[...])
    out_ref[:, 0, :] = _rms_rows(x2, nfw_ref[...])


def _sample_post(x, y, xs, z, ysc, dx, gnw, wout, nw, wffn, fcw, fcb, wdown, nfw, fst):
    nb = x.shape[0]
    return pl.pallas_call(
        _sample_post_kernel,
        out_shape=[jax.ShapeDtypeStruct((nb, 1, D_MODEL), F32),
                   jax.ShapeDtypeStruct((nb, FFN_CONV - 1, D_FF), F32)],
        compiler_params=pltpu.CompilerParams(vmem_limit_bytes=VMEM_LIMIT),
        name="sample_post",
    )(x, y, xs, z, ysc, dx, gnw, wout, nw, wffn, fcw, fcb, wdown, nfw, fst)


def kernel(x_prompt, x_sample, state_ssm, state_ssd_conv, state_short_conv, state_ffn_conv,
           norm_mix_w, w_in, ssd_conv_w, ssd_conv_b, ssd_dt_bias, ssd_a_log, ssd_d, ssd_norm_w,
           sc_conv_w, w_out, norm_ffn_w, w_ffn_in, ffn_conv_w, ffn_conv_b, w_down, norm_final_w):
    depth = w_in.shape[0]
    assert depth == 1
    w_in_t = jnp.swapaxes(w_in[0], 0, 1)
    nw = norm_mix_w[0].reshape(1, D_MODEL)
    nw2 = norm_ffn_w[0].reshape(1, D_MODEL)
    nfw = norm_final_w.reshape(1, D_MODEL)
    cw = ssd_conv_w[0]
    cb = ssd_conv_b[0].reshape(1, CONV_DIM)
    gnw = ssd_norm_w[0].reshape(1, D_SSD)
    scw = jnp.swapaxes(sc_conv_w, 0, 1)
    fcw = jnp.swapaxes(ffn_conv_w, 0, 1)
    fcb = ffn_conv_b[0].reshape(1, D_FF)
    expand = jnp.asarray(np.arange(DT_PAD)[:, None] == np.arange(D_SSD)[None, :] // HEAD_DIM, dtype=BF16)

    xs_in = x_sample
    cst = jnp.swapaxes(state_ssd_conv[0], 0, 1)
    scst, fst = state_short_conv[0], state_ffn_conv[0]
    z, xs, xdt_t, da, bc, ysc, ncst, nscst, win, dtb, a_pad, dx = _sample_pre(
        xs_in, nw, w_in_t, cw, cb, ssd_dt_bias, ssd_a_log, ssd_d, scw, expand, cst, scst)

    wout = w_out[0].astype(BF16)
    x1, p_ssm, p_cst, p_scst, wffn, wdown = _mixer_prompt(
        x_prompt, nw, win, cw, cb, dtb, a_pad, dx, gnw, scw, wout, expand, w_ffn_in[0], w_down[0])
    y_prompt, p_fst, s_ssm, y_t = _ffn_prompt(x1, nw2, wffn, fcw, fcb, wdown, nfw, state_ssm[0], da, xdt_t, bc)
    y_sample, nfst = _sample_post(xs_in, y_t, xs, z, ysc, dx, gnw, wout, nw2, wffn, fcw, fcb, wdown, nfw, fst)

    return (y_prompt, y_sample,
            p_ssm[None], p_cst[None], p_scst[None], p_fst[None],
            s_ssm[None], jnp.swapaxes(ncst, 0, 1)[None], nscst[None], nfst[None])
```

```python
import functools

import jax
import jax.numpy as jnp
import numpy as np
from jax import lax
from jax.experimental import pallas as pl
from jax.experimental.pallas import tpu as pltpu

D_MODEL = 1024
D_SSD = 1024
D_SC = 1024
NHEADS = 16
HEAD_DIM = 64
NGROUPS = 2
NSTATE = 128
SSD_CONV = 4
CONV_DIM = D_SSD + 2 * NGROUPS * NSTATE
SC_CONV = 3
D_FF = 2816
FFN_CONV = 3
EPS = 1e-5
D_MIX = D_SSD + D_SC
D_IN_PROJ = D_SSD + CONV_DIM + NHEADS + 3 * D_SC

LANES = 128
MXU_COLS = 256
OFF_Z = 0
OFF_XBC = OFF_Z + D_SSD
OFF_DT = OFF_XBC + CONV_DIM
OFF_GB = OFF_DT + NHEADS
OFF_GC = OFF_GB + D_SC
OFF_H = OFF_GC + D_SC
DT_PAD = LANES

CHUNK = 128
CARRY = 8
SEQ_TILE = 512
VMEM_LIMIT = 56 * 1024 * 1024

F32 = jnp.float32
BF16 = jnp.bfloat16


def _dot(a, b):
    return jnp.dot(a, b, preferred_element_type=F32)


def _dot_nt(a, b_t):
    return lax.dot_general(a, b_t, (((1,), (1,)), ((), ())), preferred_element_type=F32)


def _split3(v):
    hi = v.astype(BF16)
    r = v - hi.astype(F32)
    mid = r.astype(BF16)
    lo = (r - mid.astype(F32)).astype(BF16)
    return hi, mid, lo


def _dot_sel_rhs(v, sel):
    hi, mid, lo = _split3(v)
    return (_dot(hi, sel) + _dot(mid, sel)) + _dot(lo, sel)


def _dot_sel_lhs(sel, v):
    hi, mid, lo = _split3(v)
    return (_dot(sel, hi) + _dot(sel, mid)) + _dot(sel, lo)


def _silu(v):
    h = 0.5 * v
    return h + h * jnp.tanh(h)


def _softplus(v):
    return jnp.maximum(v, 0.0) + jnp.log1p(jnp.exp(-jnp.abs(v)))


def _rms_rows(x, w):
    ms = jnp.mean(x * x, axis=-1, keepdims=True)
    return x * lax.rsqrt(ms + EPS) * w


def _mixer_prompt_kernel(tiles_per_seq,
                         x_ref, xres_ref, wffn_f32_ref, wdown_f32_ref,
                         nw_ref, win_ref, cw_ref, cb_ref, dtb_ref, a_ref, dx_ref,
                         gnw_ref, scw_ref, wout_ref, e_ref,
                         x1_ref, ssm_ref, cst_ref, scst_ref, wffn_bf_ref, wdown_bf_ref,
                         xbc_ext, sc_ext, u_ref, act_ref, dt_ref, z_ref, gb_ref, s_ref, y_ref, ysc_ref, ycat_ref):
    tile = x_ref.shape[1]
    g = pl.program_id(0)
    last = pl.num_programs(0) - 1
    t = lax.rem(g, tiles_per_seq)

    @pl.when(g < CAST_FFN_STEPS)
    def _():
        wffn_bf_ref[...] = wffn_f32_ref[...].astype(BF16)

    @pl.when(g < CAST_DOWN_STEPS)
    def _():
        wdown_bf_ref[...] = wdown_f32_ref[...].astype(BF16)

    @pl.when(g == 0)
    def _():
        ycat_ref[...] = jnp.zeros_like(ycat_ref)

    @pl.when(t == 0)
    def _():
        xbc_ext[0:CARRY, :] = jnp.zeros((CARRY, CONV_DIM), F32)
        sc_ext[0:CARRY, :] = jnp.zeros((CARRY, D_SC), F32)
        s_ref[...] = jnp.zeros_like(s_ref)

    def out_block(off):
        x1_ref[0, :, off:off + MXU_COLS] = xres_ref[0, :, off:off + MXU_COLS] + _dot(
            ycat_ref[...], wout_ref[:, off:off + MXU_COLS])

    @pl.when(g == last)
    def _():
        for off in range(0, D_MODEL, MXU_COLS):
            out_block(off)

    @pl.when(g < last)
    def _():
        _mixer_tile(tile, t == tiles_per_seq - 1, out_block,
                    x_ref, nw_ref, win_ref, cw_ref, cb_ref, dtb_ref, a_ref, dx_ref, gnw_ref, scw_ref, e_ref,
                    ssm_ref, cst_ref, scst_ref,
                    xbc_ext, sc_ext, u_ref, act_ref, dt_ref, z_ref, gb_ref, s_ref, y_ref, ysc_ref, ycat_ref)


def _mixer_tile(tile, ends_sequence, out_block,
                x_ref, nw_ref, win_ref, cw_ref, cb_ref, dtb_ref, a_ref, dx_ref, gnw_ref, scw_ref, e_ref,
                ssm_ref, cst_ref, scst_ref,
                xbc_ext, sc_ext, u_ref, act_ref, dt_ref, z_ref, gb_ref, s_ref, y_ref, ysc_ref, ycat_ref):
    u_ref[...] = _rms_rows(x_ref[0], nw_ref[...]).astype(BF16)

    def project(dst, row0, base, off, width=MXU_COLS):
        dst[row0:row0 + tile, off:off + width] = _dot_nt(u_ref[...], win_ref[base + off:base + off + width, :])

    def conv_block(off):
        conv = cb_ref[:, off:off + LANES] + (
            xbc_ext[CARRY - 3:CARRY - 3 + tile, off:off + LANES] * cw_ref[0:1, off:off + LANES])
        for k in range(1, SSD_CONV):
            conv = conv + (xbc_ext[CARRY - 3 + k:CARRY - 3 + k + tile, off:off + LANES]
                           * cw_ref[k:k + 1, off:off + LANES])
        act_ref[:, off:off + LANES] = _silu(conv)

    def sc_block(off):
        sc = sc_ext[CARRY - 2:CARRY - 2 + tile, off:off + LANES] * scw_ref[0, :, off:off + LANES]
        for k in range(1, SC_CONV):
            sc = sc + (sc_ext[CARRY - 2 + k:CARRY - 2 + k + tile, off:off + LANES]
                       * scw_ref[k, :, off:off + LANES])
        ysc_ref[:, off:off + LANES] = (gb_ref[:, off:off + LANES] * sc).astype(BF16)

    for off in range(0, CONV_DIM, MXU_COLS):
        project(xbc_ext, CARRY, OFF_XBC, off)
        for sub in range(off, off + MXU_COLS, LANES):
            conv_block(sub)
    tail = xbc_ext[tile + CARRY - 3:tile + CARRY, :]
    cst_ref[0] = tail
    xbc_ext[CARRY - 3:CARRY, :] = tail
    project(dt_ref, 0, OFF_DT, 0, DT_PAD)
    for off in range(0, D_SC, MXU_COLS):
        u = u_ref[...]
        sc_ext[CARRY:CARRY + tile, off:off + MXU_COLS] = (
            _dot_nt(u, win_ref[OFF_GC + off:OFF_GC + off + MXU_COLS, :])
            * _dot_nt(u, win_ref[OFF_H + off:OFF_H + off + MXU_COLS, :]))

    fillers = []
    for k in range(D_SC // MXU_COLS):
        fillers.append(lambda k=k: out_block(k * MXU_COLS))
        fillers.append(lambda k=k: project(gb_ref, 0, OFF_GB, k * MXU_COLS))
        fillers.append(lambda k=k: sc_block(k * MXU_COLS))
        fillers.append(lambda k=k: sc_block(k * MXU_COLS + LANES))
        fillers.append(lambda k=k: project(z_ref, 0, OFF_Z, k * MXU_COLS))
    n_slots = (tile // CHUNK) * (NHEADS // 2)
    slot_of_filler = [(i * n_slots) // len(fillers) for i in range(len(fillers))]

    row_i = lax.broadcasted_iota(jnp.int32, (CHUNK, CHUNK), 0)
    col_i = lax.broadcasted_iota(jnp.int32, (CHUNK, CHUNK), 1)
    causal = row_i >= col_i
    tri = causal.astype(BF16)
    lane_i = lax.broadcasted_iota(jnp.int32, (1, LANES), 1)
    keeps = ((lane_i < HEAD_DIM).astype(BF16), (lane_i >= HEAD_DIM).astype(BF16))
    neg_inf = jnp.float32(-jnp.inf)
    pairs_per_group = NHEADS // NGROUPS // 2

    for c in range(tile // CHUNK):
        r0 = c * CHUNK
        dt = _softplus(dt_ref[r0:r0 + CHUNK, :] + dtb_ref[...])
        acum = _dot_sel_lhs(tri, dt * a_ref[...])
        acum_t = acum.T
        dt_t = dt.T
        ea = jnp.exp(acum)
        w_t = jnp.exp(acum_t[:, CHUNK - 1:CHUNK] - acum_t) * dt_t
        cd = jnp.broadcast_to(ea[CHUNK - 1:CHUNK, :], (8, LANES))
        cdx = _dot_sel_rhs(cd, e_ref[...])[0:1, :]
        for grp in range(NGROUPS):
            bg = act_ref[r0:r0 + CHUNK, D_SSD + grp * NSTATE:D_SSD + (grp + 1) * NSTATE]
            cg = act_ref[r0:r0 + CHUNK, D_SSD + (NGROUPS + grp) * NSTATE:D_SSD + (NGROUPS + grp + 1) * NSTATE]
            cb = lax.dot_general(cg.astype(BF16), bg.astype(BF16), (((1,), (1,)), ((), ())),
                                 preferred_element_type=F32)
            bg_t = bg.T
            for jp in range(pairs_per_group):
                j = grp * pairs_per_group + jp
                xp = act_ref[r0:r0 + CHUNK, j * LANES:(j + 1) * LANES].astype(BF16)
                sp = s_ref[:, j * LANES:(j + 1) * LANES]
                spb = sp.astype(BF16)
                lhs_parts, rhs_parts, bw_parts, x_parts = [], [], [], []
                for half, keep in enumerate(keeps):
                    h = 2 * j + half
                    xh = xp * keep
                    seg = acum[:, h:h + 1] - acum_t[h:h + 1, :]
                    decay = jnp.exp(jnp.where(causal, seg, neg_inf))
                    m = cb * decay * dt_t[h:h + 1, :]
                    eac = ea[:, h:h + 1] * cg
                    lhs_parts += [m.astype(BF16), eac.astype(BF16)]
                    rhs_parts += [xh, spb * keep]
                    bw_parts.append((bg_t * w_t[h:h + 1, :]).astype(BF16))
                    x_parts.append(xh)
                y_ref[r0:r0 + CHUNK, j * LANES:(j + 1) * LANES] = _dot(
                    jnp.concatenate(lhs_parts, axis=1), jnp.concatenate(rhs_parts, axis=0))
                snew = _dot(jnp.concatenate(bw_parts, axis=1), jnp.concatenate(x_parts, axis=0))
                s_ref[:, j * LANES:(j + 1) * LANES] = cdx[:, j * LANES:(j + 1) * LANES] * sp + snew
                slot = c * (NHEADS // 2) + j
                for i, filler in enumerate(fillers):
                    if slot_of_filler[i] == slot:
                        filler()
    sc_tail = sc_ext[tile + CARRY - 2:tile + CARRY, :]
    scst_ref[0] = sc_tail
    sc_ext[CARRY - 2:CARRY, :] = sc_tail

    gw = D_SSD // NGROUPS
    for grp in range(NGROUPS):
        ssq = jnp.zeros((tile, LANES), F32)
        for off in range(grp * gw, (grp + 1) * gw, LANES):
            yb = ((y_ref[:, off:off + LANES] + dx_ref[:, off:off + LANES] * act_ref[:, off:off + LANES])
                  * _silu(z_ref[:, off:off + LANES]))
            y_ref[:, off:off + LANES] = yb
            ssq = ssq + yb * yb
        scale = lax.rsqrt(jnp.sum(ssq, axis=-1, keepdims=True) * (1.0 / gw) + EPS)
        for off in range(grp * gw, (grp + 1) * gw, LANES):
            ycat_ref[:, off:off + LANES] = (y_ref[:, off:off + LANES] * scale
                                            * gnw_ref[:, off:off + LANES]).astype(BF16)
    ycat_ref[:, D_SSD:D_MIX] = ysc_ref[...]

    @pl.when(ends_sequence)
    def _():
        for j in range(NHEADS // 2):
            ssm_ref[0, 2 * j:2 * j + 2] = s_ref[:, j * LANES:(j + 1) * LANES].T.reshape(2, HEAD_DIM, NSTATE)


def _const_spec(shape):
    nd = len(shape)
    return pl.BlockSpec(shape, lambda *_: (0,) * nd, pipeline_mode=pl.Buffered(1))


CAST_FFN_STEPS = 32
CAST_DOWN_STEPS = 16


def _mixer_prompt(x, nw, win, cw, cb, dtb, a_pad, dx, gnw, scw, wout, expand, w_ffn_in, w_down):
    nb, seq, _ = x.shape
    tile = SEQ_TILE
    nt = seq // tile
    total = nb * nt
    assert total >= CAST_FFN_STEPS and total >= CAST_DOWN_STEPS
    ffn_rows = D_MODEL // CAST_FFN_STEPS
    down_rows = D_FF // CAST_DOWN_STEPS
    ffn_spec = pl.BlockSpec((ffn_rows, 2 * D_FF), lambda g: (jnp.minimum(g, CAST_FFN_STEPS - 1), 0))
    down_spec = pl.BlockSpec((down_rows, D_MODEL), lambda g: (jnp.minimum(g, CAST_DOWN_STEPS - 1), 0))

    def tile_of(g, lag):
        i = jnp.clip(g - lag, 0, total - 1)
        return i // nt, i % nt

    cur_spec = pl.BlockSpec((1, tile, D_MODEL), lambda g: (*tile_of(g, 0), 0))
    prev_spec = pl.BlockSpec((1, tile, D_MODEL), lambda g: (*tile_of(g, 1), 0))
    consts = (nw, win, cw, cb, dtb, a_pad, dx, gnw, scw, wout, expand)
    return pl.pallas_call(
        functools.partial(_mixer_prompt_kernel, nt),
        grid=(total + 1,),
        in_specs=[cur_spec, prev_spec, ffn_spec, down_spec] + [_const_spec(p.shape) for p in consts],
        out_specs=[
            prev_spec,
            pl.BlockSpec((1, NHEADS, HEAD_DIM, NSTATE), lambda g: (tile_of(g, 0)[0], 0, 0, 0)),
            pl.BlockSpec((1, SSD_CONV - 1, CONV_DIM), lambda g: (tile_of(g, 0)[0], 0, 0)),
            pl.BlockSpec((1, SC_CONV - 1, D_SC), lambda g: (tile_of(g, 0)[0], 0, 0)),
            ffn_spec,
            down_spec,
        ],
        out_shape=[
            jax.ShapeDtypeStruct((nb, seq, D_MODEL), F32),
            jax.ShapeDtypeStruct((nb, NHEADS, HEAD_DIM, NSTATE), F32),
            jax.ShapeDtypeStruct((nb, SSD_CONV - 1, CONV_DIM), F32),
            jax.ShapeDtypeStruct((nb, SC_CONV - 1, D_SC), F32),
            jax.ShapeDtypeStruct(w_ffn_in.shape, BF16),
            jax.ShapeDtypeStruct(w_down.shape, BF16),
        ],
        scratch_shapes=[
            pltpu.VMEM((tile + CARRY, CONV_DIM), F32),
            pltpu.VMEM((tile + CARRY, D_SC), F32),
            pltpu.VMEM((tile, D_MODEL), BF16),
            pltpu.VMEM((tile, CONV_DIM), F32),
            pltpu.VMEM((tile, DT_PAD), F32),
            pltpu.VMEM((tile, D_SSD), F32),
            pltpu.VMEM((tile, D_SC), F32),
            pltpu.VMEM((NSTATE, D_SSD), F32),
            pltpu.VMEM((tile, D_SSD), F32),
            pltpu.VMEM((tile, D_SC), BF16),
            pltpu.VMEM((tile, D_MIX), BF16),
        ],
        compiler_params=pltpu.CompilerParams(
            dimension_semantics=("arbitrary",), vmem_limit_bytes=VMEM_LIMIT),
        name="mixer_prompt",
    )(x, x, w_ffn_in, w_down, *consts)


STATE_HEADS = 2
STATE_SEQS = 64
STATE_ROWS = STATE_HEADS * HEAD_DIM


def _state_block_update(da_ref, step, st_ref, xdt_t_ref, b_ref, c_ref, nst_ref, yt_ref):
    seq_blocks = da_ref.shape[0] // STATE_SEQS
    seq0 = lax.rem(step, seq_blocks) * STATE_SEQS
    head0 = (step // seq_blocks) * STATE_HEADS
    c_blk = c_ref[0].astype(BF16)
    lane = lax.broadcasted_iota(jnp.int32, (STATE_ROWS, STATE_SEQS), 1)
    ycols = jnp.zeros((STATE_ROWS, STATE_SEQS), F32)
    for i in range(STATE_SEQS):
        upd = xdt_t_ref[0, :, i:i + 1] * b_ref[0, i:i + 1, :]
        parts = []
        for hh in range(STATE_HEADS):
            hnew = st_ref[i, hh] * da_ref[seq0 + i, head0 + hh] + upd[hh * HEAD_DIM:(hh + 1) * HEAD_DIM, :]
            nst_ref[i, hh] = hnew
            parts.append(hnew)
        prod = _dot_nt(jnp.concatenate(parts, axis=0).astype(BF16), c_blk)
        ycols = jnp.where(lane == i, prod, ycols)
    yt_ref[0] = ycols


def _ffn_prompt_kernel(state_steps, tiles_per_seq,
                       da_ref, x1_ref, st_ref, xdt_t_ref, b_ref, c_ref,
                       nw_ref, wffn_ref, fcw_ref, fcb_ref, wdown_ref, nfw_ref,
                       y_ref, fst_ref, nst_ref, yt_ref, gate_ext, x2_ref):
    tile = x1_ref.shape[1]
    step = pl.program_id(0)
    last = pl.num_programs(0) - 1

    @pl.when(step < state_steps)
    def _():
        _state_block_update(da_ref, step, st_ref, xdt_t_ref, b_ref, c_ref, nst_ref, yt_ref)

    @pl.when(step == 0)
    def _():
        x2_ref[...] = jnp.zeros_like(x2_ref)

    @pl.when(lax.rem(step, tiles_per_seq) == 0)
    def _():
        gate_ext[0:CARRY, :] = jnp.zeros((CARRY, D_FF), F32)

    def finish_previous():
        y_ref[0] = _rms_rows(x2_ref[...], nfw_ref[...])

    @pl.when(step == last)
    def _():
        finish_previous()

    @pl.when(step < last)
    def _():
        x1 = x1_ref[0]
        u = _rms_rows(x1, nw_ref[...]).astype(BF16)
        gate_ext[CARRY:CARRY + tile, :] = _dot(u, wffn_ref[:, 0:D_FF])
        up = _dot(u, wffn_ref[:, D_FF:2 * D_FF])
        g = fcb_ref[...] + gate_ext[CARRY - 2:CARRY - 2 + tile, :] * fcw_ref[0]
        for k in range(1, FFN_CONV):
            g = g + gate_ext[CARRY - 2 + k:CARRY - 2 + k + tile, :] * fcw_ref[k]
        tail = gate_ext[tile + CARRY - 2:tile + CARRY, :]
        fst_ref[0] = tail
        gate_ext[CARRY - 2:CARRY, :] = tail
        act = (_silu(g) * up).astype(BF16)
        finish_previous()
        x2_ref[...] = x1 + _dot(act, wdown_ref[...])


def _ffn_prompt(x1, nw, wffn, fcw, fcb, wdown, nfw, state, da, xdt_t, bc):
    nb, seq, _ = x1.shape
    tile = SEQ_TILE
    nt = seq // tile
    nseq = state.shape[0]
    seq_blocks = nseq // STATE_SEQS
    state_steps = seq_blocks * (NHEADS // STATE_HEADS)
    assert nb * nt >= state_steps
    heads_per_group = NHEADS // NGROUPS

    total = nb * nt

    def tile_of(g, lag):
        i = jnp.clip(g - lag, 0, total - 1)
        return i // nt, i % nt

    def block_of(g):
        s = jnp.minimum(g, state_steps - 1)
        return s % seq_blocks, s // seq_blocks

    def st_map(g):
        sb, hb = block_of(g)
        return (sb, hb, 0, 0)

    def col_map(g):
        sb, hb = block_of(g)
        return (sb, hb, 0)

    def b_map(g):
        sb, hb = block_of(g)
        return (hb * STATE_HEADS // heads_per_group, sb, 0)

    def c_map(g):
        sb, hb = block_of(g)
        return (NGROUPS + hb * STATE_HEADS // heads_per_group, sb, 0)

    st_spec = pl.BlockSpec((STATE_SEQS, STATE_HEADS, HEAD_DIM, NSTATE), st_map)
    col_spec = pl.BlockSpec((1, STATE_ROWS, STATE_SEQS), col_map)
    return pl.pallas_call(
        functools.partial(_ffn_prompt_kernel, state_steps, nt),
        grid=(total + 1,),
        in_specs=[pl.BlockSpec(memory_space=pltpu.SMEM),
                  pl.BlockSpec((1, tile, D_MODEL), lambda g: (*tile_of(g, 0), 0)), st_spec, col_spec,
                  pl.BlockSpec((1, STATE_SEQS, NSTATE), b_map), pl.BlockSpec((1, STATE_SEQS, NSTATE), c_map)]
        + [_const_spec(p.shape) for p in (nw, wffn, fcw, fcb, wdown, nfw)],
        out_specs=[pl.BlockSpec((1, tile, D_MODEL), lambda g: (*tile_of(g, 1), 0)),
                   pl.BlockSpec((1, FFN_CONV - 1, D_FF), lambda g: (tile_of(g, 0)[0], 0, 0)), st_spec, col_spec],
        out_shape=[jax.ShapeDtypeStruct((nb, seq, D_MODEL), F32),
                   jax.ShapeDtypeStruct((nb, FFN_CONV - 1, D_FF), F32),
                   jax.ShapeDtypeStruct(state.shape, F32),
                   jax.ShapeDtypeStruct(xdt_t.shape, F32)],
        scratch_shapes=[pltpu.VMEM((tile + CARRY, D_FF), F32),
                        pltpu.VMEM((tile, D_MODEL), F32)],
        compiler_params=pltpu.CompilerParams(
            dimension_semantics=("arbitrary",), vmem_limit_bytes=VMEM_LIMIT),
        name="ffn_prompt",
    )(da, x1, state, xdt_t, bc, bc, nw, wffn, fcw, fcb, wdown, nfw)


CAST_ROWS = 512


def _sample_pre_kernel(x_ref, nw_ref, wf32_ref, cw_ref, cb_ref, dt_bias_ref, a_log_ref, d_ref, scw_ref, e_ref,
                       cst_ref, scst_ref,
                       z_ref, xs_ref, xdt_t_ref, da_ref, bc_ref, ysc_ref, ncst_ref, nscst_ref, win_ref,
                       dtb_ref, a_ref, dx_ref):
    pad = jnp.zeros((1, DT_PAD - NHEADS), F32)
    dtb_ref[...] = jnp.concatenate([dt_bias_ref[...], pad], axis=1)
    a_ref[...] = jnp.concatenate([-jnp.exp(a_log_ref[...]), pad], axis=1)
    d_pad = jnp.broadcast_to(jnp.concatenate([d_ref[...], pad], axis=1), (8, DT_PAD))
    dx_ref[...] = _dot_sel_rhs(d_pad, e_ref[...])[0:1, :]
    for r0 in range(0, D_IN_PROJ, CAST_ROWS):
        r1 = min(r0 + CAST_ROWS, D_IN_PROJ)
        win_ref[r0:r1, :] = wf32_ref[r0:r1, :].astype(BF16)
    x = x_ref[:, 0, :]
    u = _rms_rows(x, nw_ref[...]).astype(BF16)
    z_ref[...] = _dot_nt(u, win_ref[OFF_Z:OFF_Z + D_SSD, :])
    xbc = _dot_nt(u, win_ref[OFF_XBC:OFF_XBC + CONV_DIM, :])
    conv = cb_ref[...] + xbc * cw_ref[SSD_CONV - 1:SSD_CONV, :]
    for k in range(SSD_CONV - 1):
        conv = conv + cst_ref[k] * cw_ref[k:k + 1, :]
    for k in range(SSD_CONV - 2):
        ncst_ref[k] = cst_ref[k + 1]
    ncst_ref[SSD_CONV - 2] = xbc
    act = _silu(conv)
    xs = act[:, 0:D_SSD]
    xs_ref[...] = xs
    for i in range(2 * NGROUPS):
        bc_ref[i] = act[:, D_SSD + i * NSTATE:D_SSD + (i + 1) * NSTATE]
    dt = _softplus(_dot_nt(u, win_ref[OFF_DT:OFF_DT + DT_PAD, :]) + dtb_ref[...])
    xdt_t = (xs * _dot_sel_rhs(dt, e_ref[...])).T
    for sb in range(xdt_t_ref.shape[0]):
        xdt_t_ref[sb] = xdt_t[:, sb * STATE_SEQS:(sb + 1) * STATE_SEQS]
    da_ref[...] = jnp.exp(dt * a_ref[...])[:, 0:NHEADS]
    gc = _dot_nt(u, win_ref[OFF_GC:OFF_GC + D_SC, :])
    hh = _dot_nt(u, win_ref[OFF_H:OFF_H + D_SC, :])
    gch = gc * hh
    sc = gch * scw_ref[SC_CONV - 1]
    for k in range(SC_CONV - 1):
        sc = sc + scst_ref[:, k, :] * scw_ref[k]
    for k in range(SC_CONV - 2):
        nscst_ref[:, k, :] = scst_ref[:, k + 1, :]
    nscst_ref[:, SC_CONV - 2, :] = gch
    gb = _dot_nt(u, win_ref[OFF_GB:OFF_GB + D_SC, :])
    ysc_ref[...] = gb * sc


def _sample_pre(x, nw, w_in_t, cw, cb, dt_bias, a_log, d, scw, expand, cst, scst):
    nb = x.shape[0]
    outs = [
        jax.ShapeDtypeStruct((nb, D_SSD), F32),
        jax.ShapeDtypeStruct((nb, D_SSD), F32),
        jax.ShapeDtypeStruct((nb // STATE_SEQS, D_SSD, STATE_SEQS), F32),
        jax.ShapeDtypeStruct((nb, NHEADS), F32),
        jax.ShapeDtypeStruct((2 * NGROUPS, nb, NSTATE), F32),
        jax.ShapeDtypeStruct((nb, D_SC), F32),
        jax.ShapeDtypeStruct((SSD_CONV - 1, nb, CONV_DIM), F32),
        jax.ShapeDtypeStruct((nb, SC_CONV - 1, D_SC), F32),
        jax.ShapeDtypeStruct((D_IN_PROJ, D_MODEL), BF16),
        jax.ShapeDtypeStruct((1, DT_PAD), F32),
        jax.ShapeDtypeStruct((1, DT_PAD), F32),
        jax.ShapeDtypeStruct((1, D_SSD), F32),
    ]
    return pl.pallas_call(
        _sample_pre_kernel,
        out_shape=outs,
        compiler_params=pltpu.CompilerParams(vmem_limit_bytes=VMEM_LIMIT),
        name="sample_pre",
    )(x, nw, w_in_t, cw, cb, dt_bias, a_log, d, scw, expand, cst, scst)


def _sample_post_kernel(x_ref, yt_ref, xs_ref, z_ref, ysc_ref, dx_ref, gnw_ref, wout_ref,
                        nw_ref, wffn_ref, fcw_ref, fcb_ref, wdown_ref, nfw_ref, fst_ref,
                        out_ref, nfst_ref):
    y_ssd = jnp.concatenate([yt_ref[sb] for sb in range(yt_ref.shape[0])], axis=1).T
    y = (y_ssd + dx_ref[...] * xs_ref[...]) * _silu(z_ref[...])
    gw = D_SSD // NGROUPS
    parts = [_rms_rows(y[:, g * gw:(g + 1) * gw], gnw_ref[:, g * gw:(g + 1) * gw]) for g in range(NGROUPS)]
    ycat = jnp.concatenate(parts + [ysc_ref[...]], axis=1).astype(BF16)
    x1 = x_ref[:, 0, :] + _dot(ycat, wout_ref[...])
    u = _rms_rows(x1, nw_ref[...]).astype(BF16)
    gate = _dot(u, wffn_ref[:, 0:D_FF])
    up = _dot(u, wffn_ref[:, D_FF:2 * D_FF])
    g = fcb_ref[...] + gate * fcw_ref[FFN_CONV - 1]
    for k in range(FFN_CONV - 1):
        g = g + fst_ref[:, k, :] * fcw_ref[k]
    for k in range(FFN_CONV - 2):
        nfst_ref[:, k, :] = fst_ref[:, k + 1, :]
    nfst_ref[:, FFN_CONV - 2, :] = gate
    act = (_silu(g) * up).astype(BF16)
    x2 = x1 + _dot(act, wdown_ref[...])
    out_ref[:, 0, :] = _rms_rows(x2, nfw_ref[...])


def _sample_post(x, y, xs, z, ysc, dx, gnw, wout, nw, wffn, fcw, fcb, wdown, nfw, fst):
    nb = x.shape[0]
    return pl.pallas_call(
        _sample_post_kernel,
        out_shape=[jax.ShapeDtypeStruct((nb, 1, D_MODEL), F32),
                   jax.ShapeDtypeStruct((nb, FFN_CONV - 1, D_FF), F32)],
        compiler_params=pltpu.CompilerParams(vmem_limit_bytes=VMEM_LIMIT),
        name="sample_post",
    )(x, y, xs, z, ysc, dx, gnw, wout, nw, wffn, fcw, fcb, wdown, nfw, fst)


def kernel(x_prompt, x_sample, state_ssm, state_ssd_conv, state_short_conv, state_ffn_conv,
           norm_mix_w, w_in, ssd_conv_w, ssd_conv_b, ssd_dt_bias, ssd_a_log, ssd_d, ssd_norm_w,
           sc_conv_w, w_out, norm_ffn_w, w_ffn_in, ffn_conv_w, ffn_conv_b, w_down, norm_final_w):
    depth = w_in.shape[0]
    assert depth == 1
    w_in_t = jnp.swapaxes(w_in[0], 0, 1)
    nw = norm_mix_w[0].reshape(1, D_MODEL)
    nw2 = norm_ffn_w[0].reshape(1, D_MODEL)
    nfw = norm_final_w.reshape(1, D_MODEL)
    cw = ssd_conv_w[0]
    cb = ssd_conv_b[0].reshape(1, CONV_DIM)
    gnw = ssd_norm_w[0].reshape(1, D_SSD)
    scw = jnp.swapaxes(sc_conv_w, 0, 1)
    fcw = jnp.swapaxes(ffn_conv_w, 0, 1)
    fcb = ffn_conv_b[0].reshape(1, D_FF)
    expand = jnp.asarray(np.arange(DT_PAD)[:, None] == np.arange(D_SSD)[None, :] // HEAD_DIM, dtype=BF16)

    xs_in = x_sample
    cst = jnp.swapaxes(state_ssd_conv[0], 0, 1)
    scst, fst = state_short_conv[0], state_ffn_conv[0]
    z, xs, xdt_t, da, bc, ysc, ncst, nscst, win, dtb, a_pad, dx = _sample_pre(
        xs_in, nw, w_in_t, cw, cb, ssd_dt_bias, ssd_a_log, ssd_d, scw, expand, cst, scst)

    wout = w_out[0].astype(BF16)
    x1, p_ssm, p_cst, p_scst, wffn, wdown = _mixer_prompt(
        x_prompt, nw, win, cw, cb, dtb, a_pad, dx, gnw, scw, wout, expand, w_ffn_in[0], w_down[0])
    y_prompt, p_fst, s_ssm, y_t = _ffn_prompt(x1, nw2, wffn, fcw, fcb, wdown, nfw, state_ssm[0], da, xdt_t, bc)
    y_sample, nfst = _sample_post(xs_in, y_t, xs, z, ysc, dx, gnw, wout, nw2, wffn, fcw, fcb, wdown, nfw, fst)

    return (y_prompt, y_sample,
            p_ssm[None], p_cst[None], p_scst[None], p_fst[None],
            s_ssm[None], jnp.swapaxes(ncst, 0, 1)[None], nscst[None], nfst[None])
```

```python
import functools

import jax
import jax.numpy as jnp
import numpy as np
from jax import lax
from jax.experimental import pallas as pl
from jax.experimental.pallas import tpu as pltpu

D_MODEL = 1024
D_SSD = 1024
D_SC = 1024
NHEADS = 16
HEAD_DIM = 64
NGROUPS = 2
NSTATE = 128
SSD_CONV = 4
CONV_DIM = D_SSD + 2 * NGROUPS * NSTATE
SC_CONV = 3
D_FF = 2816
FFN_CONV = 3
EPS = 1e-5
D_MIX = D_SSD + D_SC
D_IN_PROJ = D_SSD + CONV_DIM + NHEADS + 3 * D_SC

LANES = 128
MXU_COLS = 256
OFF_Z = 0
OFF_XBC = OFF_Z + D_SSD
OFF_DT = OFF_XBC + CONV_DIM
OFF_GB = OFF_DT + NHEADS
OFF_GC = OFF_GB + D_SC
OFF_H = OFF_GC + D_SC
DT_PAD = LANES

CHUNK = 128
CARRY = 8
SEQ_TILE = 512
VMEM_LIMIT = 56 * 1024 * 1024

F32 = jnp.float32
BF16 = jnp.bfloat16


def _dot(a, b):
    return jnp.dot(a, b, preferred_element_type=F32)


def _dot_nt(a, b_t):
    return lax.dot_general(a, b_t, (((1,), (1,)), ((), ())), preferred_element_type=F32)


def _split3(v):
    hi = v.astype(BF16)
    r = v - hi.astype(F32)
    mid = r.astype(BF16)
    lo = (r - mid.astype(F32)).astype(BF16)
    return hi, mid, lo


def _dot_sel_rhs(v, sel):
    hi, mid, lo = _split3(v)
    return (_dot(hi, sel) + _dot(mid, sel)) + _dot(lo, sel)


def _dot_sel_lhs(sel, v):
    hi, mid, lo = _split3(v)
    return (_dot(sel, hi) + _dot(sel, mid)) + _dot(sel, lo)


def _silu(v):
    h = 0.5 * v
    return h + h * jnp.tanh(h)


def _softplus(v):
    return jnp.maximum(v, 0.0) + jnp.log1p(jnp.exp(-jnp.abs(v)))


def _rms_rows(x, w):
    ms = jnp.mean(x * x, axis=-1, keepdims=True)
    return x * lax.rsqrt(ms + EPS) * w


def _mixer_prompt_kernel(tiles_per_seq,
                         x_ref, xres_ref, wffn_f32_ref, wdown_f32_ref,
                         nw_ref, win_ref, cw_ref, cb_ref, dtb_ref, a_ref, dx_ref,
                         gnw_ref, scw_ref, wout_ref, e_ref,
                         x1_ref, ssm_ref, cst_ref, scst_ref, wffn_bf_ref, wdown_bf_ref,
                         xbc_ext, sc_ext, u_ref, act_ref, dt_ref, z_ref, gb_ref, s_ref, y_ref, ysc_ref, ycat_ref):
    tile = x_ref.shape[1]
    g = pl.program_id(0)
    last = pl.num_programs(0) - 1
    t = lax.rem(g, tiles_per_seq)

    @pl.when(g < CAST_FFN_STEPS)
    def _():
        wffn_bf_ref[...] = wffn_f32_ref[...].astype(BF16)

    @pl.when(g < CAST_DOWN_STEPS)
    def _():
        wdown_bf_ref[...] = wdown_f32_ref[...].astype(BF16)

    @pl.when(g == 0)
    def _():
        ycat_ref[...] = jnp.zeros_like(ycat_ref)

    @pl.when(t == 0)
    def _():
        xbc_ext[0:CARRY, :] = jnp.zeros((CARRY, CONV_DIM), F32)
        sc_ext[0:CARRY, :] = jnp.zeros((CARRY, D_SC), F32)
        s_ref[...] = jnp.zeros_like(s_ref)

    def out_block(off):
        x1_ref[0, :, off:off + MXU_COLS] = xres_ref[0, :, off:off + MXU_COLS] + _dot(
            ycat_ref[...], wout_ref[:, off:off + MXU_COLS])

    @pl.when(g == last)
    def _():
        for off in range(0, D_MODEL, MXU_COLS):
            out_block(off)

    @pl.when(g < last)
    def _():
        _mixer_tile(tile, t == tiles_per_seq - 1, out_block,
                    x_ref, nw_ref, win_ref, cw_ref, cb_ref, dtb_ref, a_ref, dx_ref, gnw_ref, scw_ref, e_ref,
                    ssm_ref, cst_ref, scst_ref,
                    xbc_ext, sc_ext, u_ref, act_ref, dt_ref, z_ref, gb_ref, s_ref, y_ref, ysc_ref, ycat_ref)


def _mixer_tile(tile, ends_sequence, out_block,
                x_ref, nw_ref, win_ref, cw_ref, cb_ref, dtb_ref, a_ref, dx_ref, gnw_ref, scw_ref, e_ref,
                ssm_ref, cst_ref, scst_ref,
                xbc_ext, sc_ext, u_ref, act_ref, dt_ref, z_ref, gb_ref, s_ref, y_ref, ysc_ref, ycat_ref):
    u_ref[...] = _rms_rows(x_ref[0], nw_ref[...]).astype(BF16)

    def project(dst, row0, base, off, width=MXU_COLS):
        dst[row0:row0 + tile, off:off + width] = _dot_nt(u_ref[...], win_ref[base + off:base + off + width, :])

    def conv_block(off):
        conv = cb_ref[:, off:off + LANES] + (
            xbc_ext[CARRY - 3:CARRY - 3 + tile, off:off + LANES] * cw_ref[0:1, off:off + LANES])
        for k in range(1, SSD_CONV):
            conv = conv + (xbc_ext[CARRY - 3 + k:CARRY - 3 + k + tile, off:off + LANES]
                           * cw_ref[k:k + 1, off:off + LANES])
        act_ref[:, off:off + LANES] = _silu(conv)

    def sc_block(off):
        sc = sc_ext[CARRY - 2:CARRY - 2 + tile, off:off + LANES] * scw_ref[0, :, off:off + LANES]
        for k in range(1, SC_CONV):
            sc = sc + (sc_ext[CARRY - 2 + k:CARRY - 2 + k + tile, off:off + LANES]
                       * scw_ref[k, :, off:off + LANES])
        ysc_ref[:, off:off + LANES] = (gb_ref[:, off:off + LANES] * sc).astype(BF16)

    for off in range(0, CONV_DIM, MXU_COLS):
        project(xbc_ext, CARRY, OFF_XBC, off)
        for sub in range(off, off + MXU_COLS, LANES):
            conv_block(sub)
    tail = xbc_ext[tile + CARRY - 3:tile + CARRY, :]
    cst_ref[0] = tail
    xbc_ext[CARRY - 3:CARRY, :] = tail
    project(dt_ref, 0, OFF_DT, 0, DT_PAD)
    for off in range(0, D_SC, MXU_COLS):
        u = u_ref[...]
        sc_ext[CARRY:CARRY + tile, off:off + MXU_COLS] = (
            _dot_nt(u, win_ref[OFF_GC + off:OFF_GC + off + MXU_COLS, :])
            * _dot_nt(u, win_ref[OFF_H + off:OFF_H + off + MXU_COLS, :]))

    fillers = []
    for k in range(D_SC // MXU_COLS):
        fillers.append(lambda k=k: out_block(k * MXU_COLS))
        fillers.append(lambda k=k: project(gb_ref, 0, OFF_GB, k * MXU_COLS))
        fillers.append(lambda k=k: sc_block(k * MXU_COLS))
        fillers.append(lambda k=k: sc_block(k * MXU_COLS + LANES))
        fillers.append(lambda k=k: project(z_ref, 0, OFF_Z, k * MXU_COLS))
    n_slots = (tile // CHUNK) * (NHEADS // 2)
    slot_of_filler = [(i * n_slots) // len(fillers) for i in range(len(fillers))]

    row_i = lax.broadcasted_iota(jnp.int32, (CHUNK, CHUNK), 0)
    col_i = lax.broadcasted_iota(jnp.int32, (CHUNK, CHUNK), 1)
    causal = row_i >= col_i
    tri = causal.astype(BF16)
    lane_i = lax.broadcasted_iota(jnp.int32, (1, LANES), 1)
    keeps = ((lane_i < HEAD_DIM).astype(BF16), (lane_i >= HEAD_DIM).astype(BF16))
    neg_inf = jnp.float32(-jnp.inf)
    pairs_per_group = NHEADS // NGROUPS // 2

    for c in range(tile // CHUNK):
        r0 = c * CHUNK
        dt = _softplus(dt_ref[r0:r0 + CHUNK, :] + dtb_ref[...])
        acum = _dot_sel_lhs(tri, dt * a_ref[...])
        acum_t = acum.T
        dt_t = dt.T
        ea = jnp.exp(acum)
        w_t = jnp.exp(acum_t[:, CHUNK - 1:CHUNK] - acum_t) * dt_t
        cd = jnp.broadcast_to(ea[CHUNK - 1:CHUNK, :], (8, LANES))
        cdx = _dot_sel_rhs(cd, e_ref[...])[0:1, :]
        for grp in range(NGROUPS):
            bg = act_ref[r0:r0 + CHUNK, D_SSD + grp * NSTATE:D_SSD + (grp + 1) * NSTATE]
            cg = act_ref[r0:r0 + CHUNK, D_SSD + (NGROUPS + grp) * NSTATE:D_SSD + (NGROUPS + grp + 1) * NSTATE]
            cb = lax.dot_general(cg.astype(BF16), bg.astype(BF16), (((1,), (1,)), ((), ())),
                                 preferred_element_type=F32)
            bg_t = bg.T
            for jp in range(pairs_per_group):
                j = grp * pairs_per_group + jp
                xp = act_ref[r0:r0 + CHUNK, j * LANES:(j + 1) * LANES].astype(BF16)
                sp = s_ref[:, j * LANES:(j + 1) * LANES]
                spb = sp.astype(BF16)
                lhs_parts, rhs_parts, bw_parts, x_parts = [], [], [], []
                for half, keep in enumerate(keeps):
                    h = 2 * j + half
                    xh = xp * keep
                    seg = acum[:, h:h + 1] - acum_t[h:h + 1, :]
                    decay = jnp.exp(jnp.where(causal, seg, neg_inf))
                    m = cb * decay * dt_t[h:h + 1, :]
                    eac = ea[:, h:h + 1] * cg
                    lhs_parts += [m.astype(BF16), eac.astype(BF16)]
                    rhs_parts += [xh, spb * keep]
                    bw_parts.append((bg_t * w_t[h:h + 1, :]).astype(BF16))
                    x_parts.append(xh)
                y_ref[r0:r0 + CHUNK, j * LANES:(j + 1) * LANES] = _dot(
                    jnp.concatenate(lhs_parts, axis=1), jnp.concatenate(rhs_parts, axis=0))
                snew = _dot(jnp.concatenate(bw_parts, axis=1), jnp.concatenate(x_parts, axis=0))
                s_ref[:, j * LANES:(j + 1) * LANES] = cdx[:, j * LANES:(j + 1) * LANES] * sp + snew
                slot = c * (NHEADS // 2) + j
                for i, filler in enumerate(fillers):
                    if slot_of_filler[i] == slot:
                        filler()
    sc_tail = sc_ext[tile + CARRY - 2:tile + CARRY, :]
    scst_ref[0] = sc_tail
    sc_ext[CARRY - 2:CARRY, :] = sc_tail

    gw = D_SSD // NGROUPS
    for grp in range(NGROUPS):
        ssq = jnp.zeros((tile, LANES), F32)
        for off in range(grp * gw, (grp + 1) * gw, LANES):
            yb = ((y_ref[:, off:off + LANES] + dx_ref[:, off:off + LANES] * act_ref[:, off:off + LANES])
                  * _silu(z_ref[:, off:off + LANES]))
            y_ref[:, off:off + LANES] = yb
            ssq = ssq + yb * yb
        scale = lax.rsqrt(jnp.sum(ssq, axis=-1, keepdims=True) * (1.0 / gw) + EPS)
        for off in range(grp * gw, (grp + 1) * gw, LANES):
            ycat_ref[:, off:off + LANES] = (y_ref[:, off:off + LANES] * scale
                                            * gnw_ref[:, off:off + LANES]).astype(BF16)
    ycat_ref[:, D_SSD:D_MIX] = ysc_ref[...]

    @pl.when(ends_sequence)
    def _():
        for j in range(NHEADS // 2):
            ssm_ref[0, 2 * j:2 * j + 2] = s_ref[:, j * LANES:(j + 1) * LANES].T.reshape(2, HEAD_DIM, NSTATE)


def _const_spec(shape):
    nd = len(shape)
    return pl.BlockSpec(shape, lambda *_: (0,) * nd, pipeline_mode=pl.Buffered(1))


CAST_FFN_STEPS = 32
CAST_DOWN_STEPS = 16


def _mixer_prompt(x, nw, win, cw, cb, dtb, a_pad, dx, gnw, scw, wout, expand, w_ffn_in, w_down):
    nb, seq, _ = x.shape
    tile = SEQ_TILE
    nt = seq // tile
    total = nb * nt
    assert total >= CAST_FFN_STEPS and total >= CAST_DOWN_STEPS
    ffn_rows = D_MODEL // CAST_FFN_STEPS
    down_rows = D_FF // CAST_DOWN_STEPS
    ffn_spec = pl.BlockSpec((ffn_rows, 2 * D_FF), lambda g: (jnp.minimum(g, CAST_FFN_STEPS - 1), 0))
    down_spec = pl.BlockSpec((down_rows, D_MODEL), lambda g: (jnp.minimum(g, CAST_DOWN_STEPS - 1), 0))

    def tile_of(g, lag):
        i = jnp.clip(g - lag, 0, total - 1)
        return i // nt, i % nt

    cur_spec = pl.BlockSpec((1, tile, D_MODEL), lambda g: (*tile_of(g, 0), 0))
    prev_spec = pl.BlockSpec((1, tile, D_MODEL), lambda g: (*tile_of(g, 1), 0))
    consts = (nw, win, cw, cb, dtb, a_pad, dx, gnw, scw, wout, expand)
    return pl.pallas_call(
        functools.partial(_mixer_prompt_kernel, nt),
        grid=(total + 1,),
        in_specs=[cur_spec, prev_spec, ffn_spec, down_spec] + [_const_spec(p.shape) for p in consts],
        out_specs=[
            prev_spec,
            pl.BlockSpec((1, NHEADS, HEAD_DIM, NSTATE), lambda g: (tile_of(g, 0)[0], 0, 0, 0)),
            pl.BlockSpec((1, SSD_CONV - 1, CONV_DIM), lambda g: (tile_of(g, 0)[0], 0, 0)),
            pl.BlockSpec((1, SC_CONV - 1, D_SC), lambda g: (tile_of(g, 0)[0], 0, 0)),
            ffn_spec,
            down_spec,
        ],
        out_shape=[
            jax.ShapeDtypeStruct((nb, seq, D_MODEL), F32),
            jax.ShapeDtypeStruct((nb, NHEADS, HEAD_DIM, NSTATE), F32),
            jax.ShapeDtypeStruct((nb, SSD_CONV - 1, CONV_DIM), F32),
            jax.ShapeDtypeStruct((nb, SC_CONV - 1, D_SC), F32),
            jax.ShapeDtypeStruct(w_ffn_in.shape, BF16),
            jax.ShapeDtypeStruct(w_down.shape, BF16),
        ],
        scratch_shapes=[
            pltpu.VMEM((tile + CARRY, CONV_DIM), F32),
            pltpu.VMEM((tile + CARRY, D_SC), F32),
            pltpu.VMEM((tile, D_MODEL), BF16),
            pltpu.VMEM((tile, CONV_DIM), F32),
            pltpu.VMEM((tile, DT_PAD), F32),
            pltpu.VMEM((tile, D_SSD), F32),
            pltpu.VMEM((tile, D_SC), F32),
            pltpu.VMEM((NSTATE, D_SSD), F32),
            pltpu.VMEM((tile, D_SSD), F32),
            pltpu.VMEM((tile, D_SC), BF16),
            pltpu.VMEM((tile, D_MIX), BF16),
        ],
        compiler_params=pltpu.CompilerParams(
            dimension_semantics=("arbitrary",), vmem_limit_bytes=VMEM_LIMIT),
        name="mixer_prompt",
    )(x, x, w_ffn_in, w_down, *consts)


STATE_HEADS = 2
STATE_SEQS = 64
STATE_ROWS = STATE_HEADS * HEAD_DIM


def _state_block_update(da_ref, step, st_ref, xdt_t_ref, b_ref, c_ref, nst_ref, yt_ref):
    seq_blocks = da_ref.shape[0] // STATE_SEQS
    seq0 = lax.rem(step, seq_blocks) * STATE_SEQS
    head0 = (step // seq_blocks) * STATE_HEADS
    c_blk = c_ref[0].astype(BF16)
    lane = lax.broadcasted_iota(jnp.int32, (STATE_ROWS, STATE_SEQS), 1)
    ycols = jnp.zeros((STATE_ROWS, STATE_SEQS), F32)
    for i in range(STATE_SEQS):
        upd = xdt_t_ref[0, :, i:i + 1] * b_ref[0, i:i + 1, :]
        parts = []
        for hh in range(STATE_HEADS):
            hnew = st_ref[i, hh] * da_ref[seq0 + i, head0 + hh] + upd[hh * HEAD_DIM:(hh + 1) * HEAD_DIM, :]
            nst_ref[i, hh] = hnew
            parts.append(hnew)
        prod = _dot_nt(jnp.concatenate(parts, axis=0).astype(BF16), c_blk)
        ycols = jnp.where(lane == i, prod, ycols)
    yt_ref[0] = ycols


def _ffn_prompt_kernel(state_steps,
                       da_ref, x1_ref, st_ref, xdt_t_ref, b_ref, c_ref,
                       nw_ref, wffn_ref, fcw_ref, fcb_ref, wdown_ref, nfw_ref,
                       y_ref, fst_ref, nst_ref, yt_ref, gate_ext):
    tile = x1_ref.shape[1]
    t = pl.program_id(1)
    step = pl.program_id(0) * pl.num_programs(1) + t

    @pl.when(step < state_steps)
    def _():
        _state_block_update(da_ref, step, st_ref, xdt_t_ref, b_ref, c_ref, nst_ref, yt_ref)

    @pl.when(t == 0)
    def _():
        gate_ext[0:CARRY, :] = jnp.zeros((CARRY, D_FF), F32)

    x1 = x1_ref[0]
    u = _rms_rows(x1, nw_ref[...]).astype(BF16)
    gate_ext[CARRY:CARRY + tile, :] = _dot(u, wffn_ref[:, 0:D_FF])
    up = _dot(u, wffn_ref[:, D_FF:2 * D_FF])
    g = fcb_ref[...] + gate_ext[CARRY - 2:CARRY - 2 + tile, :] * fcw_ref[0]
    for k in range(1, FFN_CONV):
        g = g + gate_ext[CARRY - 2 + k:CARRY - 2 + k + tile, :] * fcw_ref[k]
    tail = gate_ext[tile + CARRY - 2:tile + CARRY, :]
    fst_ref[0] = tail
    gate_ext[CARRY - 2:CARRY, :] = tail
    act = (_silu(g) * up).astype(BF16)
    x2 = x1 + _dot(act, wdown_ref[...])
    y_ref[0] = _rms_rows(x2, nfw_ref[...])


def _ffn_prompt(x1, nw, wffn, fcw, fcb, wdown, nfw, state, da, xdt_t, bc):
    nb, seq, _ = x1.shape
    tile = SEQ_TILE
    nt = seq // tile
    nseq = state.shape[0]
    seq_blocks = nseq // STATE_SEQS
    state_steps = seq_blocks * (NHEADS // STATE_HEADS)
    assert nb * nt >= state_steps
    heads_per_group = NHEADS // NGROUPS

    def block_of(b, t):
        s = jnp.minimum(b * nt + t, state_steps - 1)
        return s % seq_blocks, s // seq_blocks

    def st_map(b, t):
        sb, hb = block_of(b, t)
        return (sb, hb, 0, 0)

    def col_map(b, t):
        sb, hb = block_of(b, t)
        return (sb, hb, 0)

    def b_map(b, t):
        sb, hb = block_of(b, t)
        return (hb * STATE_HEADS // heads_per_group, sb, 0)

    def c_map(b, t):
        sb, hb = block_of(b, t)
        return (NGROUPS + hb * STATE_HEADS // heads_per_group, sb, 0)

    tok_spec = pl.BlockSpec((1, tile, D_MODEL), lambda b, t: (b, t, 0))
    st_spec = pl.BlockSpec((STATE_SEQS, STATE_HEADS, HEAD_DIM, NSTATE), st_map)
    col_spec = pl.BlockSpec((1, STATE_ROWS, STATE_SEQS), col_map)
    return pl.pallas_call(
        functools.partial(_ffn_prompt_kernel, state_steps),
        grid=(nb, nt),
        in_specs=[pl.BlockSpec(memory_space=pltpu.SMEM), tok_spec, st_spec, col_spec,
                  pl.BlockSpec((1, STATE_SEQS, NSTATE), b_map), pl.BlockSpec((1, STATE_SEQS, NSTATE), c_map)]
        + [_const_spec(p.shape) for p in (nw, wffn, fcw, fcb, wdown, nfw)],
        out_specs=[tok_spec, pl.BlockSpec((1, FFN_CONV - 1, D_FF), lambda b, t: (b, 0, 0)), st_spec, col_spec],
        out_shape=[jax.ShapeDtypeStruct((nb, seq, D_MODEL), F32),
                   jax.ShapeDtypeStruct((nb, FFN_CONV - 1, D_FF), F32),
                   jax.ShapeDtypeStruct(state.shape, F32),
                   jax.ShapeDtypeStruct(xdt_t.shape, F32)],
        scratch_shapes=[pltpu.VMEM((tile + CARRY, D_FF), F32)],
        compiler_params=pltpu.CompilerParams(
            dimension_semantics=("arbitrary", "arbitrary"), vmem_limit_bytes=VMEM_LIMIT),
        name="ffn_prompt",
    )(da, x1, state, xdt_t, bc, bc, nw, wffn, fcw, fcb, wdown, nfw)


CAST_ROWS = 512


def _sample_pre_kernel(x_ref, nw_ref, wf32_ref, cw_ref, cb_ref, dt_bias_ref, a_log_ref, d_ref, scw_ref, e_ref,
                       cst_ref, scst_ref,
                       z_ref, xs_ref, xdt_t_ref, da_ref, bc_ref, ysc_ref, ncst_ref, nscst_ref, win_ref,
                       dtb_ref, a_ref, dx_ref):
    pad = jnp.zeros((1, DT_PAD - NHEADS), F32)
    dtb_ref[...] = jnp.concatenate([dt_bias_ref[...], pad], axis=1)
    a_ref[...] = jnp.concatenate([-jnp.exp(a_log_ref[...]), pad], axis=1)
    d_pad = jnp.broadcast_to(jnp.concatenate([d_ref[...], pad], axis=1), (8, DT_PAD))
    dx_ref[...] = _dot_sel_rhs(d_pad, e_ref[...])[0:1, :]
    for r0 in range(0, D_IN_PROJ, CAST_ROWS):
        r1 = min(r0 + CAST_ROWS, D_IN_PROJ)
        win_ref[r0:r1, :] = wf32_ref[r0:r1, :].astype(BF16)
    x = x_ref[:, 0, :]
    u = _rms_rows(x, nw_ref[...]).astype(BF16)
    z_ref[...] = _dot_nt(u, win_ref[OFF_Z:OFF_Z + D_SSD, :])
    xbc = _dot_nt(u, win_ref[OFF_XBC:OFF_XBC + CONV_DIM, :])
    conv = cb_ref[...] + xbc * cw_ref[SSD_CONV - 1:SSD_CONV, :]
    for k in range(SSD_CONV - 1):
        conv = conv + cst_ref[k] * cw_ref[k:k + 1, :]
    for k in range(SSD_CONV - 2):
        ncst_ref[k] = cst_ref[k + 1]
    ncst_ref[SSD_CONV - 2] = xbc
    act = _silu(conv)
    xs = act[:, 0:D_SSD]
    xs_ref[...] = xs
    for i in range(2 * NGROUPS):
        bc_ref[i] = act[:, D_SSD + i * NSTATE:D_SSD + (i + 1) * NSTATE]
    dt = _softplus(_dot_nt(u, win_ref[OFF_DT:OFF_DT + DT_PAD, :]) + dtb_ref[...])
    xdt_t = (xs * _dot_sel_rhs(dt, e_ref[...])).T
    for sb in range(xdt_t_ref.shape[0]):
        xdt_t_ref[sb] = xdt_t[:, sb * STATE_SEQS:(sb + 1) * STATE_SEQS]
    da_ref[...] = jnp.exp(dt * a_ref[...])[:, 0:NHEADS]
    gc = _dot_nt(u, win_ref[OFF_GC:OFF_GC + D_SC, :])
    hh = _dot_nt(u, win_ref[OFF_H:OFF_H + D_SC, :])
    gch = gc * hh
    sc = gch * scw_ref[SC_CONV - 1]
    for k in range(SC_CONV - 1):
        sc = sc + scst_ref[:, k, :] * scw_ref[k]
    for k in range(SC_CONV - 2):
        nscst_ref[:, k, :] = scst_ref[:, k + 1, :]
    nscst_ref[:, SC_CONV - 2, :] = gch
    gb = _dot_nt(u, win_ref[OFF_GB:OFF_GB + D_SC, :])
    ysc_ref[...] = gb * sc


def _sample_pre(x, nw, w_in_t, cw, cb, dt_bias, a_log, d, scw, expand, cst, scst):
    nb = x.shape[0]
    outs = [
        jax.ShapeDtypeStruct((nb, D_SSD), F32),
        jax.ShapeDtypeStruct((nb, D_SSD), F32),
        jax.ShapeDtypeStruct((nb // STATE_SEQS, D_SSD, STATE_SEQS), F32),
        jax.ShapeDtypeStruct((nb, NHEADS), F32),
        jax.ShapeDtypeStruct((2 * NGROUPS, nb, NSTATE), F32),
        jax.ShapeDtypeStruct((nb, D_SC), F32),
        jax.ShapeDtypeStruct((SSD_CONV - 1, nb, CONV_DIM), F32),
        jax.ShapeDtypeStruct((nb, SC_CONV - 1, D_SC), F32),
        jax.ShapeDtypeStruct((D_IN_PROJ, D_MODEL), BF16),
        jax.ShapeDtypeStruct((1, DT_PAD), F32),
        jax.ShapeDtypeStruct((1, DT_PAD), F32),
        jax.ShapeDtypeStruct((1, D_SSD), F32),
    ]
    return pl.pallas_call(
        _sample_pre_kernel,
        out_shape=outs,
        compiler_params=pltpu.CompilerParams(vmem_limit_bytes=VMEM_LIMIT),
        name="sample_pre",
    )(x, nw, w_in_t, cw, cb, dt_bias, a_log, d, scw, expand, cst, scst)


def _sample_post_kernel(x_ref, yt_ref, xs_ref, z_ref, ysc_ref, dx_ref, gnw_ref, wout_hbm,
                        nw_ref, wffn_hbm, fcw_ref, fcb_ref, wdown_hbm, nfw_ref, fst_ref,
                        out_ref, nfst_ref,
                        wout_ref, wffn_ref, wdown_ref, sems):
    copies = [pltpu.make_async_copy(src, dst, sems.at[i]) for i, (src, dst) in enumerate(
        ((wout_hbm, wout_ref), (wffn_hbm, wffn_ref), (wdown_hbm, wdown_ref)))]
    for copy in copies:
        copy.start()
    wout_copy, wffn_copy, wdown_copy = copies
    y_ssd = jnp.concatenate([yt_ref[sb] for sb in range(yt_ref.shape[0])], axis=1).T
    y = (y_ssd + dx_ref[...] * xs_ref[...]) * _silu(z_ref[...])
    gw = D_SSD // NGROUPS
    parts = [_rms_rows(y[:, g * gw:(g + 1) * gw], gnw_ref[:, g * gw:(g + 1) * gw]) for g in range(NGROUPS)]
    ycat = jnp.concatenate(parts + [ysc_ref[...]], axis=1).astype(BF16)
    wout_copy.wait()
    x1 = x_ref[:, 0, :] + _dot(ycat, wout_ref[...])
    u = _rms_rows(x1, nw_ref[...]).astype(BF16)
    wffn_copy.wait()
    gate = _dot(u, wffn_ref[:, 0:D_FF])
    up = _dot(u, wffn_ref[:, D_FF:2 * D_FF])
    g = fcb_ref[...] + gate * fcw_ref[FFN_CONV - 1]
    for k in range(FFN_CONV - 1):
        g = g + fst_ref[:, k, :] * fcw_ref[k]
    for k in range(FFN_CONV - 2):
        nfst_ref[:, k, :] = fst_ref[:, k + 1, :]
    nfst_ref[:, FFN_CONV - 2, :] = gate
    act = (_silu(g) * up).astype(BF16)
    wdown_copy.wait()
    x2 = x1 + _dot(act, wdown_ref[...])
    out_ref[:, 0, :] = _rms_rows(x2, nfw_ref[...])


def _sample_post(x, y, xs, z, ysc, dx, gnw, wout, nw, wffn, fcw, fcb, wdown, nfw, fst):
    nb = x.shape[0]
    in_vmem = pl.BlockSpec(memory_space=pltpu.VMEM)
    in_hbm = pl.BlockSpec(memory_space=pl.ANY)
    return pl.pallas_call(
        _sample_post_kernel,
        in_specs=[in_vmem] * 7 + [in_hbm, in_vmem, in_hbm, in_vmem, in_vmem, in_hbm, in_vmem, in_vmem],
        out_shape=[jax.ShapeDtypeStruct((nb, 1, D_MODEL), F32),
                   jax.ShapeDtypeStruct((nb, FFN_CONV - 1, D_FF), F32)],
        scratch_shapes=[pltpu.VMEM(wout.shape, BF16), pltpu.VMEM(wffn.shape, BF16), pltpu.VMEM(wdown.shape, BF16),
                        pltpu.SemaphoreType.DMA((3,))],
        compiler_params=pltpu.CompilerParams(vmem_limit_bytes=VMEM_LIMIT),
        name="sample_post",
    )(x, y, xs, z, ysc, dx, gnw, wout, nw, wffn, fcw, fcb, wdown, nfw, fst)


def kernel(x_prompt, x_sample, state_ssm, state_ssd_conv, state_short_conv, state_ffn_conv,
           norm_mix_w, w_in, ssd_conv_w, ssd_conv_b, ssd_dt_bias, ssd_a_log, ssd_d, ssd_norm_w,
           sc_conv_w, w_out, norm_ffn_w, w_ffn_in, ffn_conv_w, ffn_conv_b, w_down, norm_final_w):
    depth = w_in.shape[0]
    assert depth == 1
    w_in_t = jnp.swapaxes(w_in[0], 0, 1)
    nw = norm_mix_w[0].reshape(1, D_MODEL)
    nw2 = norm_ffn_w[0].reshape(1, D_MODEL)
    nfw = norm_final_w.reshape(1, D_MODEL)
    cw = ssd_conv_w[0]
    cb = ssd_conv_b[0].reshape(1, CONV_DIM)
    gnw = ssd_norm_w[0].reshape(1, D_SSD)
    scw = jnp.swapaxes(sc_conv_w, 0, 1)
    fcw = jnp.swapaxes(ffn_conv_w, 0, 1)
    fcb = ffn_conv_b[0].reshape(1, D_FF)
    expand = jnp.asarray(np.arange(DT_PAD)[:, None] == np.arange(D_SSD)[None, :] // HEAD_DIM, dtype=BF16)

    xs_in = x_sample
    cst = jnp.swapaxes(state_ssd_conv[0], 0, 1)
    scst, fst = state_short_conv[0], state_ffn_conv[0]
    z, xs, xdt_t, da, bc, ysc, ncst, nscst, win, dtb, a_pad, dx = _sample_pre(
        xs_in, nw, w_in_t, cw, cb, ssd_dt_bias, ssd_a_log, ssd_d, scw, expand, cst, scst)

    wout = w_out[0].astype(BF16)
    x1, p_ssm, p_cst, p_scst, wffn, wdown = _mixer_prompt(
        x_prompt, nw, win, cw, cb, dtb, a_pad, dx, gnw, scw, wout, expand, w_ffn_in[0], w_down[0])
    y_prompt, p_fst, s_ssm, y_t = _ffn_prompt(x1, nw2, wffn, fcw, fcb, wdown, nfw, state_ssm[0], da, xdt_t, bc)
    y_sample, nfst = _sample_post(xs_in, y_t, xs, z, ysc, dx, gnw, wout, nw2, wffn, fcw, fcb, wdown, nfw, fst)

    return (y_prompt, y_sample,
            p_ssm[None], p_cst[None], p_scst[None], p_fst[None],
            s_ssm[None], jnp.swapaxes(ncst, 0, 1)[None], nscst[None], nfst[None])
```

```python
import functools

import jax
import jax.numpy as jnp
import numpy as np
from jax import lax
from jax.experimental import pallas as pl
from jax.experimental.pallas import tpu as pltpu

D_MODEL = 1024
D_SSD = 1024
D_SC = 1024
NHEADS = 16
HEAD_DIM = 64
NGROUPS = 2
NSTATE = 128
SSD_CONV = 4
CONV_DIM = D_SSD + 2 * NGROUPS * NSTATE
SC_CONV = 3
D_FF = 2816
FFN_CONV = 3
EPS = 1e-5
D_MIX = D_SSD + D_SC
D_IN_PROJ = D_SSD + CONV_DIM + NHEADS + 3 * D_SC

LANES = 128
MXU_COLS = 256
OFF_Z = 0
OFF_XBC = OFF_Z + D_SSD
OFF_DT = OFF_XBC + CONV_DIM
OFF_GB = OFF_DT + NHEADS
OFF_GC = OFF_GB + D_SC
OFF_H = OFF_GC + D_SC
DT_PAD = LANES

CHUNK = 128
CARRY = 8
SEQ_TILE = 512
VMEM_LIMIT = 56 * 1024 * 1024

F32 = jnp.float32
BF16 = jnp.bfloat16


def _dot(a, b):
    return jnp.dot(a, b, preferred_element_type=F32)


def _dot_nt(a, b_t):
    return lax.dot_general(a, b_t, (((1,), (1,)), ((), ())), preferred_element_type=F32)


def _split3(v):
    hi = v.astype(BF16)
    r = v - hi.astype(F32)
    mid = r.astype(BF16)
    lo = (r - mid.astype(F32)).astype(BF16)
    return hi, mid, lo


def _dot_sel_rhs(v, sel):
    hi, mid, lo = _split3(v)
    return (_dot(hi, sel) + _dot(mid, sel)) + _dot(lo, sel)


def _dot_sel_lhs(sel, v):
    hi, mid, lo = _split3(v)
    return (_dot(sel, hi) + _dot(sel, mid)) + _dot(sel, lo)


def _silu(v):
    h = 0.5 * v
    return h + h * jnp.tanh(h)


def _softplus(v):
    return jnp.maximum(v, 0.0) + jnp.log1p(jnp.exp(-jnp.abs(v)))


def _rms_rows(x, w):
    ms = jnp.mean(x * x, axis=-1, keepdims=True)
    return x * lax.rsqrt(ms + EPS) * w


def _mixer_prompt_kernel(tiles_per_seq,
                         x_ref, xres_ref, wffn_f32_ref, wdown_f32_ref,
                         nw_ref, win_ref, cw_ref, cb_ref, dtb_ref, a_ref, dx_ref,
                         gnw_ref, scw_ref, wout_ref, e_ref,
                         x1_ref, ssm_ref, cst_ref, scst_ref, wffn_bf_ref, wdown_bf_ref,
                         xbc_ext, sc_ext, u_ref, act_ref, dt_ref, z_ref, gb_ref, s_ref, y_ref, ysc_ref, ycat_ref):
    tile = x_ref.shape[1]
    g = pl.program_id(0)
    last = pl.num_programs(0) - 1
    t = lax.rem(g, tiles_per_seq)

    @pl.when(g < CAST_FFN_STEPS)
    def _():
        wffn_bf_ref[...] = wffn_f32_ref[...].astype(BF16)

    @pl.when(g < CAST_DOWN_STEPS)
    def _():
        wdown_bf_ref[...] = wdown_f32_ref[...].astype(BF16)

    @pl.when(g == 0)
    def _():
        ycat_ref[...] = jnp.zeros_like(ycat_ref)

    @pl.when(t == 0)
    def _():
        xbc_ext[0:CARRY, :] = jnp.zeros((CARRY, CONV_DIM), F32)
        sc_ext[0:CARRY, :] = jnp.zeros((CARRY, D_SC), F32)
        s_ref[...] = jnp.zeros_like(s_ref)

    def out_block(off):
        x1_ref[0, :, off:off + MXU_COLS] = xres_ref[0, :, off:off + MXU_COLS] + _dot(
            ycat_ref[...], wout_ref[:, off:off + MXU_COLS])

    @pl.when(g == last)
    def _():
        for off in range(0, D_MODEL, MXU_COLS):
            out_block(off)

    @pl.when(g < last)
    def _():
        _mixer_tile(tile, t == tiles_per_seq - 1, out_block,
                    x_ref, nw_ref, win_ref, cw_ref, cb_ref, dtb_ref, a_ref, dx_ref, gnw_ref, scw_ref, e_ref,
                    ssm_ref, cst_ref, scst_ref,
                    xbc_ext, sc_ext, u_ref, act_ref, dt_ref, z_ref, gb_ref, s_ref, y_ref, ysc_ref, ycat_ref)


def _mixer_tile(tile, ends_sequence, out_block,
                x_ref, nw_ref, win_ref, cw_ref, cb_ref, dtb_ref, a_ref, dx_ref, gnw_ref, scw_ref, e_ref,
                ssm_ref, cst_ref, scst_ref,
                xbc_ext, sc_ext, u_ref, act_ref, dt_ref, z_ref, gb_ref, s_ref, y_ref, ysc_ref, ycat_ref):
    u_ref[...] = _rms_rows(x_ref[0], nw_ref[...]).astype(BF16)

    def project(dst, row0, base, off, width=MXU_COLS):
        dst[row0:row0 + tile, off:off + width] = _dot_nt(u_ref[...], win_ref[base + off:base + off + width, :])

    def conv_block(off):
        conv = cb_ref[:, off:off + LANES] + (
            xbc_ext[CARRY - 3:CARRY - 3 + tile, off:off + LANES] * cw_ref[0:1, off:off + LANES])
        for k in range(1, SSD_CONV):
            conv = conv + (xbc_ext[CARRY - 3 + k:CARRY - 3 + k + tile, off:off + LANES]
                           * cw_ref[k:k + 1, off:off + LANES])
        act_ref[:, off:off + LANES] = _silu(conv)

    def sc_block(off):
        sc = sc_ext[CARRY - 2:CARRY - 2 + tile, off:off + LANES] * scw_ref[0, :, off:off + LANES]
        for k in range(1, SC_CONV):
            sc = sc + (sc_ext[CARRY - 2 + k:CARRY - 2 + k + tile, off:off + LANES]
                       * scw_ref[k, :, off:off + LANES])
        ysc_ref[:, off:off + LANES] = (gb_ref[:, off:off + LANES] * sc).astype(BF16)

    for off in range(0, CONV_DIM, MXU_COLS):
        project(xbc_ext, CARRY, OFF_XBC, off)
        for sub in range(off, off + MXU_COLS, LANES):
            conv_block(sub)
    tail = xbc_ext[tile + CARRY - 3:tile + CARRY, :]
    cst_ref[0] = tail
    xbc_ext[CARRY - 3:CARRY, :] = tail
    project(dt_ref, 0, OFF_DT, 0, DT_PAD)
    for off in range(0, D_SC, MXU_COLS):
        u = u_ref[...]
        sc_ext[CARRY:CARRY + tile, off:off + MXU_COLS] = (
            _dot_nt(u, win_ref[OFF_GC + off:OFF_GC + off + MXU_COLS, :])
            * _dot_nt(u, win_ref[OFF_H + off:OFF_H + off + MXU_COLS, :]))

    fillers = []
    for k in range(D_SC // MXU_COLS):
        fillers.append(lambda k=k: out_block(k * MXU_COLS))
        fillers.append(lambda k=k: project(gb_ref, 0, OFF_GB, k * MXU_COLS))
        fillers.append(lambda k=k: sc_block(k * MXU_COLS))
        fillers.append(lambda k=k: sc_block(k * MXU_COLS + LANES))
        fillers.append(lambda k=k: project(z_ref, 0, OFF_Z, k * MXU_COLS))
    n_slots = (tile // CHUNK) * (NHEADS // 2)
    slot_of_filler = [(i * n_slots) // len(fillers) for i in range(len(fillers))]

    row_i = lax.broadcasted_iota(jnp.int32, (CHUNK, CHUNK), 0)
    col_i = lax.broadcasted_iota(jnp.int32, (CHUNK, CHUNK), 1)
    causal = row_i >= col_i
    tri = causal.astype(BF16)
    lane_i = lax.broadcasted_iota(jnp.int32, (1, LANES), 1)
    keeps = ((lane_i < HEAD_DIM).astype(BF16), (lane_i >= HEAD_DIM).astype(BF16))
    neg_inf = jnp.float32(-jnp.inf)
    pairs_per_group = NHEADS // NGROUPS // 2

    for c in range(tile // CHUNK):
        r0 = c * CHUNK
        dt = _softplus(dt_ref[r0:r0 + CHUNK, :] + dtb_ref[...])
        acum = _dot_sel_lhs(tri, dt * a_ref[...])
        acum_t = acum.T
        dt_t = dt.T
        ea = jnp.exp(acum)
        w_t = jnp.exp(acum_t[:, CHUNK - 1:CHUNK] - acum_t) * dt_t
        cd = jnp.broadcast_to(ea[CHUNK - 1:CHUNK, :], (8, LANES))
        cdx = _dot_sel_rhs(cd, e_ref[...])[0:1, :]
        for grp in range(NGROUPS):
            bg = act_ref[r0:r0 + CHUNK, D_SSD + grp * NSTATE:D_SSD + (grp + 1) * NSTATE]
            cg = act_ref[r0:r0 + CHUNK, D_SSD + (NGROUPS + grp) * NSTATE:D_SSD + (NGROUPS + grp + 1) * NSTATE]
            cb = lax.dot_general(cg.astype(BF16), bg.astype(BF16), (((1,), (1,)), ((), ())),
                                 preferred_element_type=F32)
            bg_t = bg.T
            for jp in range(pairs_per_group):
                j = grp * pairs_per_group + jp
                xp = act_ref[r0:r0 + CHUNK, j * LANES:(j + 1) * LANES].astype(BF16)
                sp = s_ref[:, j * LANES:(j + 1) * LANES]
                spb = sp.astype(BF16)
                lhs_parts, rhs_parts, bw_parts, x_parts = [], [], [], []
                for half, keep in enumerate(keeps):
                    h = 2 * j + half
                    xh = xp * keep
                    seg = acum[:, h:h + 1] - acum_t[h:h + 1, :]
                    decay = jnp.exp(jnp.where(causal, seg, neg_inf))
                    m = cb * decay * dt_t[h:h + 1, :]
                    eac = ea[:, h:h + 1] * cg
                    lhs_parts += [m.astype(BF16), eac.astype(BF16)]
                    rhs_parts += [xh, spb * keep]
                    bw_parts.append((bg_t * w_t[h:h + 1, :]).astype(BF16))
                    x_parts.append(xh)
                y_ref[r0:r0 + CHUNK, j * LANES:(j + 1) * LANES] = _dot(
                    jnp.concatenate(lhs_parts, axis=1), jnp.concatenate(rhs_parts, axis=0))
                snew = _dot(jnp.concatenate(bw_parts, axis=1), jnp.concatenate(x_parts, axis=0))
                s_ref[:, j * LANES:(j + 1) * LANES] = cdx[:, j * LANES:(j + 1) * LANES] * sp + snew
                slot = c * (NHEADS // 2) + j
                for i, filler in enumerate(fillers):
                    if slot_of_filler[i] == slot:
                        filler()
    sc_tail = sc_ext[tile + CARRY - 2:tile + CARRY, :]
    scst_ref[0] = sc_tail
    sc_ext[CARRY - 2:CARRY, :] = sc_tail

    gw = D_SSD // NGROUPS
    for grp in range(NGROUPS):
        ssq = jnp.zeros((tile, LANES), F32)
        for off in range(grp * gw, (grp + 1) * gw, LANES):
            yb = ((y_ref[:, off:off + LANES] + dx_ref[:, off:off + LANES] * act_ref[:, off:off + LANES])
                  * _silu(z_ref[:, off:off + LANES]))
            y_ref[:, off:off + LANES] = yb
            ssq = ssq + yb * yb
        scale = lax.rsqrt(jnp.sum(ssq, axis=-1, keepdims=True) * (1.0 / gw) + EPS)
        for off in range(grp * gw, (grp + 1) * gw, LANES):
            ycat_ref[:, off:off + LANES] = (y_ref[:, off:off + LANES] * scale
                                            * gnw_ref[:, off:off + LANES]).astype(BF16)
    ycat_ref[:, D_SSD:D_MIX] = ysc_ref[...]

    @pl.when(ends_sequence)
    def _():
        for j in range(NHEADS // 2):
            ssm_ref[0, 2 * j:2 * j + 2] = s_ref[:, j * LANES:(j + 1) * LANES].T.reshape(2, HEAD_DIM, NSTATE)


def _const_spec(shape):
    nd = len(shape)
    return pl.BlockSpec(shape, lambda *_: (0,) * nd, pipeline_mode=pl.Buffered(1))


CAST_FFN_STEPS = 32
CAST_DOWN_STEPS = 16


def _mixer_prompt(x, nw, win, cw, cb, dtb, a_pad, dx, gnw, scw, wout, expand, w_ffn_in, w_down):
    nb, seq, _ = x.shape
    tile = SEQ_TILE
    nt = seq // tile
    total = nb * nt
    assert total >= CAST_FFN_STEPS and total >= CAST_DOWN_STEPS
    ffn_rows = D_MODEL // CAST_FFN_STEPS
    down_rows = D_FF // CAST_DOWN_STEPS
    ffn_spec = pl.BlockSpec((ffn_rows, 2 * D_FF), lambda g: (jnp.minimum(g, CAST_FFN_STEPS - 1), 0))
    down_spec = pl.BlockSpec((down_rows, D_MODEL), lambda g: (jnp.minimum(g, CAST_DOWN_STEPS - 1), 0))

    def tile_of(g, lag):
        i = jnp.clip(g - lag, 0, total - 1)
        return i // nt, i % nt

    cur_spec = pl.BlockSpec((1, tile, D_MODEL), lambda g: (*tile_of(g, 0), 0))
    prev_spec = pl.BlockSpec((1, tile, D_MODEL), lambda g: (*tile_of(g, 1), 0))
    consts = (nw, win, cw, cb, dtb, a_pad, dx, gnw, scw, wout, expand)
    return pl.pallas_call(
        functools.partial(_mixer_prompt_kernel, nt),
        grid=(total + 1,),
        in_specs=[cur_spec, prev_spec, ffn_spec, down_spec] + [_const_spec(p.shape) for p in consts],
        out_specs=[
            prev_spec,
            pl.BlockSpec((1, NHEADS, HEAD_DIM, NSTATE), lambda g: (tile_of(g, 0)[0], 0, 0, 0)),
            pl.BlockSpec((1, SSD_CONV - 1, CONV_DIM), lambda g: (tile_of(g, 0)[0], 0, 0)),
            pl.BlockSpec((1, SC_CONV - 1, D_SC), lambda g: (tile_of(g, 0)[0], 0, 0)),
            ffn_spec,
            down_spec,
        ],
        out_shape=[
            jax.ShapeDtypeStruct((nb, seq, D_MODEL), F32),
            jax.ShapeDtypeStruct((nb, NHEADS, HEAD_DIM, NSTATE), F32),
            jax.ShapeDtypeStruct((nb, SSD_CONV - 1, CONV_DIM), F32),
            jax.ShapeDtypeStruct((nb, SC_CONV - 1, D_SC), F32),
            jax.ShapeDtypeStruct(w_ffn_in.shape, BF16),
            jax.ShapeDtypeStruct(w_down.shape, BF16),
        ],
        scratch_shapes=[
            pltpu.VMEM((tile + CARRY, CONV_DIM), F32),
            pltpu.VMEM((tile + CARRY, D_SC), F32),
            pltpu.VMEM((tile, D_MODEL), BF16),
            pltpu.VMEM((tile, CONV_DIM), F32),
            pltpu.VMEM((tile, DT_PAD), F32),
            pltpu.VMEM((tile, D_SSD), F32),
            pltpu.VMEM((tile, D_SC), F32),
            pltpu.VMEM((NSTATE, D_SSD), F32),
            pltpu.VMEM((tile, D_SSD), F32),
            pltpu.VMEM((tile, D_SC), BF16),
            pltpu.VMEM((tile, D_MIX), BF16),
        ],
        compiler_params=pltpu.CompilerParams(
            dimension_semantics=("arbitrary",), vmem_limit_bytes=VMEM_LIMIT),
        name="mixer_prompt",
    )(x, x, w_ffn_in, w_down, *consts)


STATE_HEADS = 2
STATE_SEQS = 64
STATE_ROWS = STATE_HEADS * HEAD_DIM


def _state_block_update(da_ref, step, st_ref, xdt_t_ref, b_ref, c_ref, nst_ref, yt_ref):
    seq_blocks = da_ref.shape[0] // STATE_SEQS
    seq0 = lax.rem(step, seq_blocks) * STATE_SEQS
    head0 = (step // seq_blocks) * STATE_HEADS
    c_blk = c_ref[0].astype(BF16)
    lane = lax.broadcasted_iota(jnp.int32, (STATE_ROWS, STATE_SEQS), 1)
    ycols = jnp.zeros((STATE_ROWS, STATE_SEQS), F32)
    for i in range(STATE_SEQS):
        upd = xdt_t_ref[0, :, i:i + 1] * b_ref[0, i:i + 1, :]
        parts = []
        for hh in range(STATE_HEADS):
            hnew = st_ref[i, hh] * da_ref[seq0 + i, head0 + hh] + upd[hh * HEAD_DIM:(hh + 1) * HEAD_DIM, :]
            nst_ref[i, hh] = hnew
            parts.append(hnew)
        prod = _dot_nt(jnp.concatenate(parts, axis=0).astype(BF16), c_blk)
        ycols = jnp.where(lane == i, prod, ycols)
    yt_ref[0] = ycols


def _ffn_prompt_kernel(state_steps,
                       da_ref, x1_ref, st_ref, xdt_t_ref, b_ref, c_ref,
                       nw_ref, wffn_ref, fcw_ref, fcb_ref, wdown_ref, nfw_ref,
                       y_ref, fst_ref, nst_ref, yt_ref, gate_ext):
    tile = x1_ref.shape[1]
    t = pl.program_id(1)
    step = pl.program_id(0) * pl.num_programs(1) + t

    @pl.when(step < state_steps)
    def _():
        _state_block_update(da_ref, step, st_ref, xdt_t_ref, b_ref, c_ref, nst_ref, yt_ref)

    @pl.when(t == 0)
    def _():
        gate_ext[0:CARRY, :] = jnp.zeros((CARRY, D_FF), F32)

    x1 = x1_ref[0]
    u = _rms_rows(x1, nw_ref[...]).astype(BF16)
    gate_ext[CARRY:CARRY + tile, :] = _dot(u, wffn_ref[:, 0:D_FF])
    up = _dot(u, wffn_ref[:, D_FF:2 * D_FF])
    g = fcb_ref[...] + gate_ext[CARRY - 2:CARRY - 2 + tile, :] * fcw_ref[0]
    for k in range(1, FFN_CONV):
        g = g + gate_ext[CARRY - 2 + k:CARRY - 2 + k + tile, :] * fcw_ref[k]
    tail = gate_ext[tile + CARRY - 2:tile + CARRY, :]
    fst_ref[0] = tail
    gate_ext[CARRY - 2:CARRY, :] = tail
    act = (_silu(g) * up).astype(BF16)
    x2 = x1 + _dot(act, wdown_ref[...])
    y_ref[0] = _rms_rows(x2, nfw_ref[...])


def _ffn_prompt(x1, nw, wffn, fcw, fcb, wdown, nfw, state, da, xdt_t, bc):
    nb, seq, _ = x1.shape
    tile = SEQ_TILE
    nt = seq // tile
    nseq = state.shape[0]
    seq_blocks = nseq // STATE_SEQS
    state_steps = seq_blocks * (NHEADS // STATE_HEADS)
    assert nb * nt >= state_steps
    heads_per_group = NHEADS // NGROUPS

    def block_of(b, t):
        s = jnp.minimum(b * nt + t, state_steps - 1)
        return s % seq_blocks, s // seq_blocks

    def st_map(b, t):
        sb, hb = block_of(b, t)
        return (sb, hb, 0, 0)

    def col_map(b, t):
        sb, hb = block_of(b, t)
        return (sb, hb, 0)

    def b_map(b, t):
        sb, hb = block_of(b, t)
        return (hb * STATE_HEADS // heads_per_group, sb, 0)

    def c_map(b, t):
        sb, hb = block_of(b, t)
        return (NGROUPS + hb * STATE_HEADS // heads_per_group, sb, 0)

    tok_spec = pl.BlockSpec((1, tile, D_MODEL), lambda b, t: (b, t, 0))
    st_spec = pl.BlockSpec((STATE_SEQS, STATE_HEADS, HEAD_DIM, NSTATE), st_map)
    col_spec = pl.BlockSpec((1, STATE_ROWS, STATE_SEQS), col_map)
    return pl.pallas_call(
        functools.partial(_ffn_prompt_kernel, state_steps),
        grid=(nb, nt),
        in_specs=[pl.BlockSpec(memory_space=pltpu.SMEM), tok_spec, st_spec, col_spec,
                  pl.BlockSpec((1, STATE_SEQS, NSTATE), b_map), pl.BlockSpec((1, STATE_SEQS, NSTATE), c_map)]
        + [_const_spec(p.shape) for p in (nw, wffn, fcw, fcb, wdown, nfw)],
        out_specs=[tok_spec, pl.BlockSpec((1, FFN_CONV - 1, D_FF), lambda b, t: (b, 0, 0)), st_spec, col_spec],
        out_shape=[jax.ShapeDtypeStruct((nb, seq, D_MODEL), F32),
                   jax.ShapeDtypeStruct((nb, FFN_CONV - 1, D_FF), F32),
                   jax.ShapeDtypeStruct(state.shape, F32),
                   jax.ShapeDtypeStruct(xdt_t.shape, F32)],
        scratch_shapes=[pltpu.VMEM((tile + CARRY, D_FF), F32)],
        compiler_params=pltpu.CompilerParams(
            dimension_semantics=("arbitrary", "arbitrary"), vmem_limit_bytes=VMEM_LIMIT),
        name="ffn_prompt",
    )(da, x1, state, xdt_t, bc, bc, nw, wffn, fcw, fcb, wdown, nfw)


CAST_ROWS = 512


def _sample_pre_kernel(x_ref, nw_ref, wf32_hbm, cw_ref, cb_ref, dt_bias_ref, a_log_ref, d_ref, scw_ref, e_ref,
                       cst_ref, scst_ref,
                       z_ref, xs_ref, xdt_t_ref, da_ref, bc_ref, ysc_ref, ncst_ref, nscst_ref, win_ref,
                       dtb_ref, a_ref, dx_ref,
                       stage_ref, sems):
    starts = list(range(0, D_IN_PROJ, CAST_ROWS))

    def chunk_copy(k):
        rows = min(CAST_ROWS, D_IN_PROJ - starts[k])
        return pltpu.make_async_copy(wf32_hbm.at[pl.ds(starts[k], rows), :],
                                     stage_ref.at[k % 2, pl.ds(0, rows), :], sems.at[k % 2])

    copies = [chunk_copy(k) for k in range(len(starts))]
    copies[0].start()
    pad = jnp.zeros((1, DT_PAD - NHEADS), F32)
    dtb_ref[...] = jnp.concatenate([dt_bias_ref[...], pad], axis=1)
    a_ref[...] = jnp.concatenate([-jnp.exp(a_log_ref[...]), pad], axis=1)
    d_pad = jnp.broadcast_to(jnp.concatenate([d_ref[...], pad], axis=1), (8, DT_PAD))
    dx_ref[...] = _dot_sel_rhs(d_pad, e_ref[...])[0:1, :]
    x = x_ref[:, 0, :]
    u = _rms_rows(x, nw_ref[...]).astype(BF16)

    sections = {"z": (OFF_Z, D_SSD), "xbc": (OFF_XBC, CONV_DIM), "dt": (OFF_DT, DT_PAD),
                "gb": (OFF_GB, D_SC), "gc": (OFF_GC, D_SC), "h": (OFF_H, D_SC)}
    proj = {}
    for k, r0 in enumerate(starts):
        if k + 1 < len(starts):
            copies[k + 1].start()
        copies[k].wait()
        r1 = min(r0 + CAST_ROWS, D_IN_PROJ)
        win_ref[r0:r1, :] = stage_ref[k % 2, 0:r1 - r0, :].astype(BF16)
        for name, (off, width) in sections.items():
            if name not in proj and off + width <= r1:
                proj[name] = _dot_nt(u, win_ref[off:off + width, :])
    z_ref[...] = proj["z"]
    xbc = proj["xbc"]
    conv = cb_ref[...] + xbc * cw_ref[SSD_CONV - 1:SSD_CONV, :]
    for k in range(SSD_CONV - 1):
        conv = conv + cst_ref[k] * cw_ref[k:k + 1, :]
    for k in range(SSD_CONV - 2):
        ncst_ref[k] = cst_ref[k + 1]
    ncst_ref[SSD_CONV - 2] = xbc
    act = _silu(conv)
    xs = act[:, 0:D_SSD]
    xs_ref[...] = xs
    for i in range(2 * NGROUPS):
        bc_ref[i] = act[:, D_SSD + i * NSTATE:D_SSD + (i + 1) * NSTATE]
    dt = _softplus(proj["dt"] + dtb_ref[...])
    xdt_t = (xs * _dot_sel_rhs(dt, e_ref[...])).T
    for sb in range(xdt_t_ref.shape[0]):
        xdt_t_ref[sb] = xdt_t[:, sb * STATE_SEQS:(sb + 1) * STATE_SEQS]
    da_ref[...] = jnp.exp(dt * a_ref[...])[:, 0:NHEADS]
    gch = proj["gc"] * proj["h"]
    sc = gch * scw_ref[SC_CONV - 1]
    for k in range(SC_CONV - 1):
        sc = sc + scst_ref[:, k, :] * scw_ref[k]
    for k in range(SC_CONV - 2):
        nscst_ref[:, k, :] = scst_ref[:, k + 1, :]
    nscst_ref[:, SC_CONV - 2, :] = gch
    ysc_ref[...] = proj["gb"] * sc


def _sample_pre(x, nw, w_in_t, cw, cb, dt_bias, a_log, d, scw, expand, cst, scst):
    nb = x.shape[0]
    outs = [
        jax.ShapeDtypeStruct((nb, D_SSD), F32),
        jax.ShapeDtypeStruct((nb, D_SSD), F32),
        jax.ShapeDtypeStruct((nb // STATE_SEQS, D_SSD, STATE_SEQS), F32),
        jax.ShapeDtypeStruct((nb, NHEADS), F32),
        jax.ShapeDtypeStruct((2 * NGROUPS, nb, NSTATE), F32),
        jax.ShapeDtypeStruct((nb, D_SC), F32),
        jax.ShapeDtypeStruct((SSD_CONV - 1, nb, CONV_DIM), F32),
        jax.ShapeDtypeStruct((nb, SC_CONV - 1, D_SC), F32),
        jax.ShapeDtypeStruct((D_IN_PROJ, D_MODEL), BF16),
        jax.ShapeDtypeStruct((1, DT_PAD), F32),
        jax.ShapeDtypeStruct((1, DT_PAD), F32),
        jax.ShapeDtypeStruct((1, D_SSD), F32),
    ]
    in_vmem = pl.BlockSpec(memory_space=pltpu.VMEM)
    in_hbm = pl.BlockSpec(memory_space=pl.ANY)
    return pl.pallas_call(
        _sample_pre_kernel,
        in_specs=[in_vmem, in_vmem, in_hbm] + [in_vmem] * 9,
        out_shape=outs,
        scratch_shapes=[pltpu.VMEM((2, CAST_ROWS, D_MODEL), F32),
                        pltpu.SemaphoreType.DMA((2,))],
        compiler_params=pltpu.CompilerParams(vmem_limit_bytes=VMEM_LIMIT),
        name="sample_pre",
    )(x, nw, w_in_t, cw, cb, dt_bias, a_log, d, scw, expand, cst, scst)


def _sample_post_kernel(x_ref, yt_ref, xs_ref, z_ref, ysc_ref, dx_ref, gnw_ref, wout_hbm,
                        nw_ref, wffn_hbm, fcw_ref, fcb_ref, wdown_hbm, nfw_ref, fst_ref,
                        out_ref, nfst_ref,
                        wout_ref, wffn_ref, wdown_ref, sems):
    copies = [pltpu.make_async_copy(src, dst, sems.at[i]) for i, (src, dst) in enumerate(
        ((wout_hbm, wout_ref), (wffn_hbm, wffn_ref), (wdown_hbm, wdown_ref)))]
    for copy in copies:
        copy.start()
    wout_copy, wffn_copy, wdown_copy = copies
    y_ssd = jnp.concatenate([yt_ref[sb] for sb in range(yt_ref.shape[0])], axis=1).T
    y = (y_ssd + dx_ref[...] * xs_ref[...]) * _silu(z_ref[...])
    gw = D_SSD // NGROUPS
    parts = [_rms_rows(y[:, g * gw:(g + 1) * gw], gnw_ref[:, g * gw:(g + 1) * gw]) for g in range(NGROUPS)]
    ycat = jnp.concatenate(parts + [ysc_ref[...]], axis=1).astype(BF16)
    wout_copy.wait()
    x1 = x_ref[:, 0, :] + _dot(ycat, wout_ref[...])
    u = _rms_rows(x1, nw_ref[...]).astype(BF16)
    wffn_copy.wait()
    gate = _dot(u, wffn_ref[:, 0:D_FF])
    up = _dot(u, wffn_ref[:, D_FF:2 * D_FF])
    g = fcb_ref[...] + gate * fcw_ref[FFN_CONV - 1]
    for k in range(FFN_CONV - 1):
        g = g + fst_ref[:, k, :] * fcw_ref[k]
    for k in range(FFN_CONV - 2):
        nfst_ref[:, k, :] = fst_ref[:, k + 1, :]
    nfst_ref[:, FFN_CONV - 2, :] = gate
    act = (_silu(g) * up).astype(BF16)
    wdown_copy.wait()
    x2 = x1 + _dot(act, wdown_ref[...])
    out_ref[:, 0, :] = _rms_rows(x2, nfw_ref[...])


def _sample_post(x, y, xs, z, ysc, dx, gnw, wout, nw, wffn, fcw, fcb, wdown, nfw, fst):
    nb = x.shape[0]
    in_vmem = pl.BlockSpec(memory_space=pltpu.VMEM)
    in_hbm = pl.BlockSpec(memory_space=pl.ANY)
    return pl.pallas_call(
        _sample_post_kernel,
        in_specs=[in_vmem] * 7 + [in_hbm, in_vmem, in_hbm, in_vmem, in_vmem, in_hbm, in_vmem, in_vmem],
        out_shape=[jax.ShapeDtypeStruct((nb, 1, D_MODEL), F32),
                   jax.ShapeDtypeStruct((nb, FFN_CONV - 1, D_FF), F32)],
        scratch_shapes=[pltpu.VMEM(wout.shape, BF16), pltpu.VMEM(wffn.shape, BF16), pltpu.VMEM(wdown.shape, BF16),
                        pltpu.SemaphoreType.DMA((3,))],
        compiler_params=pltpu.CompilerParams(vmem_limit_bytes=VMEM_LIMIT),
        name="sample_post",
    )(x, y, xs, z, ysc, dx, gnw, wout, nw, wffn, fcw, fcb, wdown, nfw, fst)


def kernel(x_prompt, x_sample, state_ssm, state_ssd_conv, state_short_conv, state_ffn_conv,
           norm_mix_w, w_in, ssd_conv_w, ssd_conv_b, ssd_dt_bias, ssd_a_log, ssd_d, ssd_norm_w,
           sc_conv_w, w_out, norm_ffn_w, w_ffn_in, ffn_conv_w, ffn_conv_b, w_down, norm_final_w):
    depth = w_in.shape[0]
    assert depth == 1
    w_in_t = jnp.swapaxes(w_in[0], 0, 1)
    nw = norm_mix_w[0].reshape(1, D_MODEL)
    nw2 = norm_ffn_w[0].reshape(1, D_MODEL)
    nfw = norm_final_w.reshape(1, D_MODEL)
    cw = ssd_conv_w[0]
    cb = ssd_conv_b[0].reshape(1, CONV_DIM)
    gnw = ssd_norm_w[0].reshape(1, D_SSD)
    scw = jnp.swapaxes(sc_conv_w, 0, 1)
    fcw = jnp.swapaxes(ffn_conv_w, 0, 1)
    fcb = ffn_conv_b[0].reshape(1, D_FF)
    expand = jnp.asarray(np.arange(DT_PAD)[:, None] == np.arange(D_SSD)[None, :] // HEAD_DIM, dtype=BF16)

    xs_in = x_sample
    cst = jnp.swapaxes(state_ssd_conv[0], 0, 1)
    scst, fst = state_short_conv[0], state_ffn_conv[0]
    z, xs, xdt_t, da, bc, ysc, ncst, nscst, win, dtb, a_pad, dx = _sample_pre(
        xs_in, nw, w_in_t, cw, cb, ssd_dt_bias, ssd_a_log, ssd_d, scw, expand, cst, scst)

    wout = w_out[0].astype(BF16)
    x1, p_ssm, p_cst, p_scst, wffn, wdown = _mixer_prompt(
        x_prompt, nw, win, cw, cb, dtb, a_pad, dx, gnw, scw, wout, expand, w_ffn_in[0], w_down[0])
    y_prompt, p_fst, s_ssm, y_t = _ffn_prompt(x1, nw2, wffn, fcw, fcb, wdown, nfw, state_ssm[0], da, xdt_t, bc)
    y_sample, nfst = _sample_post(xs_in, y_t, xs, z, ysc, dx, gnw, wout, nw2, wffn, fcw, fcb, wdown, nfw, fst)

    return (y_prompt, y_sample,
            p_ssm[None], p_cst[None], p_scst[None], p_fst[None],
            s_ssm[None], jnp.swapaxes(ncst, 0, 1)[None], nscst[None], nfst[None])
```

```python
import functools

import jax
import jax.numpy as jnp
import numpy as np
from jax import lax
from jax.experimental import pallas as pl
from jax.experimental.pallas import tpu as pltpu

D_MODEL = 1024
D_SSD = 1024
D_SC = 1024
NHEADS = 16
HEAD_DIM = 64
NGROUPS = 2
NSTATE = 128
SSD_CONV = 4
CONV_DIM = D_SSD + 2 * NGROUPS * NSTATE
SC_CONV = 3
D_FF = 2816
FFN_CONV = 3
EPS = 1e-5
D_MIX = D_SSD + D_SC
D_IN_PROJ = D_SSD + CONV_DIM + NHEADS + 3 * D_SC

LANES = 128
MXU_COLS = 256
OFF_Z = 0
OFF_XBC = OFF_Z + D_SSD
OFF_DT = OFF_XBC + CONV_DIM
OFF_GB = OFF_DT + NHEADS
OFF_GC = OFF_GB + D_SC
OFF_H = OFF_GC + D_SC
DT_PAD = LANES

CHUNK = 128
CARRY = 8
SEQ_TILE = 512
VMEM_LIMIT = 56 * 1024 * 1024

F32 = jnp.float32
BF16 = jnp.bfloat16


def _dot(a, b):
    return jnp.dot(a, b, preferred_element_type=F32)


def _dot_nt(a, b_t):
    return lax.dot_general(a, b_t, (((1,), (1,)), ((), ())), preferred_element_type=F32)


def _split3(v):
    hi = v.astype(BF16)
    r = v - hi.astype(F32)
    mid = r.astype(BF16)
    lo = (r - mid.astype(F32)).astype(BF16)
    return hi, mid, lo


def _dot_sel_rhs(v, sel):
    hi, mid, lo = _split3(v)
    return (_dot(hi, sel) + _dot(mid, sel)) + _dot(lo, sel)


def _dot_sel_lhs(sel, v):
    hi, mid, lo = _split3(v)
    return (_dot(sel, hi) + _dot(sel, mid)) + _dot(sel, lo)


def _silu(v):
    h = 0.5 * v
    return h + h * jnp.tanh(h)


def _softplus(v):
    return jnp.maximum(v, 0.0) + jnp.log1p(jnp.exp(-jnp.abs(v)))


def _rms_rows(x, w):
    ms = jnp.mean(x * x, axis=-1, keepdims=True)
    return x * lax.rsqrt(ms + EPS) * w


def _mixer_prompt_kernel(tiles_per_seq,
                         x_ref, xres_ref, wffn_f32_ref, wdown_f32_ref,
                         nw_ref, win_ref, cw_ref, cb_ref, dtb_ref, a_ref, dx_ref,
                         gnw_ref, scw_ref, wout_ref, e_ref,
                         x1_ref, ssm_ref, cst_ref, scst_ref, wffn_bf_ref, wdown_bf_ref,
                         xbc_ext, sc_ext, u_ref, act_ref, dt_ref, z_ref, gb_ref, s_ref, y_ref, ysc_ref, ycat_ref):
    tile = x_ref.shape[1]
    g = pl.program_id(0)
    last = pl.num_programs(0) - 1
    t = lax.rem(g, tiles_per_seq)

    @pl.when(g < CAST_FFN_STEPS)
    def _():
        wffn_bf_ref[...] = wffn_f32_ref[...].astype(BF16)

    @pl.when(g < CAST_DOWN_STEPS)
    def _():
        wdown_bf_ref[...] = wdown_f32_ref[...].astype(BF16)

    @pl.when(g == 0)
    def _():
        ycat_ref[...] = jnp.zeros_like(ycat_ref)

    @pl.when(t == 0)
    def _():
        xbc_ext[0:CARRY, :] = jnp.zeros((CARRY, CONV_DIM), F32)
        sc_ext[0:CARRY, :] = jnp.zeros((CARRY, D_SC), F32)
        s_ref[...] = jnp.zeros_like(s_ref)

    def out_block(off):
        x1_ref[0, :, off:off + MXU_COLS] = xres_ref[0, :, off:off + MXU_COLS] + _dot(
            ycat_ref[...], wout_ref[:, off:off + MXU_COLS])

    @pl.when(g == last)
    def _():
        for off in range(0, D_MODEL, MXU_COLS):
            out_block(off)

    @pl.when(g < last)
    def _():
        _mixer_tile(tile, t == tiles_per_seq - 1, out_block,
                    x_ref, nw_ref, win_ref, cw_ref, cb_ref, dtb_ref, a_ref, dx_ref, gnw_ref, scw_ref, e_ref,
                    ssm_ref, cst_ref, scst_ref,
                    xbc_ext, sc_ext, u_ref, act_ref, dt_ref, z_ref, gb_ref, s_ref, y_ref, ysc_ref, ycat_ref)


def _mixer_tile(tile, ends_sequence, out_block,
                x_ref, nw_ref, win_ref, cw_ref, cb_ref, dtb_ref, a_ref, dx_ref, gnw_ref, scw_ref, e_ref,
                ssm_ref, cst_ref, scst_ref,
                xbc_ext, sc_ext, u_ref, act_ref, dt_ref, z_ref, gb_ref, s_ref, y_ref, ysc_ref, ycat_ref):
    u_ref[...] = _rms_rows(x_ref[0], nw_ref[...]).astype(BF16)

    def project(dst, row0, base, off, width=MXU_COLS):
        dst[row0:row0 + tile, off:off + width] = _dot_nt(u_ref[...], win_ref[base + off:base + off + width, :])

    def conv_block(off):
        conv = cb_ref[:, off:off + LANES] + (
            xbc_ext[CARRY - 3:CARRY - 3 + tile, off:off + LANES] * cw_ref[0:1, off:off + LANES])
        for k in range(1, SSD_CONV):
            conv = conv + (xbc_ext[CARRY - 3 + k:CARRY - 3 + k + tile, off:off + LANES]
                           * cw_ref[k:k + 1, off:off + LANES])
        act_ref[:, off:off + LANES] = _silu(conv)

    def sc_block(off):
        sc = sc_ext[CARRY - 2:CARRY - 2 + tile, off:off + LANES] * scw_ref[0, :, off:off + LANES]
        for k in range(1, SC_CONV):
            sc = sc + (sc_ext[CARRY - 2 + k:CARRY - 2 + k + tile, off:off + LANES]
                       * scw_ref[k, :, off:off + LANES])
        ysc_ref[:, off:off + LANES] = (gb_ref[:, off:off + LANES] * sc).astype(BF16)

    for off in range(0, CONV_DIM, MXU_COLS):
        project(xbc_ext, CARRY, OFF_XBC, off)
        for sub in range(off, off + MXU_COLS, LANES):
            conv_block(sub)
    tail = xbc_ext[tile + CARRY - 3:tile + CARRY, :]
    cst_ref[0] = tail
    xbc_ext[CARRY - 3:CARRY, :] = tail
    project(dt_ref, 0, OFF_DT, 0, DT_PAD)
    for off in range(0, D_SC, MXU_COLS):
        u = u_ref[...]
        sc_ext[CARRY:CARRY + tile, off:off + MXU_COLS] = (
            _dot_nt(u, win_ref[OFF_GC + off:OFF_GC + off + MXU_COLS, :])
            * _dot_nt(u, win_ref[OFF_H + off:OFF_H + off + MXU_COLS, :]))

    fillers = []
    for k in range(D_SC // MXU_COLS):
        fillers.append(lambda k=k: out_block(k * MXU_COLS))
        fillers.append(lambda k=k: project(gb_ref, 0, OFF_GB, k * MXU_COLS))
        fillers.append(lambda k=k: sc_block(k * MXU_COLS))
        fillers.append(lambda k=k: sc_block(k * MXU_COLS + LANES))
        fillers.append(lambda k=k: project(z_ref, 0, OFF_Z, k * MXU_COLS))
    n_slots = (tile // CHUNK) * (NHEADS // 2)
    slot_of_filler = [(i * n_slots) // len(fillers) for i in range(len(fillers))]

    row_i = lax.broadcasted_iota(jnp.int32, (CHUNK, CHUNK), 0)
    col_i = lax.broadcasted_iota(jnp.int32, (CHUNK, CHUNK), 1)
    causal = row_i >= col_i
    tri = causal.astype(BF16)
    lane_i = lax.broadcasted_iota(jnp.int32, (1, LANES), 1)
    keeps = ((lane_i < HEAD_DIM).astype(BF16), (lane_i >= HEAD_DIM).astype(BF16))
    neg_inf = jnp.float32(-jnp.inf)
    pairs_per_group = NHEADS // NGROUPS // 2

    for c in range(tile // CHUNK):
        r0 = c * CHUNK
        dt = _softplus(dt_ref[r0:r0 + CHUNK, :] + dtb_ref[...])
        acum = _dot_sel_lhs(tri, dt * a_ref[...])
        acum_t = acum.T
        dt_t = dt.T
        ea = jnp.exp(acum)
        w_t = jnp.exp(acum_t[:, CHUNK - 1:CHUNK] - acum_t) * dt_t
        cd = jnp.broadcast_to(ea[CHUNK - 1:CHUNK, :], (8, LANES))
        cdx = _dot_sel_rhs(cd, e_ref[...])[0:1, :]
        for grp in range(NGROUPS):
            bg = act_ref[r0:r0 + CHUNK, D_SSD + grp * NSTATE:D_SSD + (grp + 1) * NSTATE]
            cg = act_ref[r0:r0 + CHUNK, D_SSD + (NGROUPS + grp) * NSTATE:D_SSD + (NGROUPS + grp + 1) * NSTATE]
            cb = lax.dot_general(cg.astype(BF16), bg.astype(BF16), (((1,), (1,)), ((), ())),
                                 preferred_element_type=F32)
            bg_t = bg.T
            for jp in range(pairs_per_group):
                j = grp * pairs_per_group + jp
                xp = act_ref[r0:r0 + CHUNK, j * LANES:(j + 1) * LANES].astype(BF16)
                sp = s_ref[:, j * LANES:(j + 1) * LANES]
                spb = sp.astype(BF16)
                lhs_parts, rhs_parts, bw_parts, x_parts = [], [], [], []
                for half, keep in enumerate(keeps):
                    h = 2 * j + half
                    xh = xp * keep
                    seg = acum[:, h:h + 1] - acum_t[h:h + 1, :]
                    decay = jnp.exp(jnp.where(causal, seg, neg_inf))
                    m = cb * decay * dt_t[h:h + 1, :]
                    eac = ea[:, h:h + 1] * cg
                    lhs_parts += [m.astype(BF16), eac.astype(BF16)]
                    rhs_parts += [xh, spb * keep]
                    bw_parts.append((bg_t * w_t[h:h + 1, :]).astype(BF16))
                    x_parts.append(xh)
                y_ref[r0:r0 + CHUNK, j * LANES:(j + 1) * LANES] = _dot(
                    jnp.concatenate(lhs_parts, axis=1), jnp.concatenate(rhs_parts, axis=0))
                snew = _dot(jnp.concatenate(bw_parts, axis=1), jnp.concatenate(x_parts, axis=0))
                s_ref[:, j * LANES:(j + 1) * LANES] = cdx[:, j * LANES:(j + 1) * LANES] * sp + snew
                slot = c * (NHEADS // 2) + j
                for i, filler in enumerate(fillers):
                    if slot_of_filler[i] == slot:
                        filler()
    sc_tail = sc_ext[tile + CARRY - 2:tile + CARRY, :]
    scst_ref[0] = sc_tail
    sc_ext[CARRY - 2:CARRY, :] = sc_tail

    gw = D_SSD // NGROUPS
    for grp in range(NGROUPS):
        ssq = jnp.zeros((tile, LANES), F32)
        for off in range(grp * gw, (grp + 1) * gw, LANES):
            yb = ((y_ref[:, off:off + LANES] + dx_ref[:, off:off + LANES] * act_ref[:, off:off + LANES])
                  * _silu(z_ref[:, off:off + LANES]))
            y_ref[:, off:off + LANES] = yb
            ssq = ssq + yb * yb
        scale = lax.rsqrt(jnp.sum(ssq, axis=-1, keepdims=True) * (1.0 / gw) + EPS)
        for off in range(grp * gw, (grp + 1) * gw, LANES):
            ycat_ref[:, off:off + LANES] = (y_ref[:, off:off + LANES] * scale
                                            * gnw_ref[:, off:off + LANES]).astype(BF16)
    ycat_ref[:, D_SSD:D_MIX] = ysc_ref[...]

    @pl.when(ends_sequence)
    def _():
        for j in range(NHEADS // 2):
            ssm_ref[0, 2 * j:2 * j + 2] = s_ref[:, j * LANES:(j + 1) * LANES].T.reshape(2, HEAD_DIM, NSTATE)


def _const_spec(shape):
    nd = len(shape)
    return pl.BlockSpec(shape, lambda *_: (0,) * nd, pipeline_mode=pl.Buffered(1))


CAST_FFN_STEPS = 32
CAST_DOWN_STEPS = 16


def _mixer_prompt(x, nw, win, cw, cb, dtb, a_pad, dx, gnw, scw, wout, expand, w_ffn_in, w_down):
    nb, seq, _ = x.shape
    tile = SEQ_TILE
    nt = seq // tile
    total = nb * nt
    assert total >= CAST_FFN_STEPS and total >= CAST_DOWN_STEPS
    ffn_rows = D_MODEL // CAST_FFN_STEPS
    down_rows = D_FF // CAST_DOWN_STEPS
    ffn_spec = pl.BlockSpec((ffn_rows, 2 * D_FF), lambda g: (jnp.minimum(g, CAST_FFN_STEPS - 1), 0))
    down_spec = pl.BlockSpec((down_rows, D_MODEL), lambda g: (jnp.minimum(g, CAST_DOWN_STEPS - 1), 0))

    def tile_of(g, lag):
        i = jnp.clip(g - lag, 0, total - 1)
        return i // nt, i % nt

    cur_spec = pl.BlockSpec((1, tile, D_MODEL), lambda g: (*tile_of(g, 0), 0))
    prev_spec = pl.BlockSpec((1, tile, D_MODEL), lambda g: (*tile_of(g, 1), 0))
    consts = (nw, win, cw, cb, dtb, a_pad, dx, gnw, scw, wout, expand)
    return pl.pallas_call(
        functools.partial(_mixer_prompt_kernel, nt),
        grid=(total + 1,),
        in_specs=[cur_spec, prev_spec, ffn_spec, down_spec] + [_const_spec(p.shape) for p in consts],
        out_specs=[
            prev_spec,
            pl.BlockSpec((1, NHEADS, HEAD_DIM, NSTATE), lambda g: (tile_of(g, 0)[0], 0, 0, 0)),
            pl.BlockSpec((1, SSD_CONV - 1, CONV_DIM), lambda g: (tile_of(g, 0)[0], 0, 0)),
            pl.BlockSpec((1, SC_CONV - 1, D_SC), lambda g: (tile_of(g, 0)[0], 0, 0)),
            ffn_spec,
            down_spec,
        ],
        out_shape=[
            jax.ShapeDtypeStruct((nb, seq, D_MODEL), F32),
            jax.ShapeDtypeStruct((nb, NHEADS, HEAD_DIM, NSTATE), F32),
            jax.ShapeDtypeStruct((nb, SSD_CONV - 1, CONV_DIM), F32),
            jax.ShapeDtypeStruct((nb, SC_CONV - 1, D_SC), F32),
            jax.ShapeDtypeStruct(w_ffn_in.shape, BF16),
            jax.ShapeDtypeStruct(w_down.shape, BF16),
        ],
        scratch_shapes=[
            pltpu.VMEM((tile + CARRY, CONV_DIM), F32),
            pltpu.VMEM((tile + CARRY, D_SC), F32),
            pltpu.VMEM((tile, D_MODEL), BF16),
            pltpu.VMEM((tile, CONV_DIM), F32),
            pltpu.VMEM((tile, DT_PAD), F32),
            pltpu.VMEM((tile, D_SSD), F32),
            pltpu.VMEM((tile, D_SC), F32),
            pltpu.VMEM((NSTATE, D_SSD), F32),
            pltpu.VMEM((tile, D_SSD), F32),
            pltpu.VMEM((tile, D_SC), BF16),
            pltpu.VMEM((tile, D_MIX), BF16),
        ],
        compiler_params=pltpu.CompilerParams(
            dimension_semantics=("arbitrary",), vmem_limit_bytes=VMEM_LIMIT),
        name="mixer_prompt",
    )(x, x, w_ffn_in, w_down, *consts)


STATE_HEADS = 2
STATE_SEQS = 64
STATE_ROWS = STATE_HEADS * HEAD_DIM


def _state_block_update(da_ref, step, st_ref, xdt_t_ref, b_ref, c_ref, nst_ref, yt_ref):
    seq_blocks = da_ref.shape[0] // STATE_SEQS
    seq0 = lax.rem(step, seq_blocks) * STATE_SEQS
    head0 = (step // seq_blocks) * STATE_HEADS
    c_blk = c_ref[0].astype(BF16)
    lane = lax.broadcasted_iota(jnp.int32, (STATE_ROWS, STATE_SEQS), 1)
    ycols = jnp.zeros((STATE_ROWS, STATE_SEQS), F32)
    for i in range(STATE_SEQS):
        upd = xdt_t_ref[0, :, i:i + 1] * b_ref[0, i:i + 1, :]
        parts = []
        for hh in range(STATE_HEADS):
            hnew = st_ref[i, hh] * da_ref[seq0 + i, head0 + hh] + upd[hh * HEAD_DIM:(hh + 1) * HEAD_DIM, :]
            nst_ref[i, hh] = hnew
            parts.append(hnew)
        prod = _dot_nt(jnp.concatenate(parts, axis=0).astype(BF16), c_blk)
        ycols = jnp.where(lane == i, prod, ycols)
    yt_ref[0] = ycols


def _ffn_prompt_kernel(state_steps,
                       da_ref, x1_ref, st_ref, xdt_t_ref, b_ref, c_ref,
                       nw_ref, wffn_hbm, fcw_ref, fcb_ref, wdown_hbm, nfw_ref,
                       y_ref, fst_ref, nst_ref, yt_ref, gate_ext, wffn_ref, wdown_ref, sems):
    tile = x1_ref.shape[1]
    t = pl.program_id(1)
    step = pl.program_id(0) * pl.num_programs(1) + t

    copies = [pltpu.make_async_copy(wffn_hbm.at[:, 0:D_FF], wffn_ref.at[:, 0:D_FF], sems.at[0]),
              pltpu.make_async_copy(wffn_hbm.at[:, D_FF:2 * D_FF], wffn_ref.at[:, D_FF:2 * D_FF], sems.at[1]),
              pltpu.make_async_copy(wdown_hbm, wdown_ref, sems.at[2])]

    def wait_at_first_step(copy):
        @pl.when(step == 0)
        def _():
            copy.wait()

    @pl.when(step == 0)
    def _():
        for copy in copies:
            copy.start()

    @pl.when(step < state_steps)
    def _():
        _state_block_update(da_ref, step, st_ref, xdt_t_ref, b_ref, c_ref, nst_ref, yt_ref)

    @pl.when(t == 0)
    def _():
        gate_ext[0:CARRY, :] = jnp.zeros((CARRY, D_FF), F32)

    x1 = x1_ref[0]
    u = _rms_rows(x1, nw_ref[...]).astype(BF16)
    wait_at_first_step(copies[0])
    gate_ext[CARRY:CARRY + tile, :] = _dot(u, wffn_ref[:, 0:D_FF])
    wait_at_first_step(copies[1])
    up = _dot(u, wffn_ref[:, D_FF:2 * D_FF])
    g = fcb_ref[...] + gate_ext[CARRY - 2:CARRY - 2 + tile, :] * fcw_ref[0]
    for k in range(1, FFN_CONV):
        g = g + gate_ext[CARRY - 2 + k:CARRY - 2 + k + tile, :] * fcw_ref[k]
    tail = gate_ext[tile + CARRY - 2:tile + CARRY, :]
    fst_ref[0] = tail
    gate_ext[CARRY - 2:CARRY, :] = tail
    act = (_silu(g) * up).astype(BF16)
    wait_at_first_step(copies[2])
    x2 = x1 + _dot(act, wdown_ref[...])
    y_ref[0] = _rms_rows(x2, nfw_ref[...])


def _ffn_prompt(x1, nw, wffn, fcw, fcb, wdown, nfw, state, da, xdt_t, bc):
    nb, seq, _ = x1.shape
    tile = SEQ_TILE
    nt = seq // tile
    nseq = state.shape[0]
    seq_blocks = nseq // STATE_SEQS
    state_steps = seq_blocks * (NHEADS // STATE_HEADS)
    assert nb * nt >= state_steps
    heads_per_group = NHEADS // NGROUPS

    def block_of(b, t):
        s = jnp.minimum(b * nt + t, state_steps - 1)
        return s % seq_blocks, s // seq_blocks

    def st_map(b, t):
        sb, hb = block_of(b, t)
        return (sb, hb, 0, 0)

    def col_map(b, t):
        sb, hb = block_of(b, t)
        return (sb, hb, 0)

    def b_map(b, t):
        sb, hb = block_of(b, t)
        return (hb * STATE_HEADS // heads_per_group, sb, 0)

    def c_map(b, t):
        sb, hb = block_of(b, t)
        return (NGROUPS + hb * STATE_HEADS // heads_per_group, sb, 0)

    tok_spec = pl.BlockSpec((1, tile, D_MODEL), lambda b, t: (b, t, 0))
    st_spec = pl.BlockSpec((STATE_SEQS, STATE_HEADS, HEAD_DIM, NSTATE), st_map)
    col_spec = pl.BlockSpec((1, STATE_ROWS, STATE_SEQS), col_map)
    return pl.pallas_call(
        functools.partial(_ffn_prompt_kernel, state_steps),
        grid=(nb, nt),
        in_specs=[pl.BlockSpec(memory_space=pltpu.SMEM), tok_spec, st_spec, col_spec,
                  pl.BlockSpec((1, STATE_SEQS, NSTATE), b_map), pl.BlockSpec((1, STATE_SEQS, NSTATE), c_map)]
        + [pl.BlockSpec(memory_space=pl.ANY) if p is wffn or p is wdown else _const_spec(p.shape)
           for p in (nw, wffn, fcw, fcb, wdown, nfw)],
        out_specs=[tok_spec, pl.BlockSpec((1, FFN_CONV - 1, D_FF), lambda b, t: (b, 0, 0)), st_spec, col_spec],
        out_shape=[jax.ShapeDtypeStruct((nb, seq, D_MODEL), F32),
                   jax.ShapeDtypeStruct((nb, FFN_CONV - 1, D_FF), F32),
                   jax.ShapeDtypeStruct(state.shape, F32),
                   jax.ShapeDtypeStruct(xdt_t.shape, F32)],
        scratch_shapes=[pltpu.VMEM((tile + CARRY, D_FF), F32),
                        pltpu.VMEM(wffn.shape, BF16),
                        pltpu.VMEM(wdown.shape, BF16),
                        pltpu.SemaphoreType.DMA((3,))],
        compiler_params=pltpu.CompilerParams(
            dimension_semantics=("arbitrary", "arbitrary"), vmem_limit_bytes=VMEM_LIMIT),
        name="ffn_prompt",
    )(da, x1, state, xdt_t, bc, bc, nw, wffn, fcw, fcb, wdown, nfw)


CAST_ROWS = 512


def _sample_pre_kernel(x_ref, nw_ref, wf32_ref, cw_ref, cb_ref, dt_bias_ref, a_log_ref, d_ref, scw_ref, e_ref,
                       cst_ref, scst_ref,
                       z_ref, xs_ref, xdt_t_ref, da_ref, bc_ref, ysc_ref, ncst_ref, nscst_ref, win_ref,
                       dtb_ref, a_ref, dx_ref):
    pad = jnp.zeros((1, DT_PAD - NHEADS), F32)
    dtb_ref[...] = jnp.concatenate([dt_bias_ref[...], pad], axis=1)
    a_ref[...] = jnp.concatenate([-jnp.exp(a_log_ref[...]), pad], axis=1)
    d_pad = jnp.broadcast_to(jnp.concatenate([d_ref[...], pad], axis=1), (8, DT_PAD))
    dx_ref[...] = _dot_sel_rhs(d_pad, e_ref[...])[0:1, :]
    for r0 in range(0, D_IN_PROJ, CAST_ROWS):
        r1 = min(r0 + CAST_ROWS, D_IN_PROJ)
        win_ref[r0:r1, :] = wf32_ref[r0:r1, :].astype(BF16)
    x = x_ref[:, 0, :]
    u = _rms_rows(x, nw_ref[...]).astype(BF16)
    z_ref[...] = _dot_nt(u, win_ref[OFF_Z:OFF_Z + D_SSD, :])
    xbc = _dot_nt(u, win_ref[OFF_XBC:OFF_XBC + CONV_DIM, :])
    conv = cb_ref[...] + xbc * cw_ref[SSD_CONV - 1:SSD_CONV, :]
    for k in range(SSD_CONV - 1):
        conv = conv + cst_ref[k] * cw_ref[k:k + 1, :]
    for k in range(SSD_CONV - 2):
        ncst_ref[k] = cst_ref[k + 1]
    ncst_ref[SSD_CONV - 2] = xbc
    act = _silu(conv)
    xs = act[:, 0:D_SSD]
    xs_ref[...] = xs
    for i in range(2 * NGROUPS):
        bc_ref[i] = act[:, D_SSD + i * NSTATE:D_SSD + (i + 1) * NSTATE]
    dt = _softplus(_dot_nt(u, win_ref[OFF_DT:OFF_DT + DT_PAD, :]) + dtb_ref[...])
    xdt_t = (xs * _dot_sel_rhs(dt, e_ref[...])).T
    for sb in range(xdt_t_ref.shape[0]):
        xdt_t_ref[sb] = xdt_t[:, sb * STATE_SEQS:(sb + 1) * STATE_SEQS]
    da_ref[...] = jnp.exp(dt * a_ref[...])[:, 0:NHEADS]
    gc = _dot_nt(u, win_ref[OFF_GC:OFF_GC + D_SC, :])
    hh = _dot_nt(u, win_ref[OFF_H:OFF_H + D_SC, :])
    gch = gc * hh
    sc = gch * scw_ref[SC_CONV - 1]
    for k in range(SC_CONV - 1):
        sc = sc + scst_ref[:, k, :] * scw_ref[k]
    for k in range(SC_CONV - 2):
        nscst_ref[:, k, :] = scst_ref[:, k + 1, :]
    nscst_ref[:, SC_CONV - 2, :] = gch
    gb = _dot_nt(u, win_ref[OFF_GB:OFF_GB + D_SC, :])
    ysc_ref[...] = gb * sc


def _sample_pre(x, nw, w_in_t, cw, cb, dt_bias, a_log, d, scw, expand, cst, scst):
    nb = x.shape[0]
    outs = [
        jax.ShapeDtypeStruct((nb, D_SSD), F32),
        jax.ShapeDtypeStruct((nb, D_SSD), F32),
        jax.ShapeDtypeStruct((nb // STATE_SEQS, D_SSD, STATE_SEQS), F32),
        jax.ShapeDtypeStruct((nb, NHEADS), F32),
        jax.ShapeDtypeStruct((2 * NGROUPS, nb, NSTATE), F32),
        jax.ShapeDtypeStruct((nb, D_SC), F32),
        jax.ShapeDtypeStruct((SSD_CONV - 1, nb, CONV_DIM), F32),
        jax.ShapeDtypeStruct((nb, SC_CONV - 1, D_SC), F32),
        jax.ShapeDtypeStruct((D_IN_PROJ, D_MODEL), BF16),
        jax.ShapeDtypeStruct((1, DT_PAD), F32),
        jax.ShapeDtypeStruct((1, DT_PAD), F32),
        jax.ShapeDtypeStruct((1, D_SSD), F32),
    ]
    return pl.pallas_call(
        _sample_pre_kernel,
        out_shape=outs,
        compiler_params=pltpu.CompilerParams(vmem_limit_bytes=VMEM_LIMIT),
        name="sample_pre",
    )(x, nw, w_in_t, cw, cb, dt_bias, a_log, d, scw, expand, cst, scst)


def _sample_post_kernel(x_ref, yt_ref, xs_ref, z_ref, ysc_ref, dx_ref, gnw_ref, wout_hbm,
                        nw_ref, wffn_hbm, fcw_ref, fcb_ref, wdown_hbm, nfw_ref, fst_ref,
                        out_ref, nfst_ref,
                        wout_ref, wffn_ref, wdown_ref, sems):
    copies = [pltpu.make_async_copy(src, dst, sems.at[i]) for i, (src, dst) in enumerate(
        ((wout_hbm, wout_ref), (wffn_hbm, wffn_ref), (wdown_hbm, wdown_ref)))]
    for copy in copies:
        copy.start()
    wout_copy, wffn_copy, wdown_copy = copies
    y_ssd = jnp.concatenate([yt_ref[sb] for sb in range(yt_ref.shape[0])], axis=1).T
    y = (y_ssd + dx_ref[...] * xs_ref[...]) * _silu(z_ref[...])
    gw = D_SSD // NGROUPS
    parts = [_rms_rows(y[:, g * gw:(g + 1) * gw], gnw_ref[:, g * gw:(g + 1) * gw]) for g in range(NGROUPS)]
    ycat = jnp.concatenate(parts + [ysc_ref[...]], axis=1).astype(BF16)
    wout_copy.wait()
    x1 = x_ref[:, 0, :] + _dot(ycat, wout_ref[...])
    u = _rms_rows(x1, nw_ref[...]).astype(BF16)
    wffn_copy.wait()
    gate = _dot(u, wffn_ref[:, 0:D_FF])
    up = _dot(u, wffn_ref[:, D_FF:2 * D_FF])
    g = fcb_ref[...] + gate * fcw_ref[FFN_CONV - 1]
    for k in range(FFN_CONV - 1):
        g = g + fst_ref[:, k, :] * fcw_ref[k]
    for k in range(FFN_CONV - 2):
        nfst_ref[:, k, :] = fst_ref[:, k + 1, :]
    nfst_ref[:, FFN_CONV - 2, :] = gate
    act = (_silu(g) * up).astype(BF16)
    wdown_copy.wait()
    x2 = x1 + _dot(act, wdown_ref[...])
    out_ref[:, 0, :] = _rms_rows(x2, nfw_ref[...])


def _sample_post(x, y, xs, z, ysc, dx, gnw, wout, nw, wffn, fcw, fcb, wdown, nfw, fst):
    nb = x.shape[0]
    in_vmem = pl.BlockSpec(memory_space=pltpu.VMEM)
    in_hbm = pl.BlockSpec(memory_space=pl.ANY)
    return pl.pallas_call(
        _sample_post_kernel,
        in_specs=[in_vmem] * 7 + [in_hbm, in_vmem, in_hbm, in_vmem, in_vmem, in_hbm, in_vmem, in_vmem],
        out_shape=[jax.ShapeDtypeStruct((nb, 1, D_MODEL), F32),
                   jax.ShapeDtypeStruct((nb, FFN_CONV - 1, D_FF), F32)],
        scratch_shapes=[pltpu.VMEM(wout.shape, BF16), pltpu.VMEM(wffn.shape, BF16), pltpu.VMEM(wdown.shape, BF16),
                        pltpu.SemaphoreType.DMA((3,))],
        compiler_params=pltpu.CompilerParams(vmem_limit_bytes=VMEM_LIMIT),
        name="sample_post",
    )(x, y, xs, z, ysc, dx, gnw, wout, nw, wffn, fcw, fcb, wdown, nfw, fst)


def kernel(x_prompt, x_sample, state_ssm, state_ssd_conv, state_short_conv, state_ffn_conv,
           norm_mix_w, w_in, ssd_conv_w, ssd_conv_b, ssd_dt_bias, ssd_a_log, ssd_d, ssd_norm_w,
           sc_conv_w, w_out, norm_ffn_w, w_ffn_in, ffn_conv_w, ffn_conv_b, w_down, norm_final_w):
    depth = w_in.shape[0]
    assert depth == 1
    w_in_t = jnp.swapaxes(w_in[0], 0, 1)
    nw = norm_mix_w[0].reshape(1, D_MODEL)
    nw2 = norm_ffn_w[0].reshape(1, D_MODEL)
    nfw = norm_final_w.reshape(1, D_MODEL)
    cw = ssd_conv_w[0]
    cb = ssd_conv_b[0].reshape(1, CONV_DIM)
    gnw = ssd_norm_w[0].reshape(1, D_SSD)
    scw = jnp.swapaxes(sc_conv_w, 0, 1)
    fcw = jnp.swapaxes(ffn_conv_w, 0, 1)
    fcb = ffn_conv_b[0].reshape(1, D_FF)
    expand = jnp.asarray(np.arange(DT_PAD)[:, None] == np.arange(D_SSD)[None, :] // HEAD_DIM, dtype=BF16)

    xs_in = x_sample
    cst = jnp.swapaxes(state_ssd_conv[0], 0, 1)
    scst, fst = state_short_conv[0], state_ffn_conv[0]
    z, xs, xdt_t, da, bc, ysc, ncst, nscst, win, dtb, a_pad, dx = _sample_pre(
        xs_in, nw, w_in_t, cw, cb, ssd_dt_bias, ssd_a_log, ssd_d, scw, expand, cst, scst)

    wout = w_out[0].astype(BF16)
    x1, p_ssm, p_cst, p_scst, wffn, wdown = _mixer_prompt(
        x_prompt, nw, win, cw, cb, dtb, a_pad, dx, gnw, scw, wout, expand, w_ffn_in[0], w_down[0])
    y_prompt, p_fst, s_ssm, y_t = _ffn_prompt(x1, nw2, wffn, fcw, fcb, wdown, nfw, state_ssm[0], da, xdt_t, bc)
    y_sample, nfst = _sample_post(xs_in, y_t, xs, z, ysc, dx, gnw, wout, nw2, wffn, fcw, fcb, wdown, nfw, fst)

    return (y_prompt, y_sample,
            p_ssm[None], p_cst[None], p_scst[None], p_fst[None],
            s_ssm[None], jnp.swapaxes(ncst, 0, 1)[None], nscst[None], nfst[None])
```

```python
import functools

import jax
import jax.numpy as jnp
import numpy as np
from jax import lax
from jax.experimental import pallas as pl
from jax.experimental.pallas import tpu as pltpu

D_MODEL = 1024
D_SSD = 1024
D_SC = 1024
NHEADS = 16
HEAD_DIM = 64
NGROUPS = 2
NSTATE = 128
SSD_CONV = 4
CONV_DIM = D_SSD + 2 * NGROUPS * NSTATE
SC_CONV = 3
D_FF = 2816
FFN_CONV = 3
EPS = 1e-5
D_MIX = D_SSD + D_SC
D_IN_PROJ = D_SSD + CONV_DIM + NHEADS + 3 * D_SC

LANES = 128
MXU_COLS = 256
OFF_Z = 0
OFF_XBC = OFF_Z + D_SSD
OFF_DT = OFF_XBC + CONV_DIM
OFF_GB = OFF_DT + NHEADS
OFF_GC = OFF_GB + D_SC
OFF_H = OFF_GC + D_SC
DT_PAD = LANES

CHUNK = 128
CARRY = 8
SEQ_TILE = 512
VMEM_LIMIT = 56 * 1024 * 1024

F32 = jnp.float32
BF16 = jnp.bfloat16


def _dot(a, b):
    return jnp.dot(a, b, preferred_element_type=F32)


def _dot_nt(a, b_t):
    return lax.dot_general(a, b_t, (((1,), (1,)), ((), ())), preferred_element_type=F32)


def _split3(v):
    hi = v.astype(BF16)
    r = v - hi.astype(F32)
    mid = r.astype(BF16)
    lo = (r - mid.astype(F32)).astype(BF16)
    return hi, mid, lo


def _dot_sel_rhs(v, sel):
    hi, mid, lo = _split3(v)
    return (_dot(hi, sel) + _dot(mid, sel)) + _dot(lo, sel)


def _dot_sel_lhs(sel, v):
    hi, mid, lo = _split3(v)
    return (_dot(sel, hi) + _dot(sel, mid)) + _dot(sel, lo)


def _silu(v):
    h = 0.5 * v
    return h + h * jnp.tanh(h)


def _softplus(v):
    return jnp.maximum(v, 0.0) + jnp.log1p(jnp.exp(-jnp.abs(v)))


def _rms_rows(x, w):
    ms = jnp.mean(x * x, axis=-1, keepdims=True)
    return x * lax.rsqrt(ms + EPS) * w


def _mixer_prompt_kernel(tiles_per_seq,
                         x_ref, xres_ref, wffn_f32_ref, wdown_f32_ref,
                         nw_ref, win_ref, cw_ref, cb_ref, dtb_ref, a_ref, dx_ref,
                         gnw_ref, scw_ref, wout_ref, e_ref,
                         x1_ref, ssm_ref, cst_ref, scst_ref, wffn_bf_ref, wdown_bf_ref,
                         xbc_ext, sc_ext, u_ref, act_ref, dt_ref, z_ref, gb_ref, s_ref, y_ref, ysc_ref, ycat_ref):
    tile = x_ref.shape[1]
    g = pl.program_id(0)
    last = pl.num_programs(0) - 1
    t = lax.rem(g, tiles_per_seq)

    @pl.when(g < CAST_FFN_STEPS)
    def _():
        wffn_bf_ref[...] = wffn_f32_ref[...].astype(BF16)

    @pl.when(g < CAST_DOWN_STEPS)
    def _():
        wdown_bf_ref[...] = wdown_f32_ref[...].astype(BF16)

    @pl.when(g == 0)
    def _():
        ycat_ref[...] = jnp.zeros_like(ycat_ref)

    @pl.when(t == 0)
    def _():
        xbc_ext[0:CARRY, :] = jnp.zeros((CARRY, CONV_DIM), F32)
        sc_ext[0:CARRY, :] = jnp.zeros((CARRY, D_SC), F32)
        s_ref[...] = jnp.zeros_like(s_ref)

    def out_block(off):
        x1_ref[0, :, off:off + MXU_COLS] = xres_ref[0, :, off:off + MXU_COLS] + _dot(
            ycat_ref[...], wout_ref[:, off:off + MXU_COLS])

    @pl.when(g == last)
    def _():
        for off in range(0, D_MODEL, MXU_COLS):
            out_block(off)

    @pl.when(g < last)
    def _():
        _mixer_tile(tile, t == tiles_per_seq - 1, out_block,
                    x_ref, nw_ref, win_ref, cw_ref, cb_ref, dtb_ref, a_ref, dx_ref, gnw_ref, scw_ref, e_ref,
                    ssm_ref, cst_ref, scst_ref,
                    xbc_ext, sc_ext, u_ref, act_ref, dt_ref, z_ref, gb_ref, s_ref, y_ref, ysc_ref, ycat_ref)


def _mixer_tile(tile, ends_sequence, out_block,
                x_ref, nw_ref, win_ref, cw_ref, cb_ref, dtb_ref, a_ref, dx_ref, gnw_ref, scw_ref, e_ref,
                ssm_ref, cst_ref, scst_ref,
                xbc_ext, sc_ext, u_ref, act_ref, dt_ref, z_ref, gb_ref, s_ref, y_ref, ysc_ref, ycat_ref):
    u_ref[...] = _rms_rows(x_ref[0], nw_ref[...]).astype(BF16)

    def project(dst, row0, base, off, width=MXU_COLS):
        dst[row0:row0 + tile, off:off + width] = _dot_nt(u_ref[...], win_ref[base + off:base + off + width, :])

    def conv_block(off):
        conv = cb_ref[:, off:off + LANES] + (
            xbc_ext[CARRY - 3:CARRY - 3 + tile, off:off + LANES] * cw_ref[0:1, off:off + LANES])
        for k in range(1, SSD_CONV):
            conv = conv + (xbc_ext[CARRY - 3 + k:CARRY - 3 + k + tile, off:off + LANES]
                           * cw_ref[k:k + 1, off:off + LANES])
        act_ref[:, off:off + LANES] = _silu(conv)

    def sc_block(off):
        sc = sc_ext[CARRY - 2:CARRY - 2 + tile, off:off + LANES] * scw_ref[0, :, off:off + LANES]
        for k in range(1, SC_CONV):
            sc = sc + (sc_ext[CARRY - 2 + k:CARRY - 2 + k + tile, off:off + LANES]
                       * scw_ref[k, :, off:off + LANES])
        ysc_ref[:, off:off + LANES] = (gb_ref[:, off:off + LANES] * sc).astype(BF16)

    for off in range(0, CONV_DIM, MXU_COLS):
        project(xbc_ext, CARRY, OFF_XBC, off)
        for sub in range(off, off + MXU_COLS, LANES):
            conv_block(sub)
    tail = xbc_ext[tile + CARRY - 3:tile + CARRY, :]
    cst_ref[0] = tail
    xbc_ext[CARRY - 3:CARRY, :] = tail
    project(dt_ref, 0, OFF_DT, 0, DT_PAD)
    for off in range(0, D_SC, MXU_COLS):
        u = u_ref[...]
        sc_ext[CARRY:CARRY + tile, off:off + MXU_COLS] = (
            _dot_nt(u, win_ref[OFF_GC + off:OFF_GC + off + MXU_COLS, :])
            * _dot_nt(u, win_ref[OFF_H + off:OFF_H + off + MXU_COLS, :]))

    fillers = []
    for k in range(D_SC // MXU_COLS):
        fillers.append(lambda k=k: out_block(k * MXU_COLS))
        fillers.append(lambda k=k: project(gb_ref, 0, OFF_GB, k * MXU_COLS))
        fillers.append(lambda k=k: sc_block(k * MXU_COLS))
        fillers.append(lambda k=k: sc_block(k * MXU_COLS + LANES))
        fillers.append(lambda k=k: project(z_ref, 0, OFF_Z, k * MXU_COLS))
    n_slots = (tile // CHUNK) * (NHEADS // 2)
    slot_of_filler = [(i * n_slots) // len(fillers) for i in range(len(fillers))]

    row_i = lax.broadcasted_iota(jnp.int32, (CHUNK, CHUNK), 0)
    col_i = lax.broadcasted_iota(jnp.int32, (CHUNK, CHUNK), 1)
    causal = row_i >= col_i
    tri = causal.astype(BF16)
    lane_i = lax.broadcasted_iota(jnp.int32, (1, LANES), 1)
    keeps = ((lane_i < HEAD_DIM).astype(BF16), (lane_i >= HEAD_DIM).astype(BF16))
    neg_inf = jnp.float32(-jnp.inf)
    pairs_per_group = NHEADS // NGROUPS // 2

    for c in range(tile // CHUNK):
        r0 = c * CHUNK
        dt = _softplus(dt_ref[r0:r0 + CHUNK, :] + dtb_ref[...])
        acum = _dot_sel_lhs(tri, dt * a_ref[...])
        acum_t = acum.T
        dt_t = dt.T
        ea = jnp.exp(acum)
        w_t = jnp.exp(acum_t[:, CHUNK - 1:CHUNK] - acum_t) * dt_t
        cd = jnp.broadcast_to(ea[CHUNK - 1:CHUNK, :], (8, LANES))
        cdx = _dot_sel_rhs(cd, e_ref[...])[0:1, :]
        for grp in range(NGROUPS):
            bg = act_ref[r0:r0 + CHUNK, D_SSD + grp * NSTATE:D_SSD + (grp + 1) * NSTATE]
            cg = act_ref[r0:r0 + CHUNK, D_SSD + (NGROUPS + grp) * NSTATE:D_SSD + (NGROUPS + grp + 1) * NSTATE]
            cb = lax.dot_general(cg.astype(BF16), bg.astype(BF16), (((1,), (1,)), ((), ())),
                                 preferred_element_type=F32)
            bg_t = bg.T
            for jp in range(pairs_per_group):
                j = grp * pairs_per_group + jp
                xp = act_ref[r0:r0 + CHUNK, j * LANES:(j + 1) * LANES].astype(BF16)
                sp = s_ref[:, j * LANES:(j + 1) * LANES]
                spb = sp.astype(BF16)
                lhs_parts, rhs_parts, bw_parts, x_parts = [], [], [], []
                for half, keep in enumerate(keeps):
                    h = 2 * j + half
                    xh = xp * keep
                    seg = acum[:, h:h + 1] - acum_t[h:h + 1, :]
                    decay = jnp.exp(jnp.where(causal, seg, neg_inf))
                    m = cb * decay * dt_t[h:h + 1, :]
                    eac = ea[:, h:h + 1] * cg
                    lhs_parts += [m.astype(BF16), eac.astype(BF16)]
                    rhs_parts += [xh, spb * keep]
                    bw_parts.append((bg_t * w_t[h:h + 1, :]).astype(BF16))
                    x_parts.append(xh)
                y_ref[r0:r0 + CHUNK, j * LANES:(j + 1) * LANES] = _dot(
                    jnp.concatenate(lhs_parts, axis=1), jnp.concatenate(rhs_parts, axis=0))
                snew = _dot(jnp.concatenate(bw_parts, axis=1), jnp.concatenate(x_parts, axis=0))
                s_ref[:, j * LANES:(j + 1) * LANES] = cdx[:, j * LANES:(j + 1) * LANES] * sp + snew
                slot = c * (NHEADS // 2) + j
                for i, filler in enumerate(fillers):
                    if slot_of_filler[i] == slot:
                        filler()
    sc_tail = sc_ext[tile + CARRY - 2:tile + CARRY, :]
    scst_ref[0] = sc_tail
    sc_ext[CARRY - 2:CARRY, :] = sc_tail

    gw = D_SSD // NGROUPS
    for grp in range(NGROUPS):
        ssq = jnp.zeros((tile, LANES), F32)
        for off in range(grp * gw, (grp + 1) * gw, LANES):
            yb = ((y_ref[:, off:off + LANES] + dx_ref[:, off:off + LANES] * act_ref[:, off:off + LANES])
                  * _silu(z_ref[:, off:off + LANES]))
            y_ref[:, off:off + LANES] = yb
            ssq = ssq + yb * yb
        scale = lax.rsqrt(jnp.sum(ssq, axis=-1, keepdims=True) * (1.0 / gw) + EPS)
        for off in range(grp * gw, (grp + 1) * gw, LANES):
            ycat_ref[:, off:off + LANES] = (y_ref[:, off:off + LANES] * scale
                                            * gnw_ref[:, off:off + LANES]).astype(BF16)
    ycat_ref[:, D_SSD:D_MIX] = ysc_ref[...]

    @pl.when(ends_sequence)
    def _():
        for j in range(NHEADS // 2):
            ssm_ref[0, 2 * j:2 * j + 2] = s_ref[:, j * LANES:(j + 1) * LANES].T.reshape(2, HEAD_DIM, NSTATE)


def _const_spec(shape):
    nd = len(shape)
    return pl.BlockSpec(shape, lambda *_: (0,) * nd, pipeline_mode=pl.Buffered(1))


CAST_FFN_STEPS = 32
CAST_DOWN_STEPS = 16


def _mixer_prompt(x, nw, win, cw, cb, dtb, a_pad, dx, gnw, scw, wout, expand, w_ffn_in, w_down):
    nb, seq, _ = x.shape
    tile = SEQ_TILE
    nt = seq // tile
    total = nb * nt
    assert total >= CAST_FFN_STEPS and total >= CAST_DOWN_STEPS
    ffn_rows = D_MODEL // CAST_FFN_STEPS
    down_rows = D_FF // CAST_DOWN_STEPS
    ffn_spec = pl.BlockSpec((ffn_rows, 2 * D_FF), lambda g: (jnp.minimum(g, CAST_FFN_STEPS - 1), 0))
    down_spec = pl.BlockSpec((down_rows, D_MODEL), lambda g: (jnp.minimum(g, CAST_DOWN_STEPS - 1), 0))

    def tile_of(g, lag):
        i = jnp.clip(g - lag, 0, total - 1)
        return i // nt, i % nt

    cur_spec = pl.BlockSpec((1, tile, D_MODEL), lambda g: (*tile_of(g, 0), 0))
    prev_spec = pl.BlockSpec((1, tile, D_MODEL), lambda g: (*tile_of(g, 1), 0))
    consts = (nw, win, cw, cb, dtb, a_pad, dx, gnw, scw, wout, expand)
    return pl.pallas_call(
        functools.partial(_mixer_prompt_kernel, nt),
        grid=(total + 1,),
        in_specs=[cur_spec, prev_spec, ffn_spec, down_spec] + [_const_spec(p.shape) for p in consts],
        out_specs=[
            prev_spec,
            pl.BlockSpec((1, NHEADS, HEAD_DIM, NSTATE), lambda g: (tile_of(g, 0)[0], 0, 0, 0)),
            pl.BlockSpec((1, SSD_CONV - 1, CONV_DIM), lambda g: (tile_of(g, 0)[0], 0, 0)),
            pl.BlockSpec((1, SC_CONV - 1, D_SC), lambda g: (tile_of(g, 0)[0], 0, 0)),
            ffn_spec,
            down_spec,
        ],
        out_shape=[
            jax.ShapeDtypeStruct((nb, seq, D_MODEL), F32),
            jax.ShapeDtypeStruct((nb, NHEADS, HEAD_DIM, NSTATE), F32),
            jax.ShapeDtypeStruct((nb, SSD_CONV - 1, CONV_DIM), F32),
            jax.ShapeDtypeStruct((nb, SC_CONV - 1, D_SC), F32),
            jax.ShapeDtypeStruct(w_ffn_in.shape, BF16),
            jax.ShapeDtypeStruct(w_down.shape, BF16),
        ],
        scratch_shapes=[
            pltpu.VMEM((tile + CARRY, CONV_DIM), F32),
            pltpu.VMEM((tile + CARRY, D_SC), F32),
            pltpu.VMEM((tile, D_MODEL), BF16),
            pltpu.VMEM((tile, CONV_DIM), F32),
            pltpu.VMEM((tile, DT_PAD), F32),
            pltpu.VMEM((tile, D_SSD), F32),
            pltpu.VMEM((tile, D_SC), F32),
            pltpu.VMEM((NSTATE, D_SSD), F32),
            pltpu.VMEM((tile, D_SSD), F32),
            pltpu.VMEM((tile, D_SC), BF16),
            pltpu.VMEM((tile, D_MIX), BF16),
        ],
        compiler_params=pltpu.CompilerParams(
            dimension_semantics=("arbitrary",), vmem_limit_bytes=VMEM_LIMIT),
        name="mixer_prompt",
    )(x, x, w_ffn_in, w_down, *consts)


STATE_HEADS = 2
STATE_SEQS = 64
STATE_ROWS = STATE_HEADS * HEAD_DIM


def _state_block_update(da_ref, step, st_ref, xdt_t_ref, b_ref, c_ref, nst_ref, yt_ref):
    seq_blocks = da_ref.shape[0] // STATE_SEQS
    seq0 = lax.rem(step, seq_blocks) * STATE_SEQS
    head0 = (step // seq_blocks) * STATE_HEADS
    c_blk = c_ref[0].astype(BF16)
    lane = lax.broadcasted_iota(jnp.int32, (STATE_ROWS, STATE_SEQS), 1)
    ycols = jnp.zeros((STATE_ROWS, STATE_SEQS), F32)
    for i in range(STATE_SEQS):
        upd = xdt_t_ref[0, :, i:i + 1] * b_ref[0, i:i + 1, :]
        parts = []
        for hh in range(STATE_HEADS):
            hnew = st_ref[i, hh] * da_ref[seq0 + i, head0 + hh] + upd[hh * HEAD_DIM:(hh + 1) * HEAD_DIM, :]
            nst_ref[i, hh] = hnew
            parts.append(hnew)
        prod = _dot_nt(jnp.concatenate(parts, axis=0).astype(BF16), c_blk)
        ycols = jnp.where(lane == i, prod, ycols)
    yt_ref[0] = ycols


def _ffn_prompt_kernel(state_steps,
                       da_ref, x1_ref, st_ref, xdt_t_ref, b_ref, c_ref,
                       nw_ref, wffn_hbm, fcw_ref, fcb_ref, wdown_hbm, nfw_ref,
                       y_ref, fst_ref, nst_ref, yt_ref, gate_ext, wffn_ref, wdown_ref, sems):
    tile = x1_ref.shape[1]
    t = pl.program_id(1)
    step = pl.program_id(0) * pl.num_programs(1) + t

    copies = [pltpu.make_async_copy(wffn_hbm, wffn_ref, sems.at[0]),
              pltpu.make_async_copy(wdown_hbm, wdown_ref, sems.at[1])]

    @pl.when(step == 0)
    def _():
        for copy in copies:
            copy.start()

    @pl.when(step < state_steps)
    def _():
        _state_block_update(da_ref, step, st_ref, xdt_t_ref, b_ref, c_ref, nst_ref, yt_ref)

    @pl.when(step == 0)
    def _():
        for copy in copies:
            copy.wait()

    @pl.when(t == 0)
    def _():
        gate_ext[0:CARRY, :] = jnp.zeros((CARRY, D_FF), F32)

    x1 = x1_ref[0]
    u = _rms_rows(x1, nw_ref[...]).astype(BF16)
    gate_ext[CARRY:CARRY + tile, :] = _dot(u, wffn_ref[:, 0:D_FF])
    up =_dot(u, wffn_ref[:, D_FF:2 * D_FF])
    g = fcb_ref[...] + gate_ext[CARRY - 2:CARRY - 2 + tile, :] * fcw_ref[0]
    for k in range(1, FFN_CONV):
        g = g + gate_ext[CARRY - 2 + k:CARRY - 2 + k + tile, :] * fcw_ref[k]
    tail = gate_ext[tile + CARRY - 2:tile + CARRY, :]
    fst_ref[0] = tail
    gate_ext[CARRY - 2:CARRY, :] = tail
    act = (_silu(g) * up).astype(BF16)
    x2 = x1 + _dot(act, wdown_ref[...])
    y_ref[0] = _rms_rows(x2, nfw_ref[...])


def _ffn_prompt(x1, nw, wffn, fcw, fcb, wdown, nfw, state, da, xdt_t, bc):
    nb, seq, _ = x1.shape
    tile = SEQ_TILE
    nt = seq // tile
    nseq = state.shape[0]
    seq_blocks = nseq // STATE_SEQS
    state_steps = seq_blocks * (NHEADS // STATE_HEADS)
    assert nb * nt >= state_steps
    heads_per_group = NHEADS // NGROUPS

    def block_of(b, t):
        s = jnp.minimum(b * nt + t, state_steps - 1)
        return s % seq_blocks, s // seq_blocks

    def st_map(b, t):
        sb, hb = block_of(b, t)
        return (sb, hb, 0, 0)

    def col_map(b, t):
        sb, hb = block_of(b, t)
        return (sb, hb, 0)

    def b_map(b, t):
        sb, hb = block_of(b, t)
        return (hb * STATE_HEADS // heads_per_group, sb, 0)

    def c_map(b, t):
        sb, hb = block_of(b, t)
        return (NGROUPS + hb * STATE_HEADS // heads_per_group, sb, 0)

    tok_spec = pl.BlockSpec((1, tile, D_MODEL), lambda b, t: (b, t, 0))
    st_spec = pl.BlockSpec((STATE_SEQS, STATE_HEADS, HEAD_DIM, NSTATE), st_map)
    col_spec = pl.BlockSpec((1, STATE_ROWS, STATE_SEQS), col_map)
    return pl.pallas_call(
        functools.partial(_ffn_prompt_kernel, state_steps),
        grid=(nb, nt),
        in_specs=[pl.BlockSpec(memory_space=pltpu.SMEM), tok_spec, st_spec, col_spec,
                  pl.BlockSpec((1, STATE_SEQS, NSTATE), b_map), pl.BlockSpec((1, STATE_SEQS, NSTATE), c_map)]
        + [pl.BlockSpec(memory_space=pl.ANY) if p is wffn or p is wdown else _const_spec(p.shape)
           for p in (nw, wffn, fcw, fcb, wdown, nfw)],
        out_specs=[tok_spec, pl.BlockSpec((1, FFN_CONV - 1, D_FF), lambda b, t: (b, 0, 0)), st_spec, col_spec],
        out_shape=[jax.ShapeDtypeStruct((nb, seq, D_MODEL), F32),
                   jax.ShapeDtypeStruct((nb, FFN_CONV - 1, D_FF), F32),
                   jax.ShapeDtypeStruct(state.shape, F32),
                   jax.ShapeDtypeStruct(xdt_t.shape, F32)],
        scratch_shapes=[pltpu.VMEM((tile + CARRY, D_FF), F32),
                        pltpu.VMEM(wffn.shape, BF16),
                        pltpu.VMEM(wdown.shape, BF16),
                        pltpu.SemaphoreType.DMA((2,))],
        compiler_params=pltpu.CompilerParams(
            dimension_semantics=("arbitrary", "arbitrary"), vmem_limit_bytes=VMEM_LIMIT),
        name="ffn_prompt",
    )(da, x1, state, xdt_t, bc, bc, nw, wffn, fcw, fcb, wdown, nfw)


CAST_ROWS = 512


def _sample_pre_kernel(x_ref, nw_ref, wf32_ref, cw_ref, cb_ref, dt_bias_ref, a_log_ref, d_ref, scw_ref, e_ref,
                       cst_ref, scst_ref,
                       z_ref, xs_ref, xdt_t_ref, da_ref, bc_ref, ysc_ref, ncst_ref, nscst_ref, win_ref,
                       dtb_ref, a_ref, dx_ref):
    pad = jnp.zeros((1, DT_PAD - NHEADS), F32)
    dtb_ref[...] = jnp.concatenate([dt_bias_ref[...], pad], axis=1)
    a_ref[...] = jnp.concatenate([-jnp.exp(a_log_ref[...]), pad], axis=1)
    d_pad = jnp.broadcast_to(jnp.concatenate([d_ref[...], pad], axis=1), (8, DT_PAD))
    dx_ref[...] = _dot_sel_rhs(d_pad, e_ref[...])[0:1, :]
    for r0 in range(0, D_IN_PROJ, CAST_ROWS):
        r1 = min(r0 + CAST_ROWS, D_IN_PROJ)
        win_ref[r0:r1, :] = wf32_ref[r0:r1, :].astype(BF16)
    x = x_ref[:, 0, :]
    u = _rms_rows(x, nw_ref[...]).astype(BF16)
    z_ref[...] = _dot_nt(u, win_ref[OFF_Z:OFF_Z + D_SSD, :])
    xbc = _dot_nt(u, win_ref[OFF_XBC:OFF_XBC + CONV_DIM, :])
    conv = cb_ref[...] + xbc * cw_ref[SSD_CONV - 1:SSD_CONV, :]
    for k in range(SSD_CONV - 1):
        conv = conv + cst_ref[k] * cw_ref[k:k + 1, :]
    for k in range(SSD_CONV - 2):
        ncst_ref[k] = cst_ref[k + 1]
    ncst_ref[SSD_CONV - 2] = xbc
    act = _silu(conv)
    xs = act[:, 0:D_SSD]
    xs_ref[...] = xs
    for i in range(2 * NGROUPS):
        bc_ref[i] = act[:, D_SSD + i * NSTATE:D_SSD + (i + 1) * NSTATE]
    dt = _softplus(_dot_nt(u, win_ref[OFF_DT:OFF_DT + DT_PAD, :]) + dtb_ref[...])
    xdt_t = (xs * _dot_sel_rhs(dt, e_ref[...])).T
    for sb in range(xdt_t_ref.shape[0]):
        xdt_t_ref[sb] = xdt_t[:, sb * STATE_SEQS:(sb + 1) * STATE_SEQS]
    da_ref[...] = jnp.exp(dt * a_ref[...])[:, 0:NHEADS]
    gc = _dot_nt(u, win_ref[OFF_GC:OFF_GC + D_SC, :])
    hh = _dot_nt(u, win_ref[OFF_H:OFF_H + D_SC, :])
    gch = gc * hh
    sc = gch * scw_ref[SC_CONV - 1]
    for k in range(SC_CONV - 1):
        sc = sc + scst_ref[:, k, :] * scw_ref[k]
    for k in range(SC_CONV - 2):
        nscst_ref[:, k, :] = scst_ref[:, k + 1, :]
    nscst_ref[:, SC_CONV - 2, :] = gch
    gb = _dot_nt(u, win_ref[OFF_GB:OFF_GB + D_SC, :])
    ysc_ref[...] = gb * sc


def _sample_pre(x, nw, w_in_t, cw, cb, dt_bias, a_log, d, scw, expand, cst, scst):
    nb = x.shape[0]
    outs = [
        jax.ShapeDtypeStruct((nb, D_SSD), F32),
        jax.ShapeDtypeStruct((nb, D_SSD), F32),
        jax.ShapeDtypeStruct((nb // STATE_SEQS, D_SSD, STATE_SEQS), F32),
        jax.ShapeDtypeStruct((nb, NHEADS), F32),
        jax.ShapeDtypeStruct((2 * NGROUPS, nb, NSTATE), F32),
        jax.ShapeDtypeStruct((nb, D_SC), F32),
        jax.ShapeDtypeStruct((SSD_CONV - 1, nb, CONV_DIM), F32),
        jax.ShapeDtypeStruct((nb, SC_CONV - 1, D_SC), F32),
        jax.ShapeDtypeStruct((D_IN_PROJ, D_MODEL), BF16),
        jax.ShapeDtypeStruct((1, DT_PAD), F32),
        jax.ShapeDtypeStruct((1, DT_PAD), F32),
        jax.ShapeDtypeStruct((1, D_SSD), F32),
    ]
    return pl.pallas_call(
        _sample_pre_kernel,
        out_shape=outs,
        compiler_params=pltpu.CompilerParams(vmem_limit_bytes=VMEM_LIMIT),
        name="sample_pre",
    )(x, nw, w_in_t, cw, cb, dt_bias, a_log, d, scw, expand, cst, scst)


def _sample_post_kernel(x_ref, yt_ref, xs_ref, z_ref, ysc_ref, dx_ref, gnw_ref, wout_hbm,
                        nw_ref, wffn_hbm, fcw_ref, fcb_ref, wdown_hbm, nfw_ref, fst_ref,
                        out_ref, nfst_ref,
                        wout_ref, wffn_ref, wdown_ref, sems):
    copies = [pltpu.make_async_copy(src, dst, sems.at[i]) for i, (src, dst) in enumerate(
        ((wout_hbm, wout_ref), (wffn_hbm, wffn_ref), (wdown_hbm, wdown_ref)))]
    for copy in copies:
        copy.start()
    wout_copy, wffn_copy, wdown_copy = copies
    y_ssd = jnp.concatenate([yt_ref[sb] for sb in range(yt_ref.shape[0])], axis=1).T
    y = (y_ssd + dx_ref[...] * xs_ref[...]) * _silu(z_ref[...])
    gw = D_SSD // NGROUPS
    parts = [_rms_rows(y[:, g * gw:(g + 1) * gw], gnw_ref[:, g * gw:(g + 1) * gw]) for g in range(NGROUPS)]
    ycat = jnp.concatenate(parts + [ysc_ref[...]], axis=1).astype(BF16)
    wout_copy.wait()
    x1 = x_ref[:, 0, :] + _dot(ycat, wout_ref[...])
    u = _rms_rows(x1, nw_ref[...]).astype(BF16)
    wffn_copy.wait()
    gate = _dot(u, wffn_ref[:, 0:D_FF])
    up = _dot(u, wffn_ref[:, D_FF:2 * D_FF])
    g = fcb_ref[...] + gate * fcw_ref[FFN_CONV - 1]
    for k in range(FFN_CONV - 1):
        g = g + fst_ref[:, k, :] * fcw_ref[k]
    for k in range(FFN_CONV - 2):
        nfst_ref[:, k, :] = fst_ref[:, k + 1, :]
    nfst_ref[:, FFN_CONV - 2, :] = gate
    act = (_silu(g) * up).astype(BF16)
    wdown_copy.wait()
    x2 = x1 + _dot(act, wdown_ref[...])
    out_ref[:, 0, :] = _rms_rows(x2, nfw_ref[...])


def _sample_post(x, y, xs, z, ysc, dx, gnw, wout, nw, wffn, fcw, fcb, wdown, nfw, fst):
    nb = x.shape[0]
    in_vmem = pl.BlockSpec(memory_space=pltpu.VMEM)
    in_hbm = pl.BlockSpec(memory_space=pl.ANY)
    return pl.pallas_call(
        _sample_post_kernel,
        in_specs=[in_vmem] * 7 + [in_hbm, in_vmem, in_hbm, in_vmem, in_vmem, in_hbm, in_vmem, in_vmem],
        out_shape=[jax.ShapeDtypeStruct((nb, 1, D_MODEL), F32),
                   jax.ShapeDtypeStruct((nb, FFN_CONV - 1, D_FF), F32)],
        scratch_shapes=[pltpu.VMEM(wout.shape, BF16), pltpu.VMEM(wffn.shape, BF16), pltpu.VMEM(wdown.shape, BF16),
                        pltpu.SemaphoreType.DMA((3,))],
        compiler_params=pltpu.CompilerParams(vmem_limit_bytes=VMEM_LIMIT),
        name="sample_post",
    )(x, y, xs, z, ysc, dx, gnw, wout, nw, wffn, fcw, fcb, wdown, nfw, fst)


def kernel(x_prompt, x_sample, state_ssm, state_ssd_conv, state_short_conv, state_ffn_conv,
           norm_mix_w, w_in, ssd_conv_w, ssd_conv_b, ssd_dt_bias, ssd_a_log, ssd_d, ssd_norm_w,
           sc_conv_w, w_out, norm_ffn_w, w_ffn_in, ffn_conv_w, ffn_conv_b, w_down, norm_final_w):
    depth = w_in.shape[0]
    assert depth == 1
    w_in_t = jnp.swapaxes(w_in[0], 0, 1)
    nw = norm_mix_w[0].reshape(1, D_MODEL)
    nw2 = norm_ffn_w[0].reshape(1, D_MODEL)
    nfw = norm_final_w.reshape(1, D_MODEL)
    cw = ssd_conv_w[0]
    cb = ssd_conv_b[0].reshape(1, CONV_DIM)
    gnw = ssd_norm_w[0].reshape(1, D_SSD)
    scw = jnp.swapaxes(sc_conv_w, 0, 1)
    fcw = jnp.swapaxes(ffn_conv_w, 0, 1)
    fcb = ffn_conv_b[0].reshape(1, D_FF)
    expand = jnp.asarray(np.arange(DT_PAD)[:, None] == np.arange(D_SSD)[None, :] // HEAD_DIM, dtype=BF16)

    xs_in = x_sample
    cst = jnp.swapaxes(state_ssd_conv[0], 0, 1)
    scst, fst = state_short_conv[0], state_ffn_conv[0]
    z, xs, xdt_t, da, bc, ysc, ncst, nscst, win, dtb, a_pad, dx = _sample_pre(
        xs_in, nw, w_in_t, cw, cb, ssd_dt_bias, ssd_a_log, ssd_d, scw, expand, cst, scst)

    wout = w_out[0].astype(BF16)
    x1, p_ssm, p_cst, p_scst, wffn, wdown = _mixer_prompt(
        x_prompt, nw, win, cw, cb, dtb, a_pad, dx, gnw, scw, wout, expand, w_ffn_in[0], w_down[0])
    y_prompt, p_fst, s_ssm, y_t = _ffn_prompt(x1, nw2, wffn, fcw, fcb, wdown, nfw, state_ssm[0], da, xdt_t, bc)
    y_sample, nfst = _sample_post(xs_in, y_t, xs, z, ysc, dx, gnw, wout, nw2, wffn, fcw, fcb, wdown, nfw, fst)

    return (y_prompt, y_sample,
            p_ssm[None], p_cst[None], p_scst[None], p_fst[None],
            s_ssm[None], jnp.swapaxes(ncst, 0, 1)[None], nscst[None], nfst[None])
```
